```python
import math
import jax, jax.numpy as jnp
from jax import lax
import numpy as np

D_MODEL = 1024
BATCH = 4
SEQ = 8192
DEPTH = 2
DEC_BATCH = 8
DEC_SEQ = 64
PAST_LEN = 2048

CHUNK = 64
Q_BLOCK = 128
N_EVEN = (DEPTH + 1) // 2
N_ODD = DEPTH // 2
NORM_EPS = 1e-6
SUBLN_EPS = 1e-5

H_A = 8
NOPE_DIM = 64
ROPE_DIM = 32
V_DIM_A = 64
Q_RANK = 384
KV_RANK = 256
ROPE_BASE = 10000.0
H_B = 4
DH_B = 64
V_DIM_B = 2 * DH_B
B_QK = H_B * 2 * DH_B
H_C = 16
DH_C = 64
LEFT_CHUNKS = 8
REL_CLIP = 128
N_GROUPS = 4
EXPERTS_PER_GROUP = 8
N_EXPERTS = N_GROUPS * EXPERTS_PER_GROUP
TOP_K = 2
D_EXPERT = 512
MOE_BLOCK = 128

A_IN = Q_RANK + KV_RANK + ROPE_DIM
EVEN_IN = A_IN + 3 * B_QK
EVEN_SPLITS = [Q_RANK, Q_RANK + KV_RANK, A_IN, A_IN + B_QK, A_IN + 2 * B_QK]
EVEN_MIX = H_A * V_DIM_A + H_B * V_DIM_B
ODD_MIX = H_C * DH_C

kernel_name = "hybrid_stream_mla_diff_band_hmoe"


def rmsnorm(x, g, eps=NORM_EPS):
    xf = x.astype(jnp.float32)
    y = xf * lax.rsqrt(jnp.mean(xf * xf, axis=-1, keepdims=True) + eps)
    return (y * g.astype(jnp.float32)).astype(x.dtype)


def rope(x, pos):
    half = x.shape[-1] // 2
    inv = jnp.power(ROPE_BASE, -jnp.arange(half, dtype=jnp.float32) / half)
    ang = pos.astype(jnp.float32)[:, None] * inv[None, :]
    cos = jnp.cos(ang)[None, :, None, :]
    sin = jnp.sin(ang)[None, :, None, :]
    x1 = x[..., :half].astype(jnp.float32)
    x2 = x[..., half:].astype(jnp.float32)
    return jnp.concatenate([x1 * cos - x2 * sin, x2 * cos + x1 * sin], axis=-1).astype(x.dtype)


def alibi_slopes(n):
    return jnp.exp2(-8.0 * jnp.arange(1, n + 1, dtype=jnp.float32) / n)


def alibi_bias(slopes, qpos, kpos):
    if slopes is None:
        return 0.0
    dist = jnp.abs(qpos[:, None] - kpos[None, :]).astype(jnp.float32)
    return -slopes[:, None, None] * dist[None]


def relpos_bias(table, qpos, kpos):
    idx = jnp.clip(qpos[:, None] - kpos[None, :], -REL_CLIP, REL_CLIP) + REL_CLIP
    return table.astype(jnp.float32)[:, idx]


def chunk_visible(qpos, kpos, left_chunks=None):
    qc = (qpos // CHUNK)[:, None]
    kc = (kpos // CHUNK)[None, :]
    vis = kc <= qc
    if left_chunks is not None:
        vis = vis & (kc >= qc - left_chunks)
    return vis & (kpos >= 0)[None, :]


def attend(q, k, v, bias, visible, scale):
    s = jnp.einsum('bqhd,bkhd->bhqk', q, k).astype(jnp.float32) * scale + bias
    s = jnp.where(visible, s, -jnp.inf)
    p = jax.nn.softmax(s, axis=-1).astype(v.dtype)
    return jnp.einsum('bhqk,bkhd->bqhd', p, v)


def prompt_dense_attention(q, k, v, slopes, scale):
    S = q.shape[1]
    outs = []
    for b in range(S // Q_BLOCK):
        lo, hi = b * Q_BLOCK, (b + 1) * Q_BLOCK
        qpos = jnp.arange(lo, hi)
        kpos = jnp.arange(hi)
        outs.append(attend(q[:, lo:hi], k[:, :hi], v[:, :hi], alibi_bias(slopes, qpos, kpos),
                           chunk_visible(qpos, kpos), scale))
    return jnp.concatenate(outs, axis=1)


def band_prompt_attention(q, k, v, rel_table, scale):
    B, S, H, D = q.shape
    nc = S // CHUNK
    pad = LEFT_CHUNKS * CHUNK
    band = (LEFT_CHUNKS + 1) * CHUNK
    widths = ((0, 0), (pad, 0), (0, 0), (0, 0))
    kp = jnp.pad(k, widths)
    vp = jnp.pad(v, widths)
    qc = jnp.moveaxis(q.reshape(B, nc, CHUNK, H, D), 1, 0)

    def one(args):
        q_blk, c = args
        start = c * CHUNK
        k_blk = lax.dynamic_slice_in_dim(kp, start, band, axis=1)
        v_blk = lax.dynamic_slice_in_dim(vp, start, band, axis=1)
        qpos = start + jnp.arange(CHUNK)
        kpos = start - pad + jnp.arange(band)
        return attend(q_blk, k_blk, v_blk, relpos_bias(rel_table, qpos, kpos),
                      chunk_visible(qpos, kpos, LEFT_CHUNKS), scale)

    out = lax.map(one, (qc, jnp.arange(nc)))
    return jnp.moveaxis(out, 0, 1).reshape(B, S, H, D)


def even_mixer(h, pos, past, w_in, q_norm, w_qup, kv_norm, w_kvup,
               lam_q1, lam_k1, lam_q2, lam_k2, subln, w_out, layer_idx):
    B, T, _ = h.shape
    cq, ckv, krope, qd, kd, vd = jnp.split(h @ w_in, EVEN_SPLITS, axis=-1)
    q = (rmsnorm(cq, q_norm) @ w_qup).reshape(B, T, H_A, NOPE_DIM + ROPE_DIM)
    q = jnp.concatenate([q[..., :NOPE_DIM], rope(q[..., NOPE_DIM:], pos)], axis=-1)
    ckv = rmsnorm(ckv, kv_norm)
    krope = rope(krope[:, :, None, :], pos)[:, :, 0]
    qd = qd.reshape(B, T, H_B, 2, DH_B)
    kd = kd.reshape(B, T, H_B, 2 * DH_B)
    vd = vd.reshape(B, T, H_B, V_DIM_B)
    new_state = (ckv, krope, kd, vd)
    if past is None:
        ckv_all, krope_all, kd_all, vd_all = ckv, krope, kd, vd
        attn = prompt_dense_attention
    else:
        p_ckv, p_krope, p_kd, p_vd = past
        n_past = p_ckv.shape[1]
        ckv_all = jnp.concatenate([p_ckv, ckv], axis=1)
        krope_all = jnp.concatenate([p_krope, krope], axis=1)
        kd_all = jnp.concatenate([p_kd, kd], axis=1)
        vd_all = jnp.concatenate([p_vd, vd], axis=1)
        kpos = jnp.concatenate([jnp.arange(n_past), pos])

        def attn(q_, k_, v_, slopes, scale):
            return attend(q_, k_, v_, alibi_bias(slopes, pos, kpos), chunk_visible(pos, kpos), scale)
    S_all = ckv_all.shape[1]
    kv = (ckv_all @ w_kvup).reshape(B, S_all, H_A, NOPE_DIM + V_DIM_A)
    k_mla = jnp.concatenate([kv[..., :NOPE_DIM],
                             jnp.broadcast_to(krope_all[:, :, None, :], (B, S_all, H_A, ROPE_DIM))], axis=-1)
    o_a = attn(q, k_mla, kv[..., NOPE_DIM:], None, (NOPE_DIM + ROPE_DIM) ** -0.5)
    kd2 = kd_all.reshape(B, S_all, H_B, 2, DH_B)
    q_st = jnp.concatenate([qd[..., 0, :], qd[..., 1, :]], axis=2)
    k_st = jnp.concatenate([kd2[..., 0, :], kd2[..., 1, :]], axis=2)
    v_st = jnp.concatenate([vd_all, vd_all], axis=2)
    slopes = alibi_slopes(H_B)
    o = attn(q_st, k_st, v_st, jnp.concatenate([slopes, slopes]), DH_B ** -0.5)
    lam_init = 0.8 - 0.6 * math.exp(-0.3 * layer_idx)
    f32 = jnp.float32
    lam = (jnp.exp(jnp.sum(lam_q1.astype(f32) * lam_k1.astype(f32)))
           - jnp.exp(jnp.sum(lam_q2.astype(f32) * lam_k2.astype(f32))) + lam_init).astype(o.dtype)
    o_b = o[:, :, :H_B] - lam * o[:, :, H_B:]
    o_b = rmsnorm(o_b, subln, SUBLN_EPS) * (1.0 - lam_init)
    mix = jnp.concatenate([o_a.reshape(B, T, -1), o_b.reshape(B, T, -1)], axis=-1)
    return mix @ w_out, new_state


def odd_mixer(h, pos, past, c_past, w_in, rel_table, w_out):
    B, T, _ = h.shape
    q, k, v = [a.reshape(B, T, H_C, DH_C) for a in jnp.split(h @ w_in, 3, axis=-1)]
    scale = DH_C ** -0.5
    if past is None:
        o = band_prompt_attention(q, k, v, rel_table, scale)
        widths = ((0, 0), (c_past, 0), (0, 0), (0, 0))
        new_k = jnp.pad(k, widths)[:, -c_past:]
        new_v = jnp.pad(v, widths)[:, -c_past:]
    else:
        pk, pv = past
        kpos = jnp.concatenate([pos[0] - c_past + jnp.arange(c_past), pos])
        k_all = jnp.concatenate([pk, k], axis=1)
        v_all = jnp.concatenate([pv, v], axis=1)
        o = attend(q, k_all, v_all, relpos_bias(rel_table, pos, kpos),
                   chunk_visible(pos, kpos, LEFT_CHUNKS), scale)
        new_k = k_all[:, -c_past:]
        new_v = v_all[:, -c_past:]
    return o.reshape(B, T, -1) @ w_out, (new_k, new_v)


def routed_experts(xf, experts, gates, w_gate, w_up, w_down):
    N, D = xf.shape
    NK = N * TOP_K
    flat_e = experts.reshape(-1).astype(jnp.int32)
    order = jnp.argsort(flat_e)
    sorted_e = flat_e[order]
    tok = (order // TOP_K).astype(jnp.int32)
    counts = jnp.bincount(flat_e, length=N_EXPERTS)
    padded = ((counts + MOE_BLOCK - 1) // MOE_BLOCK) * MOE_BLOCK
    pstart = jnp.cumsum(padded) - padded
    start = jnp.cumsum(counts) - counts
    dest = pstart[sorted_e] + (jnp.arange(NK) - start[sorted_e])
    n_blocks = -(-NK // MOE_BLOCK) + N_EXPERTS
    L = n_blocks * MOE_BLOCK
    buf_tok = jnp.full((L,), N, jnp.int32).at[dest].set(tok)
    block_expert = jnp.minimum(jnp.searchsorted(jnp.cumsum(padded), jnp.arange(n_blocks) * MOE_BLOCK,
                                                side='right'), N_EXPERTS - 1)
    xpad = jnp.concatenate([xf, jnp.zeros((1, D), xf.dtype)], axis=0)
    xb = xpad[buf_tok].reshape(n_blocks, MOE_BLOCK, D)

    def run(args):
        xblk, e = args
        return (jax.nn.silu(xblk @ w_gate[e]) * (xblk @ w_up[e])) @ w_down[e]

    yb = lax.map(run, (xb, block_expert)).reshape(L, D)
    g_sorted = gates.reshape(-1)[order].astype(yb.dtype)
    return jax.ops.segment_sum(yb[dest] * g_sorted[:, None], tok, num_segments=N)


def moe_ffn(x, w_group, b_group, w_router, b_router, w_gate, w_up, w_down):
    B, T, D = x.shape
    xf = x.reshape(-1, D)
    N = xf.shape[0]
    g_logits = (xf @ w_group).astype(jnp.float32) + b_group.astype(jnp.float32)
    g_prob = jax.nn.softmax(g_logits, axis=-1)
    g_sel = jnp.argmax(g_logits, axis=-1)
    p_grp = jnp.take_along_axis(g_prob, g_sel[:, None], axis=-1)
    e_logits = ((xf @ w_router).astype(jnp.float32) + b_router.astype(jnp.float32)).reshape(
        N, N_GROUPS, EXPERTS_PER_GROUP)
    e_logits = jnp.take_along_axis(e_logits, g_sel[:, None, None], axis=1)[:, 0]
    top_v, top_i = lax.top_k(e_logits, TOP_K)
    gates = jax.nn.softmax(top_v, axis=-1) * p_grp
    experts = g_sel[:, None] * EXPERTS_PER_GROUP + top_i
    return routed_experts(xf, experts, gates, w_gate, w_up, w_down).reshape(B, T, D)


def setup_inputs(seed: int = 0) -> dict:
    key = jax.random.key(seed)
    keys = jax.random.split(key, 40)
    counter = [0]

    def nrm(shape, scale=1.0):
        k = keys[counter[0]]
        counter[0] += 1
        return scale * jax.random.normal(k, shape, jnp.float32)

    def gain(shape):
        return 1.0 + 0.02 * nrm(shape)

    c_past = min(LEFT_CHUNKS * CHUNK, PAST_LEN)
    D = D_MODEL
    return {
        "x_prompt": nrm((BATCH, SEQ, D)),
        "x_sample": nrm((DEC_BATCH, DEC_SEQ, D)),
        "cache_mla_ckv": nrm((N_EVEN, DEC_BATCH, PAST_LEN, KV_RANK)),
        "cache_mla_krope": nrm((N_EVEN, DEC_BATCH, PAST_LEN, ROPE_DIM)),
        "cache_diff_k": nrm((N_EVEN, DEC_BATCH, PAST_LEN, H_B, 2 * DH_B)),
        "cache_diff_v": nrm((N_EVEN, DEC_BATCH, PAST_LEN, H_B, V_DIM_B)),
        "cache_band_k": nrm((N_ODD, DEC_BATCH, c_past, H_C, DH_C)),
        "cache_band_v": nrm((N_ODD, DEC_BATCH, c_past, H_C, DH_C)),
        "ln_mix": gain((DEPTH, D)),
        "w_in_even": nrm((N_EVEN, D, EVEN_IN), D ** -0.5),
        "mla_q_norm": gain((N_EVEN, Q_RANK)),
        "mla_w_qup": nrm((N_EVEN, Q_RANK, H_A * (NOPE_DIM + ROPE_DIM)), Q_RANK ** -0.5),
        "mla_kv_norm": gain((N_EVEN, KV_RANK)),
        "mla_w_kvup": nrm((N_EVEN, KV_RANK, H_A * (NOPE_DIM + V_DIM_A)), KV_RANK ** -0.5),
        "diff_lam_q1": nrm((N_EVEN, DH_B), 0.1),
        "diff_lam_k1": nrm((N_EVEN, DH_B), 0.1),
        "diff_lam_q2": nrm((N_EVEN, DH_B), 0.1),
        "diff_lam_k2": nrm((N_EVEN, DH_B), 0.1),
        "diff_subln": gain((N_EVEN, V_DIM_B)),
        "w_out_even": nrm((N_EVEN, EVEN_MIX, D), EVEN_MIX ** -0.5),
        "w_in_odd": nrm((N_ODD, D, 3 * ODD_MIX), D ** -0.5),
        "band_rel_bias": nrm((N_ODD, H_C, 2 * REL_CLIP + 1), 0.1),
        "w_out_odd": nrm((N_ODD, ODD_MIX, D), ODD_MIX ** -0.5),
        "ln_ffn": gain((DEPTH, D)),
        "moe_w_group": nrm((DEPTH, D, N_GROUPS), D ** -0.5),
        "moe_b_group": nrm((DEPTH, N_GROUPS), 0.01),
        "moe_w_router": nrm((DEPTH, D, N_EXPERTS), D ** -0.5),
        "moe_b_router": nrm((DEPTH, N_EXPERTS), 0.01),
        "moe_w_gate": nrm((DEPTH, N_EXPERTS, D, D_EXPERT), D ** -0.5),
        "moe_w_up": nrm((DEPTH, N_EXPERTS, D, D_EXPERT), D ** -0.5),
        "moe_w_down": nrm((DEPTH, N_EXPERTS, D_EXPERT, D), D_EXPERT ** -0.5),
        "ln_final": gain((D,)),
    }


def reference(x_prompt, x_sample, cache_mla_ckv, cache_mla_krope, cache_diff_k, cache_diff_v,
              cache_band_k, cache_band_v, ln_mix, w_in_even, mla_q_norm, mla_w_qup, mla_kv_norm,
              mla_w_kvup, diff_lam_q1, diff_lam_k1, diff_lam_q2, diff_lam_k2, diff_subln, w_out_even,
              w_in_odd, band_rel_bias, w_out_odd, ln_ffn, moe_w_group, moe_b_group, moe_w_router,
              moe_b_router, moe_w_gate, moe_w_up, moe_w_down, ln_final):
    c_past = cache_band_k.shape[2]
    pos_p = jnp.arange(x_prompt.shape[1])
    pos_s = PAST_LEN + jnp.arange(x_sample.shape[1])
    xp, xs = x_prompt, x_sample
    even_p, even_s, odd_p, odd_s = [], [], [], []
    for l in range(DEPTH):
        hp = rmsnorm(xp, ln_mix[l])
        hs = rmsnorm(xs, ln_mix[l])
        if l % 2 == 0:
            i = l // 2
            w = (w_in_even[i], mla_q_norm[i], mla_w_qup[i], mla_kv_norm[i], mla_w_kvup[i],
                 diff_lam_q1[i], diff_lam_k1[i], diff_lam_q2[i], diff_lam_k2[i], diff_subln[i], w_out_even[i])
            yp, sp = even_mixer(hp, pos_p, None, *w, layer_idx=l)
            ys, ss = even_mixer(hs, pos_s, (cache_mla_ckv[i], cache_mla_krope[i], cache_diff_k[i],
                                            cache_diff_v[i]), *w, layer_idx=l)
            even_p.append(sp)
            even_s.append(ss)
        else:
            i = l // 2
            w = (w_in_odd[i], band_rel_bias[i], w_out_odd[i])
            yp, sp = odd_mixer(hp, pos_p, None, c_past, *w)
            ys, ss = odd_mixer(hs, pos_s, (cache_band_k[i], cache_band_v[i]), c_past, *w)
            odd_p.append(sp)
            odd_s.append(ss)
        xp = xp + yp
        xs = xs + ys
        wm = (moe_w_group[l], moe_b_group[l], moe_w_router[l], moe_b_router[l],
              moe_w_gate[l], moe_w_up[l], moe_w_down[l])
        xp = xp + moe_ffn(rmsnorm(xp, ln_ffn[l]), *wm)
        xs = xs + moe_ffn(rmsnorm(xs, ln_ffn[l]), *wm)
    y_prompt = rmsnorm(xp, ln_final)
    y_sample = rmsnorm(xs, ln_final)

    def stk(states, j):
        return jnp.stack([s[j] for s in states], axis=0)

    return (y_prompt, y_sample,
            stk(even_p, 0), stk(even_p, 1), stk(even_p, 2), stk(even_p, 3), stk(odd_p, 0), stk(odd_p, 1),
            stk(even_s, 0), stk(even_s, 1), stk(even_s, 2), stk(even_s, 3), stk(odd_s, 0), stk(odd_s, 1))
```

```python
import functools
import math

import jax
import jax.numpy as jnp
from jax import lax
from jax.experimental import pallas as pl
from jax.experimental.pallas import tpu as pltpu

BF = jnp.bfloat16
F32 = jnp.float32
NEG_INF = float("-inf")

CHUNK = 64
NORM_EPS = 1e-6
SUBLN_EPS = 1e-5
H_A, NOPE, ROPE, V_A, Q_RANK, KV_RANK = 8, 64, 32, 64, 384, 256
ROPE_BASE = 10000.0
H_B, DH_B, V_B = 4, 64, 128
H_C, DH_C, LEFT_CHUNKS, REL_CLIP = 16, 64, 8, 128
N_GROUPS, EPG, N_EXPERTS, D_EXPERT = 4, 8, 32, 512
LANE = 128
HEAD_PAD = 128
ROUTE_OFF = N_GROUPS

ROW_TILE = 256
ATT_TILE = 512
BAND_TILE = 256
MOE_BLOCK = 256
VMEM_LIMIT = 56 * 1024 * 1024


def _cparams(*sem):
    return pltpu.CompilerParams(dimension_semantics=sem, vmem_limit_bytes=VMEM_LIMIT)


def _rms(x, g, eps):
    return x * lax.rsqrt(jnp.mean(x * x, axis=-1, keepdims=True) + eps) * g


def _dot(a, b):
    return jnp.dot(a, b, preferred_element_type=F32)


def _dot_nt(a, b):
    return lax.dot_general(a, b, (((1,), (1,)), ((), ())), preferred_element_type=F32)


def _lane_iota(shape):
    return lax.broadcasted_iota(jnp.int32, shape, len(shape) - 1)


def _split_halves(q):
    qf = q.astype(F32)
    lane = _lane_iota(qf.shape)
    return jnp.concatenate([jnp.where(lane < 64, qf, 0.0), jnp.where(lane >= 64, qf, 0.0)], axis=0).astype(BF)


def _softmax_pv(s_list, v_list):
    m = functools.reduce(jnp.maximum, [jnp.max(s, axis=-1, keepdims=True) for s in s_list])
    acc, l = None, None
    for s, v in zip(s_list, v_list):
        p = jnp.exp(s - m)
        ls = jnp.sum(p, axis=-1, keepdims=True)
        a = _dot(p.astype(BF), v)
        l = ls if l is None else l + ls
        acc = a if acc is None else acc + a
    return acc / l


def _diff_lambda(lamv, lam_init):
    a = jnp.exp(jnp.sum(lamv[0:1] * lamv[1:2], axis=-1, keepdims=True))
    b = jnp.exp(jnp.sum(lamv[2:3] * lamv[3:4], axis=-1, keepdims=True))
    return a - b + lam_init


def _diff_finish(o1, o2, lam, subln, lam_init):
    o = o1 - lam * o2
    return _rms(o, subln, SUBLN_EPS) * (1.0 - lam_init)


def _even_in_kernel(xp_ref, xs_ref, g_ref, w0_ref, qn_ref, wq_ref, wqs_ref, kvn_ref, wk_ref, wv_ref,
                    cs_ref, sn_ref,
                    qa_ref, ka_ref, va_ref, ckv_ref, kr_ref, qd_ref, kdb_ref, vdb_ref, kd_ref, vd_ref,
                    *, n_prompt_tiles, a_scale, b_scale):
    i = pl.program_id(0)
    x = jnp.where(i < n_prompt_tiles, xp_ref[...], xs_ref[...])
    h = _rms(x, g_ref[...], NORM_EPS).astype(BF)
    y = _dot(h, w0_ref[...])
    cq, ckv = y[:, 0:384], y[:, 384:640]
    kr_m, kr_s = y[:, 640:768], y[:, 768:896]
    qd, kd, vd = y[:, 896:1408], y[:, 1408:1920], y[:, 1920:2432]
    cs, sn = cs_ref[...], sn_ref[...]
    cqn = _rms(cq, qn_ref[...], NORM_EPS).astype(BF)
    qm = _dot(cqn, wq_ref[...])
    qs = _dot(cqn, wqs_ref[...])
    ckvn = _rms(ckv, kvn_ref[...], NORM_EPS)
    ckv_ref[...] = ckvn
    krp = kr_m * cs + kr_s * sn
    kr_ref[...] = krp[:, NOPE:NOPE + ROPE]
    cb = ckvn.astype(BF)
    kn = _dot(cb, wk_ref[...])
    for hh in range(H_A):
        sl = slice(hh * HEAD_PAD, (hh + 1) * HEAD_PAD)
        qa_ref[:, sl] = ((qm[:, sl] * cs + qs[:, sl] * sn) * a_scale).astype(BF)
        ka_ref[:, sl] = (kn[:, sl] + krp).astype(BF)
    va_ref[...] = _dot(cb, wv_ref[...]).astype(BF)
    qd_ref[...] = (qd * b_scale).astype(BF)
    kd_ref[...] = kd
    vd_ref[...] = vd
    kdb_ref[...] = kd.astype(BF)
    vdb_ref[...] = vd.astype(BF)


def _even_in(xp, xs, g, w0, qn, wq, wqs, kvn, wk, wv, cs_tab, sn_tab, *, seq):
    n_p, n_s = xp.shape[0] // ROW_TILE, xs.shape[0] // ROW_TILE
    T = xp.shape[0] + xs.shape[0]
    D = xp.shape[1]
    pos_blocks = seq // ROW_TILE

    def full(a):
        return pl.BlockSpec(a.shape, lambda i: (0,) * a.ndim)

    def rows(width):
        return pl.BlockSpec((ROW_TILE, width), lambda i: (i, 0))

    pos_spec = pl.BlockSpec((ROW_TILE, LANE), lambda i: (jnp.where(i < n_p, i % pos_blocks, pos_blocks), 0))
    widths = [(1024, BF), (1024, BF), (512, BF), (KV_RANK, F32), (ROPE, F32),
              (512, BF), (512, BF), (512, BF), (512, F32), (512, F32)]
    return pl.pallas_call(
        functools.partial(_even_in_kernel, n_prompt_tiles=n_p,
                          a_scale=(NOPE + ROPE) ** -0.5, b_scale=DH_B ** -0.5),
        grid=(n_p + n_s,),
        in_specs=[pl.BlockSpec((ROW_TILE, D), lambda i: (jnp.minimum(i, n_p - 1), 0)),
                  pl.BlockSpec((ROW_TILE, D), lambda i: (jnp.maximum(i - n_p, 0), 0)),
                  full(g), full(w0), full(qn), full(wq), full(wqs), full(kvn), full(wk), full(wv),
                  pos_spec, pos_spec],
        out_specs=[rows(w) for w, _ in widths],
        out_shape=[jax.ShapeDtypeStruct((T, w), dt) for w, dt in widths],
        compiler_params=_cparams("parallel"),
        name="even_in",
    )(xp, xs, g, w0, qn, wq, wqs, kvn, wk, wv, cs_tab, sn_tab)


def _chunk_causal_mask(tq, tk):
    row = lax.broadcasted_iota(jnp.int32, (tq, tk), 0)
    col = lax.broadcasted_iota(jnp.int32, (tq, tk), 1)
    return (col // CHUNK) <= (row // CHUNK)


def _mla_prompt_kernel(q_ref, k_ref, v_ref, o_ref, *, tile):
    qi = pl.program_id(2)
    diag_mask = _chunk_causal_mask(tile, tile)
    res = []
    for hh in range(2):
        sl = slice(hh * HEAD_PAD, (hh + 1) * HEAD_PAD)
        q = q_ref[:, sl]

        def step(j, carry, masked, sl=sl, q=q):
            m, l, acc = carry
            start = pl.multiple_of(j * tile, tile)
            k = k_ref[pl.ds(start, tile), sl]
            v = v_ref[pl.ds(start, tile), :]
            s = _dot_nt(q, k)
            if masked:
                s = jnp.where(diag_mask, s, NEG_INF)
            m_new = jnp.maximum(m, jnp.max(s, axis=-1, keepdims=True))
            p = jnp.exp(s - m_new)
            alpha = jnp.exp(m - m_new)
            l = alpha * l + jnp.sum(p, axis=-1, keepdims=True)
            acc = alpha * acc + _dot(p.astype(BF), v)
            return m_new, l, acc

        init = (jnp.full((tile, 1), NEG_INF, F32), jnp.zeros((tile, 1), F32), jnp.zeros((tile, LANE), F32))
        carry = lax.fori_loop(0, qi, functools.partial(step, masked=False), init)
        _, l, acc = step(qi, carry, True)
        res.append(acc / l)
    lane = _lane_iota((tile, LANE))
    o_ref[...] = jnp.where(lane < V_A, res[0], res[1]).astype(BF)


def _mla_prompt(qa, ka, va, *, batch, seq):
    tile = min(ATT_TILE, seq)
    nq = seq // tile
    return pl.pallas_call(
        functools.partial(_mla_prompt_kernel, tile=tile),
        grid=(batch, H_A // 2, nq),
        in_specs=[pl.BlockSpec((tile, 2 * HEAD_PAD), lambda b, p, qi: (b * nq + qi, p)),
                  pl.BlockSpec((seq, 2 * HEAD_PAD), lambda b, p, qi: (b, p)),
                  pl.BlockSpec((seq, LANE), lambda b, p, qi: (b, p))],
        out_specs=pl.BlockSpec((tile, LANE), lambda b, p, qi: (b * nq + qi, p)),
        out_shape=jax.ShapeDtypeStruct((batch * seq, H_A * V_A), BF),
        compiler_params=_cparams("parallel", "parallel", "arbitrary"),
        name="mla_prompt",
    )(qa, ka, va)


def _diff_prompt_kernel(slopes_ref, lamv_ref, subln_ref, q_ref, k_ref, v_ref, o_ref, *, tile, lam_init):
    h = pl.program_id(1)
    qi = pl.program_id(2)
    slope = slopes_ref[h]
    q2x = _split_halves(q_ref[...])
    row = lax.broadcasted_iota(jnp.int32, (tile, tile), 0)
    col = lax.broadcasted_iota(jnp.int32, (tile, tile), 1)
    dist = (row - col).astype(F32)
    dist2 = jnp.concatenate([dist, dist], axis=0)
    mask2 = jnp.concatenate([_chunk_causal_mask(tile, tile)] * 2, axis=0)

    def step(j, carry, masked):
        m, l, acc = carry
        start = pl.multiple_of(j * tile, tile)
        k = k_ref[pl.ds(start, tile), :]
        v = v_ref[pl.ds(start, tile), :]
        s = _dot_nt(q2x, k)
        if masked:
            s = jnp.where(mask2, s - slope * jnp.abs(dist2), NEG_INF)
        else:
            off = ((qi - j) * tile).astype(F32)
            s = s - slope * (dist2 + off)
        m_new = jnp.maximum(m, jnp.max(s, axis=-1, keepdims=True))
        p = jnp.exp(s - m_new)
        alpha = jnp.exp(m - m_new)
        l = alpha * l + jnp.sum(p, axis=-1, keepdims=True)
        acc = alpha * acc + _dot(p.astype(BF), v)
        return m_new, l, acc

    init = (jnp.full((2 * tile, 1), NEG_INF, F32), jnp.zeros((2 * tile, 1), F32),
            jnp.zeros((2 * tile, LANE), F32))
    carry = lax.fori_loop(0, qi, functools.partial(step, masked=False), init)
    _, l, acc = step(qi, carry, True)
    o = acc / l
    lam = _diff_lambda(lamv_ref[...], lam_init)
    o_ref[...] = _diff_finish(o[:tile], o[tile:], lam, subln_ref[...], lam_init).astype(BF)


def _diff_prompt(slopes, lamv, subln, qd, kdb, vdb, *, batch, seq, lam_init):
    tile = min(ATT_TILE, seq)
    nq = seq // tile
    return pl.pallas_call(
        functools.partial(_diff_prompt_kernel, tile=tile, lam_init=lam_init),
        grid=(batch, H_B, nq),
        in_specs=[pl.BlockSpec(memory_space=pltpu.SMEM),
                  pl.BlockSpec(lamv.shape, lambda b, h, qi: (0, 0)),
                  pl.BlockSpec(subln.shape, lambda b, h, qi: (0, 0)),
                  pl.BlockSpec((tile, LANE), lambda b, h, qi: (b * nq + qi, h)),
                  pl.BlockSpec((seq, LANE), lambda b, h, qi: (b, h)),
                  pl.BlockSpec((seq, LANE), lambda b, h, qi: (b, h))],
        out_specs=pl.BlockSpec((tile, LANE), lambda b, h, qi: (b * nq + qi, h)),
        out_shape=jax.ShapeDtypeStruct((batch * seq, H_B * V_B), BF),
        compiler_params=_cparams("parallel", "parallel", "arbitrary"),
        name="diff_prompt",
    )(slopes, lamv, subln, qd, kdb, vdb)


def _even_sample_kernel(slopes_ref, lamv_ref, subln_ref, wk_ref, wv_ref, e_ref,
                        qa_ref, ka_ref, va_ref, ckv_ref, kr_ref,
                        qd_ref, kdb_ref, vdb_ref, ck_ref, cv_ref,
                        oa_ref, ob_ref, *, lam_init):
    n_new = qa_ref.shape[0]
    n_past = ckv_ref.shape[0]
    ckvp = ckv_ref[...].astype(BF)
    krp = kr_ref[...].astype(BF)
    lane = _lane_iota((n_new, LANE))
    for pr in range(H_A // 2):
        psl = slice(pr * LANE, (pr + 1) * LANE)
        vp = _dot(ckvp, wv_ref[:, psl]).astype(BF)
        vn = va_ref[:, psl]
        res = []
        for hh in range(2):
            sl = slice((2 * pr + hh) * HEAD_PAD, (2 * pr + hh + 1) * HEAD_PAD)
            q = qa_ref[:, sl]
            kp = (_dot(ckvp, wk_ref[:, sl]) + _dot(krp, e_ref[:, sl])).astype(BF)
            res.append(_softmax_pv([_dot_nt(q, kp), _dot_nt(q, ka_ref[:, sl])], [vp, vn]))
        oa_ref[:, psl] = jnp.where(lane < V_A, res[0], res[1]).astype(BF)
    rowp = lax.broadcasted_iota(jnp.int32, (n_new, n_past), 0)
    colp = lax.broadcasted_iota(jnp.int32, (n_new, n_past), 1)
    dist_p = (rowp - colp + n_past).astype(F32)
    dist_p = jnp.concatenate([dist_p, dist_p], axis=0)
    rown = lax.broadcasted_iota(jnp.int32, (n_new, n_new), 0)
    coln = lax.broadcasted_iota(jnp.int32, (n_new, n_new), 1)
    dist_n = jnp.abs(rown - coln).astype(F32)
    dist_n = jnp.concatenate([dist_n, dist_n], axis=0)
    lam = _diff_lambda(lamv_ref[...], lam_init)
    for h in range(H_B):
        sl = slice(h * LANE, (h + 1) * LANE)
        slope = slopes_ref[h]
        q2x = _split_halves(qd_ref[:, sl])
        kp = ck_ref[:, sl].astype(BF)
        vp = cv_ref[:, sl].astype(BF)
        s_p = _dot_nt(q2x, kp) - slope * dist_p
        s_n = _dot_nt(q2x, kdb_ref[:, sl]) - slope * dist_n
        o = _softmax_pv([s_p, s_n], [vp, vdb_ref[:, sl]])
        ob_ref[:, sl] = _diff_finish(o[:n_new], o[n_new:], lam, subln_ref[...], lam_init).astype(BF)


def _even_sample(slopes, lamv, subln, wk, wv, e_mat, qa, ka, va, ckv_c, kr_c, qd, kdb, vdb, ck_c, cv_c,
                 *, n_prompt_rows, dec_seq, lam_init):
    dec_batch, n_past = ckv_c.shape[0], ckv_c.shape[1]
    base = n_prompt_rows // dec_seq

    def full(a):
        return pl.BlockSpec(a.shape, lambda s: (0,) * a.ndim)

    def new(width):
        return pl.BlockSpec((dec_seq, width), lambda s: (base + s, 0))

    def cache(width):
        return pl.BlockSpec((None, n_past, width), lambda s: (s, 0, 0))

    return pl.pallas_call(
        functools.partial(_even_sample_kernel, lam_init=lam_init),
        grid=(dec_batch,),
        in_specs=[pl.BlockSpec(memory_space=pltpu.SMEM), full(lamv), full(subln), full(wk), full(wv), full(e_mat),
                  new(1024), new(1024), new(512), cache(KV_RANK), cache(ROPE),
                  new(512), new(512), new(512), cache(512), cache(512)],
        out_specs=[pl.BlockSpec((dec_seq, 512), lambda s: (s, 0))] * 2,
        out_shape=[jax.ShapeDtypeStruct((dec_batch * dec_seq, 512), BF)] * 2,
        compiler_params=_cparams("parallel"),
        name="even_sample",
    )(slopes, lamv, subln, wk, wv, e_mat, qa, ka, va, ckv_c, kr_c, qd, kdb, vdb, ck_c, cv_c)


def _route(logits, carry):
    tm = logits.shape[0]
    lane = _lane_iota(logits.shape).astype(F32)
    big = float(LANE)
    g_mask = lane < N_GROUPS
    gl = jnp.where(g_mask, logits, NEG_INF)
    gmax = jnp.max(gl, axis=-1, keepdims=True)
    g_sel = jnp.min(jnp.where(gl == gmax, lane, big), axis=-1, keepdims=True)
    p_grp = 1.0 / jnp.sum(jnp.exp(gl - gmax), axis=-1, keepdims=True)
    lo = ROUTE_OFF + EPG * g_sel
    el = jnp.where((lane >= lo) & (lane < lo + EPG), logits, NEG_INF)
    v1 = jnp.max(el, axis=-1, keepdims=True)
    i1 = jnp.min(jnp.where(el == v1, lane, big), axis=-1, keepdims=True)
    el2 = jnp.where(lane == i1, NEG_INF, el)
    v2 = jnp.max(el2, axis=-1, keepdims=True)
    i2 = jnp.min(jnp.where(el2 == v2, lane, big), axis=-1, keepdims=True)
    ex = jnp.exp(v2 - v1)
    den = 1.0 + ex
    gate1 = (1.0 / den) * p_grp
    gate2 = (ex / den) * p_grp
    onehot = jnp.where((lane == i1) | (lane == i2), 1.0, 0.0)
    row = lax.broadcasted_iota(jnp.int32, (tm, tm), 0)
    col = lax.broadcasted_iota(jnp.int32, (tm, tm), 1)
    tri = jnp.where(row > col, 1.0, 0.0).astype(BF)
    cum = _dot(tri, onehot.astype(BF)) + carry
    r1 = jnp.sum(jnp.where(lane == i1, cum, 0.0), axis=-1, keepdims=True)
    r2 = jnp.sum(jnp.where(lane == i2, cum, 0.0), axis=-1, keepdims=True)
    packed = jnp.zeros_like(logits)
    for pos, val in enumerate([i1 - ROUTE_OFF, i2 - ROUTE_OFF, gate1, gate2, r1, r2]):
        packed = jnp.where(lane == pos, val, packed)
    return packed, carry + jnp.sum(onehot, axis=0, keepdims=True)


def _out_router_kernel(*refs, n_mix, n_prompt_tiles, first):
    if first:
        xp_ref, xs_ref = refs[0], refs[1]
        refs = refs[2:]
    else:
        x_ref = refs[0]
        refs = refs[1:]
    mix_refs = refs[:n_mix]
    w_refs = refs[n_mix:2 * n_mix]
    g_ref, wr_ref, br_ref, x1_ref, xn_ref, route_ref, cnt_ref, carry_ref = refs[2 * n_mix:]
    i = pl.program_id(0)

    @pl.when(i == 0)
    def _():
        carry_ref[...] = jnp.zeros_like(carry_ref)

    if first:
        x1 = jnp.where(i < n_prompt_tiles, xp_ref[...], xs_ref[...])
    else:
        x1 = x_ref[...]
    for m_ref, w_ref in zip(mix_refs, w_refs):
        x1 = x1 + _dot(m_ref[...], w_ref[...])
    x1_ref[...] = x1
    xb = _rms(x1, g_ref[...], NORM_EPS).astype(BF)
    xn_ref[...] = xb
    logits = _dot(xb, wr_ref[...]) + br_ref[...]
    packed, carry = _route(logits, carry_ref[...])
    route_ref[...] = packed
    carry_ref[...] = carry
    cnt_ref[...] = carry


def _out_router(x_parts, mixes, ws, g, wr, br):
    first = len(x_parts) == 2
    if first:
        n_p, n_s = x_parts[0].shape[0] // ROW_TILE, x_parts[1].shape[0] // ROW_TILE
        T, D = x_parts[0].shape[0] + x_parts[1].shape[0], x_parts[0].shape[1]
        x_specs = [pl.BlockSpec((ROW_TILE, D), lambda i: (jnp.minimum(i, n_p - 1), 0)),
                   pl.BlockSpec((ROW_TILE, D), lambda i: (jnp.maximum(i - n_p, 0), 0))]
    else:
        T, D = x_parts[0].shape
        n_p = 0
        x_specs = [pl.BlockSpec((ROW_TILE, D), lambda i: (i, 0))]

    def full(a):
        return pl.BlockSpec(a.shape, lambda i: (0,) * a.ndim)

    def rows(width):
        return pl.BlockSpec((ROW_TILE, width), lambda i: (i, 0))

    return pl.pallas_call(
        functools.partial(_out_router_kernel, n_mix=len(mixes), n_prompt_tiles=n_p, first=first),
        grid=(T // ROW_TILE,),
        in_specs=x_specs + [rows(m.shape[1]) for m in mixes] + [full(w) for w in ws] + [full(g), full(wr), full(br)],
        out_specs=[rows(D), rows(D), rows(LANE), pl.BlockSpec((1, LANE), lambda i: (0, 0))],
        out_shape=[jax.ShapeDtypeStruct((T, D), F32), jax.ShapeDtypeStruct((T, D), BF),
                   jax.ShapeDtypeStruct((T, LANE), F32), jax.ShapeDtypeStruct((1, LANE), F32)],
        scratch_shapes=[pltpu.VMEM((1, LANE), F32)],
        compiler_params=_cparams("arbitrary"),
        name="out_router",
    )(*x_parts, *mixes, *ws, g, wr, br)


def _experts_kernel(be_ref, nu_ref, xb_ref, wg_ref, wu_ref, wd_ref, y_ref):
    i = pl.program_id(0)

    @pl.when(i < nu_ref[0])
    def _():
        xb = xb_ref[...]
        a = _dot(xb, wg_ref[...])
        b = _dot(xb, wu_ref[...])
        hid = (a * jax.nn.sigmoid(a)) * b
        y_ref[...] = _dot(hid.astype(BF), wd_ref[...])

    @pl.when(i >= nu_ref[0])
    def _():
        y_ref[...] = jnp.zeros_like(y_ref)


def _experts(block_expert, n_used, xb, wg, wu, wd):
    L, D = xb.shape
    n_blocks = L // MOE_BLOCK
    grid_spec = pltpu.PrefetchScalarGridSpec(
        num_scalar_prefetch=2,
        grid=(n_blocks,),
        in_specs=[pl.BlockSpec((MOE_BLOCK, D), lambda i, be, nu: (i, 0)),
                  pl.BlockSpec((None, D, D_EXPERT), lambda i, be, nu: (be[i], 0, 0)),
                  pl.BlockSpec((None, D, D_EXPERT), lambda i, be, nu: (be[i], 0, 0)),
                  pl.BlockSpec((None, D_EXPERT, D), lambda i, be, nu: (be[i], 0, 0))],
        out_specs=pl.BlockSpec((MOE_BLOCK, D), lambda i, be, nu: (i, 0)),
    )
    return pl.pallas_call(
        _experts_kernel,
        grid_spec=grid_spec,
        out_shape=jax.ShapeDtypeStruct((L, D), F32),
        compiler_params=_cparams("arbitrary"),
        name="experts",
    )(block_expert, n_used, xb, wg, wu, wd)


def _moe(xn, route, cnt, wg, wu, wd):
    T = xn.shape[0]
    e = route[:, 0:2].astype(jnp.int32)
    rank = route[:, 4:6].astype(jnp.int32)
    counts = cnt[0, ROUTE_OFF:ROUTE_OFF + N_EXPERTS].astype(jnp.int32)
    padded = ((counts + MOE_BLOCK - 1) // MOE_BLOCK) * MOE_BLOCK
    pend = jnp.cumsum(padded)
    pstart = pend - padded
    dest = pstart[e] + rank
    n_blocks = -(-(2 * T) // MOE_BLOCK) + N_EXPERTS
    L = n_blocks * MOE_BLOCK
    tok = jnp.repeat(jnp.arange(T, dtype=jnp.int32), 2)
    buf_tok = jnp.zeros((L,), jnp.int32).at[dest.reshape(-1)].set(tok)
    block_expert = jnp.minimum(
        jnp.searchsorted(pend, jnp.arange(n_blocks, dtype=jnp.int32) * MOE_BLOCK, side="right"),
        N_EXPERTS - 1).astype(jnp.int32)
    n_used = (pend[-1:] // MOE_BLOCK).astype(jnp.int32)
    xb = jnp.take(xn, buf_tok, axis=0)
    yb = _experts(block_expert, n_used, xb, wg, wu, wd)
    return jnp.take(yb, dest[:, 0], axis=0), jnp.take(yb, dest[:, 1], axis=0)


def _gates(route):
    lane = _lane_iota(route.shape)
    g0 = jnp.sum(jnp.where(lane == 2, route, 0.0), axis=-1, keepdims=True)
    g1 = jnp.sum(jnp.where(lane == 3, route, 0.0), axis=-1, keepdims=True)
    return g0, g1


def _odd_in_kernel(x_ref, y0_ref, y1_ref, route_ref, g_ref, w_ref, x2_ref, q_ref, k_ref, v_ref, *, scale):
    g0, g1 = _gates(route_ref[...])
    x2 = x_ref[...] + (y0_ref[...] * g0 + y1_ref[...] * g1)
    x2_ref[...] = x2
    h = _rms(x2, g_ref[...], NORM_EPS).astype(BF)
    y = _dot(h, w_ref[...])
    n = q_ref.shape[1]
    q_ref[...] = (y[:, :n] * scale).astype(BF)
    k_ref[...] = y[:, n:2 * n].astype(BF)
    v_ref[...] = y[:, 2 * n:].astype(BF)


def _odd_in(x1, y0, y1, route, g, w):
    T, D = x1.shape
    n = w.shape[1] // 3

    def rows(width):
        return pl.BlockSpec((ROW_TILE, width), lambda i: (i, 0))

    def full(a):
        return pl.BlockSpec(a.shape, lambda i: (0,) * a.ndim)

    return pl.pallas_call(
        functools.partial(_odd_in_kernel, scale=DH_C ** -0.5),
        grid=(T // ROW_TILE,),
        in_specs=[rows(D), rows(D), rows(D), rows(LANE), full(g), full(w)],
        out_specs=[rows(D), rows(n), rows(n), rows(n)],
        out_shape=[jax.ShapeDtypeStruct((T, D), F32)] + [jax.ShapeDtypeStruct((T, n), BF)] * 3,
        compiler_params=_cparams("parallel"),
        name="odd_in",
    )(x1, y0, y1, route, g, w)


def _state_rows_kernel(ids_ref, x_ref, g_ref, w_ref, o_ref):
    del ids_ref
    h = _rms(x_ref[...], g_ref[...], NORM_EPS).astype(BF)
    o_ref[...] = _dot(h, w_ref[...])


def _state_rows(tile_ids, x, g, w):
    D = x.shape[1]
    n = tile_ids.shape[0]
    grid_spec = pltpu.PrefetchScalarGridSpec(
        num_scalar_prefetch=1,
        grid=(n,),
        in_specs=[pl.BlockSpec((ROW_TILE, D), lambda i, ids: (ids[i], 0)),
                  pl.BlockSpec(g.shape, lambda i, ids: (0, 0)),
                  pl.BlockSpec(w.shape, lambda i, ids: (0, 0))],
        out_specs=pl.BlockSpec((ROW_TILE, w.shape[1]), lambda i, ids: (i, 0)),
    )
    return pl.pallas_call(
        _state_rows_kernel,
        grid_spec=grid_spec,
        out_shape=jax.ShapeDtypeStruct((n * ROW_TILE, w.shape[1]), F32),
        compiler_params=_cparams("parallel"),
        name="state_rows",
    )(tile_ids, x, g, w)


def _band_prompt_kernel(q_ref, k_ref, v_ref, bias_ref, o_ref, *, tile):
    qi = pl.program_id(2)
    q2x = _split_halves(q_ref[...])
    s_list, v_list = [], []
    for t in range(3):
        start = (qi + t - 2) * tile
        valid = start >= 0
        start = pl.multiple_of(jnp.maximum(start, 0), tile)
        s = _dot_nt(q2x, k_ref[pl.ds(start, tile), :]) + bias_ref[:, t * tile:(t + 1) * tile]
        if t < 2:
            s = jnp.where(valid, s, NEG_INF)
        s_list.append(s)
        v_list.append(v_ref[pl.ds(start, tile), :])
    o = _softmax_pv(s_list, v_list)
    lane = _lane_iota((tile, LANE))
    o_ref[...] = jnp.where(lane < DH_C, o[:tile], o[tile:]).astype(BF)


def _band_prompt(q, k, v, bias, *, batch, seq):
    tile = BAND_TILE
    nq = seq // tile
    return pl.pallas_call(
        functools.partial(_band_prompt_kernel, tile=tile),
        grid=(H_C // 2, batch, nq),
        in_specs=[pl.BlockSpec((tile, LANE), lambda p, b, qi: (b * nq + qi, p)),
                  pl.BlockSpec((seq, LANE), lambda p, b, qi: (b, p)),
                  pl.BlockSpec((seq, LANE), lambda p, b, qi: (b, p)),
                  pl.BlockSpec((None, 2 * tile, 3 * tile), lambda p, b, qi: (p, 0, 0))],
        out_specs=pl.BlockSpec((tile, LANE), lambda p, b, qi: (b * nq + qi, p)),
        out_shape=jax.ShapeDtypeStruct((batch * seq, H_C * DH_C), BF),
        compiler_params=_cparams("parallel", "parallel", "arbitrary"),
        name="band_prompt",
    )(q, k, v, bias)


def _band_sample_kernel(q_ref, k_ref, v_ref, ck_ref, cv_ref, bias_ref, o_ref):
    n_new = q_ref.shape[0]
    n_past = ck_ref.shape[0]
    lane = _lane_iota((n_new, LANE))
    for pr in range(H_C // 2):
        sl = slice(pr * LANE, (pr + 1) * LANE)
        q2x = _split_halves(q_ref[:, sl])
        s_p = _dot_nt(q2x, ck_ref[:, sl].astype(BF)) + bias_ref[pr, :, 0:n_past]
        s_n = _dot_nt(q2x, k_ref[:, sl]) + bias_ref[pr, :, n_past:n_past + n_new]
        o = _softmax_pv([s_p, s_n], [cv_ref[:, sl].astype(BF), v_ref[:, sl]])
        o_ref[:, sl] = jnp.where(lane < DH_C, o[:n_new], o[n_new:]).astype(BF)


def _band_sample(q, k, v, ck, cv, bias, *, n_prompt_rows, dec_seq):
    dec_batch, n_past, width = ck.shape
    base = n_prompt_rows // dec_seq
    new = pl.BlockSpec((dec_seq, width), lambda s: (base + s, 0))
    cache = pl.BlockSpec((None, n_past, width), lambda s: (s, 0, 0))
    return pl.pallas_call(
        _band_sample_kernel,
        grid=(dec_batch,),
        in_specs=[new, new, new, cache, cache, pl.BlockSpec(bias.shape, lambda s: (0, 0, 0))],
        out_specs=pl.BlockSpec((dec_seq, width), lambda s: (s, 0)),
        out_shape=jax.ShapeDtypeStruct((dec_batch * dec_seq, width), BF),
        compiler_params=_cparams("parallel"),
        name="band_sample",
    )(q, k, v, ck, cv, bias)


def _final_kernel(x_ref, y0_ref, y1_ref, route_ref, g_ref, o_ref):
    g0, g1 = _gates(route_ref[...])
    x = x_ref[...] + (y0_ref[...] * g0 + y1_ref[...] * g1)
    o_ref[...] = _rms(x, g_ref[...], NORM_EPS)


def _final(x, y0, y1, route, g, *, first_tile, n_tiles):
    D = x.shape[1]

    def rows(width):
        return pl.BlockSpec((ROW_TILE, width), lambda i: (first_tile + i, 0))

    return pl.pallas_call(
        _final_kernel,
        grid=(n_tiles,),
        in_specs=[rows(D), rows(D), rows(D), rows(LANE), pl.BlockSpec(g.shape, lambda i: (0, 0))],
        out_specs=pl.BlockSpec((ROW_TILE, D), lambda i: (i, 0)),
        out_shape=jax.ShapeDtypeStruct((n_tiles * ROW_TILE, D), F32),
        compiler_params=_cparams("parallel"),
        name="final_norm",
    )(x, y0, y1, route, g)


def _prep_even_weights(w_in, w_qup, w_kvup):
    D = w_in.shape[0]
    a_in = Q_RANK + KV_RANK + ROPE
    bq = H_B * 2 * DH_B
    wcq, wckv, wkr = w_in[:, :Q_RANK], w_in[:, Q_RANK:Q_RANK + KV_RANK], w_in[:, Q_RANK + KV_RANK:a_in]
    wqd, wkd, wvd = w_in[:, a_in:a_in + bq], w_in[:, a_in + bq:a_in + 2 * bq], w_in[:, a_in + 2 * bq:]
    half = ROPE // 2

    def z(n):
        return jnp.zeros((D, n), w_in.dtype)

    kr_m = jnp.concatenate([z(NOPE), wkr, z(HEAD_PAD - NOPE - ROPE)], axis=1)
    kr_s = jnp.concatenate([z(NOPE), wkr[:, half:], wkr[:, :half], z(HEAD_PAD - NOPE - ROPE)], axis=1)
    w0 = jnp.concatenate([wcq, wckv, kr_m, kr_s, wqd, wkd, wvd], axis=1).astype(BF)
    wq3 = w_qup.reshape(Q_RANK, H_A, NOPE + ROPE)
    nope, r1, r2 = wq3[:, :, :NOPE], wq3[:, :, NOPE:NOPE + half], wq3[:, :, NOPE + half:]
    zq = jnp.zeros((Q_RANK, H_A, HEAD_PAD - NOPE - ROPE), w_qup.dtype)
    wq = jnp.concatenate([nope, r1, r2, zq], axis=-1).reshape(Q_RANK, H_A * HEAD_PAD).astype(BF)
    wqs = jnp.concatenate([jnp.zeros_like(nope), r2, r1, zq], axis=-1).reshape(Q_RANK, H_A * HEAD_PAD).astype(BF)
    wkv3 = w_kvup.reshape(KV_RANK, H_A, NOPE + V_A)
    wk = jnp.concatenate([wkv3[:, :, :NOPE], jnp.zeros((KV_RANK, H_A, HEAD_PAD - NOPE), w_kvup.dtype)],
                         axis=-1).reshape(KV_RANK, H_A * HEAD_PAD).astype(BF)
    wv = wkv3[:, :, NOPE:].reshape(KV_RANK, H_A * V_A).astype(BF)
    r = jnp.arange(ROPE)
    e_mat = jnp.zeros((ROPE, H_A, HEAD_PAD), F32).at[r[:, None], jnp.arange(H_A)[None, :], NOPE + r[:, None]].set(1.0)
    return w0, wq, wqs, wk, wv, e_mat.reshape(ROPE, H_A * HEAD_PAD).astype(BF)


def _rope_tables(pos):
    half = ROPE // 2
    inv = jnp.power(ROPE_BASE, -jnp.arange(half, dtype=F32) / half)
    ang = pos.astype(F32)[:, None] * inv[None, :]
    c, s = jnp.cos(ang), jnp.sin(ang)
    n = pos.shape[0]
    pad = jnp.zeros((n, HEAD_PAD - NOPE - ROPE), F32)
    cs = jnp.concatenate([jnp.ones((n, NOPE), F32), c, c, pad], axis=1)
    sn = jnp.concatenate([jnp.zeros((n, NOPE), F32), -s, s, pad], axis=1)
    return cs, sn


def _band_bias_tiles(table, n_rows, n_keys, key_offset, masked):
    i = jnp.arange(n_rows)[:, None]
    kpos = jnp.arange(n_keys)[None, :] - key_offset
    idx = jnp.clip(i - kpos, -REL_CLIP, REL_CLIP) + REL_CLIP
    bias = table.astype(F32)[:, idx]
    if masked:
        qc, kc = i // CHUNK, jnp.floor_divide(kpos, CHUNK)
        vis = (kc <= qc) & (kc >= qc - LEFT_CHUNKS)
        bias = jnp.where(vis[None], bias, NEG_INF)
    return bias.reshape(H_C // 2, 2 * n_rows, n_keys)


def kernel(x_prompt, x_sample, cache_mla_ckv, cache_mla_krope, cache_diff_k, cache_diff_v, cache_band_k, cache_band_v, ln_mix, w_in_even, mla_q_norm, mla_w_qup, mla_kv_norm, mla_w_kvup, diff_lam_q1, diff_lam_k1, diff_lam_q2, diff_lam_k2, diff_subln, w_out_even, w_in_odd, band_rel_bias, w_out_odd, ln_ffn, moe_w_group, moe_b_group, moe_w_router, moe_b_router, moe_w_gate, moe_w_up, moe_w_down, ln_final):
    B, S, D = x_prompt.shape
    DB, DS, _ = x_sample.shape
    n_past = cache_mla_ckv.shape[2]
    c_past = cache_band_k.shape[2]
    assert ln_mix.shape[0] == 2 and S % min(ATT_TILE, S) == 0 and S % BAND_TILE == 0
    assert (B * S) % ROW_TILE == 0 and (DB * DS) % ROW_TILE == 0 and ROW_TILE % DS == 0 and DS == CHUNK
    assert n_past % CHUNK == 0 and c_past == LEFT_CHUNKS * CHUNK and S >= c_past and c_past % ROW_TILE == 0
    BS, NS = B * S, DB * DS
    T = BS + NS
    n_p, n_s = BS // ROW_TILE, NS // ROW_TILE
    xp, xs = x_prompt.reshape(BS, D), x_sample.reshape(NS, D)
    row = lambda a: a.reshape(1, -1)

    w0, wq, wqs, wk, wv, e_mat = _prep_even_weights(w_in_even[0], mla_w_qup[0], mla_w_kvup[0])
    cs_p, sn_p = _rope_tables(jnp.arange(S))
    cs_s, sn_s = _rope_tables(n_past + jnp.arange(DS))
    reps = ROW_TILE // DS
    cs_tab = jnp.concatenate([cs_p, jnp.tile(cs_s, (reps, 1))], axis=0)
    sn_tab = jnp.concatenate([sn_p, jnp.tile(sn_s, (reps, 1))], axis=0)
    qa, ka, va, ckv, kr, qd, kdb, vdb, kd, vd = _even_in(
        xp, xs, row(ln_mix[0]), w0, row(mla_q_norm[0]), wq, wqs, row(mla_kv_norm[0]), wk, wv,
        cs_tab, sn_tab, seq=S)
    slopes = jnp.exp2(-8.0 * jnp.arange(1, H_B + 1, dtype=F32) / H_B)
    lamv = jnp.stack([diff_lam_q1[0], diff_lam_k1[0], diff_lam_q2[0], diff_lam_k2[0]]).astype(F32)
    subln = row(diff_subln[0])
    lam_init = 0.8 - 0.6 * math.exp(-0.3 * 0)
    oa_p = _mla_prompt(qa, ka, va, batch=B, seq=S)
    ob_p = _diff_prompt(slopes, lamv, subln, qd, kdb, vdb, batch=B, seq=S, lam_init=lam_init)
    oa_s, ob_s = _even_sample(
        slopes, lamv, subln, wk, wv, e_mat, qa, ka, va, cache_mla_ckv[0], cache_mla_krope[0],
        qd, kdb, vdb, cache_diff_k[0].reshape(DB, n_past, -1), cache_diff_v[0].reshape(DB, n_past, -1),
        n_prompt_rows=BS, dec_seq=DS, lam_init=lam_init)
    oa = jnp.concatenate([oa_p, oa_s], axis=0)
    ob = jnp.concatenate([ob_p, ob_s], axis=0)

    def router_weights(l):
        wr = jnp.concatenate([moe_w_group[l], moe_w_router[l],
                              jnp.zeros((D, LANE - N_GROUPS - N_EXPERTS), F32)], axis=1).astype(BF)
        br = jnp.concatenate([moe_b_group[l], moe_b_router[l],
                              jnp.zeros((LANE - N_GROUPS - N_EXPERTS,), F32)]).astype(F32)
        return wr, row(br)

    n_a = H_A * V_A
    wo = w_out_even[0].astype(BF)
    wr, br = router_weights(0)
    x1, xn, route, cnt = _out_router([xp, xs], [oa, ob], [wo[:n_a], wo[n_a:]], row(ln_ffn[0]), wr, br)
    y0, y1 = _moe(xn, route, cnt, moe_w_gate[0].astype(BF), moe_w_up[0].astype(BF), moe_w_down[0].astype(BF))

    w_odd = w_in_odd[0].astype(BF)
    n_c = H_C * DH_C
    x2, qc, kc, vc = _odd_in(x1, y0, y1, route, row(ln_mix[1]), w_odd)
    tail = c_past // ROW_TILE
    tiles_per_seq = S // ROW_TILE
    tile_ids = jnp.concatenate(
        [jnp.arange(tiles_per_seq - tail, tiles_per_seq, dtype=jnp.int32) + b * tiles_per_seq for b in range(B)]
        + [jnp.arange(n_p, n_p + n_s, dtype=jnp.int32)])
    st = _state_rows(tile_ids, x2, row(ln_mix[1]), w_odd[:, n_c:])
    bias_p = _band_bias_tiles(band_rel_bias[0], BAND_TILE, 3 * BAND_TILE, 2 * BAND_TILE, True)
    bias_s = _band_bias_tiles(band_rel_bias[0], DS, c_past + DS, c_past, False)
    oc_p = _band_prompt(qc, kc, vc, bias_p, batch=B, seq=S)
    oc_s = _band_sample(qc, kc, vc, cache_band_k[0].reshape(DB, c_past, n_c), cache_band_v[0].reshape(DB, c_past, n_c),
                        bias_s, n_prompt_rows=BS, dec_seq=DS)
    oc = jnp.concatenate([oc_p, oc_s], axis=0)
    wr, br = router_weights(1)
    x3, xn, route, cnt = _out_router([x2], [oc], [w_out_odd[0].astype(BF)], row(ln_ffn[1]), wr, br)
    y0, y1 = _moe(xn, route, cnt, moe_w_gate[1].astype(BF), moe_w_up[1].astype(BF), moe_w_down[1].astype(BF))
    g_fin = row(ln_final)
    y_prompt = _final(x3, y0, y1, route, g_fin, first_tile=0, n_tiles=n_p).reshape(B, S, D)
    y_sample = _final(x3, y0, y1, route, g_fin, first_tile=n_p, n_tiles=n_s).reshape(DB, DS, D)

    def split(a, *tail_shape):
        return a[:BS].reshape(1, B, S, *tail_shape), a[BS:].reshape(1, DB, DS, *tail_shape)

    ckv_p, ckv_s = split(ckv, KV_RANK)
    kr_p, kr_s = split(kr, ROPE)
    kd_p, kd_s = split(kd, H_B, 2 * DH_B)
    vd_p, vd_s = split(vd, H_B, V_B)
    n_tail = B * c_past
    bk_p = st[:n_tail, :n_c].reshape(1, B, c_past, H_C, DH_C)
    bv_p = st[:n_tail, n_c:].reshape(1, B, c_past, H_C, DH_C)
    k_new = st[n_tail:, :n_c].reshape(DB, DS, H_C, DH_C)
    v_new = st[n_tail:, n_c:].reshape(DB, DS, H_C, DH_C)
    bk_s = jnp.concatenate([cache_band_k[0][:, DS:], k_new], axis=1)[None]
    bv_s = jnp.concatenate([cache_band_v[0][:, DS:], v_new], axis=1)[None]
    return (y_prompt, y_sample, ckv_p, kr_p, kd_p, vd_p, bk_p, bv_p, ckv_s, kr_s, kd_s, vd_s, bk_s, bv_s)
```

```python
import functools
import math

import jax
import jax.numpy as jnp
from jax import lax
from jax.experimental import pallas as pl
from jax.experimental.pallas import tpu as pltpu

BF = jnp.bfloat16
F32 = jnp.float32
NEG_INF = float("-inf")
LOG2E = math.log2(math.e)

CHUNK = 64
NORM_EPS = 1e-6
SUBLN_EPS = 1e-5
H_A, NOPE, ROPE, V_A, Q_RANK, KV_RANK = 8, 64, 32, 64, 384, 256
ROPE_BASE = 10000.0
H_B, DH_B, V_B = 4, 64, 128
H_C, DH_C, LEFT_CHUNKS, REL_CLIP = 16, 64, 8, 128
N_GROUPS, EPG, N_EXPERTS, D_EXPERT = 4, 8, 32, 512
LANE = 128
HEAD_PAD = 128
ROUTE_OFF = N_GROUPS

ROW_TILE = 256
ATT_TILE = 512
SOFTMAX_ROWS = 32
BAND_TILE = 256
BAND_TILES_PER_STEP = 4
MOE_BLOCK = 256
VMEM_LIMIT = 56 * 1024 * 1024


def _cparams(*sem):
    return pltpu.CompilerParams(dimension_semantics=sem, vmem_limit_bytes=VMEM_LIMIT)


def _rms(x, g, eps):
    return x * lax.rsqrt(jnp.mean(x * x, axis=-1, keepdims=True) + eps) * g


def _dot(a, b):
    return jnp.dot(a, b, preferred_element_type=F32)


def _dot_nt(a, b):
    return lax.dot_general(a, b, (((1,), (1,)), ((), ())), preferred_element_type=F32)


def _lane_iota(shape):
    return lax.broadcasted_iota(jnp.int32, shape, len(shape) - 1)


def _split_halves(q):
    qf = q.astype(F32)
    lane = _lane_iota(qf.shape)
    return jnp.concatenate([jnp.where(lane < 64, qf, 0.0), jnp.where(lane >= 64, qf, 0.0)], axis=0).astype(BF)


def _softmax_pv(s_list, v_list):
    m = functools.reduce(jnp.maximum, [jnp.max(s, axis=-1, keepdims=True) for s in s_list])
    acc, l = None, None
    for s, v in zip(s_list, v_list):
        p = jnp.exp2(s - m)
        ls = jnp.sum(p, axis=-1, keepdims=True)
        a = _dot(p.astype(BF), v)
        l = ls if l is None else l + ls
        acc = a if acc is None else acc + a
    return acc / l


def _diff_lambda(lamv, lam_init):
    a = jnp.exp(jnp.sum(lamv[0:1] * lamv[1:2], axis=-1, keepdims=True))
    b = jnp.exp(jnp.sum(lamv[2:3] * lamv[3:4], axis=-1, keepdims=True))
    return a - b + lam_init


def _diff_finish(o1, o2, lam, subln, lam_init):
    o = o1 - lam * o2
    return _rms(o, subln, SUBLN_EPS) * (1.0 - lam_init)


def _even_in_kernel(xp_ref, xs_ref, g_ref, w0_ref, qn_ref, wq_ref, wqs_ref, kvn_ref, wk_ref, wv_ref,
                    cs_ref, sn_ref,
                    qa_ref, ka_ref, va_ref, ckv_ref, kr_ref, qd_ref, kdb_ref, vdb_ref, kd_ref, vd_ref,
                    *, n_prompt_tiles, a_scale, b_scale):
    i = pl.program_id(0)
    x = jnp.where(i < n_prompt_tiles, xp_ref[...], xs_ref[...])
    h = _rms(x, g_ref[...], NORM_EPS).astype(BF)
    y = _dot(h, w0_ref[...])
    cq, ckv = y[:, 0:384], y[:, 384:640]
    kr_m, kr_s = y[:, 640:768], y[:, 768:896]
    qd, kd, vd = y[:, 896:1408], y[:, 1408:1920], y[:, 1920:2432]
    cs, sn = cs_ref[...], sn_ref[...]
    cqn = _rms(cq, qn_ref[...], NORM_EPS).astype(BF)
    qm = _dot(cqn, wq_ref[...])
    qs = _dot(cqn, wqs_ref[...])
    ckvn = _rms(ckv, kvn_ref[...], NORM_EPS)
    ckv_ref[...] = ckvn
    krp = kr_m * cs + kr_s * sn
    kr_ref[...] = krp[:, NOPE:NOPE + ROPE]
    cb = ckvn.astype(BF)
    kn = _dot(cb, wk_ref[...])
    for hh in range(H_A):
        sl = slice(hh * HEAD_PAD, (hh + 1) * HEAD_PAD)
        qa_ref[:, sl] = ((qm[:, sl] * cs + qs[:, sl] * sn) * a_scale).astype(BF)
        ka_ref[:, sl] = (kn[:, sl] + krp).astype(BF)
    va_ref[...] = _dot(cb, wv_ref[...]).astype(BF)
    qd_ref[...] = (qd * b_scale).astype(BF)
    kd_ref[...] = kd
    vd_ref[...] = vd
    kdb_ref[...] = kd.astype(BF)
    vdb_ref[...] = vd.astype(BF)


def _even_in(xp, xs, g, w0, qn, wq, wqs, kvn, wk, wv, cs_tab, sn_tab, *, seq):
    n_p, n_s = xp.shape[0] // ROW_TILE, xs.shape[0] // ROW_TILE
    T = xp.shape[0] + xs.shape[0]
    D = xp.shape[1]
    pos_blocks = seq // ROW_TILE

    def full(a):
        return pl.BlockSpec(a.shape, lambda i: (0,) * a.ndim)

    def rows(width):
        return pl.BlockSpec((ROW_TILE, width), lambda i: (i, 0))

    pos_spec = pl.BlockSpec((ROW_TILE, LANE), lambda i: (jnp.where(i < n_p, i % pos_blocks, pos_blocks), 0))
    widths = [(1024, BF), (1024, BF), (512, BF), (KV_RANK, F32), (ROPE, F32),
              (512, BF), (512, BF), (512, BF), (512, F32), (512, F32)]
    return pl.pallas_call(
        functools.partial(_even_in_kernel, n_prompt_tiles=n_p,
                          a_scale=(NOPE + ROPE) ** -0.5 * LOG2E, b_scale=DH_B ** -0.5 * LOG2E),
        grid=(n_p + n_s,),
        in_specs=[pl.BlockSpec((ROW_TILE, D), lambda i: (jnp.minimum(i, n_p - 1), 0)),
                  pl.BlockSpec((ROW_TILE, D), lambda i: (jnp.maximum(i - n_p, 0), 0)),
                  full(g), full(w0), full(qn), full(wq), full(wqs), full(kvn), full(wk), full(wv),
                  pos_spec, pos_spec],
        out_specs=[rows(w) for w, _ in widths],
        out_shape=[jax.ShapeDtypeStruct((T, w), dt) for w, dt in widths],
        compiler_params=_cparams("parallel"),
        name="even_in",
    )(xp, xs, g, w0, qn, wq, wqs, kvn, wk, wv, cs_tab, sn_tab)


def _chunk_causal_mask(tq, tk):
    row = lax.broadcasted_iota(jnp.int32, (tq, tk), 0)
    col = lax.broadcasted_iota(jnp.int32, (tq, tk), 1)
    return (col // CHUNK) <= (row // CHUNK)


def _softmax_tile(s_ref, p_ref, m_ref, l_ref, a_ref, *, add_ref=None, sub_ref=None, off=None):
    tile = s_ref.shape[0]
    reps = tile // LANE
    for rb in range(tile // SOFTMAX_ROWS):
        rs = slice(rb * SOFTMAX_ROWS, (rb + 1) * SOFTMAX_ROWS)
        s = s_ref[rs, :]
        if add_ref is not None:
            s = s + add_ref[rs, :]
        if sub_ref is not None:
            s = s - sub_ref[rs, :]
        m_old = m_ref[rs, :]
        red = jnp.broadcast_to(jnp.max(s, axis=-1, keepdims=True), m_old.shape)
        if off is not None:
            red = red - off
        m_new = jnp.maximum(m_old, red)
        shift = m_new if off is None else m_new + off
        p = jnp.exp2(s - jnp.concatenate([shift] * reps, axis=1))
        alpha = jnp.exp2(m_old - m_new)
        l_ref[rs, :] = alpha * l_ref[rs, :] + jnp.broadcast_to(jnp.sum(p, axis=-1, keepdims=True), m_old.shape)
        m_ref[rs, :] = m_new
        a_ref[rs, :] = alpha
        p_ref[rs, :] = p.astype(BF)


def _flash_causal(qi, n_chains, bufs, scores, update):
    s0, s1, p0, p1, m_ref, l_ref, a_ref, acc_ref = bufs
    m_ref[...] = jnp.full(m_ref.shape, NEG_INF, F32)
    l_ref[...] = jnp.zeros(l_ref.shape, F32)
    acc_ref[...] = jnp.zeros(acc_ref.shape, F32)
    scores(0, s0)

    def pair(jj, carry):
        j = 2 * jj
        scores(j + 1, s1)
        update(j, s0, p0, False)
        scores(j + 2, s0)
        update(j + 1, s1, p1, False)
        return carry

    lax.fori_loop(0, qi // 2, pair, 0)

    @pl.when(qi % 2 == 1)
    def _():
        scores(qi, s1)
        update(qi - 1, s0, p0, False)
        s0[...] = s1[...]

    update(qi, s0, p0, True)


def _flash_scratch(n_chains, tile, acc_width=LANE):
    s = pltpu.VMEM((n_chains, tile, tile), F32)
    p = pltpu.VMEM((n_chains, tile, tile), BF)
    stat = pltpu.VMEM((n_chains, tile, LANE), F32)
    return [s, s, p, p, stat, stat, stat, pltpu.VMEM((n_chains, tile, acc_width), F32)]


def _mla_prompt_kernel(q_ref, k_ref, v_ref, o_ref, s0, s1, p0, p1, m_ref, l_ref, a_ref, acc_ref, dmask_ref,
                       *, tile):
    qi = pl.program_id(2)
    dmask_ref[...] = jnp.where(_chunk_causal_mask(tile, tile), 0.0, NEG_INF)
    sls = [slice(hh * HEAD_PAD, (hh + 1) * HEAD_PAD) for hh in range(2)]

    def scores(j, s_buf):
        start = pl.multiple_of(j * tile, tile)
        for c, sl in enumerate(sls):
            s_buf[c] = _dot_nt(q_ref[:, sl], k_ref[pl.ds(start, tile), sl])

    def update(j, s_buf, p_buf, diag):
        start = pl.multiple_of(j * tile, tile)
        for c in range(2):
            _softmax_tile(s_buf.at[c], p_buf.at[c], m_ref.at[c], l_ref.at[c], a_ref.at[c],
                          add_ref=dmask_ref if diag else None)
            acc_ref[c] = a_ref[c] * acc_ref[c] + _dot(p_buf[c], v_ref[pl.ds(start, tile), :])

    _flash_causal(qi, 2, (s0, s1, p0, p1, m_ref, l_ref, a_ref, acc_ref), scores, update)
    lane = _lane_iota((tile, LANE))
    o_ref[...] = jnp.where(lane < V_A, acc_ref[0] / l_ref[0], acc_ref[1] / l_ref[1]).astype(BF)


def _mla_prompt(qa, ka, va, *, batch, seq):
    tile = min(ATT_TILE, seq)
    nq = seq // tile
    return pl.pallas_call(
        functools.partial(_mla_prompt_kernel, tile=tile),
        grid=(batch, H_A // 2, nq),
        in_specs=[pl.BlockSpec((tile, 2 * HEAD_PAD), lambda b, p, qi: (b * nq + qi, p)),
                  pl.BlockSpec((seq, 2 * HEAD_PAD), lambda b, p, qi: (b, p)),
                  pl.BlockSpec((seq, LANE), lambda b, p, qi: (b, p))],
        out_specs=pl.BlockSpec((tile, LANE), lambda b, p, qi: (b * nq + qi, p)),
        out_shape=jax.ShapeDtypeStruct((batch * seq, H_A * V_A), BF),
        scratch_shapes=_flash_scratch(2, tile) + [pltpu.VMEM((tile, tile), F32)],
        compiler_params=_cparams("parallel", "parallel", "arbitrary"),
        name="mla_prompt",
    )(qa, ka, va)


def _diff_prompt_kernel(slopes_ref, lamv_ref, subln_ref, q_ref, k_ref, v_ref, o_ref,
                        s0, s1, p0, p1, m_ref, l_ref, a_ref, acc_ref, pen_ref, dbias_ref, q2_ref,
                        *, tile, lam_init):
    h = pl.program_id(1)
    qi = pl.program_id(2)
    slope = slopes_ref[h]
    q2_ref[...] = _split_halves(q_ref[...])
    row = lax.broadcasted_iota(jnp.int32, (tile, tile), 0)
    col = lax.broadcasted_iota(jnp.int32, (tile, tile), 1)
    pen = slope * (row - col).astype(F32)
    pen_ref[...] = pen
    dbias_ref[...] = jnp.where(_chunk_causal_mask(tile, tile), -jnp.abs(pen), NEG_INF)

    def scores(j, s_buf):
        k = k_ref[pl.ds(pl.multiple_of(j * tile, tile), tile), :]
        for c in range(2):
            s_buf[c] = _dot_nt(q2_ref[c * tile:(c + 1) * tile, :], k)

    def update(j, s_buf, p_buf, diag):
        start = pl.multiple_of(j * tile, tile)
        off = None if diag else slope * ((qi - j) * tile).astype(F32)
        for c in range(2):
            _softmax_tile(s_buf.at[c], p_buf.at[c], m_ref.at[c], l_ref.at[c], a_ref.at[c],
                          add_ref=dbias_ref if diag else None, sub_ref=None if diag else pen_ref, off=off)
            acc_ref[c] = a_ref[c] * acc_ref[c] + _dot(p_buf[c], v_ref[pl.ds(start, tile), :])

    _flash_causal(qi, 2, (s0, s1, p0, p1, m_ref, l_ref, a_ref, acc_ref), scores, update)
    lam = _diff_lambda(lamv_ref[...], lam_init)
    o_ref[...] = _diff_finish(acc_ref[0] / l_ref[0], acc_ref[1] / l_ref[1], lam, subln_ref[...],
                              lam_init).astype(BF)


def _diff_prompt(slopes, lamv, subln, qd, kdb, vdb, *, batch, seq, lam_init):
    tile = min(ATT_TILE, seq)
    nq = seq // tile
    return pl.pallas_call(
        functools.partial(_diff_prompt_kernel, tile=tile, lam_init=lam_init),
        grid=(batch, H_B, nq),
        in_specs=[pl.BlockSpec(memory_space=pltpu.SMEM),
                  pl.BlockSpec(lamv.shape, lambda b, h, qi: (0, 0)),
                  pl.BlockSpec(subln.shape, lambda b, h, qi: (0, 0)),
                  pl.BlockSpec((tile, LANE), lambda b, h, qi: (b * nq + qi, h)),
                  pl.BlockSpec((seq, LANE), lambda b, h, qi: (b, h)),
                  pl.BlockSpec((seq, LANE), lambda b, h, qi: (b, h))],
        out_specs=pl.BlockSpec((tile, LANE), lambda b, h, qi: (b * nq + qi, h)),
        out_shape=jax.ShapeDtypeStruct((batch * seq, H_B * V_B), BF),
        scratch_shapes=_flash_scratch(2, tile) + [pltpu.VMEM((tile, tile), F32), pltpu.VMEM((tile, tile), F32),
                                                  pltpu.VMEM((2 * tile, LANE), BF)],
        compiler_params=_cparams("parallel", "parallel", "arbitrary"),
        name="diff_prompt",
    )(slopes, lamv, subln, qd, kdb, vdb)


def _even_sample_kernel(slopes_ref, lamv_ref, subln_ref, wk_ref, wv_ref, e_ref,
                        qa_ref, ka_ref, va_ref, ckv_ref, kr_ref,
                        qd_ref, kdb_ref, vdb_ref, ck_ref, cv_ref,
                        oa_ref, ob_ref, *, lam_init):
    n_new = qa_ref.shape[0]
    n_past = ckv_ref.shape[0]
    ckvp = ckv_ref[...].astype(BF)
    krp = kr_ref[...].astype(BF)
    lane = _lane_iota((n_new, LANE))
    for pr in range(H_A // 2):
        psl = slice(pr * LANE, (pr + 1) * LANE)
        vp = _dot(ckvp, wv_ref[:, psl]).astype(BF)
        vn = va_ref[:, psl]
        res = []
        for hh in range(2):
            sl = slice((2 * pr + hh) * HEAD_PAD, (2 * pr + hh + 1) * HEAD_PAD)
            q = qa_ref[:, sl]
            kp = (_dot(ckvp, wk_ref[:, sl]) + _dot(krp, e_ref[:, sl])).astype(BF)
            res.append(_softmax_pv([_dot_nt(q, kp), _dot_nt(q, ka_ref[:, sl])], [vp, vn]))
        oa_ref[:, psl] = jnp.where(lane < V_A, res[0], res[1]).astype(BF)
    rowp = lax.broadcasted_iota(jnp.int32, (n_new, n_past), 0)
    colp = lax.broadcasted_iota(jnp.int32, (n_new, n_past), 1)
    dist_p = (rowp - colp + n_past).astype(F32)
    dist_p = jnp.concatenate([dist_p, dist_p], axis=0)
    rown = lax.broadcasted_iota(jnp.int32, (n_new, n_new), 0)
    coln = lax.broadcasted_iota(jnp.int32, (n_new, n_new), 1)
    dist_n = jnp.abs(rown - coln).astype(F32)
    dist_n = jnp.concatenate([dist_n, dist_n], axis=0)
    lam = _diff_lambda(lamv_ref[...], lam_init)
    for h in range(H_B):
        sl = slice(h * LANE, (h + 1) * LANE)
        slope = slopes_ref[h]
        q2x = _split_halves(qd_ref[:, sl])
        kp = ck_ref[:, sl].astype(BF)
        vp = cv_ref[:, sl].astype(BF)
        s_p = _dot_nt(q2x, kp) - slope * dist_p
        s_n = _dot_nt(q2x, kdb_ref[:, sl]) - slope * dist_n
        o = _softmax_pv([s_p, s_n], [vp, vdb_ref[:, sl]])
        ob_ref[:, sl] = _diff_finish(o[:n_new], o[n_new:], lam, subln_ref[...], lam_init).astype(BF)


def _even_sample(slopes, lamv, subln, wk, wv, e_mat, qa, ka, va, ckv_c, kr_c, qd, kdb, vdb, ck_c, cv_c,
                 *, n_prompt_rows, dec_seq, lam_init):
    dec_batch, n_past = ckv_c.shape[0], ckv_c.shape[1]
    base = n_prompt_rows // dec_seq

    def full(a):
        return pl.BlockSpec(a.shape, lambda s: (0,) * a.ndim)

    def new(width):
        return pl.BlockSpec((dec_seq, width), lambda s: (base + s, 0))

    def cache(width):
        return pl.BlockSpec((None, n_past, width), lambda s: (s, 0, 0))

    return pl.pallas_call(
        functools.partial(_even_sample_kernel, lam_init=lam_init),
        grid=(dec_batch,),
        in_specs=[pl.BlockSpec(memory_space=pltpu.SMEM), full(lamv), full(subln), full(wk), full(wv), full(e_mat),
                  new(1024), new(1024), new(512), cache(KV_RANK), cache(ROPE),
                  new(512), new(512), new(512), cache(512), cache(512)],
        out_specs=[pl.BlockSpec((dec_seq, 512), lambda s: (s, 0))] * 2,
        out_shape=[jax.ShapeDtypeStruct((dec_batch * dec_seq, 512), BF)] * 2,
        compiler_params=_cparams("parallel"),
        name="even_sample",
    )(slopes, lamv, subln, wk, wv, e_mat, qa, ka, va, ckv_c, kr_c, qd, kdb, vdb, ck_c, cv_c)


def _route(logits, carry):
    tm = logits.shape[0]
    lane = _lane_iota(logits.shape).astype(F32)
    big = float(LANE)
    g_mask = lane < N_GROUPS
    gl = jnp.where(g_mask, logits, NEG_INF)
    gmax = jnp.max(gl, axis=-1, keepdims=True)
    g_sel = jnp.min(jnp.where(gl == gmax, lane, big), axis=-1, keepdims=True)
    p_grp = 1.0 / jnp.sum(jnp.exp(gl - gmax), axis=-1, keepdims=True)
    lo = ROUTE_OFF + EPG * g_sel
    el = jnp.where((lane >= lo) & (lane < lo + EPG), logits, NEG_INF)
    v1 = jnp.max(el, axis=-1, keepdims=True)
    i1 = jnp.min(jnp.where(el == v1, lane, big), axis=-1, keepdims=True)
    el2 = jnp.where(lane == i1, NEG_INF, el)
    v2 = jnp.max(el2, axis=-1, keepdims=True)
    i2 = jnp.min(jnp.where(el2 == v2, lane, big), axis=-1, keepdims=True)
    ex = jnp.exp(v2 - v1)
    den = 1.0 + ex
    gate1 = (1.0 / den) * p_grp
    gate2 = (ex / den) * p_grp
    onehot = jnp.where((lane == i1) | (lane == i2), 1.0, 0.0)
    row = lax.broadcasted_iota(jnp.int32, (tm, tm), 0)
    col = lax.broadcasted_iota(jnp.int32, (tm, tm), 1)
    tri = jnp.where(row > col, 1.0, 0.0).astype(BF)
    cum = _dot(tri, onehot.astype(BF)) + carry
    r1 = jnp.sum(jnp.where(lane == i1, cum, 0.0), axis=-1, keepdims=True)
    r2 = jnp.sum(jnp.where(lane == i2, cum, 0.0), axis=-1, keepdims=True)
    packed = jnp.zeros_like(logits)
    for pos, val in enumerate([i1 - ROUTE_OFF, i2 - ROUTE_OFF, gate1, gate2, r1, r2]):
        packed = jnp.where(lane == pos, val, packed)
    return packed, carry + jnp.sum(onehot, axis=0, keepdims=True)


def _out_router_kernel(*refs, n_mix, n_prompt_tiles, first):
    if first:
        xp_ref, xs_ref = refs[0], refs[1]
        refs = refs[2:]
    else:
        x_ref = refs[0]
        refs = refs[1:]
    mix_refs = refs[:n_mix]
    w_refs = refs[n_mix:2 * n_mix]
    g_ref, wr_ref, br_ref, x1_ref, xn_ref, route_ref, cnt_ref, carry_ref = refs[2 * n_mix:]
    i = pl.program_id(0)

    @pl.when(i == 0)
    def _():
        carry_ref[...] = jnp.zeros_like(carry_ref)

    if first:
        x1 = jnp.where(i < n_prompt_tiles, xp_ref[...], xs_ref[...])
    else:
        x1 = x_ref[...]
    for m_ref, w_ref in zip(mix_refs, w_refs):
        x1 = x1 + _dot(m_ref[...], w_ref[...])
    x1_ref[...] = x1
    xb = _rms(x1, g_ref[...], NORM_EPS).astype(BF)
    xn_ref[...] = xb
    logits = _dot(xb, wr_ref[...]) + br_ref[...]
    packed, carry = _route(logits, carry_ref[...])
    route_ref[...] = packed
    carry_ref[...] = carry
    cnt_ref[...] = carry


def _out_router(x_parts, mixes, ws, g, wr, br):
    first = len(x_parts) == 2
    if first:
        n_p, n_s = x_parts[0].shape[0] // ROW_TILE, x_parts[1].shape[0] // ROW_TILE
        T, D = x_parts[0].shape[0] + x_parts[1].shape[0], x_parts[0].shape[1]
        x_specs = [pl.BlockSpec((ROW_TILE, D), lambda i: (jnp.minimum(i, n_p - 1), 0)),
                   pl.BlockSpec((ROW_TILE, D), lambda i: (jnp.maximum(i - n_p, 0), 0))]
    else:
        T, D = x_parts[0].shape
        n_p = 0
        x_specs = [pl.BlockSpec((ROW_TILE, D), lambda i: (i, 0))]

    def full(a):
        return pl.BlockSpec(a.shape, lambda i: (0,) * a.ndim)

    def rows(width):
        return pl.BlockSpec((ROW_TILE, width), lambda i: (i, 0))

    return pl.pallas_call(
        functools.partial(_out_router_kernel, n_mix=len(mixes), n_prompt_tiles=n_p, first=first),
        grid=(T // ROW_TILE,),
        in_specs=x_specs + [rows(m.shape[1]) for m in mixes] + [full(w) for w in ws] + [full(g), full(wr), full(br)],
        out_specs=[rows(D), rows(D), rows(LANE), pl.BlockSpec((1, LANE), lambda i: (0, 0))],
        out_shape=[jax.ShapeDtypeStruct((T, D), F32), jax.ShapeDtypeStruct((T, D), BF),
                   jax.ShapeDtypeStruct((T, LANE), F32), jax.ShapeDtypeStruct((1, LANE), F32)],
        scratch_shapes=[pltpu.VMEM((1, LANE), F32)],
        compiler_params=_cparams("arbitrary"),
        name="out_router",
    )(*x_parts, *mixes, *ws, g, wr, br)


def _experts_kernel(be_ref, nu_ref, xb_ref, wg_ref, wu_ref, wd_ref, y_ref):
    i = pl.program_id(0)

    @pl.when(i < nu_ref[0])
    def _():
        xb = xb_ref[...]
        a = _dot(xb, wg_ref[...])
        b = _dot(xb, wu_ref[...])
        hid = (a * jax.nn.sigmoid(a)) * b
        y_ref[...] = _dot(hid.astype(BF), wd_ref[...])

    @pl.when(i >= nu_ref[0])
    def _():
        y_ref[...] = jnp.zeros_like(y_ref)


def _experts(block_expert, n_used, xb, wg, wu, wd):
    L, D = xb.shape
    n_blocks = L // MOE_BLOCK
    grid_spec = pltpu.PrefetchScalarGridSpec(
        num_scalar_prefetch=2,
        grid=(n_blocks,),
        in_specs=[pl.BlockSpec((MOE_BLOCK, D), lambda i, be, nu: (i, 0)),
                  pl.BlockSpec((None, D, D_EXPERT), lambda i, be, nu: (be[i], 0, 0)),
                  pl.BlockSpec((None, D, D_EXPERT), lambda i, be, nu: (be[i], 0, 0)),
                  pl.BlockSpec((None, D_EXPERT, D), lambda i, be, nu: (be[i], 0, 0))],
        out_specs=pl.BlockSpec((MOE_BLOCK, D), lambda i, be, nu: (i, 0)),
    )
    return pl.pallas_call(
        _experts_kernel,
        grid_spec=grid_spec,
        out_shape=jax.ShapeDtypeStruct((L, D), F32),
        compiler_params=_cparams("arbitrary"),
        name="experts",
    )(block_expert, n_used, xb, wg, wu, wd)


def _moe(xn, route, cnt, wg, wu, wd):
    T = xn.shape[0]
    e = route[:, 0:2].astype(jnp.int32)
    rank = route[:, 4:6].astype(jnp.int32)
    counts = cnt[0, ROUTE_OFF:ROUTE_OFF + N_EXPERTS].astype(jnp.int32)
    padded = ((counts + MOE_BLOCK - 1) // MOE_BLOCK) * MOE_BLOCK
    pend = jnp.cumsum(padded)
    pstart = pend - padded
    dest = pstart[e] + rank
    n_blocks = -(-(2 * T) // MOE_BLOCK) + N_EXPERTS
    L = n_blocks * MOE_BLOCK
    tok = jnp.repeat(jnp.arange(T, dtype=jnp.int32), 2)
    buf_tok = jnp.zeros((L,), jnp.int32).at[dest.reshape(-1)].set(tok)
    block_expert = jnp.minimum(
        jnp.searchsorted(pend, jnp.arange(n_blocks, dtype=jnp.int32) * MOE_BLOCK, side="right"),
        N_EXPERTS - 1).astype(jnp.int32)
    n_used = (pend[-1:] // MOE_BLOCK).astype(jnp.int32)
    xb = jnp.take(xn, buf_tok, axis=0)
    yb = _experts(block_expert, n_used, xb, wg, wu, wd)
    return jnp.take(yb, dest[:, 0], axis=0), jnp.take(yb, dest[:, 1], axis=0)


def _gates(route):
    lane = _lane_iota(route.shape)
    g0 = jnp.sum(jnp.where(lane == 2, route, 0.0), axis=-1, keepdims=True)
    g1 = jnp.sum(jnp.where(lane == 3, route, 0.0), axis=-1, keepdims=True)
    return g0, g1


def _odd_in_kernel(x_ref, y0_ref, y1_ref, route_ref, g_ref, w_ref, x2_ref, q_ref, k_ref, v_ref, *, scale):
    g0, g1 = _gates(route_ref[...])
    x2 = x_ref[...] + (y0_ref[...] * g0 + y1_ref[...] * g1)
    x2_ref[...] = x2
    h = _rms(x2, g_ref[...], NORM_EPS).astype(BF)
    y = _dot(h, w_ref[...])
    n = q_ref.shape[1]
    q_ref[...] = (y[:, :n] * scale).astype(BF)
    k_ref[...] = y[:, n:2 * n].astype(BF)
    v_ref[...] = y[:, 2 * n:].astype(BF)


def _odd_in(x1, y0, y1, route, g, w):
    T, D = x1.shape
    n = w.shape[1] // 3

    def rows(width):
        return pl.BlockSpec((ROW_TILE, width), lambda i: (i, 0))

    def full(a):
        return pl.BlockSpec(a.shape, lambda i: (0,) * a.ndim)

    return pl.pallas_call(
        functools.partial(_odd_in_kernel, scale=DH_C ** -0.5 * LOG2E),
        grid=(T // ROW_TILE,),
        in_specs=[rows(D), rows(D), rows(D), rows(LANE), full(g), full(w)],
        out_specs=[rows(D), rows(n), rows(n), rows(n)],
        out_shape=[jax.ShapeDtypeStruct((T, D), F32)] + [jax.ShapeDtypeStruct((T, n), BF)] * 3,
        compiler_params=_cparams("parallel"),
        name="odd_in",
    )(x1, y0, y1, route, g, w)


def _state_rows_kernel(ids_ref, x_ref, g_ref, w_ref, o_ref):
    del ids_ref
    h = _rms(x_ref[...], g_ref[...], NORM_EPS).astype(BF)
    o_ref[...] = _dot(h, w_ref[...])


def _state_rows(tile_ids, x, g, w):
    D = x.shape[1]
    n = tile_ids.shape[0]
    grid_spec = pltpu.PrefetchScalarGridSpec(
        num_scalar_prefetch=1,
        grid=(n,),
        in_specs=[pl.BlockSpec((ROW_TILE, D), lambda i, ids: (ids[i], 0)),
                  pl.BlockSpec(g.shape, lambda i, ids: (0, 0)),
                  pl.BlockSpec(w.shape, lambda i, ids: (0, 0))],
        out_specs=pl.BlockSpec((ROW_TILE, w.shape[1]), lambda i, ids: (i, 0)),
    )
    return pl.pallas_call(
        _state_rows_kernel,
        grid_spec=grid_spec,
        out_shape=jax.ShapeDtypeStruct((n * ROW_TILE, w.shape[1]), F32),
        compiler_params=_cparams("parallel"),
        name="state_rows",
    )(tile_ids, x, g, w)


def _band_prompt_kernel(q_ref, k_ref, v_ref, bias_ref, o_ref, *, tile, tiles_per_step):
    lane = _lane_iota((tile, LANE))

    def body(t, _):
        qi = pl.program_id(2) * tiles_per_step + t
        q0 = pl.multiple_of(t * tile, tile)
        q2x = _split_halves(q_ref[pl.ds(q0, tile), :])
        starts, valids = [], []
        for kt in range(3):
            start = (qi + kt - 2) * tile
            valids.append(start >= 0)
            starts.append(pl.multiple_of(jnp.maximum(start, 0), tile))
        outs = []
        for hh in range(2):
            q = q2x[hh * tile:(hh + 1) * tile]
            s_list = []
            for kt in range(3):
                s = _dot_nt(q, k_ref[pl.ds(starts[kt], tile), :])
                s = s + bias_ref[hh * tile:(hh + 1) * tile, kt * tile:(kt + 1) * tile]
                if kt < 2:
                    s = jnp.where(valids[kt], s, NEG_INF)
                s_list.append(s)
            outs.append(_softmax_pv(s_list, [v_ref[pl.ds(st, tile), :] for st in starts]))
        o_ref[pl.ds(q0, tile), :] = jnp.where(lane < DH_C, outs[0], outs[1]).astype(BF)
        return 0

    lax.fori_loop(0, tiles_per_step, body, 0)


def _band_prompt(q, k, v, bias, *, batch, seq):
    tile = BAND_TILE
    tps = min(BAND_TILES_PER_STEP, seq // tile)
    nq = seq // (tile * tps)
    return pl.pallas_call(
        functools.partial(_band_prompt_kernel, tile=tile, tiles_per_step=tps),
        grid=(H_C // 2, batch, nq),
        in_specs=[pl.BlockSpec((tile * tps, LANE), lambda p, b, qi: (b * nq + qi, p)),
                  pl.BlockSpec((seq, LANE), lambda p, b, qi: (b, p)),
                  pl.BlockSpec((seq, LANE), lambda p, b, qi: (b, p)),
                  pl.BlockSpec((None, 2 * tile, 3 * tile), lambda p, b, qi: (p, 0, 0))],
        out_specs=pl.BlockSpec((tile * tps, LANE), lambda p, b, qi: (b * nq + qi, p)),
        out_shape=jax.ShapeDtypeStruct((batch * seq, H_C * DH_C), BF),
        compiler_params=_cparams("parallel", "parallel", "arbitrary"),
        name="band_prompt",
    )(q, k, v, bias)


def _band_sample_kernel(q_ref, k_ref, v_ref, ck_ref, cv_ref, bias_ref, o_ref):
    n_new = q_ref.shape[0]
    n_past = ck_ref.shape[0]
    lane = _lane_iota((n_new, LANE))
    for pr in range(H_C // 2):
        sl = slice(pr * LANE, (pr + 1) * LANE)
        q2x = _split_halves(q_ref[:, sl])
        s_p = _dot_nt(q2x, ck_ref[:, sl].astype(BF)) + bias_ref[pr, :, 0:n_past]
        s_n = _dot_nt(q2x, k_ref[:, sl]) + bias_ref[pr, :, n_past:n_past + n_new]
        o = _softmax_pv([s_p, s_n], [cv_ref[:, sl].astype(BF), v_ref[:, sl]])
        o_ref[:, sl] = jnp.where(lane < DH_C, o[:n_new], o[n_new:]).astype(BF)


def _band_sample(q, k, v, ck, cv, bias, *, n_prompt_rows, dec_seq):
    dec_batch, n_past, width = ck.shape
    base = n_prompt_rows // dec_seq
    new = pl.BlockSpec((dec_seq, width), lambda s: (base + s, 0))
    cache = pl.BlockSpec((None, n_past, width), lambda s: (s, 0, 0))
    return pl.pallas_call(
        _band_sample_kernel,
        grid=(dec_batch,),
        in_specs=[new, new, new, cache, cache, pl.BlockSpec(bias.shape, lambda s: (0, 0, 0))],
        out_specs=pl.BlockSpec((dec_seq, width), lambda s: (s, 0)),
        out_shape=jax.ShapeDtypeStruct((dec_batch * dec_seq, width), BF),
        compiler_params=_cparams("parallel"),
        name="band_sample",
    )(q, k, v, ck, cv, bias)


def _final_kernel(x_ref, y0_ref, y1_ref, route_ref, g_ref, o_ref):
    g0, g1 = _gates(route_ref[...])
    x = x_ref[...] + (y0_ref[...] * g0 + y1_ref[...] * g1)
    o_ref[...] = _rms(x, g_ref[...], NORM_EPS)


def _final(x, y0, y1, route, g, *, first_tile, n_tiles):
    D = x.shape[1]

    def rows(width):
        return pl.BlockSpec((ROW_TILE, width), lambda i: (first_tile + i, 0))

    return pl.pallas_call(
        _final_kernel,
        grid=(n_tiles,),
        in_specs=[rows(D), rows(D), rows(D), rows(LANE), pl.BlockSpec(g.shape, lambda i: (0, 0))],
        out_specs=pl.BlockSpec((ROW_TILE, D), lambda i: (i, 0)),
        out_shape=jax.ShapeDtypeStruct((n_tiles * ROW_TILE, D), F32),
        compiler_params=_cparams("parallel"),
        name="final_norm",
    )(x, y0, y1, route, g)


def _prep_even_weights(w_in, w_qup, w_kvup):
    D = w_in.shape[0]
    a_in = Q_RANK + KV_RANK + ROPE
    bq = H_B * 2 * DH_B
    wcq, wckv, wkr = w_in[:, :Q_RANK], w_in[:, Q_RANK:Q_RANK + KV_RANK], w_in[:, Q_RANK + KV_RANK:a_in]
    wqd, wkd, wvd = w_in[:, a_in:a_in + bq], w_in[:, a_in + bq:a_in + 2 * bq], w_in[:, a_in + 2 * bq:]
    half = ROPE // 2

    def z(n):
        return jnp.zeros((D, n), w_in.dtype)

    kr_m = jnp.concatenate([z(NOPE), wkr, z(HEAD_PAD - NOPE - ROPE)], axis=1)
    kr_s = jnp.concatenate([z(NOPE), wkr[:, half:], wkr[:, :half], z(HEAD_PAD - NOPE - ROPE)], axis=1)
    w0 = jnp.concatenate([wcq, wckv, kr_m, kr_s, wqd, wkd, wvd], axis=1).astype(BF)
    wq3 = w_qup.reshape(Q_RANK, H_A, NOPE + ROPE)
    nope, r1, r2 = wq3[:, :, :NOPE], wq3[:, :, NOPE:NOPE + half], wq3[:, :, NOPE + half:]
    zq = jnp.zeros((Q_RANK, H_A, HEAD_PAD - NOPE - ROPE), w_qup.dtype)
    wq = jnp.concatenate([nope, r1, r2, zq], axis=-1).reshape(Q_RANK, H_A * HEAD_PAD).astype(BF)
    wqs = jnp.concatenate([jnp.zeros_like(nope), r2, r1, zq], axis=-1).reshape(Q_RANK, H_A * HEAD_PAD).astype(BF)
    wkv3 = w_kvup.reshape(KV_RANK, H_A, NOPE + V_A)
    wk = jnp.concatenate([wkv3[:, :, :NOPE], jnp.zeros((KV_RANK, H_A, HEAD_PAD - NOPE), w_kvup.dtype)],
                         axis=-1).reshape(KV_RANK, H_A * HEAD_PAD).astype(BF)
    wv = wkv3[:, :, NOPE:].reshape(KV_RANK, H_A * V_A).astype(BF)
    r = jnp.arange(ROPE)
    e_mat = jnp.zeros((ROPE, H_A, HEAD_PAD), F32).at[r[:, None], jnp.arange(H_A)[None, :], NOPE + r[:, None]].set(1.0)
    return w0, wq, wqs, wk, wv, e_mat.reshape(ROPE, H_A * HEAD_PAD).astype(BF)


def _rope_tables(pos):
    half = ROPE // 2
    inv = jnp.power(ROPE_BASE, -jnp.arange(half, dtype=F32) / half)
    ang = pos.astype(F32)[:, None] * inv[None, :]
    c, s = jnp.cos(ang), jnp.sin(ang)
    n = pos.shape[0]
    pad = jnp.zeros((n, HEAD_PAD - NOPE - ROPE), F32)
    cs = jnp.concatenate([jnp.ones((n, NOPE), F32), c, c, pad], axis=1)
    sn = jnp.concatenate([jnp.zeros((n, NOPE), F32), -s, s, pad], axis=1)
    return cs, sn


def _band_bias_tiles(table, n_rows, n_keys, key_offset, masked):
    i = jnp.arange(n_rows)[:, None]
    kpos = jnp.arange(n_keys)[None, :] - key_offset
    idx = jnp.clip(i - kpos, -REL_CLIP, REL_CLIP) + REL_CLIP
    bias = (table.astype(F32) * LOG2E)[:, idx]
    if masked:
        qc, kc = i // CHUNK, jnp.floor_divide(kpos, CHUNK)
        vis = (kc <= qc) & (kc >= qc - LEFT_CHUNKS)
        bias = jnp.where(vis[None], bias, NEG_INF)
    return bias.reshape(H_C // 2, 2 * n_rows, n_keys)


def kernel(x_prompt, x_sample, cache_mla_ckv, cache_mla_krope, cache_diff_k, cache_diff_v, cache_band_k, cache_band_v, ln_mix, w_in_even, mla_q_norm, mla_w_qup, mla_kv_norm, mla_w_kvup, diff_lam_q1, diff_lam_k1, diff_lam_q2, diff_lam_k2, diff_subln, w_out_even, w_in_odd, band_rel_bias, w_out_odd, ln_ffn, moe_w_group, moe_b_group, moe_w_router, moe_b_router, moe_w_gate, moe_w_up, moe_w_down, ln_final):
    B, S, D = x_prompt.shape
    DB, DS, _ = x_sample.shape
    n_past = cache_mla_ckv.shape[2]
    c_past = cache_band_k.shape[2]
    assert ln_mix.shape[0] == 2 and S % min(ATT_TILE, S) == 0 and S % BAND_TILE == 0
    assert (B * S) % ROW_TILE == 0 and (DB * DS) % ROW_TILE == 0 and ROW_TILE % DS == 0 and DS == CHUNK
    assert n_past % CHUNK == 0 and c_past == LEFT_CHUNKS * CHUNK and S >= c_past and c_past % ROW_TILE == 0
    BS, NS = B * S, DB * DS
    T = BS + NS
    n_p, n_s = BS // ROW_TILE, NS // ROW_TILE
    xp, xs = x_prompt.reshape(BS, D), x_sample.reshape(NS, D)
    row = lambda a: a.reshape(1, -1)

    w0, wq, wqs, wk, wv, e_mat = _prep_even_weights(w_in_even[0], mla_w_qup[0], mla_w_kvup[0])
    cs_p, sn_p = _rope_tables(jnp.arange(S))
    cs_s, sn_s = _rope_tables(n_past + jnp.arange(DS))
    reps = ROW_TILE // DS
    cs_tab = jnp.concatenate([cs_p, jnp.tile(cs_s, (reps, 1))], axis=0)
    sn_tab = jnp.concatenate([sn_p, jnp.tile(sn_s, (reps, 1))], axis=0)
    qa, ka, va, ckv, kr, qd, kdb, vdb, kd, vd = _even_in(
        xp, xs, row(ln_mix[0]), w0, row(mla_q_norm[0]), wq, wqs, row(mla_kv_norm[0]), wk, wv,
        cs_tab, sn_tab, seq=S)
    slopes = jnp.exp2(-8.0 * jnp.arange(1, H_B + 1, dtype=F32) / H_B) * LOG2E
    lamv = jnp.stack([diff_lam_q1[0], diff_lam_k1[0], diff_lam_q2[0], diff_lam_k2[0]]).astype(F32)
    subln = row(diff_subln[0])
    lam_init = 0.8 - 0.6 * math.exp(-0.3 * 0)
    oa_p = _mla_prompt(qa, ka, va, batch=B, seq=S)
    ob_p = _diff_prompt(slopes, lamv, subln, qd, kdb, vdb, batch=B, seq=S, lam_init=lam_init)
    oa_s, ob_s = _even_sample(
        slopes, lamv, subln, wk, wv, e_mat, qa, ka, va, cache_mla_ckv[0], cache_mla_krope[0],
        qd, kdb, vdb, cache_diff_k[0].reshape(DB, n_past, -1), cache_diff_v[0].reshape(DB, n_past, -1),
        n_prompt_rows=BS, dec_seq=DS, lam_init=lam_init)
    oa = jnp.concatenate([oa_p, oa_s], axis=0)
    ob = jnp.concatenate([ob_p, ob_s], axis=0)

    def router_weights(l):
        wr = jnp.concatenate([moe_w_group[l], moe_w_router[l],
                              jnp.zeros((D, LANE - N_GROUPS - N_EXPERTS), F32)], axis=1).astype(BF)
        br = jnp.concatenate([moe_b_group[l], moe_b_router[l],
                              jnp.zeros((LANE - N_GROUPS - N_EXPERTS,), F32)]).astype(F32)
        return wr, row(br)

    n_a = H_A * V_A
    wo = w_out_even[0].astype(BF)
    wr, br = router_weights(0)
    x1, xn, route, cnt = _out_router([xp, xs], [oa, ob], [wo[:n_a], wo[n_a:]], row(ln_ffn[0]), wr, br)
    y0, y1 = _moe(xn, route, cnt, moe_w_gate[0].astype(BF), moe_w_up[0].astype(BF), moe_w_down[0].astype(BF))

    w_odd = w_in_odd[0].astype(BF)
    n_c = H_C * DH_C
    x2, qc, kc, vc = _odd_in(x1, y0, y1, route, row(ln_mix[1]), w_odd)
    tail = c_past // ROW_TILE
    tiles_per_seq = S // ROW_TILE
    tile_ids = jnp.concatenate(
        [jnp.arange(tiles_per_seq - tail, tiles_per_seq, dtype=jnp.int32) + b * tiles_per_seq for b in range(B)]
        + [jnp.arange(n_p, n_p + n_s, dtype=jnp.int32)])
    st = _state_rows(tile_ids, x2, row(ln_mix[1]), w_odd[:, n_c:])
    bias_p = _band_bias_tiles(band_rel_bias[0], BAND_TILE, 3 * BAND_TILE, 2 * BAND_TILE, True)
    bias_s = _band_bias_tiles(band_rel_bias[0], DS, c_past + DS, c_past, False)
    oc_p = _band_prompt(qc, kc, vc, bias_p, batch=B, seq=S)
    oc_s = _band_sample(qc, kc, vc, cache_band_k[0].reshape(DB, c_past, n_c), cache_band_v[0].reshape(DB, c_past, n_c),
                        bias_s, n_prompt_rows=BS, dec_seq=DS)
    oc = jnp.concatenate([oc_p, oc_s], axis=0)
    wr, br = router_weights(1)
    x3, xn, route, cnt = _out_router([x2], [oc], [w_out_odd[0].astype(BF)], row(ln_ffn[1]), wr, br)
    y0, y1 = _moe(xn, route, cnt, moe_w_gate[1].astype(BF), moe_w_up[1].astype(BF), moe_w_down[1].astype(BF))
    g_fin = row(ln_final)
    y_prompt = _final(x3, y0, y1, route, g_fin, first_tile=0, n_tiles=n_p).reshape(B, S, D)
    y_sample = _final(x3, y0, y1, route, g_fin, first_tile=n_p, n_tiles=n_s).reshape(DB, DS, D)

    def split(a, *tail_shape):
        return a[:BS].reshape(1, B, S, *tail_shape), a[BS:].reshape(1, DB, DS, *tail_shape)

    ckv_p, ckv_s = split(ckv, KV_RANK)
    kr_p, kr_s = split(kr, ROPE)
    kd_p, kd_s = split(kd, H_B, 2 * DH_B)
    vd_p, vd_s = split(vd, H_B, V_B)
    n_tail = B * c_past
    bk_p = st[:n_tail, :n_c].reshape(1, B, c_past, H_C, DH_C)
    bv_p = st[:n_tail, n_c:].reshape(1, B, c_past, H_C, DH_C)
    k_new = st[n_tail:, :n_c].reshape(DB, DS, H_C, DH_C)
    v_new = st[n_tail:, n_c:].reshape(DB, DS, H_C, DH_C)
    bk_s = jnp.concatenate([cache_band_k[0][:, DS:], k_new], axis=1)[None]
    bv_s = jnp.concatenate([cache_band_v[0][:, DS:], v_new], axis=1)[None]
    return (y_prompt, y_sample, ckv_p, kr_p, kd_p, vd_p, bk_p, bv_p, ckv_s, kr_s, kd_s, vd_s, bk_s, bv_s)
```

```python
import functools
import math

import jax
import jax.numpy as jnp
from jax import lax
from jax.experimental import pallas as pl
from jax.experimental.pallas import tpu as pltpu

BF = jnp.bfloat16
F32 = jnp.float32
NEG_INF = float("-inf")
LOG2E = math.log2(math.e)

CHUNK = 64
NORM_EPS = 1e-6
SUBLN_EPS = 1e-5
H_A, NOPE, ROPE, V_A, Q_RANK, KV_RANK = 8, 64, 32, 64, 384, 256
ROPE_BASE = 10000.0
H_B, DH_B, V_B = 4, 64, 128
H_C, DH_C, LEFT_CHUNKS, REL_CLIP = 16, 64, 8, 128
N_GROUPS, EPG, N_EXPERTS, D_EXPERT = 4, 8, 32, 512
LANE = 128
HEAD_PAD = 128
ROUTE_OFF = N_GROUPS

ROW_TILE = 256
ATT_TILE = 512
SOFTMAX_ROWS = 32
BAND_TILE = 256
BAND_TILES_PER_STEP = 4
MOE_BLOCK = 256
VMEM_LIMIT = 56 * 1024 * 1024


def _cparams(*sem):
    return pltpu.CompilerParams(dimension_semantics=sem, vmem_limit_bytes=VMEM_LIMIT)


def _rms(x, g, eps):
    return x * lax.rsqrt(jnp.mean(x * x, axis=-1, keepdims=True) + eps) * g


def _dot(a, b):
    return jnp.dot(a, b, preferred_element_type=F32)


def _dot_nt(a, b):
    return lax.dot_general(a, b, (((1,), (1,)), ((), ())), preferred_element_type=F32)


def _lane_iota(shape):
    return lax.broadcasted_iota(jnp.int32, shape, len(shape) - 1)


def _split_halves(q):
    qf = q.astype(F32)
    lane = _lane_iota(qf.shape)
    return jnp.concatenate([jnp.where(lane < 64, qf, 0.0), jnp.where(lane >= 64, qf, 0.0)], axis=0).astype(BF)


def _softmax_pv(s_list, v_list):
    m = functools.reduce(jnp.maximum, [jnp.max(s, axis=-1, keepdims=True) for s in s_list])
    acc, l = None, None
    for s, v in zip(s_list, v_list):
        p = jnp.exp2(s - m)
        ls = jnp.sum(p, axis=-1, keepdims=True)
        a = _dot(p.astype(BF), v)
        l = ls if l is None else l + ls
        acc = a if acc is None else acc + a
    return acc / l


def _diff_lambda(lamv, lam_init):
    a = jnp.exp(jnp.sum(lamv[0:1] * lamv[1:2], axis=-1, keepdims=True))
    b = jnp.exp(jnp.sum(lamv[2:3] * lamv[3:4], axis=-1, keepdims=True))
    return a - b + lam_init


def _diff_finish(o1, o2, lam, subln, lam_init):
    o = o1 - lam * o2
    return _rms(o, subln, SUBLN_EPS) * (1.0 - lam_init)


def _even_in_kernel(x_ref, g_ref, w0_ref, qn_ref, wq_ref, wqs_ref, kvn_ref, wk_ref, wv_ref,
                    cs_ref, sn_ref,
                    qa_ref, ka_ref, va_ref, ckv_ref, kr_ref, qd_ref, kdb_ref, vdb_ref, kd_ref, vd_ref,
                    *, a_scale, b_scale):
    h = _rms(x_ref[...], g_ref[...], NORM_EPS).astype(BF)
    y = _dot(h, w0_ref[...])
    cq, ckv = y[:, 0:384], y[:, 384:640]
    kr_m, kr_s = y[:, 640:768], y[:, 768:896]
    qd, kd, vd = y[:, 896:1408], y[:, 1408:1920], y[:, 1920:2432]
    cs, sn = cs_ref[...], sn_ref[...]
    cqn = _rms(cq, qn_ref[...], NORM_EPS).astype(BF)
    qm = _dot(cqn, wq_ref[...])
    qs = _dot(cqn, wqs_ref[...])
    ckvn = _rms(ckv, kvn_ref[...], NORM_EPS)
    ckv_ref[...] = ckvn
    krp = kr_m * cs + kr_s * sn
    kr_ref[...] = krp[:, NOPE:NOPE + ROPE]
    cb = ckvn.astype(BF)
    kn = _dot(cb, wk_ref[...])
    for hh in range(H_A):
        sl = slice(hh * HEAD_PAD, (hh + 1) * HEAD_PAD)
        qa_ref[:, sl] = ((qm[:, sl] * cs + qs[:, sl] * sn) * a_scale).astype(BF)
        ka_ref[:, sl] = (kn[:, sl] + krp).astype(BF)
    va_ref[...] = _dot(cb, wv_ref[...]).astype(BF)
    qd_ref[...] = (qd * b_scale).astype(BF)
    kd_ref[...] = kd
    vd_ref[...] = vd
    kdb_ref[...] = kd.astype(BF)
    vdb_ref[...] = vd.astype(BF)


def _even_in(x, g, w0, qn, wq, wqs, kvn, wk, wv, cs_tab, sn_tab):
    T, D = x.shape
    pos_blocks = cs_tab.shape[0] // ROW_TILE

    def full(a):
        return pl.BlockSpec(a.shape, lambda i: (0,) * a.ndim)

    def rows(width):
        return pl.BlockSpec((ROW_TILE, width), lambda i: (i, 0))

    pos_spec = pl.BlockSpec((ROW_TILE, LANE), lambda i: (i % pos_blocks, 0))
    widths = [(1024, BF), (1024, BF), (512, BF), (KV_RANK, F32), (ROPE, F32),
              (512, BF), (512, BF), (512, BF), (512, F32), (512, F32)]
    return pl.pallas_call(
        functools.partial(_even_in_kernel,
                          a_scale=(NOPE + ROPE) ** -0.5 * LOG2E, b_scale=DH_B ** -0.5 * LOG2E),
        grid=(T // ROW_TILE,),
        in_specs=[rows(D), full(g), full(w0), full(qn), full(wq), full(wqs), full(kvn), full(wk), full(wv),
                  pos_spec, pos_spec],
        out_specs=[rows(w) for w, _ in widths],
        out_shape=[jax.ShapeDtypeStruct((T, w), dt) for w, dt in widths],
        compiler_params=_cparams("parallel"),
        name="even_in",
    )(x, g, w0, qn, wq, wqs, kvn, wk, wv, cs_tab, sn_tab)


def _chunk_causal_mask(tq, tk):
    row = lax.broadcasted_iota(jnp.int32, (tq, tk), 0)
    col = lax.broadcasted_iota(jnp.int32, (tq, tk), 1)
    return (col // CHUNK) <= (row // CHUNK)


def _softmax_tile(s_ref, p_ref, m_ref, l_ref, a_ref, *, add_ref=None, sub_ref=None, off=None):
    tile = s_ref.shape[0]
    reps = tile // LANE
    for rb in range(tile // SOFTMAX_ROWS):
        rs = slice(rb * SOFTMAX_ROWS, (rb + 1) * SOFTMAX_ROWS)
        s = s_ref[rs, :]
        if add_ref is not None:
            s = s + add_ref[rs, :]
        if sub_ref is not None:
            s = s - sub_ref[rs, :]
        m_old = m_ref[rs, :]
        red = jnp.broadcast_to(jnp.max(s, axis=-1, keepdims=True), m_old.shape)
        if off is not None:
            red = red - off
        m_new = jnp.maximum(m_old, red)
        shift = m_new if off is None else m_new + off
        p = jnp.exp2(s - jnp.concatenate([shift] * reps, axis=1))
        alpha = jnp.exp2(m_old - m_new)
        l_ref[rs, :] = alpha * l_ref[rs, :] + jnp.broadcast_to(jnp.sum(p, axis=-1, keepdims=True), m_old.shape)
        m_ref[rs, :] = m_new
        a_ref[rs, :] = alpha
        p_ref[rs, :] = p.astype(BF)


def _flash_causal(qi, n_chains, bufs, scores, update):
    s0, s1, p0, p1, m_ref, l_ref, a_ref, acc_ref = bufs
    m_ref[...] = jnp.full(m_ref.shape, NEG_INF, F32)
    l_ref[...] = jnp.zeros(l_ref.shape, F32)
    acc_ref[...] = jnp.zeros(acc_ref.shape, F32)
    scores(0, s0)

    def pair(jj, carry):
        j = 2 * jj
        scores(j + 1, s1)
        update(j, s0, p0, False)
        scores(j + 2, s0)
        update(j + 1, s1, p1, False)
        return carry

    lax.fori_loop(0, qi // 2, pair, 0)

    @pl.when(qi % 2 == 1)
    def _():
        scores(qi, s1)
        update(qi - 1, s0, p0, False)
        s0[...] = s1[...]

    update(qi, s0, p0, True)


def _flash_scratch(n_chains, tile, acc_width=LANE):
    s = pltpu.VMEM((n_chains, tile, tile), F32)
    p = pltpu.VMEM((n_chains, tile, tile), BF)
    stat = pltpu.VMEM((n_chains, tile, LANE), F32)
    return [s, s, p, p, stat, stat, stat, pltpu.VMEM((n_chains, tile, acc_width), F32)]


def _mla_prompt_kernel(q_ref, k_ref, v_ref, o_ref, s0, s1, p0, p1, m_ref, l_ref, a_ref, acc_ref, dmask_ref,
                       *, tile):
    qi = pl.program_id(2)
    dmask_ref[...] = jnp.where(_chunk_causal_mask(tile, tile), 0.0, NEG_INF)
    sls = [slice(hh * HEAD_PAD, (hh + 1) * HEAD_PAD) for hh in range(2)]

    def scores(j, s_buf):
        start = pl.multiple_of(j * tile, tile)
        for c, sl in enumerate(sls):
            s_buf[c] = _dot_nt(q_ref[:, sl], k_ref[pl.ds(start, tile), sl])

    def update(j, s_buf, p_buf, diag):
        start = pl.multiple_of(j * tile, tile)
        for c in range(2):
            _softmax_tile(s_buf.at[c], p_buf.at[c], m_ref.at[c], l_ref.at[c], a_ref.at[c],
                          add_ref=dmask_ref if diag else None)
            acc_ref[c] = a_ref[c] * acc_ref[c] + _dot(p_buf[c], v_ref[pl.ds(start, tile), :])

    _flash_causal(qi, 2, (s0, s1, p0, p1, m_ref, l_ref, a_ref, acc_ref), scores, update)
    lane = _lane_iota((tile, LANE))
    o_ref[...] = jnp.where(lane < V_A, acc_ref[0] / l_ref[0], acc_ref[1] / l_ref[1]).astype(BF)


def _mla_prompt(qa, ka, va, *, batch, seq):
    tile = min(ATT_TILE, seq)
    nq = seq // tile
    return pl.pallas_call(
        functools.partial(_mla_prompt_kernel, tile=tile),
        grid=(batch, H_A // 2, nq),
        in_specs=[pl.BlockSpec((tile, 2 * HEAD_PAD), lambda b, p, qi: (b * nq + qi, p)),
                  pl.BlockSpec((seq, 2 * HEAD_PAD), lambda b, p, qi: (b, p)),
                  pl.BlockSpec((seq, LANE), lambda b, p, qi: (b, p))],
        out_specs=pl.BlockSpec((tile, LANE), lambda b, p, qi: (b * nq + qi, p)),
        out_shape=jax.ShapeDtypeStruct((batch * seq, H_A * V_A), BF),
        scratch_shapes=_flash_scratch(2, tile) + [pltpu.VMEM((tile, tile), F32)],
        compiler_params=_cparams("parallel", "parallel", "arbitrary"),
        name="mla_prompt",
    )(qa, ka, va)


def _diff_prompt_kernel(slopes_ref, lamv_ref, subln_ref, q_ref, k_ref, v_ref, o_ref,
                        s0, s1, p0, p1, m_ref, l_ref, a_ref, acc_ref, pen_ref, dbias_ref, q2_ref,
                        *, tile, lam_init):
    h = pl.program_id(1)
    qi = pl.program_id(2)
    slope = slopes_ref[h]
    q2_ref[...] = _split_halves(q_ref[...])
    row = lax.broadcasted_iota(jnp.int32, (tile, tile), 0)
    col = lax.broadcasted_iota(jnp.int32, (tile, tile), 1)
    pen = slope * (row - col).astype(F32)
    pen_ref[...] = pen
    dbias_ref[...] = jnp.where(_chunk_causal_mask(tile, tile), -jnp.abs(pen), NEG_INF)

    def scores(j, s_buf):
        k = k_ref[pl.ds(pl.multiple_of(j * tile, tile), tile), :]
        for c in range(2):
            s_buf[c] = _dot_nt(q2_ref[c * tile:(c + 1) * tile, :], k)

    def update(j, s_buf, p_buf, diag):
        start = pl.multiple_of(j * tile, tile)
        off = None if diag else slope * ((qi - j) * tile).astype(F32)
        for c in range(2):
            _softmax_tile(s_buf.at[c], p_buf.at[c], m_ref.at[c], l_ref.at[c], a_ref.at[c],
                          add_ref=dbias_ref if diag else None, sub_ref=None if diag else pen_ref, off=off)
            acc_ref[c] = a_ref[c] * acc_ref[c] + _dot(p_buf[c], v_ref[pl.ds(start, tile), :])

    _flash_causal(qi, 2, (s0, s1, p0, p1, m_ref, l_ref, a_ref, acc_ref), scores, update)
    lam = _diff_lambda(lamv_ref[...], lam_init)
    o_ref[...] = _diff_finish(acc_ref[0] / l_ref[0], acc_ref[1] / l_ref[1], lam, subln_ref[...],
                              lam_init).astype(BF)


def _diff_prompt(slopes, lamv, subln, qd, kdb, vdb, *, batch, seq, lam_init):
    tile = min(ATT_TILE, seq)
    nq = seq // tile
    return pl.pallas_call(
        functools.partial(_diff_prompt_kernel, tile=tile, lam_init=lam_init),
        grid=(batch, H_B, nq),
        in_specs=[pl.BlockSpec(memory_space=pltpu.SMEM),
                  pl.BlockSpec(lamv.shape, lambda b, h, qi: (0, 0)),
                  pl.BlockSpec(subln.shape, lambda b, h, qi: (0, 0)),
                  pl.BlockSpec((tile, LANE), lambda b, h, qi: (b * nq + qi, h)),
                  pl.BlockSpec((seq, LANE), lambda b, h, qi: (b, h)),
                  pl.BlockSpec((seq, LANE), lambda b, h, qi: (b, h))],
        out_specs=pl.BlockSpec((tile, LANE), lambda b, h, qi: (b * nq + qi, h)),
        out_shape=jax.ShapeDtypeStruct((batch * seq, H_B * V_B), BF),
        scratch_shapes=_flash_scratch(2, tile) + [pltpu.VMEM((tile, tile), F32), pltpu.VMEM((tile, tile), F32),
                                                  pltpu.VMEM((2 * tile, LANE), BF)],
        compiler_params=_cparams("parallel", "parallel", "arbitrary"),
        name="diff_prompt",
    )(slopes, lamv, subln, qd, kdb, vdb)


def _even_sample_kernel(slopes_ref, lamv_ref, subln_ref, wk_ref, wv_ref, e_ref,
                        qa_ref, ka_ref, va_ref, ckv_ref, kr_ref,
                        qd_ref, kdb_ref, vdb_ref, ck_ref, cv_ref,
                        oa_ref, ob_ref, *, lam_init):
    n_new = qa_ref.shape[0]
    n_past = ckv_ref.shape[0]
    ckvp = ckv_ref[...].astype(BF)
    krp = kr_ref[...].astype(BF)
    lane = _lane_iota((n_new, LANE))
    for pr in range(H_A // 2):
        psl = slice(pr * LANE, (pr + 1) * LANE)
        vp = _dot(ckvp, wv_ref[:, psl]).astype(BF)
        vn = va_ref[:, psl]
        res = []
        for hh in range(2):
            sl = slice((2 * pr + hh) * HEAD_PAD, (2 * pr + hh + 1) * HEAD_PAD)
            q = qa_ref[:, sl]
            kp = (_dot(ckvp, wk_ref[:, sl]) + _dot(krp, e_ref[:, sl])).astype(BF)
            res.append(_softmax_pv([_dot_nt(q, kp), _dot_nt(q, ka_ref[:, sl])], [vp, vn]))
        oa_ref[:, psl] = jnp.where(lane < V_A, res[0], res[1]).astype(BF)
    rowp = lax.broadcasted_iota(jnp.int32, (n_new, n_past), 0)
    colp = lax.broadcasted_iota(jnp.int32, (n_new, n_past), 1)
    dist_p = (rowp - colp + n_past).astype(F32)
    dist_p = jnp.concatenate([dist_p, dist_p], axis=0)
    rown = lax.broadcasted_iota(jnp.int32, (n_new, n_new), 0)
    coln = lax.broadcasted_iota(jnp.int32, (n_new, n_new), 1)
    dist_n = jnp.abs(rown - coln).astype(F32)
    dist_n = jnp.concatenate([dist_n, dist_n], axis=0)
    lam = _diff_lambda(lamv_ref[...], lam_init)
    for h in range(H_B):
        sl = slice(h * LANE, (h + 1) * LANE)
        slope = slopes_ref[h]
        q2x = _split_halves(qd_ref[:, sl])
        kp = ck_ref[:, sl].astype(BF)
        vp = cv_ref[:, sl].astype(BF)
        s_p = _dot_nt(q2x, kp) - slope * dist_p
        s_n = _dot_nt(q2x, kdb_ref[:, sl]) - slope * dist_n
        o = _softmax_pv([s_p, s_n], [vp, vdb_ref[:, sl]])
        ob_ref[:, sl] = _diff_finish(o[:n_new], o[n_new:], lam, subln_ref[...], lam_init).astype(BF)


def _even_sample(slopes, lamv, subln, wk, wv, e_mat, qa, ka, va, ckv_c, kr_c, qd, kdb, vdb, ck_c, cv_c,
                 *, n_prompt_rows, dec_seq, lam_init):
    dec_batch, n_past = ckv_c.shape[0], ckv_c.shape[1]
    base = n_prompt_rows // dec_seq

    def full(a):
        return pl.BlockSpec(a.shape, lambda s: (0,) * a.ndim)

    def new(width):
        return pl.BlockSpec((dec_seq, width), lambda s: (base + s, 0))

    def cache(width):
        return pl.BlockSpec((None, n_past, width), lambda s: (s, 0, 0))

    return pl.pallas_call(
        functools.partial(_even_sample_kernel, lam_init=lam_init),
        grid=(dec_batch,),
        in_specs=[pl.BlockSpec(memory_space=pltpu.SMEM), full(lamv), full(subln), full(wk), full(wv), full(e_mat),
                  new(1024), new(1024), new(512), cache(KV_RANK), cache(ROPE),
                  new(512), new(512), new(512), cache(512), cache(512)],
        out_specs=[pl.BlockSpec((dec_seq, 512), lambda s: (s, 0))] * 2,
        out_shape=[jax.ShapeDtypeStruct((dec_batch * dec_seq, 512), BF)] * 2,
        compiler_params=_cparams("parallel"),
        name="even_sample",
    )(slopes, lamv, subln, wk, wv, e_mat, qa, ka, va, ckv_c, kr_c, qd, kdb, vdb, ck_c, cv_c)


def _route(logits, carry):
    tm = logits.shape[0]
    lane = _lane_iota(logits.shape).astype(F32)
    big = float(LANE)
    g_mask = lane < N_GROUPS
    gl = jnp.where(g_mask, logits, NEG_INF)
    gmax = jnp.max(gl, axis=-1, keepdims=True)
    g_sel = jnp.min(jnp.where(gl == gmax, lane, big), axis=-1, keepdims=True)
    p_grp = 1.0 / jnp.sum(jnp.exp(gl - gmax), axis=-1, keepdims=True)
    lo = ROUTE_OFF + EPG * g_sel
    el = jnp.where((lane >= lo) & (lane < lo + EPG), logits, NEG_INF)
    v1 = jnp.max(el, axis=-1, keepdims=True)
    i1 = jnp.min(jnp.where(el == v1, lane, big), axis=-1, keepdims=True)
    el2 = jnp.where(lane == i1, NEG_INF, el)
    v2 = jnp.max(el2, axis=-1, keepdims=True)
    i2 = jnp.min(jnp.where(el2 == v2, lane, big), axis=-1, keepdims=True)
    ex = jnp.exp(v2 - v1)
    den = 1.0 + ex
    gate1 = (1.0 / den) * p_grp
    gate2 = (ex / den) * p_grp
    onehot = jnp.where((lane == i1) | (lane == i2), 1.0, 0.0)
    row = lax.broadcasted_iota(jnp.int32, (tm, tm), 0)
    col = lax.broadcasted_iota(jnp.int32, (tm, tm), 1)
    tri = jnp.where(row > col, 1.0, 0.0).astype(BF)
    cum = _dot(tri, onehot.astype(BF)) + carry
    r1 = jnp.sum(jnp.where(lane == i1, cum, 0.0), axis=-1, keepdims=True)
    r2 = jnp.sum(jnp.where(lane == i2, cum, 0.0), axis=-1, keepdims=True)
    packed = jnp.zeros_like(logits)
    for pos, val in enumerate([i1 - ROUTE_OFF, i2 - ROUTE_OFF, gate1, gate2, r1, r2]):
        packed = jnp.where(lane == pos, val, packed)
    return packed, carry + jnp.sum(onehot, axis=0, keepdims=True)


def _out_router_kernel(*refs, splits, n_prompt_tiles):
    i = pl.program_id(0)
    offs = [sum(splits[:k]) for k in range(len(splits))]
    n_mix = len(splits) - 1

    def pick(k):
        parts = refs[offs[k]:offs[k] + splits[k]]
        if splits[k] == 1:
            return parts[0][...]
        return jnp.where(i < n_prompt_tiles, parts[0][...], parts[1][...])

    rest = refs[sum(splits):]
    w_refs = rest[:n_mix]
    g_ref, wr_ref, br_ref, x1_ref, xn_ref, route_ref, cnt_ref, carry_ref = rest[n_mix:]

    @pl.when(i == 0)
    def _():
        carry_ref[...] = jnp.zeros_like(carry_ref)

    x1 = pick(0)
    for k, w_ref in enumerate(w_refs):
        x1 = x1 + _dot(pick(1 + k), w_ref[...])
    x1_ref[...] = x1
    xn = _rms(x1, g_ref[...], NORM_EPS)
    xn_ref[...] = xn
    logits = _dot(xn.astype(BF), wr_ref[...]) + br_ref[...]
    packed, carry = _route(logits, carry_ref[...])
    route_ref[...] = packed
    carry_ref[...] = carry
    cnt_ref[...] = carry


def _out_router(row_inputs, ws, g, wr, br):
    splits = tuple(len(parts) for parts in row_inputs)
    T = sum(a.shape[0] for a in row_inputs[0])
    D = row_inputs[0][0].shape[1]
    n_p = max([parts[0].shape[0] // ROW_TILE for parts in row_inputs if len(parts) == 2], default=0)

    def row_specs(parts):
        if len(parts) == 1:
            return [pl.BlockSpec((ROW_TILE, parts[0].shape[1]), lambda i: (i, 0))]
        return [pl.BlockSpec((ROW_TILE, parts[0].shape[1]), lambda i: (jnp.minimum(i, n_p - 1), 0)),
                pl.BlockSpec((ROW_TILE, parts[1].shape[1]), lambda i: (jnp.maximum(i - n_p, 0), 0))]

    def full(a):
        return pl.BlockSpec(a.shape, lambda i: (0,) * a.ndim)

    def rows(width):
        return pl.BlockSpec((ROW_TILE, width), lambda i: (i, 0))

    flat_rows = [a for parts in row_inputs for a in parts]
    return pl.pallas_call(
        functools.partial(_out_router_kernel, splits=splits, n_prompt_tiles=n_p),
        grid=(T // ROW_TILE,),
        in_specs=[s for parts in row_inputs for s in row_specs(parts)] + [full(w) for w in ws]
        + [full(g), full(wr), full(br)],
        out_specs=[rows(D), rows(D), rows(LANE), pl.BlockSpec((1, LANE), lambda i: (0, 0))],
        out_shape=[jax.ShapeDtypeStruct((T, D), F32), jax.ShapeDtypeStruct((T, D), F32),
                   jax.ShapeDtypeStruct((T, LANE), F32), jax.ShapeDtypeStruct((1, LANE), F32)],
        scratch_shapes=[pltpu.VMEM((1, LANE), F32)],
        compiler_params=_cparams("arbitrary"),
        name="out_router",
    )(*flat_rows, *ws, g, wr, br)


def _experts_kernel(be_ref, nu_ref, xb_ref, wg_ref, wu_ref, wd_ref, y_ref, wgb_ref, wub_ref, wdb_ref):
    i = pl.program_id(0)
    used = i < nu_ref[0]

    @pl.when(used & ((i == 0) | (be_ref[i] != be_ref[jnp.maximum(i - 1, 0)])))
    def _():
        wgb_ref[...] = wg_ref[...].astype(BF)
        wub_ref[...] = wu_ref[...].astype(BF)
        wdb_ref[...] = wd_ref[...].astype(BF)

    @pl.when(used)
    def _():
        xb = xb_ref[...].astype(BF)
        a = _dot(xb, wgb_ref[...])
        b = _dot(xb, wub_ref[...])
        hid = (a * jax.nn.sigmoid(a)) * b
        y_ref[...] = _dot(hid.astype(BF), wdb_ref[...])

    @pl.when(jnp.logical_not(used))
    def _():
        y_ref[...] = jnp.zeros_like(y_ref)


def _experts(block_expert, n_used, xb, wg, wu, wd):
    L, D = xb.shape
    n_blocks = L // MOE_BLOCK
    grid_spec = pltpu.PrefetchScalarGridSpec(
        num_scalar_prefetch=2,
        grid=(n_blocks,),
        in_specs=[pl.BlockSpec((MOE_BLOCK, D), lambda i, be, nu: (i, 0)),
                  pl.BlockSpec((None, D, D_EXPERT), lambda i, be, nu: (be[i], 0, 0)),
                  pl.BlockSpec((None, D, D_EXPERT), lambda i, be, nu: (be[i], 0, 0)),
                  pl.BlockSpec((None, D_EXPERT, D), lambda i, be, nu: (be[i], 0, 0))],
        out_specs=pl.BlockSpec((MOE_BLOCK, D), lambda i, be, nu: (i, 0)),
        scratch_shapes=[pltpu.VMEM((D, D_EXPERT), BF), pltpu.VMEM((D, D_EXPERT), BF), pltpu.VMEM((D_EXPERT, D), BF)],
    )
    return pl.pallas_call(
        _experts_kernel,
        grid_spec=grid_spec,
        out_shape=jax.ShapeDtypeStruct((L, D), F32),
        compiler_params=_cparams("arbitrary"),
        name="experts",
    )(block_expert, n_used, xb, wg, wu, wd)


def _moe(xn, route, cnt, wg, wu, wd):
    T = xn.shape[0]
    e = route[:, 0:2].astype(jnp.int32)
    rank = route[:, 4:6].astype(jnp.int32)
    counts = cnt[0, ROUTE_OFF:ROUTE_OFF + N_EXPERTS].astype(jnp.int32)
    padded = ((counts + MOE_BLOCK - 1) // MOE_BLOCK) * MOE_BLOCK
    pend = jnp.cumsum(padded)
    pstart = pend - padded
    dest = jnp.sum(jnp.where(e[:, :, None] == jnp.arange(N_EXPERTS)[None, None, :], pstart[None, None, :], 0),
                   axis=-1) + rank
    n_blocks = -(-(2 * T) // MOE_BLOCK) + N_EXPERTS
    L = n_blocks * MOE_BLOCK
    tok = jnp.repeat(jnp.arange(T, dtype=jnp.int32), 2)
    buf_tok = jnp.zeros((L,), jnp.int32).at[dest.reshape(-1)].set(tok, unique_indices=True, mode="promise_in_bounds")
    block_start = jnp.arange(n_blocks, dtype=jnp.int32) * MOE_BLOCK
    block_expert = jnp.minimum(jnp.sum(pend[None, :] <= block_start[:, None], axis=1), N_EXPERTS - 1).astype(jnp.int32)
    n_used = (pend[-1:] // MOE_BLOCK).astype(jnp.int32)
    xb = xn.at[buf_tok].get(mode="promise_in_bounds")
    yb = _experts(block_expert, n_used, xb, wg, wu, wd)
    return (yb.at[dest[:, 0]].get(mode="promise_in_bounds"), yb.at[dest[:, 1]].get(mode="promise_in_bounds"))


def _gates(route):
    lane = _lane_iota(route.shape)
    g0 = jnp.sum(jnp.where(lane == 2, route, 0.0), axis=-1, keepdims=True)
    g1 = jnp.sum(jnp.where(lane == 3, route, 0.0), axis=-1, keepdims=True)
    return g0, g1


def _odd_in_kernel(x_ref, y0_ref, y1_ref, route_ref, g_ref, w_ref, x2_ref, q_ref, k_ref, v_ref, *, scale):
    g0, g1 = _gates(route_ref[...])
    x2 = x_ref[...] + (y0_ref[...] * g0 + y1_ref[...] * g1)
    x2_ref[...] = x2
    h = _rms(x2, g_ref[...], NORM_EPS).astype(BF)
    y = _dot(h, w_ref[...])
    n = q_ref.shape[1]
    q_ref[...] = (y[:, :n] * scale).astype(BF)
    k_ref[...] = y[:, n:2 * n].astype(BF)
    v_ref[...] = y[:, 2 * n:].astype(BF)


def _odd_in(x1, y0, y1, route, g, w):
    T, D = x1.shape
    n = w.shape[1] // 3

    def rows(width):
        return pl.BlockSpec((ROW_TILE, width), lambda i: (i, 0))

    def full(a):
        return pl.BlockSpec(a.shape, lambda i: (0,) * a.ndim)

    return pl.pallas_call(
        functools.partial(_odd_in_kernel, scale=DH_C ** -0.5 * LOG2E),
        grid=(T // ROW_TILE,),
        in_specs=[rows(D), rows(D), rows(D), rows(LANE), full(g), full(w)],
        out_specs=[rows(D), rows(n), rows(n), rows(n)],
        out_shape=[jax.ShapeDtypeStruct((T, D), F32)] + [jax.ShapeDtypeStruct((T, n), BF)] * 3,
        compiler_params=_cparams("parallel"),
        name="odd_in",
    )(x1, y0, y1, route, g, w)


def _state_rows_kernel(ids_ref, x_ref, g_ref, w_ref, o_ref):
    del ids_ref
    h = _rms(x_ref[...], g_ref[...], NORM_EPS).astype(BF)
    o_ref[...] = _dot(h, w_ref[...])


def _state_rows(tile_ids, x, g, w):
    D = x.shape[1]
    n = tile_ids.shape[0]
    grid_spec = pltpu.PrefetchScalarGridSpec(
        num_scalar_prefetch=1,
        grid=(n,),
        in_specs=[pl.BlockSpec((ROW_TILE, D), lambda i, ids: (ids[i], 0)),
                  pl.BlockSpec(g.shape, lambda i, ids: (0, 0)),
                  pl.BlockSpec(w.shape, lambda i, ids: (0, 0))],
        out_specs=pl.BlockSpec((ROW_TILE, w.shape[1]), lambda i, ids: (i, 0)),
    )
    return pl.pallas_call(
        _state_rows_kernel,
        grid_spec=grid_spec,
        out_shape=jax.ShapeDtypeStruct((n * ROW_TILE, w.shape[1]), F32),
        compiler_params=_cparams("parallel"),
        name="state_rows",
    )(tile_ids, x, g, w)


def _band_prompt_kernel(q_ref, k_ref, v_ref, bias_ref, o_ref, *, tile, tiles_per_step):
    lane = _lane_iota((tile, LANE))

    def body(t, _):
        qi = pl.program_id(2) * tiles_per_step + t
        q0 = pl.multiple_of(t * tile, tile)
        q2x = _split_halves(q_ref[pl.ds(q0, tile), :])
        starts, valids = [], []
        for kt in range(3):
            start = (qi + kt - 2) * tile
            valids.append(start >= 0)
            starts.append(pl.multiple_of(jnp.maximum(start, 0), tile))
        outs = []
        for hh in range(2):
            q = q2x[hh * tile:(hh + 1) * tile]
            s_list = []
            for kt in range(3):
                s = _dot_nt(q, k_ref[pl.ds(starts[kt], tile), :])
                s = s + bias_ref[hh * tile:(hh + 1) * tile, kt * tile:(kt + 1) * tile]
                if kt < 2:
                    s = jnp.where(valids[kt], s, NEG_INF)
                s_list.append(s)
            outs.append(_softmax_pv(s_list, [v_ref[pl.ds(st, tile), :] for st in starts]))
        o_ref[pl.ds(q0, tile), :] = jnp.where(lane < DH_C, outs[0], outs[1]).astype(BF)
        return 0

    lax.fori_loop(0, tiles_per_step, body, 0)


def _band_prompt(q, k, v, bias, *, batch, seq):
    tile = BAND_TILE
    tps = min(BAND_TILES_PER_STEP, seq // tile)
    nq = seq // (tile * tps)
    return pl.pallas_call(
        functools.partial(_band_prompt_kernel, tile=tile, tiles_per_step=tps),
        grid=(H_C // 2, batch, nq),
        in_specs=[pl.BlockSpec((tile * tps, LANE), lambda p, b, qi: (b * nq + qi, p)),
                  pl.BlockSpec((seq, LANE), lambda p, b, qi: (b, p)),
                  pl.BlockSpec((seq, LANE), lambda p, b, qi: (b, p)),
                  pl.BlockSpec((None, 2 * tile, 3 * tile), lambda p, b, qi: (p, 0, 0))],
        out_specs=pl.BlockSpec((tile * tps, LANE), lambda p, b, qi: (b * nq + qi, p)),
        out_shape=jax.ShapeDtypeStruct((batch * seq, H_C * DH_C), BF),
        compiler_params=_cparams("parallel", "parallel", "arbitrary"),
        name="band_prompt",
    )(q, k, v, bias)


def _band_sample_kernel(q_ref, k_ref, v_ref, ck_ref, cv_ref, bias_ref, o_ref):
    n_new = q_ref.shape[0]
    n_past = ck_ref.shape[0]
    lane = _lane_iota((n_new, LANE))
    for pr in range(H_C // 2):
        sl = slice(pr * LANE, (pr + 1) * LANE)
        q2x = _split_halves(q_ref[:, sl])
        s_p = _dot_nt(q2x, ck_ref[:, sl].astype(BF)) + bias_ref[pr, :, 0:n_past]
        s_n = _dot_nt(q2x, k_ref[:, sl]) + bias_ref[pr, :, n_past:n_past + n_new]
        o = _softmax_pv([s_p, s_n], [cv_ref[:, sl].astype(BF), v_ref[:, sl]])
        o_ref[:, sl] = jnp.where(lane < DH_C, o[:n_new], o[n_new:]).astype(BF)


def _band_sample(q, k, v, ck, cv, bias, *, n_prompt_rows, dec_seq):
    dec_batch, n_past, width = ck.shape
    base = n_prompt_rows // dec_seq
    new = pl.BlockSpec((dec_seq, width), lambda s: (base + s, 0))
    cache = pl.BlockSpec((None, n_past, width), lambda s: (s, 0, 0))
    return pl.pallas_call(
        _band_sample_kernel,
        grid=(dec_batch,),
        in_specs=[new, new, new, cache, cache, pl.BlockSpec(bias.shape, lambda s: (0, 0, 0))],
        out_specs=pl.BlockSpec((dec_seq, width), lambda s: (s, 0)),
        out_shape=jax.ShapeDtypeStruct((dec_batch * dec_seq, width), BF),
        compiler_params=_cparams("parallel"),
        name="band_sample",
    )(q, k, v, ck, cv, bias)


def _final_kernel(x_ref, y0_ref, y1_ref, route_ref, g_ref, o_ref):
    g0, g1 = _gates(route_ref[...])
    x = x_ref[...] + (y0_ref[...] * g0 + y1_ref[...] * g1)
    o_ref[...] = _rms(x, g_ref[...], NORM_EPS)


def _final(x, y0, y1, route, g, *, first_tile, n_tiles):
    D = x.shape[1]

    def rows(width):
        return pl.BlockSpec((ROW_TILE, width), lambda i: (first_tile + i, 0))

    return pl.pallas_call(
        _final_kernel,
        grid=(n_tiles,),
        in_specs=[rows(D), rows(D), rows(D), rows(LANE), pl.BlockSpec(g.shape, lambda i: (0, 0))],
        out_specs=pl.BlockSpec((ROW_TILE, D), lambda i: (i, 0)),
        out_shape=jax.ShapeDtypeStruct((n_tiles * ROW_TILE, D), F32),
        compiler_params=_cparams("parallel"),
        name="final_norm",
    )(x, y0, y1, route, g)


def _prep_even_weights(w_in, w_qup, w_kvup):
    D = w_in.shape[0]
    a_in = Q_RANK + KV_RANK + ROPE
    bq = H_B * 2 * DH_B
    wcq, wckv, wkr = w_in[:, :Q_RANK], w_in[:, Q_RANK:Q_RANK + KV_RANK], w_in[:, Q_RANK + KV_RANK:a_in]
    wqd, wkd, wvd = w_in[:, a_in:a_in + bq], w_in[:, a_in + bq:a_in + 2 * bq], w_in[:, a_in + 2 * bq:]
    half = ROPE // 2

    def z(n):
        return jnp.zeros((D, n), w_in.dtype)

    kr_m = jnp.concatenate([z(NOPE), wkr, z(HEAD_PAD - NOPE - ROPE)], axis=1)
    kr_s = jnp.concatenate([z(NOPE), wkr[:, half:], wkr[:, :half], z(HEAD_PAD - NOPE - ROPE)], axis=1)
    w0 = jnp.concatenate([wcq, wckv, kr_m, kr_s, wqd, wkd, wvd], axis=1).astype(BF)
    wq3 = w_qup.reshape(Q_RANK, H_A, NOPE + ROPE)
    nope, r1, r2 = wq3[:, :, :NOPE], wq3[:, :, NOPE:NOPE + half], wq3[:, :, NOPE + half:]
    zq = jnp.zeros((Q_RANK, H_A, HEAD_PAD - NOPE - ROPE), w_qup.dtype)
    wq = jnp.concatenate([nope, r1, r2, zq], axis=-1).reshape(Q_RANK, H_A * HEAD_PAD).astype(BF)
    wqs = jnp.concatenate([jnp.zeros_like(nope), r2, r1, zq], axis=-1).reshape(Q_RANK, H_A * HEAD_PAD).astype(BF)
    wkv3 = w_kvup.reshape(KV_RANK, H_A, NOPE + V_A)
    wk = jnp.concatenate([wkv3[:, :, :NOPE], jnp.zeros((KV_RANK, H_A, HEAD_PAD - NOPE), w_kvup.dtype)],
                         axis=-1).reshape(KV_RANK, H_A * HEAD_PAD).astype(BF)
    wv = wkv3[:, :, NOPE:].reshape(KV_RANK, H_A * V_A).astype(BF)
    r = jnp.arange(ROPE)
    e_mat = jnp.zeros((ROPE, H_A, HEAD_PAD), F32).at[r[:, None], jnp.arange(H_A)[None, :], NOPE + r[:, None]].set(1.0)
    return w0, wq, wqs, wk, wv, e_mat.reshape(ROPE, H_A * HEAD_PAD).astype(BF)


def _rope_tables(pos):
    half = ROPE // 2
    inv = jnp.power(ROPE_BASE, -jnp.arange(half, dtype=F32) / half)
    ang = pos.astype(F32)[:, None] * inv[None, :]
    c, s = jnp.cos(ang), jnp.sin(ang)
    n = pos.shape[0]
    pad = jnp.zeros((n, HEAD_PAD - NOPE - ROPE), F32)
    cs = jnp.concatenate([jnp.ones((n, NOPE), F32), c, c, pad], axis=1)
    sn = jnp.concatenate([jnp.zeros((n, NOPE), F32), -s, s, pad], axis=1)
    return cs, sn


def _band_bias_tiles(table, n_rows, n_keys, key_offset, masked):
    i = jnp.arange(n_rows)[:, None]
    kpos = jnp.arange(n_keys)[None, :] - key_offset
    d_max = n_rows - 1 + key_offset
    d = d_max - jnp.arange(n_rows + n_keys - 1)
    rev = (table.astype(F32) * LOG2E)[:, jnp.clip(d, -REL_CLIP, REL_CLIP) + REL_CLIP]
    bias = jnp.stack([rev[:, n_rows - 1 - r:n_rows - 1 - r + n_keys] for r in range(n_rows)], axis=1)
    if masked:
        qc, kc = i // CHUNK, jnp.floor_divide(kpos, CHUNK)
        vis = (kc <= qc) & (kc >= qc - LEFT_CHUNKS)
        bias = jnp.where(vis[None], bias, NEG_INF)
    return bias.reshape(H_C // 2, 2 * n_rows, n_keys)


def kernel(x_prompt, x_sample, cache_mla_ckv, cache_mla_krope, cache_diff_k, cache_diff_v, cache_band_k, cache_band_v, ln_mix, w_in_even, mla_q_norm, mla_w_qup, mla_kv_norm, mla_w_kvup, diff_lam_q1, diff_lam_k1, diff_lam_q2, diff_lam_k2, diff_subln, w_out_even, w_in_odd, band_rel_bias, w_out_odd, ln_ffn, moe_w_group, moe_b_group, moe_w_router, moe_b_router, moe_w_gate, moe_w_up, moe_w_down, ln_final):
    B, S, D = x_prompt.shape
    DB, DS, _ = x_sample.shape
    n_past = cache_mla_ckv.shape[2]
    c_past = cache_band_k.shape[2]
    assert ln_mix.shape[0] == 2 and S % min(ATT_TILE, S) == 0 and S % BAND_TILE == 0
    assert (B * S) % ROW_TILE == 0 and (DB * DS) % ROW_TILE == 0 and ROW_TILE % DS == 0 and DS == CHUNK
    assert n_past % CHUNK == 0 and c_past == LEFT_CHUNKS * CHUNK and S >= c_past and c_past % ROW_TILE == 0
    BS, NS = B * S, DB * DS
    T = BS + NS
    n_p, n_s = BS // ROW_TILE, NS // ROW_TILE
    xp, xs = x_prompt.reshape(BS, D), x_sample.reshape(NS, D)
    row = lambda a: a.reshape(1, -1)

    w0, wq, wqs, wk, wv, e_mat = _prep_even_weights(w_in_even[0], mla_w_qup[0], mla_w_kvup[0])
    cs_p, sn_p = _rope_tables(jnp.arange(S))
    cs_s, sn_s = _rope_tables(n_past + jnp.arange(DS))
    reps = ROW_TILE // DS
    even_w = (row(ln_mix[0]), w0, row(mla_q_norm[0]), wq, wqs, row(mla_kv_norm[0]), wk, wv)
    qa, ka, va, ckv_p, kr_p, qd, kdb, vdb, kd_p, vd_p = _even_in(xp, *even_w, cs_p, sn_p)
    qa_s, ka_s, va_s, ckv_s, kr_s, qd_s, kdb_s, vdb_s, kd_s, vd_s = _even_in(
        xs, *even_w, jnp.tile(cs_s, (reps, 1)), jnp.tile(sn_s, (reps, 1)))
    slopes = jnp.exp2(-8.0 * jnp.arange(1, H_B + 1, dtype=F32) / H_B) * LOG2E
    lamv = jnp.stack([diff_lam_q1[0], diff_lam_k1[0], diff_lam_q2[0], diff_lam_k2[0]]).astype(F32)
    subln = row(diff_subln[0])
    lam_init = 0.8 - 0.6 * math.exp(-0.3 * 0)
    oa_p = _mla_prompt(qa, ka, va, batch=B, seq=S)
    ob_p = _diff_prompt(slopes, lamv, subln, qd, kdb, vdb, batch=B, seq=S, lam_init=lam_init)
    oa_s, ob_s = _even_sample(
        slopes, lamv, subln, wk, wv, e_mat, qa_s, ka_s, va_s, cache_mla_ckv[0], cache_mla_krope[0],
        qd_s, kdb_s, vdb_s, cache_diff_k[0].reshape(DB, n_past, -1), cache_diff_v[0].reshape(DB, n_past, -1),
        n_prompt_rows=0, dec_seq=DS, lam_init=lam_init)

    def router_weights(l):
        wr = jnp.concatenate([moe_w_group[l], moe_w_router[l],
                              jnp.zeros((D, LANE - N_GROUPS - N_EXPERTS), F32)], axis=1).astype(BF)
        br = jnp.concatenate([moe_b_group[l], moe_b_router[l],
                              jnp.zeros((LANE - N_GROUPS - N_EXPERTS,), F32)]).astype(F32)
        return wr, row(br)

    n_a = H_A * V_A
    wo = w_out_even[0].astype(BF)
    wr, br = router_weights(0)
    x1, xn, route, cnt = _out_router([(xp, xs), (oa_p, oa_s), (ob_p, ob_s)], [wo[:n_a], wo[n_a:]],
                                     row(ln_ffn[0]), wr, br)
    y0, y1 = _moe(xn, route, cnt, moe_w_gate[0], moe_w_up[0], moe_w_down[0])

    w_odd = w_in_odd[0].astype(BF)
    n_c = H_C * DH_C
    x2, qc, kc, vc = _odd_in(x1, y0, y1, route, row(ln_mix[1]), w_odd)
    tail = c_past // ROW_TILE
    tiles_per_seq = S // ROW_TILE
    tile_ids = jnp.concatenate(
        [jnp.arange(tiles_per_seq - tail, tiles_per_seq, dtype=jnp.int32) + b * tiles_per_seq for b in range(B)]
        + [jnp.arange(n_p, n_p + n_s, dtype=jnp.int32)])
    st = _state_rows(tile_ids, x2, row(ln_mix[1]), w_odd[:, n_c:])
    bias_p = _band_bias_tiles(band_rel_bias[0], BAND_TILE, 3 * BAND_TILE, 2 * BAND_TILE, True)
    bias_s = _band_bias_tiles(band_rel_bias[0], DS, c_past + DS, c_past, False)
    oc_p = _band_prompt(qc, kc, vc, bias_p, batch=B, seq=S)
    oc_s = _band_sample(qc, kc, vc, cache_band_k[0].reshape(DB, c_past, n_c), cache_band_v[0].reshape(DB, c_past, n_c),
                        bias_s, n_prompt_rows=BS, dec_seq=DS)
    wr, br = router_weights(1)
    x3, xn, route, cnt = _out_router([(x2,), (oc_p, oc_s)], [w_out_odd[0].astype(BF)], row(ln_ffn[1]), wr, br)
    y0, y1 = _moe(xn, route, cnt, moe_w_gate[1], moe_w_up[1], moe_w_down[1])
    g_fin = row(ln_final)
    y_prompt = _final(x3, y0, y1, route, g_fin, first_tile=0, n_tiles=n_p).reshape(B, S, D)
    y_sample = _final(x3, y0, y1, route, g_fin, first_tile=n_p, n_tiles=n_s).reshape(DB, DS, D)

    def shaped(a_p, a_s, *tail_shape):
        return a_p.reshape(1, B, S, *tail_shape), a_s.reshape(1, DB, DS, *tail_shape)

    ckv_p, ckv_s = shaped(ckv_p, ckv_s, KV_RANK)
    kr_p, kr_s = shaped(kr_p, kr_s, ROPE)
    kd_p, kd_s = shaped(kd_p, kd_s, H_B, 2 * DH_B)
    vd_p, vd_s = shaped(vd_p, vd_s, H_B, V_B)
    n_tail = B * c_past
    bk_p = st[:n_tail, :n_c].reshape(1, B, c_past, H_C, DH_C)
    bv_p = st[:n_tail, n_c:].reshape(1, B, c_past, H_C, DH_C)
    k_new = st[n_tail:, :n_c].reshape(DB, DS, H_C, DH_C)
    v_new = st[n_tail:, n_c:].reshape(DB, DS, H_C, DH_C)
    bk_s = jnp.concatenate([cache_band_k[0][:, DS:], k_new], axis=1)[None]
    bv_s = jnp.concatenate([cache_band_v[0][:, DS:], v_new], axis=1)[None]
    return (y_prompt, y_sample, ckv_p, kr_p, kd_p, vd_p, bk_p, bv_p, ckv_s, kr_s, kd_s, vd_s, bk_s, bv_s)
```

```python
import functools
import math

import jax
import jax.numpy as jnp
from jax import lax
from jax.experimental import pallas as pl
from jax.experimental.pallas import tpu as pltpu

BF = jnp.bfloat16
F32 = jnp.float32
NEG_INF = float("-inf")
LOG2E = math.log2(math.e)

CHUNK = 64
NORM_EPS = 1e-6
SUBLN_EPS = 1e-5
H_A, NOPE, ROPE, V_A, Q_RANK, KV_RANK = 8, 64, 32, 64, 384, 256
ROPE_BASE = 10000.0
H_B, DH_B, V_B = 4, 64, 128
H_C, DH_C, LEFT_CHUNKS, REL_CLIP = 16, 64, 8, 128
N_GROUPS, EPG, N_EXPERTS, D_EXPERT = 4, 8, 32, 512
LANE = 128
HEAD_PAD = 128
ROUTE_OFF = N_GROUPS

ROW_TILE = 256
ATT_TILE = 512
SOFTMAX_ROWS = 32
BAND_TILE = 256
BAND_TILES_PER_STEP = 4
MOE_BLOCK = 512
VMEM_LIMIT = 56 * 1024 * 1024


def _cparams(*sem):
    return pltpu.CompilerParams(dimension_semantics=sem, vmem_limit_bytes=VMEM_LIMIT)


def _rms(x, g, eps):
    return x * lax.rsqrt(jnp.mean(x * x, axis=-1, keepdims=True) + eps) * g


def _dot(a, b):
    return jnp.dot(a, b, preferred_element_type=F32)


def _dot_nt(a, b):
    return lax.dot_general(a, b, (((1,), (1,)), ((), ())), preferred_element_type=F32)


def _lane_iota(shape):
    return lax.broadcasted_iota(jnp.int32, shape, len(shape) - 1)


def _split_halves(q):
    qf = q.astype(F32)
    lane = _lane_iota(qf.shape)
    return jnp.concatenate([jnp.where(lane < 64, qf, 0.0), jnp.where(lane >= 64, qf, 0.0)], axis=0).astype(BF)


def _softmax_pv(s_list, v_list):
    m = functools.reduce(jnp.maximum, [jnp.max(s, axis=-1, keepdims=True) for s in s_list])
    acc, l = None, None
    for s, v in zip(s_list, v_list):
        p = jnp.exp2(s - m)
        ls = jnp.sum(p, axis=-1, keepdims=True)
        a = _dot(p.astype(BF), v)
        l = ls if l is None else l + ls
        acc = a if acc is None else acc + a
    return acc / l


def _diff_lambda(lamv, lam_init):
    a = jnp.exp(jnp.sum(lamv[0:1] * lamv[1:2], axis=-1, keepdims=True))
    b = jnp.exp(jnp.sum(lamv[2:3] * lamv[3:4], axis=-1, keepdims=True))
    return a - b + lam_init


def _diff_finish(o1, o2, lam, subln, lam_init):
    o = o1 - lam * o2
    return _rms(o, subln, SUBLN_EPS) * (1.0 - lam_init)


def _even_in_kernel(x_ref, g_ref, w0_ref, qn_ref, wq_ref, wqs_ref, kvn_ref, wk_ref, wv_ref, vone_ref,
                    cs_ref, sn_ref,
                    qa_ref, ka_ref, va_ref, ckv_ref, kr_ref, qd_ref, kdb_ref, vdb_ref, kd_ref, vd_ref,
                    *, a_scale, b_scale):
    h = _rms(x_ref[...], g_ref[...], NORM_EPS).astype(BF)
    y = _dot(h, w0_ref[...])
    cq, ckv = y[:, 0:384], y[:, 384:640]
    kr_m, kr_s = y[:, 640:768], y[:, 768:896]
    qd, kd, vd = y[:, 896:1408], y[:, 1408:1920], y[:, 1920:2432]
    cs, sn = cs_ref[...], sn_ref[...]
    cqn = _rms(cq, qn_ref[...], NORM_EPS).astype(BF)
    qm = _dot(cqn, wq_ref[...])
    qs = _dot(cqn, wqs_ref[...])
    ckvn = _rms(ckv, kvn_ref[...], NORM_EPS)
    ckv_ref[...] = ckvn
    krp = kr_m * cs + kr_s * sn
    kr_ref[...] = krp[:, NOPE:NOPE + ROPE]
    cb = ckvn.astype(BF)
    kn = _dot(cb, wk_ref[...])
    for hh in range(H_A):
        sl = slice(hh * HEAD_PAD, (hh + 1) * HEAD_PAD)
        qa_ref[:, sl] = ((qm[:, sl] * cs + qs[:, sl] * sn) * a_scale).astype(BF)
        ka_ref[:, sl] = (kn[:, sl] + krp).astype(BF)
    va_ref[...] = (_dot(cb, wv_ref[...]) + vone_ref[...]).astype(BF)
    qd_ref[...] = (qd * b_scale).astype(BF)
    for hh in range(H_B):
        kd_ref[:, hh, :] = kd[:, hh * V_B:(hh + 1) * V_B]
        vd_ref[:, hh, :] = vd[:, hh * V_B:(hh + 1) * V_B]
    kdb_ref[...] = kd.astype(BF)
    vdb_ref[...] = vd.astype(BF)


def _even_in(x, g, w0, qn, wq, wqs, kvn, wk, wv, vone, cs_tab, sn_tab):
    T, D = x.shape
    pos_blocks = cs_tab.shape[0] // ROW_TILE

    def full(a):
        return pl.BlockSpec(a.shape, lambda i: (0,) * a.ndim)

    def rows(*tail):
        return pl.BlockSpec((ROW_TILE,) + tail, lambda i: (i,) + (0,) * len(tail))

    pos_spec = pl.BlockSpec((ROW_TILE, LANE), lambda i: (i % pos_blocks, 0))
    outs = [((1024,), BF), ((1024,), BF), ((1024,), BF), ((KV_RANK,), F32), ((ROPE,), F32),
            ((512,), BF), ((512,), BF), ((512,), BF), ((H_B, V_B), F32), ((H_B, V_B), F32)]
    return pl.pallas_call(
        functools.partial(_even_in_kernel,
                          a_scale=(NOPE + ROPE) ** -0.5 * LOG2E, b_scale=DH_B ** -0.5 * LOG2E),
        grid=(T // ROW_TILE,),
        in_specs=[rows(D), full(g), full(w0), full(qn), full(wq), full(wqs), full(kvn), full(wk), full(wv),
                  full(vone), pos_spec, pos_spec],
        out_specs=[rows(*tail) for tail, _ in outs],
        out_shape=[jax.ShapeDtypeStruct((T,) + tail, dt) for tail, dt in outs],
        compiler_params=_cparams("parallel"),
        name="even_in",
    )(x, g, w0, qn, wq, wqs, kvn, wk, wv, vone, cs_tab, sn_tab)


def _chunk_causal_mask(tq, tk):
    row = lax.broadcasted_iota(jnp.int32, (tq, tk), 0)
    col = lax.broadcasted_iota(jnp.int32, (tq, tk), 1)
    return (col // CHUNK) <= (row // CHUNK)


def _softmax_tile(s_ref, p_ref, m_ref, l_ref, a_ref, *, add_ref=None, sub_ref=None, off=None):
    tile = s_ref.shape[0]
    reps = tile // LANE
    for rb in range(tile // SOFTMAX_ROWS):
        rs = slice(rb * SOFTMAX_ROWS, (rb + 1) * SOFTMAX_ROWS)
        s = s_ref[rs, :]
        if add_ref is not None:
            s = s + add_ref[rs, :]
        if sub_ref is not None:
            s = s - sub_ref[rs, :]
        m_old = m_ref[rs, :]
        red = jnp.broadcast_to(jnp.max(s, axis=-1, keepdims=True), m_old.shape)
        if off is not None:
            red = red - off
        m_new = jnp.maximum(m_old, red)
        shift = m_new if off is None else m_new + off
        p = jnp.exp2(s - jnp.concatenate([shift] * reps, axis=1))
        alpha = jnp.exp2(m_old - m_new)
        if l_ref is not None:
            l_ref[rs, :] = alpha * l_ref[rs, :] + jnp.broadcast_to(jnp.sum(p, axis=-1, keepdims=True), m_old.shape)
        m_ref[rs, :] = m_new
        a_ref[rs, :] = alpha
        p_ref[rs, :] = p.astype(BF)


def _flash_causal(qi, n_chains, bufs, scores, update):
    s0, s1, p0, p1, m_ref, l_ref, a_ref, acc_ref = bufs
    m_ref[...] = jnp.full(m_ref.shape, NEG_INF, F32)
    l_ref[...] = jnp.zeros(l_ref.shape, F32)
    acc_ref[...] = jnp.zeros(acc_ref.shape, F32)
    scores(0, s0)

    def pair(jj, carry):
        j = 2 * jj
        scores(j + 1, s1)
        update(j, s0, p0, False)
        scores(j + 2, s0)
        update(j + 1, s1, p1, False)
        return carry

    lax.fori_loop(0, qi // 2, pair, 0)

    @pl.when(qi % 2 == 1)
    def _():
        scores(qi, s1)
        update(qi - 1, s0, p0, False)
        s0[...] = s1[...]

    update(qi, s0, p0, True)


def _flash_scratch(n_chains, tile, acc_width=LANE):
    s = pltpu.VMEM((n_chains, tile, tile), F32)
    p = pltpu.VMEM((n_chains, tile, tile), BF)
    stat = pltpu.VMEM((n_chains, tile, LANE), F32)
    return [s, s, p, p, stat, stat, stat, pltpu.VMEM((n_chains, tile, acc_width), F32)]


def _mla_prompt_kernel(q_ref, k_ref, v_ref, o_ref, s0, s1, p0, p1, m_ref, l_ref, a_ref, acc_ref, dmask_ref,
                       *, tile):
    qi = pl.program_id(2)
    dmask_ref[...] = jnp.where(_chunk_causal_mask(tile, tile), 0.0, NEG_INF)
    sls = [slice(hh * HEAD_PAD, (hh + 1) * HEAD_PAD) for hh in range(2)]

    def scores(j, s_buf):
        start = pl.multiple_of(j * tile, tile)
        for c, sl in enumerate(sls):
            s_buf[c] = _dot_nt(q_ref[:, sl], k_ref[pl.ds(start, tile), sl])

    def update(j, s_buf, p_buf, diag):
        start = pl.multiple_of(j * tile, tile)
        for c, sl in enumerate(sls):
            _softmax_tile(s_buf.at[c], p_buf.at[c], m_ref.at[c], None, a_ref.at[c],
                          add_ref=dmask_ref if diag else None)
            acc_ref[c] = a_ref[c] * acc_ref[c] + _dot(p_buf[c], v_ref[pl.ds(start, tile), sl])

    _flash_causal(qi, 2, (s0, s1, p0, p1, m_ref, l_ref, a_ref, acc_ref), scores, update)
    lane = _lane_iota((tile, LANE))
    a0, a1 = acc_ref[0], acc_ref[1]
    l0 = jnp.sum(jnp.where(lane == V_A, a0, 0.0), axis=-1, keepdims=True)
    l1 = jnp.sum(jnp.where(lane == 0, a1, 0.0), axis=-1, keepdims=True)
    o_ref[...] = jnp.where(lane < V_A, a0 / l0, a1 / l1).astype(BF)


def _mla_prompt(qa, ka, va, *, batch, seq):
    tile = min(ATT_TILE, seq)
    nq = seq // tile
    return pl.pallas_call(
        functools.partial(_mla_prompt_kernel, tile=tile),
        grid=(batch, H_A // 2, nq),
        in_specs=[pl.BlockSpec((tile, 2 * HEAD_PAD), lambda b, p, qi: (b * nq + qi, p)),
                  pl.BlockSpec((seq, 2 * HEAD_PAD), lambda b, p, qi: (b, p)),
                  pl.BlockSpec((seq, 2 * HEAD_PAD), lambda b, p, qi: (b, p))],
        out_specs=pl.BlockSpec((tile, LANE), lambda b, p, qi: (b * nq + qi, p)),
        out_shape=jax.ShapeDtypeStruct((batch * seq, H_A * V_A), BF),
        scratch_shapes=_flash_scratch(2, tile) + [pltpu.VMEM((tile, tile), F32)],
        compiler_params=_cparams("parallel", "parallel", "arbitrary"),
        name="mla_prompt",
    )(qa, ka, va)


def _diff_prompt_kernel(slopes_ref, lamv_ref, subln_ref, q_ref, k_ref, v_ref, o_ref,
                        s0, s1, p0, p1, m_ref, l_ref, a_ref, acc_ref, pen_ref, dbias_ref, q2_ref,
                        *, tile, lam_init):
    h = pl.program_id(1)
    qi = pl.program_id(2)
    slope = slopes_ref[h]
    q2_ref[...] = _split_halves(q_ref[...])
    row = lax.broadcasted_iota(jnp.int32, (tile, tile), 0)
    col = lax.broadcasted_iota(jnp.int32, (tile, tile), 1)
    pen = slope * (row - col).astype(F32)
    pen_ref[...] = pen
    dbias_ref[...] = jnp.where(_chunk_causal_mask(tile, tile), -jnp.abs(pen), NEG_INF)

    def scores(j, s_buf):
        k = k_ref[pl.ds(pl.multiple_of(j * tile, tile), tile), :]
        for c in range(2):
            s_buf[c] = _dot_nt(q2_ref[c * tile:(c + 1) * tile, :], k)

    def update(j, s_buf, p_buf, diag):
        start = pl.multiple_of(j * tile, tile)
        off = None if diag else slope * ((qi - j) * tile).astype(F32)
        for c in range(2):
            _softmax_tile(s_buf.at[c], p_buf.at[c], m_ref.at[c], l_ref.at[c], a_ref.at[c],
                          add_ref=dbias_ref if diag else None, sub_ref=None if diag else pen_ref, off=off)
            acc_ref[c] = a_ref[c] * acc_ref[c] + _dot(p_buf[c], v_ref[pl.ds(start, tile), :])

    _flash_causal(qi, 2, (s0, s1, p0, p1, m_ref, l_ref, a_ref, acc_ref), scores, update)
    lam = _diff_lambda(lamv_ref[...], lam_init)
    o_ref[...] = _diff_finish(acc_ref[0] / l_ref[0], acc_ref[1] / l_ref[1], lam, subln_ref[...],
                              lam_init).astype(BF)


def _diff_prompt(slopes, lamv, subln, qd, kdb, vdb, *, batch, seq, lam_init):
    tile = min(ATT_TILE, seq)
    nq = seq // tile
    return pl.pallas_call(
        functools.partial(_diff_prompt_kernel, tile=tile, lam_init=lam_init),
        grid=(batch, H_B, nq),
        in_specs=[pl.BlockSpec(memory_space=pltpu.SMEM),
                  pl.BlockSpec(lamv.shape, lambda b, h, qi: (0, 0)),
                  pl.BlockSpec(subln.shape, lambda b, h, qi: (0, 0)),
                  pl.BlockSpec((tile, LANE), lambda b, h, qi: (b * nq + qi, h)),
                  pl.BlockSpec((seq, LANE), lambda b, h, qi: (b, h)),
                  pl.BlockSpec((seq, LANE), lambda b, h, qi: (b, h))],
        out_specs=pl.BlockSpec((tile, LANE), lambda b, h, qi: (b * nq + qi, h)),
        out_shape=jax.ShapeDtypeStruct((batch * seq, H_B * V_B), BF),
        scratch_shapes=_flash_scratch(2, tile) + [pltpu.VMEM((tile, tile), F32), pltpu.VMEM((tile, tile), F32),
                                                  pltpu.VMEM((2 * tile, LANE), BF)],
        compiler_params=_cparams("parallel", "parallel", "arbitrary"),
        name="diff_prompt",
    )(slopes, lamv, subln, qd, kdb, vdb)


def _even_sample_kernel(slopes_ref, lamv_ref, subln_ref, wk_ref, wv_ref, vone_ref, e_ref,
                        qa_ref, ka_ref, va_ref, ckv_ref, kr_ref,
                        qd_ref, kdb_ref, vdb_ref, ck_ref, cv_ref,
                        oa_ref, ob_ref, *, lam_init):
    n_new = qa_ref.shape[0]
    n_past = ckv_ref.shape[0]
    ckvp = ckv_ref[...].astype(BF)
    krp = kr_ref[...].astype(BF)
    lane = _lane_iota((n_new, LANE))
    for pr in range(H_A // 2):
        res = []
        for hh in range(2):
            sl = slice((2 * pr + hh) * HEAD_PAD, (2 * pr + hh + 1) * HEAD_PAD)
            q = qa_ref[:, sl]
            kp = (_dot(ckvp, wk_ref[:, sl]) + _dot(krp, e_ref[:, sl])).astype(BF)
            vp = (_dot(ckvp, wv_ref[:, sl]) + vone_ref[:, sl]).astype(BF)
            res.append(_softmax_pv([_dot_nt(q, kp), _dot_nt(q, ka_ref[:, sl])], [vp, va_ref[:, sl]]))
        oa_ref[:, pr * LANE:(pr + 1) * LANE] = jnp.where(lane < V_A, res[0], res[1]).astype(BF)
    rowp = lax.broadcasted_iota(jnp.int32, (n_new, n_past), 0)
    colp = lax.broadcasted_iota(jnp.int32, (n_new, n_past), 1)
    dist_p = (rowp - colp + n_past).astype(F32)
    dist_p = jnp.concatenate([dist_p, dist_p], axis=0)
    rown = lax.broadcasted_iota(jnp.int32, (n_new, n_new), 0)
    coln = lax.broadcasted_iota(jnp.int32, (n_new, n_new), 1)
    dist_n = jnp.abs(rown - coln).astype(F32)
    dist_n = jnp.concatenate([dist_n, dist_n], axis=0)
    lam = _diff_lambda(lamv_ref[...], lam_init)
    for h in range(H_B):
        sl = slice(h * LANE, (h + 1) * LANE)
        slope = slopes_ref[h]
        q2x = _split_halves(qd_ref[:, sl])
        kp = ck_ref[:, h, :].astype(BF)
        vp = cv_ref[:, h, :].astype(BF)
        s_p = _dot_nt(q2x, kp) - slope * dist_p
        s_n = _dot_nt(q2x, kdb_ref[:, sl]) - slope * dist_n
        o = _softmax_pv([s_p, s_n], [vp, vdb_ref[:, sl]])
        ob_ref[:, sl] = _diff_finish(o[:n_new], o[n_new:], lam, subln_ref[...], lam_init).astype(BF)


def _even_sample(slopes, lamv, subln, wk, wv, vone, e_mat, qa, ka, va, ckv_c, kr_c, qd, kdb, vdb, ck_c, cv_c,
                 *, n_prompt_rows, dec_seq, lam_init):
    dec_batch, n_past = ckv_c.shape[0], ckv_c.shape[1]
    base = n_prompt_rows // dec_seq

    def full(a):
        return pl.BlockSpec(a.shape, lambda s: (0,) * a.ndim)

    def new(width):
        return pl.BlockSpec((dec_seq, width), lambda s: (base + s, 0))

    def cache(*tail):
        return pl.BlockSpec((None, n_past) + tail, lambda s: (s, 0) + (0,) * len(tail))

    return pl.pallas_call(
        functools.partial(_even_sample_kernel, lam_init=lam_init),
        grid=(dec_batch,),
        in_specs=[pl.BlockSpec(memory_space=pltpu.SMEM), full(lamv), full(subln), full(wk), full(wv), full(vone),
                  full(e_mat), new(1024), new(1024), new(1024), cache(KV_RANK), cache(ROPE),
                  new(512), new(512), new(512), cache(H_B, V_B), cache(H_B, V_B)],
        out_specs=[pl.BlockSpec((dec_seq, 512), lambda s: (s, 0))] * 2,
        out_shape=[jax.ShapeDtypeStruct((dec_batch * dec_seq, 512), BF)] * 2,
        compiler_params=_cparams("parallel"),
        name="even_sample",
    )(slopes, lamv, subln, wk, wv, vone, e_mat, qa, ka, va, ckv_c, kr_c, qd, kdb, vdb, ck_c, cv_c)


def _route(logits, carry):
    tm = logits.shape[0]
    lane = _lane_iota(logits.shape).astype(F32)
    big = float(LANE)
    g_mask = lane < N_GROUPS
    gl = jnp.where(g_mask, logits, NEG_INF)
    gmax = jnp.max(gl, axis=-1, keepdims=True)
    g_sel = jnp.min(jnp.where(gl == gmax, lane, big), axis=-1, keepdims=True)
    p_grp = 1.0 / jnp.sum(jnp.exp(gl - gmax), axis=-1, keepdims=True)
    lo = ROUTE_OFF + EPG * g_sel
    el = jnp.where((lane >= lo) & (lane < lo + EPG), logits, NEG_INF)
    v1 = jnp.max(el, axis=-1, keepdims=True)
    i1 = jnp.min(jnp.where(el == v1, lane, big), axis=-1, keepdims=True)
    el2 = jnp.where(lane == i1, NEG_INF, el)
    v2 = jnp.max(el2, axis=-1, keepdims=True)
    i2 = jnp.min(jnp.where(el2 == v2, lane, big), axis=-1, keepdims=True)
    ex = jnp.exp(v2 - v1)
    den = 1.0 + ex
    gate1 = (1.0 / den) * p_grp
    gate2 = (ex / den) * p_grp
    onehot = jnp.where((lane == i1) | (lane == i2), 1.0, 0.0)
    row = lax.broadcasted_iota(jnp.int32, (tm, tm), 0)
    col = lax.broadcasted_iota(jnp.int32, (tm, tm), 1)
    tri = jnp.where(row > col, 1.0, 0.0).astype(BF)
    cum = _dot(tri, onehot.astype(BF)) + carry
    r1 = jnp.sum(jnp.where(lane == i1, cum, 0.0), axis=-1, keepdims=True)
    r2 = jnp.sum(jnp.where(lane == i2, cum, 0.0), axis=-1, keepdims=True)
    packed = jnp.zeros_like(logits)
    for pos, val in enumerate([i1 - ROUTE_OFF, i2 - ROUTE_OFF, gate1, gate2, r1, r2]):
        packed = jnp.where(lane == pos, val, packed)
    return packed, carry + jnp.sum(onehot, axis=0, keepdims=True)


def _out_router_kernel(*refs, splits, n_prompt_tiles):
    i = pl.program_id(0)
    offs = [sum(splits[:k]) for k in range(len(splits))]
    n_mix = len(splits) - 1

    def pick(k):
        parts = refs[offs[k]:offs[k] + splits[k]]
        if splits[k] == 1:
            return parts[0][...]
        return jnp.where(i < n_prompt_tiles, parts[0][...], parts[1][...])

    rest = refs[sum(splits):]
    w_refs = rest[:n_mix]
    g_ref, wr_ref, br_ref, x1_ref, xn_ref, route_ref, cnt_ref, carry_ref = rest[n_mix:]

    @pl.when(i == 0)
    def _():
        carry_ref[...] = jnp.zeros_like(carry_ref)

    x1 = pick(0)
    for k, w_ref in enumerate(w_refs):
        x1 = x1 + _dot(pick(1 + k), w_ref[...])
    x1_ref[...] = x1
    xb = _rms(x1, g_ref[...], NORM_EPS).astype(BF)
    xn_ref[...] = xb
    logits = _dot(xb, wr_ref[...]) + br_ref[...]
    packed, carry = _route(logits, carry_ref[...])
    route_ref[...] = packed
    carry_ref[...] = carry
    cnt_ref[...] = carry


def _out_router(row_inputs, ws, g, wr, br):
    splits = tuple(len(parts) for parts in row_inputs)
    T = sum(a.shape[0] for a in row_inputs[0])
    D = row_inputs[0][0].shape[1]
    n_p = max([parts[0].shape[0] // ROW_TILE for parts in row_inputs if len(parts) == 2], default=0)

    def row_specs(parts):
        if len(parts) == 1:
            return [pl.BlockSpec((ROW_TILE, parts[0].shape[1]), lambda i: (i, 0))]
        return [pl.BlockSpec((ROW_TILE, parts[0].shape[1]), lambda i: (jnp.minimum(i, n_p - 1), 0)),
                pl.BlockSpec((ROW_TILE, parts[1].shape[1]), lambda i: (jnp.maximum(i - n_p, 0), 0))]

    def full(a):
        return pl.BlockSpec(a.shape, lambda i: (0,) * a.ndim)

    def rows(width):
        return pl.BlockSpec((ROW_TILE, width), lambda i: (i, 0))

    flat_rows = [a for parts in row_inputs for a in parts]
    return pl.pallas_call(
        functools.partial(_out_router_kernel, splits=splits, n_prompt_tiles=n_p),
        grid=(T // ROW_TILE,),
        in_specs=[s for parts in row_inputs for s in row_specs(parts)] + [full(w) for w in ws]
        + [full(g), full(wr), full(br)],
        out_specs=[rows(D), rows(D), rows(LANE), pl.BlockSpec((1, LANE), lambda i: (0, 0))],
        out_shape=[jax.ShapeDtypeStruct((T, D), F32), jax.ShapeDtypeStruct((T, D), BF),
                   jax.ShapeDtypeStruct((T, LANE), F32), jax.ShapeDtypeStruct((1, LANE), F32)],
        scratch_shapes=[pltpu.VMEM((1, LANE), F32)],
        compiler_params=_cparams("arbitrary"),
        name="out_router",
    )(*flat_rows, *ws, g, wr, br)


def _experts_kernel(be_ref, nu_ref, xb_ref, wg_ref, wu_ref, wd_ref, y_ref, wgb_ref, wub_ref, wdb_ref):
    i = pl.program_id(0)
    used = i < nu_ref[0]

    @pl.when(used & ((i == 0) | (be_ref[i] != be_ref[jnp.maximum(i - 1, 0)])))
    def _():
        wgb_ref[...] = wg_ref[...].astype(BF)
        wub_ref[...] = wu_ref[...].astype(BF)
        wdb_ref[...] = wd_ref[...].astype(BF)

    @pl.when(used)
    def _():
        xb = xb_ref[...]
        a = _dot(xb, wgb_ref[...])
        b = _dot(xb, wub_ref[...])
        hid = (a * jax.nn.sigmoid(a)) * b
        y_ref[...] = _dot(hid.astype(BF), wdb_ref[...])

    @pl.when(jnp.logical_not(used))
    def _():
        y_ref[...] = jnp.zeros_like(y_ref)


def _experts(block_expert, n_used, xb, wg, wu, wd, *, layer):
    L, D = xb.shape
    n_blocks = L // MOE_BLOCK
    grid_spec = pltpu.PrefetchScalarGridSpec(
        num_scalar_prefetch=2,
        grid=(n_blocks,),
        in_specs=[pl.BlockSpec((MOE_BLOCK, D), lambda i, be, nu: (i, 0)),
                  pl.BlockSpec((None, None, D, D_EXPERT), lambda i, be, nu: (layer, be[i], 0, 0)),
                  pl.BlockSpec((None, None, D, D_EXPERT), lambda i, be, nu: (layer, be[i], 0, 0)),
                  pl.BlockSpec((None, None, D_EXPERT, D), lambda i, be, nu: (layer, be[i], 0, 0))],
        out_specs=pl.BlockSpec((MOE_BLOCK, D), lambda i, be, nu: (i, 0)),
        scratch_shapes=[pltpu.VMEM((D, D_EXPERT), BF), pltpu.VMEM((D, D_EXPERT), BF), pltpu.VMEM((D_EXPERT, D), BF)],
    )
    return pl.pallas_call(
        _experts_kernel,
        grid_spec=grid_spec,
        out_shape=jax.ShapeDtypeStruct((L, D), F32),
        compiler_params=_cparams("arbitrary"),
        name="experts",
    )(block_expert, n_used, xb, wg, wu, wd)


def _moe(xn, route, cnt, wg, wu, wd, *, layer):
    T = xn.shape[0]
    e = route[:, 0:2].astype(jnp.int32)
    rank = route[:, 4:6].astype(jnp.int32)
    counts = cnt[0, ROUTE_OFF:ROUTE_OFF + N_EXPERTS].astype(jnp.int32)
    padded = ((counts + MOE_BLOCK - 1) // MOE_BLOCK) * MOE_BLOCK
    pend = jnp.cumsum(padded)
    pstart = pend - padded
    dest = jnp.sum(jnp.where(e[:, :, None] == jnp.arange(N_EXPERTS)[None, None, :], pstart[None, None, :], 0),
                   axis=-1) + rank
    n_blocks = -(-(2 * T) // MOE_BLOCK) + N_EXPERTS
    L = n_blocks * MOE_BLOCK
    tok = jnp.repeat(jnp.arange(T, dtype=jnp.int32), 2)
    buf_tok = (jnp.arange(L, dtype=jnp.int32) % T).at[dest.reshape(-1)].set(
        tok, unique_indices=True, mode="promise_in_bounds")
    block_start = jnp.arange(n_blocks, dtype=jnp.int32) * MOE_BLOCK
    block_expert = jnp.minimum(jnp.sum(pend[None, :] <= block_start[:, None], axis=1), N_EXPERTS - 1).astype(jnp.int32)
    n_used = (pend[-1:] // MOE_BLOCK).astype(jnp.int32)
    xb = xn.at[buf_tok].get(mode="promise_in_bounds")
    yb = _experts(block_expert, n_used, xb, wg, wu, wd, layer=layer)
    return (yb.at[dest[:, 0]].get(mode="promise_in_bounds"), yb.at[dest[:, 1]].get(mode="promise_in_bounds"))


def _gates(route):
    lane = _lane_iota(route.shape)
    g0 = jnp.sum(jnp.where(lane == 2, route, 0.0), axis=-1, keepdims=True)
    g1 = jnp.sum(jnp.where(lane == 3, route, 0.0), axis=-1, keepdims=True)
    return g0, g1


def _odd_in_kernel(x_ref, y0_ref, y1_ref, route_ref, g_ref, w_ref, x2_ref, q_ref, k_ref, v_ref, *, scale):
    g0, g1 = _gates(route_ref[...])
    x2 = x_ref[...] + (y0_ref[...] * g0 + y1_ref[...] * g1)
    x2_ref[...] = x2
    h = _rms(x2, g_ref[...], NORM_EPS).astype(BF)
    y = _dot(h, w_ref[...])
    n = q_ref.shape[1]
    q_ref[...] = (y[:, :n] * scale).astype(BF)
    k_ref[...] = y[:, n:2 * n].astype(BF)
    v_ref[...] = y[:, 2 * n:].astype(BF)


def _odd_in(x1, y0, y1, route, g, w):
    T, D = x1.shape
    n = w.shape[1] // 3

    def rows(width):
        return pl.BlockSpec((ROW_TILE, width), lambda i: (i, 0))

    def full(a):
        return pl.BlockSpec(a.shape, lambda i: (0,) * a.ndim)

    return pl.pallas_call(
        functools.partial(_odd_in_kernel, scale=DH_C ** -0.5 * LOG2E),
        grid=(T // ROW_TILE,),
        in_specs=[rows(D), rows(D), rows(D), rows(LANE), full(g), full(w)],
        out_specs=[rows(D), rows(n), rows(n), rows(n)],
        out_shape=[jax.ShapeDtypeStruct((T, D), F32)] + [jax.ShapeDtypeStruct((T, n), BF)] * 3,
        compiler_params=_cparams("parallel"),
        name="odd_in",
    )(x1, y0, y1, route, g, w)


def _state_rows_kernel(ids_ref, x_ref, g_ref, w_ref, o_ref):
    del ids_ref
    h = _rms(x_ref[...], g_ref[...], NORM_EPS).astype(BF)
    o_ref[...] = _dot(h, w_ref[...])


def _state_rows(tile_ids, x, g, w):
    D = x.shape[1]
    n = tile_ids.shape[0]
    grid_spec = pltpu.PrefetchScalarGridSpec(
        num_scalar_prefetch=1,
        grid=(n,),
        in_specs=[pl.BlockSpec((ROW_TILE, D), lambda i, ids: (ids[i], 0)),
                  pl.BlockSpec(g.shape, lambda i, ids: (0, 0)),
                  pl.BlockSpec(w.shape, lambda i, ids: (0, 0))],
        out_specs=pl.BlockSpec((ROW_TILE, w.shape[1]), lambda i, ids: (i, 0)),
    )
    return pl.pallas_call(
        _state_rows_kernel,
        grid_spec=grid_spec,
        out_shape=jax.ShapeDtypeStruct((n * ROW_TILE, w.shape[1]), F32),
        compiler_params=_cparams("parallel"),
        name="state_rows",
    )(tile_ids, x, g, w)


def _band_prompt_kernel(q_ref, k_ref, v_ref, bias_ref, o_ref, *, tile, tiles_per_step):
    lane = _lane_iota((tile, LANE))

    def body(t, _):
        qi = pl.program_id(2) * tiles_per_step + t
        q0 = pl.multiple_of(t * tile, tile)
        q2x = _split_halves(q_ref[pl.ds(q0, tile), :])
        starts, valids = [], []
        for kt in range(3):
            start = (qi + kt - 2) * tile
            valids.append(start >= 0)
            starts.append(pl.multiple_of(jnp.maximum(start, 0), tile))
        outs = []
        for hh in range(2):
            q = q2x[hh * tile:(hh + 1) * tile]
            s_list = []
            for kt in range(3):
                s = _dot_nt(q, k_ref[pl.ds(starts[kt], tile), :])
                s = s + bias_ref[hh * tile:(hh + 1) * tile, kt * tile:(kt + 1) * tile]
                if kt < 2:
                    s = jnp.where(valids[kt], s, NEG_INF)
                s_list.append(s)
            outs.append(_softmax_pv(s_list, [v_ref[pl.ds(st, tile), :] for st in starts]))
        o_ref[pl.ds(q0, tile), :] = jnp.where(lane < DH_C, outs[0], outs[1]).astype(BF)
        return 0

    lax.fori_loop(0, tiles_per_step, body, 0)


def _band_prompt(q, k, v, bias, *, batch, seq):
    tile = BAND_TILE
    tps = min(BAND_TILES_PER_STEP, seq // tile)
    nq = seq // (tile * tps)
    return pl.pallas_call(
        functools.partial(_band_prompt_kernel, tile=tile, tiles_per_step=tps),
        grid=(H_C // 2, batch, nq),
        in_specs=[pl.BlockSpec((tile * tps, LANE), lambda p, b, qi: (b * nq + qi, p)),
                  pl.BlockSpec((seq, LANE), lambda p, b, qi: (b, p)),
                  pl.BlockSpec((seq, LANE), lambda p, b, qi: (b, p)),
                  pl.BlockSpec((None, 2 * tile, 3 * tile), lambda p, b, qi: (p, 0, 0))],
        out_specs=pl.BlockSpec((tile * tps, LANE), lambda p, b, qi: (b * nq + qi, p)),
        out_shape=jax.ShapeDtypeStruct((batch * seq, H_C * DH_C), BF),
        compiler_params=_cparams("parallel", "parallel", "arbitrary"),
        name="band_prompt",
    )(q, k, v, bias)


def _band_sample_kernel(q_ref, k_ref, v_ref, ck_ref, cv_ref, bias_ref, o_ref):
    n_new = q_ref.shape[0]
    n_past = ck_ref.shape[0]
    lane = _lane_iota((n_new, LANE))
    for pr in range(H_C // 2):
        sl = slice(pr * LANE, (pr + 1) * LANE)
        q2x = _split_halves(q_ref[:, sl])
        s_p = _dot_nt(q2x, ck_ref[:, sl].astype(BF)) + bias_ref[pr, :, 0:n_past]
        s_n = _dot_nt(q2x, k_ref[:, sl]) + bias_ref[pr, :, n_past:n_past + n_new]
        o = _softmax_pv([s_p, s_n], [cv_ref[:, sl].astype(BF), v_ref[:, sl]])
        o_ref[:, sl] = jnp.where(lane < DH_C, o[:n_new], o[n_new:]).astype(BF)


def _band_sample(q, k, v, ck, cv, bias, *, n_prompt_rows, dec_seq):
    dec_batch, n_past, width = ck.shape
    base = n_prompt_rows // dec_seq
    new = pl.BlockSpec((dec_seq, width), lambda s: (base + s, 0))
    cache = pl.BlockSpec((None, n_past, width), lambda s: (s, 0, 0))
    return pl.pallas_call(
        _band_sample_kernel,
        grid=(dec_batch,),
        in_specs=[new, new, new, cache, cache, pl.BlockSpec(bias.shape, lambda s: (0, 0, 0))],
        out_specs=pl.BlockSpec((dec_seq, width), lambda s: (s, 0)),
        out_shape=jax.ShapeDtypeStruct((dec_batch * dec_seq, width), BF),
        compiler_params=_cparams("parallel"),
        name="band_sample",
    )(q, k, v, ck, cv, bias)


def _final_kernel(x_ref, y0_ref, y1_ref, route_ref, g_ref, o_ref):
    g0, g1 = _gates(route_ref[...])
    x = x_ref[...] + (y0_ref[...] * g0 + y1_ref[...] * g1)
    o_ref[...] = _rms(x, g_ref[...], NORM_EPS)


def _final(x, y0, y1, route, g, *, first_tile, n_tiles):
    D = x.shape[1]

    def rows(width):
        return pl.BlockSpec((ROW_TILE, width), lambda i: (first_tile + i, 0))

    return pl.pallas_call(
        _final_kernel,
        grid=(n_tiles,),
        in_specs=[rows(D), rows(D), rows(D), rows(LANE), pl.BlockSpec(g.shape, lambda i: (0, 0))],
        out_specs=pl.BlockSpec((ROW_TILE, D), lambda i: (i, 0)),
        out_shape=jax.ShapeDtypeStruct((n_tiles * ROW_TILE, D), F32),
        compiler_params=_cparams("parallel"),
        name="final_norm",
    )(x, y0, y1, route, g)


def _prep_even_weights(w_in, w_qup, w_kvup):
    D = w_in.shape[0]
    a_in = Q_RANK + KV_RANK + ROPE
    bq = H_B * 2 * DH_B
    wcq, wckv, wkr = w_in[:, :Q_RANK], w_in[:, Q_RANK:Q_RANK + KV_RANK], w_in[:, Q_RANK + KV_RANK:a_in]
    wqd, wkd, wvd = w_in[:, a_in:a_in + bq], w_in[:, a_in + bq:a_in + 2 * bq], w_in[:, a_in + 2 * bq:]
    half = ROPE // 2

    def z(n):
        return jnp.zeros((D, n), w_in.dtype)

    kr_m = jnp.concatenate([z(NOPE), wkr, z(HEAD_PAD - NOPE - ROPE)], axis=1)
    kr_s = jnp.concatenate([z(NOPE), wkr[:, half:], wkr[:, :half], z(HEAD_PAD - NOPE - ROPE)], axis=1)
    w0 = jnp.concatenate([wcq, wckv, kr_m, kr_s, wqd, wkd, wvd], axis=1).astype(BF)
    wq3 = w_qup.reshape(Q_RANK, H_A, NOPE + ROPE)
    nope, r1, r2 = wq3[:, :, :NOPE], wq3[:, :, NOPE:NOPE + half], wq3[:, :, NOPE + half:]
    zq = jnp.zeros((Q_RANK, H_A, HEAD_PAD - NOPE - ROPE), w_qup.dtype)
    wq = jnp.concatenate([nope, r1, r2, zq], axis=-1).reshape(Q_RANK, H_A * HEAD_PAD).astype(BF)
    wqs = jnp.concatenate([jnp.zeros_like(nope), r2, r1, zq], axis=-1).reshape(Q_RANK, H_A * HEAD_PAD).astype(BF)
    wkv3 = w_kvup.reshape(KV_RANK, H_A, NOPE + V_A)
    wk = jnp.concatenate([wkv3[:, :, :NOPE], jnp.zeros((KV_RANK, H_A, HEAD_PAD - NOPE), w_kvup.dtype)],
                         axis=-1).reshape(KV_RANK, H_A * HEAD_PAD).astype(BF)
    wv4 = wkv3[:, :, NOPE:].reshape(KV_RANK, H_A // 2, 2, V_A)
    zv = jnp.zeros((KV_RANK, H_A // 2, HEAD_PAD - V_A), w_kvup.dtype)
    wv = jnp.concatenate([wv4[:, :, 0], zv, zv, wv4[:, :, 1]], axis=-1).reshape(KV_RANK, H_A * HEAD_PAD).astype(BF)
    vone = jnp.zeros((H_A // 2, 2 * HEAD_PAD), F32).at[:, V_A].set(1.0).at[:, HEAD_PAD].set(1.0)
    r = jnp.arange(ROPE)
    e_mat = jnp.zeros((ROPE, H_A, HEAD_PAD), F32).at[r[:, None], jnp.arange(H_A)[None, :], NOPE + r[:, None]].set(1.0)
    return w0, wq, wqs, wk, wv, vone.reshape(1, H_A * HEAD_PAD), e_mat.reshape(ROPE, H_A * HEAD_PAD).astype(BF)


def _rope_tables(pos):
    half = ROPE // 2
    inv = jnp.power(ROPE_BASE, -jnp.arange(half, dtype=F32) / half)
    ang = pos.astype(F32)[:, None] * inv[None, :]
    c, s = jnp.cos(ang), jnp.sin(ang)
    n = pos.shape[0]
    pad = jnp.zeros((n, HEAD_PAD - NOPE - ROPE), F32)
    cs = jnp.concatenate([jnp.ones((n, NOPE), F32), c, c, pad], axis=1)
    sn = jnp.concatenate([jnp.zeros((n, NOPE), F32), -s, s, pad], axis=1)
    return cs, sn


def _band_bias_tiles(table, n_rows, n_keys, key_offset, masked):
    i = jnp.arange(n_rows)[:, None]
    kpos = jnp.arange(n_keys)[None, :] - key_offset
    d_max = n_rows - 1 + key_offset
    d = d_max - jnp.arange(n_rows + n_keys - 1)
    rev = (table.astype(F32) * LOG2E)[:, jnp.clip(d, -REL_CLIP, REL_CLIP) + REL_CLIP]
    bias = jnp.stack([rev[:, n_rows - 1 - r:n_rows - 1 - r + n_keys] for r in range(n_rows)], axis=1)
    if masked:
        qc, kc = i // CHUNK, jnp.floor_divide(kpos, CHUNK)
        vis = (kc <= qc) & (kc >= qc - LEFT_CHUNKS)
        bias = jnp.where(vis[None], bias, NEG_INF)
    return bias.reshape(H_C // 2, 2 * n_rows, n_keys)


def kernel(x_prompt, x_sample, cache_mla_ckv, cache_mla_krope, cache_diff_k, cache_diff_v, cache_band_k, cache_band_v, ln_mix, w_in_even, mla_q_norm, mla_w_qup, mla_kv_norm, mla_w_kvup, diff_lam_q1, diff_lam_k1, diff_lam_q2, diff_lam_k2, diff_subln, w_out_even, w_in_odd, band_rel_bias, w_out_odd, ln_ffn, moe_w_group, moe_b_group, moe_w_router, moe_b_router, moe_w_gate, moe_w_up, moe_w_down, ln_final):
    B, S, D = x_prompt.shape
    DB, DS, _ = x_sample.shape
    n_past = cache_mla_ckv.shape[2]
    c_past = cache_band_k.shape[2]
    assert ln_mix.shape[0] == 2 and S % min(ATT_TILE, S) == 0 and S % BAND_TILE == 0
    assert (B * S) % ROW_TILE == 0 and (DB * DS) % ROW_TILE == 0 and ROW_TILE % DS == 0 and DS == CHUNK
    assert n_past % CHUNK == 0 and c_past == LEFT_CHUNKS * CHUNK and S >= c_past and c_past % ROW_TILE == 0
    BS, NS = B * S, DB * DS
    T = BS + NS
    n_p, n_s = BS // ROW_TILE, NS // ROW_TILE
    xp, xs = x_prompt.reshape(BS, D), x_sample.reshape(NS, D)
    row = lambda a: a.reshape(1, -1)

    w0, wq, wqs, wk, wv, vone, e_mat = _prep_even_weights(w_in_even[0], mla_w_qup[0], mla_w_kvup[0])
    cs_p, sn_p = _rope_tables(jnp.arange(S))
    cs_s, sn_s = _rope_tables(n_past + jnp.arange(DS))
    reps = ROW_TILE // DS
    even_w = (row(ln_mix[0]), w0, row(mla_q_norm[0]), wq, wqs, row(mla_kv_norm[0]), wk, wv, vone)
    qa, ka, va, ckv_p, kr_p, qd, kdb, vdb, kd_p, vd_p = _even_in(xp, *even_w, cs_p, sn_p)
    qa_s, ka_s, va_s, ckv_s, kr_s, qd_s, kdb_s, vdb_s, kd_s, vd_s = _even_in(
        xs, *even_w, jnp.tile(cs_s, (reps, 1)), jnp.tile(sn_s, (reps, 1)))
    slopes = jnp.exp2(-8.0 * jnp.arange(1, H_B + 1, dtype=F32) / H_B) * LOG2E
    lamv = jnp.stack([diff_lam_q1[0], diff_lam_k1[0], diff_lam_q2[0], diff_lam_k2[0]]).astype(F32)
    subln = row(diff_subln[0])
    lam_init = 0.8 - 0.6 * math.exp(-0.3 * 0)
    oa_p = _mla_prompt(qa, ka, va, batch=B, seq=S)
    ob_p = _diff_prompt(slopes, lamv, subln, qd, kdb, vdb, batch=B, seq=S, lam_init=lam_init)
    oa_s, ob_s = _even_sample(
        slopes, lamv, subln, wk, wv, vone, e_mat, qa_s, ka_s, va_s, cache_mla_ckv[0], cache_mla_krope[0],
        qd_s, kdb_s, vdb_s, cache_diff_k[0], cache_diff_v[0],
        n_prompt_rows=0, dec_seq=DS, lam_init=lam_init)

    def router_weights(l):
        wr = jnp.concatenate([moe_w_group[l], moe_w_router[l],
                              jnp.zeros((D, LANE - N_GROUPS - N_EXPERTS), F32)], axis=1).astype(BF)
        br = jnp.concatenate([moe_b_group[l], moe_b_router[l],
                              jnp.zeros((LANE - N_GROUPS - N_EXPERTS,), F32)]).astype(F32)
        return wr, row(br)

    n_a = H_A * V_A
    wo = w_out_even[0].astype(BF)
    wr, br = router_weights(0)
    x1, xn, route, cnt = _out_router([(xp, xs), (oa_p, oa_s), (ob_p, ob_s)], [wo[:n_a], wo[n_a:]],
                                     row(ln_ffn[0]), wr, br)
    y0, y1 = _moe(xn, route, cnt, moe_w_gate, moe_w_up, moe_w_down, layer=0)

    w_odd = w_in_odd[0].astype(BF)
    n_c = H_C * DH_C
    x2, qc, kc, vc = _odd_in(x1, y0, y1, route, row(ln_mix[1]), w_odd)
    tail = c_past // ROW_TILE
    tiles_per_seq = S // ROW_TILE
    tile_ids = jnp.concatenate(
        [jnp.arange(tiles_per_seq - tail, tiles_per_seq, dtype=jnp.int32) + b * tiles_per_seq for b in range(B)]
        + [jnp.arange(n_p, n_p + n_s, dtype=jnp.int32)])
    st = _state_rows(tile_ids, x2, row(ln_mix[1]), w_odd[:, n_c:])
    bias_p = _band_bias_tiles(band_rel_bias[0], BAND_TILE, 3 * BAND_TILE, 2 * BAND_TILE, True)
    bias_s = _band_bias_tiles(band_rel_bias[0], DS, c_past + DS, c_past, False)
    oc_p = _band_prompt(qc, kc, vc, bias_p, batch=B, seq=S)
    oc_s = _band_sample(qc, kc, vc, cache_band_k[0].reshape(DB, c_past, n_c), cache_band_v[0].reshape(DB, c_past, n_c),
                        bias_s, n_prompt_rows=BS, dec_seq=DS)
    wr, br = router_weights(1)
    x3, xn, route, cnt = _out_router([(x2,), (oc_p, oc_s)], [w_out_odd[0].astype(BF)], row(ln_ffn[1]), wr, br)
    y0, y1 = _moe(xn, route, cnt, moe_w_gate, moe_w_up, moe_w_down, layer=1)
    g_fin = row(ln_final)
    y_prompt = _final(x3, y0, y1, route, g_fin, first_tile=0, n_tiles=n_p).reshape(B, S, D)
    y_sample = _final(x3, y0, y1, route, g_fin, first_tile=n_p, n_tiles=n_s).reshape(DB, DS, D)

    def shaped(a_p, a_s, *tail_shape):
        return a_p.reshape(1, B, S, *tail_shape), a_s.reshape(1, DB, DS, *tail_shape)

    ckv_p, ckv_s = shaped(ckv_p, ckv_s, KV_RANK)
    kr_p, kr_s = shaped(kr_p, kr_s, ROPE)
    kd_p, kd_s = shaped(kd_p, kd_s, H_B, 2 * DH_B)
    vd_p, vd_s = shaped(vd_p, vd_s, H_B, V_B)
    n_tail = B * c_past
    bk_p = st[:n_tail, :n_c].reshape(1, B, c_past, H_C, DH_C)
    bv_p = st[:n_tail, n_c:].reshape(1, B, c_past, H_C, DH_C)
    k_new = st[n_tail:, :n_c].reshape(DB, DS, H_C, DH_C)
    v_new = st[n_tail:, n_c:].reshape(DB, DS, H_C, DH_C)
    bk_s = jnp.concatenate([cache_band_k[0][:, DS:], k_new], axis=1)[None]
    bv_s = jnp.concatenate([cache_band_v[0][:, DS:], v_new], axis=1)[None]
    return (y_prompt, y_sample, ckv_p, kr_p, kd_p, vd_p, bk_p, bv_p, ckv_s, kr_s, kd_s, vd_s, bk_s, bv_s)
```

```python
import functools
import math

import jax
import jax.numpy as jnp
from jax import lax
from jax.experimental import pallas as pl
from jax.experimental.pallas import tpu as pltpu

BF = jnp.bfloat16
F32 = jnp.float32
NEG_INF = float("-inf")
LOG2E = math.log2(math.e)

CHUNK = 64
NORM_EPS = 1e-6
SUBLN_EPS = 1e-5
H_A, NOPE, ROPE, V_A, Q_RANK, KV_RANK = 8, 64, 32, 64, 384, 256
ROPE_BASE = 10000.0
H_B, DH_B, V_B = 4, 64, 128
H_C, DH_C, LEFT_CHUNKS, REL_CLIP = 16, 64, 8, 128
N_GROUPS, EPG, N_EXPERTS, D_EXPERT = 4, 8, 32, 512
LANE = 128
HEAD_PAD = 128
ROUTE_OFF = N_GROUPS

ROW_TILE = 256
ATT_TILE = 512
SOFTMAX_ROWS = 32
BAND_TILE = 256
BAND_TILES_PER_STEP = 4
MOE_BLOCK = 512
VMEM_LIMIT = 56 * 1024 * 1024


def _cparams(*sem):
    return pltpu.CompilerParams(dimension_semantics=sem, vmem_limit_bytes=VMEM_LIMIT)


def _rms(x, g, eps):
    return x * lax.rsqrt(jnp.mean(x * x, axis=-1, keepdims=True) + eps) * g


def _dot(a, b):
    return jnp.dot(a, b, preferred_element_type=F32)


def _dot_nt(a, b):
    return lax.dot_general(a, b, (((1,), (1,)), ((), ())), preferred_element_type=F32)


def _lane_iota(shape):
    return lax.broadcasted_iota(jnp.int32, shape, len(shape) - 1)


def _split_halves(q):
    qf = q.astype(F32)
    lane = _lane_iota(qf.shape)
    return jnp.concatenate([jnp.where(lane < 64, qf, 0.0), jnp.where(lane >= 64, qf, 0.0)], axis=0).astype(BF)


def _softmax_pv(s_list, v_list):
    m = functools.reduce(jnp.maximum, [jnp.max(s, axis=-1, keepdims=True) for s in s_list])
    acc, l = None, None
    for s, v in zip(s_list, v_list):
        p = jnp.exp2(s - m)
        ls = jnp.sum(p, axis=-1, keepdims=True)
        a = _dot(p.astype(BF), v)
        l = ls if l is None else l + ls
        acc = a if acc is None else acc + a
    return acc / l


def _diff_lambda(lamv, lam_init):
    a = jnp.exp(jnp.sum(lamv[0:1] * lamv[1:2], axis=-1, keepdims=True))
    b = jnp.exp(jnp.sum(lamv[2:3] * lamv[3:4], axis=-1, keepdims=True))
    return a - b + lam_init


def _diff_finish(o1, o2, lam, subln, lam_init):
    o = o1 - lam * o2
    return _rms(o, subln, SUBLN_EPS) * (1.0 - lam_init)


def _even_in_kernel(x_ref, g_ref, w0_ref, qn_ref, wq_ref, wqs_ref, kvn_ref, wk_ref, wv_ref, vone_ref,
                    cs_ref, sn_ref,
                    qa_ref, ka_ref, va_ref, ckv_ref, kr_ref, qd_ref, kdb_ref, vdb_ref, kd_ref, vd_ref,
                    *, a_scale, b_scale):
    h = _rms(x_ref[...], g_ref[...], NORM_EPS).astype(BF)
    y = _dot(h, w0_ref[...])
    cq, ckv = y[:, 0:384], y[:, 384:640]
    kr_m, kr_s = y[:, 640:768], y[:, 768:896]
    qd, kd, vd = y[:, 896:1408], y[:, 1408:1920], y[:, 1920:2432]
    cs, sn = cs_ref[...], sn_ref[...]
    cqn = _rms(cq, qn_ref[...], NORM_EPS).astype(BF)
    qm = _dot(cqn, wq_ref[...])
    qs = _dot(cqn, wqs_ref[...])
    ckvn = _rms(ckv, kvn_ref[...], NORM_EPS)
    ckv_ref[...] = ckvn
    krp = kr_m * cs + kr_s * sn
    kr_ref[...] = krp[:, NOPE:NOPE + ROPE]
    cb = ckvn.astype(BF)
    kn = _dot(cb, wk_ref[...])
    for hh in range(H_A):
        sl = slice(hh * HEAD_PAD, (hh + 1) * HEAD_PAD)
        qa_ref[:, sl] = ((qm[:, sl] * cs + qs[:, sl] * sn) * a_scale).astype(BF)
        ka_ref[:, sl] = (kn[:, sl] + krp).astype(BF)
    va_ref[...] = (_dot(cb, wv_ref[...]) + vone_ref[...]).astype(BF)
    qd_ref[...] = (qd * b_scale).astype(BF)
    for hh in range(H_B):
        kd_ref[:, hh, :] = kd[:, hh * V_B:(hh + 1) * V_B]
        vd_ref[:, hh, :] = vd[:, hh * V_B:(hh + 1) * V_B]
    kdb_ref[...] = kd.astype(BF)
    vdb_ref[...] = vd.astype(BF)


def _even_in(x, g, w0, qn, wq, wqs, kvn, wk, wv, vone, cs_tab, sn_tab):
    T, D = x.shape
    pos_blocks = cs_tab.shape[0] // ROW_TILE

    def full(a):
        return pl.BlockSpec(a.shape, lambda i: (0,) * a.ndim)

    def rows(*tail):
        return pl.BlockSpec((ROW_TILE,) + tail, lambda i: (i,) + (0,) * len(tail))

    pos_spec = pl.BlockSpec((ROW_TILE, LANE), lambda i: (i % pos_blocks, 0))
    outs = [((1024,), BF), ((1024,), BF), ((1024,), BF), ((KV_RANK,), F32), ((ROPE,), F32),
            ((512,), BF), ((512,), BF), ((512,), BF), ((H_B, V_B), F32), ((H_B, V_B), F32)]
    return pl.pallas_call(
        functools.partial(_even_in_kernel,
                          a_scale=(NOPE + ROPE) ** -0.5 * LOG2E, b_scale=DH_B ** -0.5 * LOG2E),
        grid=(T // ROW_TILE,),
        in_specs=[rows(D), full(g), full(w0), full(qn), full(wq), full(wqs), full(kvn), full(wk), full(wv),
                  full(vone), pos_spec, pos_spec],
        out_specs=[rows(*tail) for tail, _ in outs],
        out_shape=[jax.ShapeDtypeStruct((T,) + tail, dt) for tail, dt in outs],
        compiler_params=_cparams("parallel"),
        name="even_in",
    )(x, g, w0, qn, wq, wqs, kvn, wk, wv, vone, cs_tab, sn_tab)


def _chunk_causal_mask(tq, tk):
    row = lax.broadcasted_iota(jnp.int32, (tq, tk), 0)
    col = lax.broadcasted_iota(jnp.int32, (tq, tk), 1)
    return (col // CHUNK) <= (row // CHUNK)


def _softmax_tile(s_ref, p_ref, m_ref, l_ref, a_ref, *, add_ref=None, off=None):
    tile = s_ref.shape[0]
    reps = tile // LANE
    blocks = [slice(rb * SOFTMAX_ROWS, (rb + 1) * SOFTMAX_ROWS) for rb in range(tile // SOFTMAX_ROWS)]

    def biased(rs):
        s = s_ref[rs, :]
        if add_ref is not None:
            s = s + add_ref[rs, :]
        return s

    for rs in blocks:
        m_old = m_ref[rs, :]
        red = jnp.broadcast_to(jnp.max(biased(rs), axis=-1, keepdims=True), m_old.shape)
        if off is not None:
            red = red + off
        m_new = jnp.maximum(m_old, red)
        a_ref[rs, :] = jnp.exp2(m_old - m_new)
        m_ref[rs, :] = m_new
    for rs in blocks:
        m_new = m_ref[rs, :]
        shift = m_new if off is None else m_new - off
        p = jnp.exp2(biased(rs) - jnp.concatenate([shift] * reps, axis=1))
        if l_ref is not None:
            l_ref[rs, :] = a_ref[rs, :] * l_ref[rs, :] + jnp.broadcast_to(
                jnp.sum(p, axis=-1, keepdims=True), m_new.shape)
        p_ref[rs, :] = p.astype(BF)


def _flash_causal(qi, n_chains, bufs, scores, update):
    s0, s1, p0, p1, m_ref, l_ref, a_ref, acc_ref = bufs
    m_ref[...] = jnp.full(m_ref.shape, NEG_INF, F32)
    l_ref[...] = jnp.zeros(l_ref.shape, F32)
    acc_ref[...] = jnp.zeros(acc_ref.shape, F32)
    scores(0, s0)

    def pair(jj, carry):
        j = 2 * jj
        scores(j + 1, s1)
        update(j, s0, p0, False)
        scores(j + 2, s0)
        update(j + 1, s1, p1, False)
        return carry

    lax.fori_loop(0, qi // 2, pair, 0)

    @pl.when(qi % 2 == 1)
    def _():
        scores(qi, s1)
        update(qi - 1, s0, p0, False)
        s0[...] = s1[...]

    update(qi, s0, p0, True)


def _flash_scratch(n_chains, tile, acc_width=LANE):
    s = pltpu.VMEM((n_chains, tile, tile), F32)
    p = pltpu.VMEM((n_chains, tile, tile), BF)
    stat = pltpu.VMEM((n_chains, tile, LANE), F32)
    return [s, s, p, p, stat, stat, stat, pltpu.VMEM((n_chains, tile, acc_width), F32)]


def _mla_prompt_kernel(q_ref, k_ref, v_ref, o_ref, s0, s1, p0, p1, m_ref, l_ref, a_ref, acc_ref, dmask_ref,
                       *, tile):
    qi = pl.program_id(2)
    dmask_ref[...] = jnp.where(_chunk_causal_mask(tile, tile), 0.0, NEG_INF)
    sls = [slice(hh * HEAD_PAD, (hh + 1) * HEAD_PAD) for hh in range(2)]

    def scores(j, s_buf):
        start = pl.multiple_of(j * tile, tile)
        for c, sl in enumerate(sls):
            s_buf[c] = _dot_nt(q_ref[:, sl], k_ref[pl.ds(start, tile), sl])

    def update(j, s_buf, p_buf, diag):
        start = pl.multiple_of(j * tile, tile)
        for c, sl in enumerate(sls):
            _softmax_tile(s_buf.at[c], p_buf.at[c], m_ref.at[c], None, a_ref.at[c],
                          add_ref=dmask_ref if diag else None)
            acc_ref[c] = a_ref[c] * acc_ref[c] + _dot(p_buf[c], v_ref[pl.ds(start, tile), sl])

    _flash_causal(qi, 2, (s0, s1, p0, p1, m_ref, l_ref, a_ref, acc_ref), scores, update)
    lane = _lane_iota((tile, LANE))
    a0, a1 = acc_ref[0], acc_ref[1]
    l0 = jnp.sum(jnp.where(lane == V_A, a0, 0.0), axis=-1, keepdims=True)
    l1 = jnp.sum(jnp.where(lane == 0, a1, 0.0), axis=-1, keepdims=True)
    o_ref[...] = jnp.where(lane < V_A, a0 / l0, a1 / l1).astype(BF)


def _mla_prompt(qa, ka, va, *, batch, seq):
    tile = min(ATT_TILE, seq)
    nq = seq // tile
    return pl.pallas_call(
        functools.partial(_mla_prompt_kernel, tile=tile),
        grid=(batch, H_A // 2, nq),
        in_specs=[pl.BlockSpec((tile, 2 * HEAD_PAD), lambda b, p, qi: (b * nq + qi, p)),
                  pl.BlockSpec((seq, 2 * HEAD_PAD), lambda b, p, qi: (b, p)),
                  pl.BlockSpec((seq, 2 * HEAD_PAD), lambda b, p, qi: (b, p))],
        out_specs=pl.BlockSpec((tile, LANE), lambda b, p, qi: (b * nq + qi, p)),
        out_shape=jax.ShapeDtypeStruct((batch * seq, H_A * V_A), BF),
        scratch_shapes=_flash_scratch(2, tile) + [pltpu.VMEM((tile, tile), F32)],
        compiler_params=_cparams("parallel", "parallel", "arbitrary"),
        name="mla_prompt",
    )(qa, ka, va)


def _diff_prompt_kernel(slopes_ref, lamv_ref, subln_ref, q_ref, k_ref, v_ref, o_ref,
                        s0, s1, p0, p1, m_ref, l_ref, a_ref, acc_ref, key_ref, dbias_ref, q2_ref,
                        *, tile, lam_init):
    h = pl.program_id(1)
    qi = pl.program_id(2)
    slope = slopes_ref[h]
    q2_ref[...] = _split_halves(q_ref[...])
    key_ref[...] = slope * lax.broadcasted_iota(jnp.int32, (8, tile), 1).astype(F32)
    row = lax.broadcasted_iota(jnp.int32, (tile, tile), 0)
    col = lax.broadcasted_iota(jnp.int32, (tile, tile), 1)
    later = (2.0 * slope) * jnp.minimum(row - col, 0).astype(F32)
    dbias_ref[...] = jnp.where(_chunk_causal_mask(tile, tile), later, NEG_INF)

    def scores(j, s_buf):
        k = k_ref[pl.ds(pl.multiple_of(j * tile, tile), tile), :]
        key_term = jnp.concatenate([key_ref[...]] * (tile // 8), axis=0)
        for c in range(2):
            s_buf[c] = _dot_nt(q2_ref[c * tile:(c + 1) * tile, :], k) + key_term

    def update(j, s_buf, p_buf, diag):
        start = pl.multiple_of(j * tile, tile)
        off = slope * (j * tile).astype(F32)
        for c in range(2):
            _softmax_tile(s_buf.at[c], p_buf.at[c], m_ref.at[c], l_ref.at[c], a_ref.at[c],
                          add_ref=dbias_ref if diag else None, off=off)
            acc_ref[c] = a_ref[c] * acc_ref[c] + _dot(p_buf[c], v_ref[pl.ds(start, tile), :])

    _flash_causal(qi, 2, (s0, s1, p0, p1, m_ref, l_ref, a_ref, acc_ref), scores, update)
    lam = _diff_lambda(lamv_ref[...], lam_init)
    o_ref[...] = _diff_finish(acc_ref[0] / l_ref[0], acc_ref[1] / l_ref[1], lam, subln_ref[...],
                              lam_init).astype(BF)


def _diff_prompt(slopes, lamv, subln, qd, kdb, vdb, *, batch, seq, lam_init):
    tile = min(ATT_TILE, seq)
    nq = seq // tile
    return pl.pallas_call(
        functools.partial(_diff_prompt_kernel, tile=tile, lam_init=lam_init),
        grid=(batch, H_B, nq),
        in_specs=[pl.BlockSpec(memory_space=pltpu.SMEM),
                  pl.BlockSpec(lamv.shape, lambda b, h, qi: (0, 0)),
                  pl.BlockSpec(subln.shape, lambda b, h, qi: (0, 0)),
                  pl.BlockSpec((tile, LANE), lambda b, h, qi: (b * nq + qi, h)),
                  pl.BlockSpec((seq, LANE), lambda b, h, qi: (b, h)),
                  pl.BlockSpec((seq, LANE), lambda b, h, qi: (b, h))],
        out_specs=pl.BlockSpec((tile, LANE), lambda b, h, qi: (b * nq + qi, h)),
        out_shape=jax.ShapeDtypeStruct((batch * seq, H_B * V_B), BF),
        scratch_shapes=_flash_scratch(2, tile) + [pltpu.VMEM((8, tile), F32), pltpu.VMEM((tile, tile), F32),
                                                  pltpu.VMEM((2 * tile, LANE), BF)],
        compiler_params=_cparams("parallel", "parallel", "arbitrary"),
        name="diff_prompt",
    )(slopes, lamv, subln, qd, kdb, vdb)


def _even_sample_kernel(slopes_ref, lamv_ref, subln_ref, wk_ref, wv_ref, vone_ref, e_ref,
                        qa_ref, ka_ref, va_ref, ckv_ref, kr_ref,
                        qd_ref, kdb_ref, vdb_ref, ck_ref, cv_ref,
                        oa_ref, ob_ref, *, lam_init):
    n_new = qa_ref.shape[0]
    n_past = ckv_ref.shape[0]
    ckvp = ckv_ref[...].astype(BF)
    krp = kr_ref[...].astype(BF)
    lane = _lane_iota((n_new, LANE))
    for pr in range(H_A // 2):
        res = []
        for hh in range(2):
            sl = slice((2 * pr + hh) * HEAD_PAD, (2 * pr + hh + 1) * HEAD_PAD)
            q = qa_ref[:, sl]
            kp = (_dot(ckvp, wk_ref[:, sl]) + _dot(krp, e_ref[:, sl])).astype(BF)
            vp = (_dot(ckvp, wv_ref[:, sl]) + vone_ref[:, sl]).astype(BF)
            res.append(_softmax_pv([_dot_nt(q, kp), _dot_nt(q, ka_ref[:, sl])], [vp, va_ref[:, sl]]))
        oa_ref[:, pr * LANE:(pr + 1) * LANE] = jnp.where(lane < V_A, res[0], res[1]).astype(BF)
    rowp = lax.broadcasted_iota(jnp.int32, (n_new, n_past), 0)
    colp = lax.broadcasted_iota(jnp.int32, (n_new, n_past), 1)
    dist_p = (rowp - colp + n_past).astype(F32)
    dist_p = jnp.concatenate([dist_p, dist_p], axis=0)
    rown = lax.broadcasted_iota(jnp.int32, (n_new, n_new), 0)
    coln = lax.broadcasted_iota(jnp.int32, (n_new, n_new), 1)
    dist_n = jnp.abs(rown - coln).astype(F32)
    dist_n = jnp.concatenate([dist_n, dist_n], axis=0)
    lam = _diff_lambda(lamv_ref[...], lam_init)
    for h in range(H_B):
        sl = slice(h * LANE, (h + 1) * LANE)
        slope = slopes_ref[h]
        q2x = _split_halves(qd_ref[:, sl])
        kp = ck_ref[:, h, :].astype(BF)
        vp = cv_ref[:, h, :].astype(BF)
        s_p = _dot_nt(q2x, kp) - slope * dist_p
        s_n = _dot_nt(q2x, kdb_ref[:, sl]) - slope * dist_n
        o = _softmax_pv([s_p, s_n], [vp, vdb_ref[:, sl]])
        ob_ref[:, sl] = _diff_finish(o[:n_new], o[n_new:], lam, subln_ref[...], lam_init).astype(BF)


def _even_sample(slopes, lamv, subln, wk, wv, vone, e_mat, qa, ka, va, ckv_c, kr_c, qd, kdb, vdb, ck_c, cv_c,
                 *, n_prompt_rows, dec_seq, lam_init):
    dec_batch, n_past = ckv_c.shape[0], ckv_c.shape[1]
    base = n_prompt_rows // dec_seq

    def full(a):
        return pl.BlockSpec(a.shape, lambda s: (0,) * a.ndim)

    def new(width):
        return pl.BlockSpec((dec_seq, width), lambda s: (base + s, 0))

    def cache(*tail):
        return pl.BlockSpec((None, n_past) + tail, lambda s: (s, 0) + (0,) * len(tail))

    return pl.pallas_call(
        functools.partial(_even_sample_kernel, lam_init=lam_init),
        grid=(dec_batch,),
        in_specs=[pl.BlockSpec(memory_space=pltpu.SMEM), full(lamv), full(subln), full(wk), full(wv), full(vone),
                  full(e_mat), new(1024), new(1024), new(1024), cache(KV_RANK), cache(ROPE),
                  new(512), new(512), new(512), cache(H_B, V_B), cache(H_B, V_B)],
        out_specs=[pl.BlockSpec((dec_seq, 512), lambda s: (s, 0))] * 2,
        out_shape=[jax.ShapeDtypeStruct((dec_batch * dec_seq, 512), BF)] * 2,
        compiler_params=_cparams("parallel"),
        name="even_sample",
    )(slopes, lamv, subln, wk, wv, vone, e_mat, qa, ka, va, ckv_c, kr_c, qd, kdb, vdb, ck_c, cv_c)


def _route(logits, carry):
    tm = logits.shape[0]
    lane = _lane_iota(logits.shape).astype(F32)
    big = float(LANE)
    g_mask = lane < N_GROUPS
    gl = jnp.where(g_mask, logits, NEG_INF)
    gmax = jnp.max(gl, axis=-1, keepdims=True)
    g_sel = jnp.min(jnp.where(gl == gmax, lane, big), axis=-1, keepdims=True)
    p_grp = 1.0 / jnp.sum(jnp.exp(gl - gmax), axis=-1, keepdims=True)
    lo = ROUTE_OFF + EPG * g_sel
    el = jnp.where((lane >= lo) & (lane < lo + EPG), logits, NEG_INF)
    v1 = jnp.max(el, axis=-1, keepdims=True)
    i1 = jnp.min(jnp.where(el == v1, lane, big), axis=-1, keepdims=True)
    el2 = jnp.where(lane == i1, NEG_INF, el)
    v2 = jnp.max(el2, axis=-1, keepdims=True)
    i2 = jnp.min(jnp.where(el2 == v2, lane, big), axis=-1, keepdims=True)
    ex = jnp.exp(v2 - v1)
    den = 1.0 + ex
    gate1 = (1.0 / den) * p_grp
    gate2 = (ex / den) * p_grp
    onehot = jnp.where((lane == i1) | (lane == i2), 1.0, 0.0)
    row = lax.broadcasted_iota(jnp.int32, (tm, tm), 0)
    col = lax.broadcasted_iota(jnp.int32, (tm, tm), 1)
    tri = jnp.where(row > col, 1.0, 0.0).astype(BF)
    cum = _dot(tri, onehot.astype(BF)) + carry
    r1 = jnp.sum(jnp.where(lane == i1, cum, 0.0), axis=-1, keepdims=True)
    r2 = jnp.sum(jnp.where(lane == i2, cum, 0.0), axis=-1, keepdims=True)
    packed = jnp.zeros_like(logits)
    for pos, val in enumerate([i1 - ROUTE_OFF, i2 - ROUTE_OFF, gate1, gate2, r1, r2]):
        packed = jnp.where(lane == pos, val, packed)
    return packed, carry + jnp.sum(onehot, axis=0, keepdims=True)


def _out_router_kernel(*refs, splits, n_prompt_tiles):
    i = pl.program_id(0)
    offs = [sum(splits[:k]) for k in range(len(splits))]
    n_mix = len(splits) - 1

    def pick(k):
        parts = refs[offs[k]:offs[k] + splits[k]]
        if splits[k] == 1:
            return parts[0][...]
        return jnp.where(i < n_prompt_tiles, parts[0][...], parts[1][...])

    rest = refs[sum(splits):]
    w_refs = rest[:n_mix]
    g_ref, wr_ref, br_ref, x1_ref, xn_ref, route_ref, cnt_ref, carry_ref = rest[n_mix:]

    @pl.when(i == 0)
    def _():
        carry_ref[...] = jnp.zeros_like(carry_ref)

    x1 = pick(0)
    for k, w_ref in enumerate(w_refs):
        x1 = x1 + _dot(pick(1 + k), w_ref[...])
    x1_ref[...] = x1
    xb = _rms(x1, g_ref[...], NORM_EPS).astype(BF)
    xn_ref[...] = xb
    logits = _dot(xb, wr_ref[...]) + br_ref[...]
    packed, carry = _route(logits, carry_ref[...])
    route_ref[...] = packed
    carry_ref[...] = carry
    cnt_ref[...] = carry


def _out_router(row_inputs, ws, g, wr, br):
    splits = tuple(len(parts) for parts in row_inputs)
    T = sum(a.shape[0] for a in row_inputs[0])
    D = row_inputs[0][0].shape[1]
    n_p = max([parts[0].shape[0] // ROW_TILE for parts in row_inputs if len(parts) == 2], default=0)

    def row_specs(parts):
        if len(parts) == 1:
            return [pl.BlockSpec((ROW_TILE, parts[0].shape[1]), lambda i: (i, 0))]
        return [pl.BlockSpec((ROW_TILE, parts[0].shape[1]), lambda i: (jnp.minimum(i, n_p - 1), 0)),
                pl.BlockSpec((ROW_TILE, parts[1].shape[1]), lambda i: (jnp.maximum(i - n_p, 0), 0))]

    def full(a):
        return pl.BlockSpec(a.shape, lambda i: (0,) * a.ndim)

    def rows(width):
        return pl.BlockSpec((ROW_TILE, width), lambda i: (i, 0))

    flat_rows = [a for parts in row_inputs for a in parts]
    return pl.pallas_call(
        functools.partial(_out_router_kernel, splits=splits, n_prompt_tiles=n_p),
        grid=(T // ROW_TILE,),
        in_specs=[s for parts in row_inputs for s in row_specs(parts)] + [full(w) for w in ws]
        + [full(g), full(wr), full(br)],
        out_specs=[rows(D), rows(D), rows(LANE), pl.BlockSpec((1, LANE), lambda i: (0, 0))],
        out_shape=[jax.ShapeDtypeStruct((T, D), F32), jax.ShapeDtypeStruct((T, D), BF),
                   jax.ShapeDtypeStruct((T, LANE), F32), jax.ShapeDtypeStruct((1, LANE), F32)],
        scratch_shapes=[pltpu.VMEM((1, LANE), F32)],
        compiler_params=_cparams("arbitrary"),
        name="out_router",
    )(*flat_rows, *ws, g, wr, br)


def _experts_kernel(be_ref, nu_ref, xb_ref, wg_ref, wu_ref, wd_ref, y_ref, wgb_ref, wub_ref, wdb_ref):
    i = pl.program_id(0)
    used = i < nu_ref[0]

    @pl.when(used & ((i == 0) | (be_ref[i] != be_ref[jnp.maximum(i - 1, 0)])))
    def _():
        wgb_ref[...] = wg_ref[...].astype(BF)
        wub_ref[...] = wu_ref[...].astype(BF)
        wdb_ref[...] = wd_ref[...].astype(BF)

    @pl.when(used)
    def _():
        xb = xb_ref[...]
        a = _dot(xb, wgb_ref[...])
        b = _dot(xb, wub_ref[...])
        hid = (a * jax.nn.sigmoid(a)) * b
        y_ref[...] = _dot(hid.astype(BF), wdb_ref[...]).astype(y_ref.dtype)

    @pl.when(jnp.logical_not(used))
    def _():
        y_ref[...] = jnp.zeros_like(y_ref)


def _experts(block_expert, n_used, xb, wg, wu, wd, *, layer):
    L, D = xb.shape
    n_blocks = L // MOE_BLOCK
    grid_spec = pltpu.PrefetchScalarGridSpec(
        num_scalar_prefetch=2,
        grid=(n_blocks,),
        in_specs=[pl.BlockSpec((MOE_BLOCK, D), lambda i, be, nu: (i, 0)),
                  pl.BlockSpec((None, None, D, D_EXPERT), lambda i, be, nu: (layer, be[i], 0, 0)),
                  pl.BlockSpec((None, None, D, D_EXPERT), lambda i, be, nu: (layer, be[i], 0, 0)),
                  pl.BlockSpec((None, None, D_EXPERT, D), lambda i, be, nu: (layer, be[i], 0, 0))],
        out_specs=pl.BlockSpec((MOE_BLOCK, D), lambda i, be, nu: (i, 0)),
        scratch_shapes=[pltpu.VMEM((D, D_EXPERT), BF), pltpu.VMEM((D, D_EXPERT), BF), pltpu.VMEM((D_EXPERT, D), BF)],
    )
    return pl.pallas_call(
        _experts_kernel,
        grid_spec=grid_spec,
        out_shape=jax.ShapeDtypeStruct((L, D), BF),
        compiler_params=_cparams("arbitrary"),
        name="experts",
    )(block_expert, n_used, xb, wg, wu, wd)


def _moe(xn, route, cnt, wg, wu, wd, *, layer):
    T = xn.shape[0]
    e = route[:, 0:2].astype(jnp.int32)
    rank = route[:, 4:6].astype(jnp.int32)
    counts = cnt[0, ROUTE_OFF:ROUTE_OFF + N_EXPERTS].astype(jnp.int32)
    padded = ((counts + MOE_BLOCK - 1) // MOE_BLOCK) * MOE_BLOCK
    pend = jnp.cumsum(padded)
    pstart = pend - padded
    dest = jnp.sum(jnp.where(e[:, :, None] == jnp.arange(N_EXPERTS)[None, None, :], pstart[None, None, :], 0),
                   axis=-1) + rank
    n_blocks = -(-(2 * T) // MOE_BLOCK) + N_EXPERTS
    L = n_blocks * MOE_BLOCK
    tok = jnp.repeat(jnp.arange(T, dtype=jnp.int32), 2)
    buf_tok = (jnp.arange(L, dtype=jnp.int32) % T).at[dest.reshape(-1)].set(
        tok, unique_indices=True, mode="promise_in_bounds")
    block_start = jnp.arange(n_blocks, dtype=jnp.int32) * MOE_BLOCK
    block_expert = jnp.minimum(jnp.sum(pend[None, :] <= block_start[:, None], axis=1), N_EXPERTS - 1).astype(jnp.int32)
    n_used = (pend[-1:] // MOE_BLOCK).astype(jnp.int32)
    xb = xn.at[buf_tok].get(mode="promise_in_bounds")
    yb = _experts(block_expert, n_used, xb, wg, wu, wd, layer=layer)
    return (yb.at[dest[:, 0]].get(mode="promise_in_bounds"), yb.at[dest[:, 1]].get(mode="promise_in_bounds"))


def _gates(route):
    lane = _lane_iota(route.shape)
    g0 = jnp.sum(jnp.where(lane == 2, route, 0.0), axis=-1, keepdims=True)
    g1 = jnp.sum(jnp.where(lane == 3, route, 0.0), axis=-1, keepdims=True)
    return g0, g1


def _odd_in_kernel(x_ref, y0_ref, y1_ref, route_ref, g_ref, w_ref, x2_ref, q_ref, k_ref, v_ref, *, scale):
    g0, g1 = _gates(route_ref[...])
    x2 = x_ref[...] + (y0_ref[...].astype(F32) * g0 + y1_ref[...].astype(F32) * g1)
    x2_ref[...] = x2
    h = _rms(x2, g_ref[...], NORM_EPS).astype(BF)
    y = _dot(h, w_ref[...])
    n = q_ref.shape[1]
    q_ref[...] = (y[:, :n] * scale).astype(BF)
    k_ref[...] = y[:, n:2 * n].astype(BF)
    v_ref[...] = y[:, 2 * n:].astype(BF)


def _odd_in(x1, y0, y1, route, g, w):
    T, D = x1.shape
    n = w.shape[1] // 3

    def rows(width):
        return pl.BlockSpec((ROW_TILE, width), lambda i: (i, 0))

    def full(a):
        return pl.BlockSpec(a.shape, lambda i: (0,) * a.ndim)

    return pl.pallas_call(
        functools.partial(_odd_in_kernel, scale=DH_C ** -0.5 * LOG2E),
        grid=(T // ROW_TILE,),
        in_specs=[rows(D), rows(D), rows(D), rows(LANE), full(g), full(w)],
        out_specs=[rows(D), rows(n), rows(n), rows(n)],
        out_shape=[jax.ShapeDtypeStruct((T, D), F32)] + [jax.ShapeDtypeStruct((T, n), BF)] * 3,
        compiler_params=_cparams("parallel"),
        name="odd_in",
    )(x1, y0, y1, route, g, w)


def _state_rows_kernel(ids_ref, x_ref, g_ref, w_ref, o_ref):
    del ids_ref
    h = _rms(x_ref[...], g_ref[...], NORM_EPS).astype(BF)
    o_ref[...] = _dot(h, w_ref[...])


def _state_rows(tile_ids, x, g, w):
    D = x.shape[1]
    n = tile_ids.shape[0]
    grid_spec = pltpu.PrefetchScalarGridSpec(
        num_scalar_prefetch=1,
        grid=(n,),
        in_specs=[pl.BlockSpec((ROW_TILE, D), lambda i, ids: (ids[i], 0)),
                  pl.BlockSpec(g.shape, lambda i, ids: (0, 0)),
                  pl.BlockSpec(w.shape, lambda i, ids: (0, 0))],
        out_specs=pl.BlockSpec((ROW_TILE, w.shape[1]), lambda i, ids: (i, 0)),
    )
    return pl.pallas_call(
        _state_rows_kernel,
        grid_spec=grid_spec,
        out_shape=jax.ShapeDtypeStruct((n * ROW_TILE, w.shape[1]), F32),
        compiler_params=_cparams("parallel"),
        name="state_rows",
    )(tile_ids, x, g, w)


def _band_prompt_kernel(q_ref, k_ref, v_ref, bias_ref, o_ref, q2_ref, s0, s1, p_ref, l_ref,
                        *, tile, tiles_per_step):
    lane = _lane_iota((tile, LANE))
    q2_ref[...] = _split_halves(q_ref[...])
    n_rows = tile * tiles_per_step

    def key_tiles(t):
        qi = pl.program_id(2) * tiles_per_step + t
        out = []
        for kt in range(3):
            start = (qi + kt - 2) * tile
            neg = jnp.where(start >= 0, 0.0, NEG_INF)
            out.append((pl.multiple_of(jnp.maximum(start, 0), tile), neg))
        return out

    def scores(t, s_buf):
        for hh in range(2):
            q = q2_ref[hh * n_rows + t * tile:hh * n_rows + (t + 1) * tile, :]
            for kt, (start, neg) in enumerate(key_tiles(t)):
                s = _dot_nt(q, k_ref[pl.ds(start, tile), :])
                s_buf[hh, :, kt * tile:(kt + 1) * tile] = s + neg if kt < 2 else s

    def softmax_pv(t, s_buf):
        outs = []
        slot = t % 2
        for hh in range(2):
            for rb in range(tile // SOFTMAX_ROWS):
                rs = slice(rb * SOFTMAX_ROWS, (rb + 1) * SOFTMAX_ROWS)
                s = s_buf[hh, rs, :] + bias_ref[hh * tile + rb * SOFTMAX_ROWS:hh * tile + (rb + 1) * SOFTMAX_ROWS, :]
                p = jnp.exp2(s - jnp.max(s, axis=-1, keepdims=True))
                l_ref[slot, hh, rs, :] = jnp.broadcast_to(jnp.sum(p, axis=-1, keepdims=True), (SOFTMAX_ROWS, LANE))
                p_ref[slot, hh, rs, :] = p.astype(BF)
            acc = None
            for kt, (start, _) in enumerate(key_tiles(t)):
                pv = _dot(p_ref[slot, hh, :, kt * tile:(kt + 1) * tile], v_ref[pl.ds(start, tile), :])
                acc = pv if acc is None else acc + pv
            outs.append(acc / l_ref[slot, hh])
        o_ref[t * tile:(t + 1) * tile, :] = jnp.where(lane < DH_C, outs[0], outs[1]).astype(BF)

    bufs = [s0, s1]
    scores(0, bufs[0])
    for t in range(tiles_per_step):
        if t + 1 < tiles_per_step:
            scores(t + 1, bufs[(t + 1) % 2])
        softmax_pv(t, bufs[t % 2])


def _band_prompt(q, k, v, bias, *, batch, seq):
    tile = BAND_TILE
    tps = min(BAND_TILES_PER_STEP, seq // tile)
    nq = seq // (tile * tps)
    return pl.pallas_call(
        functools.partial(_band_prompt_kernel, tile=tile, tiles_per_step=tps),
        grid=(H_C // 2, batch, nq),
        in_specs=[pl.BlockSpec((tile * tps, LANE), lambda p, b, qi: (b * nq + qi, p)),
                  pl.BlockSpec((seq, LANE), lambda p, b, qi: (b, p)),
                  pl.BlockSpec((seq, LANE), lambda p, b, qi: (b, p)),
                  pl.BlockSpec((None, 2 * tile, 3 * tile), lambda p, b, qi: (p, 0, 0))],
        out_specs=pl.BlockSpec((tile * tps, LANE), lambda p, b, qi: (b * nq + qi, p)),
        out_shape=jax.ShapeDtypeStruct((batch * seq, H_C * DH_C), BF),
        scratch_shapes=[pltpu.VMEM((2 * tile * tps, LANE), BF),
                        pltpu.VMEM((2, tile, 3 * tile), F32), pltpu.VMEM((2, tile, 3 * tile), F32),
                        pltpu.VMEM((2, 2, tile, 3 * tile), BF), pltpu.VMEM((2, 2, tile, LANE), F32)],
        compiler_params=_cparams("parallel", "parallel", "arbitrary"),
        name="band_prompt",
    )(q, k, v, bias)


def _band_sample_kernel(q_ref, k_ref, v_ref, ck_ref, cv_ref, bias_ref, o_ref):
    n_new = q_ref.shape[0]
    n_past = ck_ref.shape[0]
    lane = _lane_iota((n_new, LANE))
    for pr in range(H_C // 2):
        sl = slice(pr * LANE, (pr + 1) * LANE)
        q2x = _split_halves(q_ref[:, sl])
        s_p = _dot_nt(q2x, ck_ref[:, sl].astype(BF)) + bias_ref[pr, :, 0:n_past]
        s_n = _dot_nt(q2x, k_ref[:, sl]) + bias_ref[pr, :, n_past:n_past + n_new]
        o = _softmax_pv([s_p, s_n], [cv_ref[:, sl].astype(BF), v_ref[:, sl]])
        o_ref[:, sl] = jnp.where(lane < DH_C, o[:n_new], o[n_new:]).astype(BF)


def _band_sample(q, k, v, ck, cv, bias, *, n_prompt_rows, dec_seq):
    dec_batch, n_past, width = ck.shape
    base = n_prompt_rows // dec_seq
    new = pl.BlockSpec((dec_seq, width), lambda s: (base + s, 0))
    cache = pl.BlockSpec((None, n_past, width), lambda s: (s, 0, 0))
    return pl.pallas_call(
        _band_sample_kernel,
        grid=(dec_batch,),
        in_specs=[new, new, new, cache, cache, pl.BlockSpec(bias.shape, lambda s: (0, 0, 0))],
        out_specs=pl.BlockSpec((dec_seq, width), lambda s: (s, 0)),
        out_shape=jax.ShapeDtypeStruct((dec_batch * dec_seq, width), BF),
        compiler_params=_cparams("parallel"),
        name="band_sample",
    )(q, k, v, ck, cv, bias)


def _final_kernel(x_ref, y0_ref, y1_ref, route_ref, g_ref, o_ref):
    g0, g1 = _gates(route_ref[...])
    x = x_ref[...] + (y0_ref[...].astype(F32) * g0 + y1_ref[...].astype(F32) * g1)
    o_ref[...] = _rms(x, g_ref[...], NORM_EPS)


def _final(x, y0, y1, route, g, *, first_tile, n_tiles):
    D = x.shape[1]

    def rows(width):
        return pl.BlockSpec((ROW_TILE, width), lambda i: (first_tile + i, 0))

    return pl.pallas_call(
        _final_kernel,
        grid=(n_tiles,),
        in_specs=[rows(D), rows(D), rows(D), rows(LANE), pl.BlockSpec(g.shape, lambda i: (0, 0))],
        out_specs=pl.BlockSpec((ROW_TILE, D), lambda i: (i, 0)),
        out_shape=jax.ShapeDtypeStruct((n_tiles * ROW_TILE, D), F32),
        compiler_params=_cparams("parallel"),
        name="final_norm",
    )(x, y0, y1, route, g)


def _prep_even_weights(w_in, w_qup, w_kvup):
    D = w_in.shape[0]
    a_in = Q_RANK + KV_RANK + ROPE
    bq = H_B * 2 * DH_B
    wcq, wckv, wkr = w_in[:, :Q_RANK], w_in[:, Q_RANK:Q_RANK + KV_RANK], w_in[:, Q_RANK + KV_RANK:a_in]
    wqd, wkd, wvd = w_in[:, a_in:a_in + bq], w_in[:, a_in + bq:a_in + 2 * bq], w_in[:, a_in + 2 * bq:]
    half = ROPE // 2

    def z(n):
        return jnp.zeros((D, n), w_in.dtype)

    kr_m = jnp.concatenate([z(NOPE), wkr, z(HEAD_PAD - NOPE - ROPE)], axis=1)
    kr_s = jnp.concatenate([z(NOPE), wkr[:, half:], wkr[:, :half], z(HEAD_PAD - NOPE - ROPE)], axis=1)
    w0 = jnp.concatenate([wcq, wckv, kr_m, kr_s, wqd, wkd, wvd], axis=1).astype(BF)
    wq3 = w_qup.reshape(Q_RANK, H_A, NOPE + ROPE)
    nope, r1, r2 = wq3[:, :, :NOPE], wq3[:, :, NOPE:NOPE + half], wq3[:, :, NOPE + half:]
    zq = jnp.zeros((Q_RANK, H_A, HEAD_PAD - NOPE - ROPE), w_qup.dtype)
    wq = jnp.concatenate([nope, r1, r2, zq], axis=-1).reshape(Q_RANK, H_A * HEAD_PAD).astype(BF)
    wqs = jnp.concatenate([jnp.zeros_like(nope), r2, r1, zq], axis=-1).reshape(Q_RANK, H_A * HEAD_PAD).astype(BF)
    wkv3 = w_kvup.reshape(KV_RANK, H_A, NOPE + V_A)
    wk = jnp.concatenate([wkv3[:, :, :NOPE], jnp.zeros((KV_RANK, H_A, HEAD_PAD - NOPE), w_kvup.dtype)],
                         axis=-1).reshape(KV_RANK, H_A * HEAD_PAD).astype(BF)
    wv4 = wkv3[:, :, NOPE:].reshape(KV_RANK, H_A // 2, 2, V_A)
    zv = jnp.zeros((KV_RANK, H_A // 2, HEAD_PAD - V_A), w_kvup.dtype)
    wv = jnp.concatenate([wv4[:, :, 0], zv, zv, wv4[:, :, 1]], axis=-1).reshape(KV_RANK, H_A * HEAD_PAD).astype(BF)
    vone = jnp.zeros((H_A // 2, 2 * HEAD_PAD), F32).at[:, V_A].set(1.0).at[:, HEAD_PAD].set(1.0)
    r = jnp.arange(ROPE)
    e_mat = jnp.zeros((ROPE, H_A, HEAD_PAD), F32).at[r[:, None], jnp.arange(H_A)[None, :], NOPE + r[:, None]].set(1.0)
    return w0, wq, wqs, wk, wv, vone.reshape(1, H_A * HEAD_PAD), e_mat.reshape(ROPE, H_A * HEAD_PAD).astype(BF)


def _rope_tables(pos):
    half = ROPE // 2
    inv = jnp.power(ROPE_BASE, -jnp.arange(half, dtype=F32) / half)
    ang = pos.astype(F32)[:, None] * inv[None, :]
    c, s = jnp.cos(ang), jnp.sin(ang)
    n = pos.shape[0]
    pad = jnp.zeros((n, HEAD_PAD - NOPE - ROPE), F32)
    cs = jnp.concatenate([jnp.ones((n, NOPE), F32), c, c, pad], axis=1)
    sn = jnp.concatenate([jnp.zeros((n, NOPE), F32), -s, s, pad], axis=1)
    return cs, sn


def _band_bias_tiles(table, n_rows, n_keys, key_offset, masked):
    i = jnp.arange(n_rows)[:, None]
    kpos = jnp.arange(n_keys)[None, :] - key_offset
    d_max = n_rows - 1 + key_offset
    d = d_max - jnp.arange(n_rows + n_keys - 1)
    rev = (table.astype(F32) * LOG2E)[:, jnp.clip(d, -REL_CLIP, REL_CLIP) + REL_CLIP]
    bias = jnp.stack([rev[:, n_rows - 1 - r:n_rows - 1 - r + n_keys] for r in range(n_rows)], axis=1)
    if masked:
        qc, kc = i // CHUNK, jnp.floor_divide(kpos, CHUNK)
        vis = (kc <= qc) & (kc >= qc - LEFT_CHUNKS)
        bias = jnp.where(vis[None], bias, NEG_INF)
    return bias.reshape(H_C // 2, 2 * n_rows, n_keys)


def kernel(x_prompt, x_sample, cache_mla_ckv, cache_mla_krope, cache_diff_k, cache_diff_v, cache_band_k, cache_band_v, ln_mix, w_in_even, mla_q_norm, mla_w_qup, mla_kv_norm, mla_w_kvup, diff_lam_q1, diff_lam_k1, diff_lam_q2, diff_lam_k2, diff_subln, w_out_even, w_in_odd, band_rel_bias, w_out_odd, ln_ffn, moe_w_group, moe_b_group, moe_w_router, moe_b_router, moe_w_gate, moe_w_up, moe_w_down, ln_final):
    B, S, D = x_prompt.shape
    DB, DS, _ = x_sample.shape
    n_past = cache_mla_ckv.shape[2]
    c_past = cache_band_k.shape[2]
    assert ln_mix.shape[0] == 2 and S % min(ATT_TILE, S) == 0 and S % BAND_TILE == 0
    assert (B * S) % ROW_TILE == 0 and (DB * DS) % ROW_TILE == 0 and ROW_TILE % DS == 0 and DS == CHUNK
    assert n_past % CHUNK == 0 and c_past == LEFT_CHUNKS * CHUNK and S >= c_past and c_past % ROW_TILE == 0
    BS, NS = B * S, DB * DS
    T = BS + NS
    n_p, n_s = BS // ROW_TILE, NS // ROW_TILE
    xp, xs = x_prompt.reshape(BS, D), x_sample.reshape(NS, D)
    row = lambda a: a.reshape(1, -1)

    w0, wq, wqs, wk, wv, vone, e_mat = _prep_even_weights(w_in_even[0], mla_w_qup[0], mla_w_kvup[0])
    cs_p, sn_p = _rope_tables(jnp.arange(S))
    cs_s, sn_s = _rope_tables(n_past + jnp.arange(DS))
    reps = ROW_TILE // DS
    even_w = (row(ln_mix[0]), w0, row(mla_q_norm[0]), wq, wqs, row(mla_kv_norm[0]), wk, wv, vone)
    qa, ka, va, ckv_p, kr_p, qd, kdb, vdb, kd_p, vd_p = _even_in(xp, *even_w, cs_p, sn_p)
    qa_s, ka_s, va_s, ckv_s, kr_s, qd_s, kdb_s, vdb_s, kd_s, vd_s = _even_in(
        xs, *even_w, jnp.tile(cs_s, (reps, 1)), jnp.tile(sn_s, (reps, 1)))
    slopes = jnp.exp2(-8.0 * jnp.arange(1, H_B + 1, dtype=F32) / H_B) * LOG2E
    lamv = jnp.stack([diff_lam_q1[0], diff_lam_k1[0], diff_lam_q2[0], diff_lam_k2[0]]).astype(F32)
    subln = row(diff_subln[0])
    lam_init = 0.8 - 0.6 * math.exp(-0.3 * 0)
    oa_p = _mla_prompt(qa, ka, va, batch=B, seq=S)
    ob_p = _diff_prompt(slopes, lamv, subln, qd, kdb, vdb, batch=B, seq=S, lam_init=lam_init)
    oa_s, ob_s = _even_sample(
        slopes, lamv, subln, wk, wv, vone, e_mat, qa_s, ka_s, va_s, cache_mla_ckv[0], cache_mla_krope[0],
        qd_s, kdb_s, vdb_s, cache_diff_k[0], cache_diff_v[0],
        n_prompt_rows=0, dec_seq=DS, lam_init=lam_init)

    def router_weights(l):
        wr = jnp.concatenate([moe_w_group[l], moe_w_router[l],
                              jnp.zeros((D, LANE - N_GROUPS - N_EXPERTS), F32)], axis=1).astype(BF)
        br = jnp.concatenate([moe_b_group[l], moe_b_router[l],
                              jnp.zeros((LANE - N_GROUPS - N_EXPERTS,), F32)]).astype(F32)
        return wr, row(br)

    n_a = H_A * V_A
    wo = w_out_even[0].astype(BF)
    wr, br = router_weights(0)
    x1, xn, route, cnt = _out_router([(xp, xs), (oa_p, oa_s), (ob_p, ob_s)], [wo[:n_a], wo[n_a:]],
                                     row(ln_ffn[0]), wr, br)
    y0, y1 = _moe(xn, route, cnt, moe_w_gate, moe_w_up, moe_w_down, layer=0)

    w_odd = w_in_odd[0].astype(BF)
    n_c = H_C * DH_C
    x2, qc, kc, vc = _odd_in(x1, y0, y1, route, row(ln_mix[1]), w_odd)
    tail = c_past // ROW_TILE
    tiles_per_seq = S // ROW_TILE
    tile_ids = jnp.concatenate(
        [jnp.arange(tiles_per_seq - tail, tiles_per_seq, dtype=jnp.int32) + b * tiles_per_seq for b in range(B)]
        + [jnp.arange(n_p, n_p + n_s, dtype=jnp.int32)])
    st = _state_rows(tile_ids, x2, row(ln_mix[1]), w_odd[:, n_c:])
    bias_p = _band_bias_tiles(band_rel_bias[0], BAND_TILE, 3 * BAND_TILE, 2 * BAND_TILE, True)
    bias_s = _band_bias_tiles(band_rel_bias[0], DS, c_past + DS, c_past, False)
    oc_p = _band_prompt(qc, kc, vc, bias_p, batch=B, seq=S)
    oc_s = _band_sample(qc, kc, vc, cache_band_k[0].reshape(DB, c_past, n_c), cache_band_v[0].reshape(DB, c_past, n_c),
                        bias_s, n_prompt_rows=BS, dec_seq=DS)
    wr, br = router_weights(1)
    x3, xn, route, cnt = _out_router([(x2,), (oc_p, oc_s)], [w_out_odd[0].astype(BF)], row(ln_ffn[1]), wr, br)
    y0, y1 = _moe(xn, route, cnt, moe_w_gate, moe_w_up, moe_w_down, layer=1)
    g_fin = row(ln_final)
    y_prompt = _final(x3, y0, y1, route, g_fin, first_tile=0, n_tiles=n_p).reshape(B, S, D)
    y_sample = _final(x3, y0, y1, route, g_fin, first_tile=n_p, n_tiles=n_s).reshape(DB, DS, D)

    def shaped(a_p, a_s, *tail_shape):
        return a_p.reshape(1, B, S, *tail_shape), a_s.reshape(1, DB, DS, *tail_shape)

    ckv_p, ckv_s = shaped(ckv_p, ckv_s, KV_RANK)
    kr_p, kr_s = shaped(kr_p, kr_s, ROPE)
    kd_p, kd_s = shaped(kd_p, kd_s, H_B, 2 * DH_B)
    vd_p, vd_s = shaped(vd_p, vd_s, H_B, V_B)
    n_tail = B * c_past
    bk_p = st[:n_tail, :n_c].reshape(1, B, c_past, H_C, DH_C)
    bv_p = st[:n_tail, n_c:].reshape(1, B, c_past, H_C, DH_C)
    k_new = st[n_tail:, :n_c].reshape(DB, DS, H_C, DH_C)
    v_new = st[n_tail:, n_c:].reshape(DB, DS, H_C, DH_C)
    bk_s = jnp.concatenate([cache_band_k[0][:, DS:], k_new], axis=1)[None]
    bv_s = jnp.concatenate([cache_band_v[0][:, DS:], v_new], axis=1)[None]
    return (y_prompt, y_sample, ckv_p, kr_p, kd_p, vd_p, bk_p, bv_p, ckv_s, kr_s, kd_s, vd_s, bk_s, bv_s)
```

```python
import functools
import math

import jax
import jax.numpy as jnp
from jax import lax
from jax.experimental import pallas as pl
from jax.experimental.pallas import tpu as pltpu

BF = jnp.bfloat16
F32 = jnp.float32
NEG_INF = float("-inf")
LOG2E = math.log2(math.e)

CHUNK = 64
NORM_EPS = 1e-6
SUBLN_EPS = 1e-5
H_A, NOPE, ROPE, V_A, Q_RANK, KV_RANK = 8, 64, 32, 64, 384, 256
ROPE_BASE = 10000.0
H_B, DH_B, V_B = 4, 64, 128
H_C, DH_C, LEFT_CHUNKS, REL_CLIP = 16, 64, 8, 128
N_GROUPS, EPG, N_EXPERTS, D_EXPERT = 4, 8, 32, 512
LANE = 128
HEAD_PAD = 128
ROUTE_OFF = N_GROUPS

ROW_TILE = 256
WIDE_TILE = 512
ATT_TILE = 512
SOFTMAX_ROWS = 32
BAND_TILE = 256
BAND_TILES_PER_STEP = 4
MOE_BLOCK = 512
VMEM_LIMIT = 56 * 1024 * 1024


def _cparams(*sem):
    return pltpu.CompilerParams(dimension_semantics=sem, vmem_limit_bytes=VMEM_LIMIT)


def _rms(x, g, eps):
    return x * lax.rsqrt(jnp.mean(x * x, axis=-1, keepdims=True) + eps) * g


def _dot(a, b):
    return jnp.dot(a, b, preferred_element_type=F32)


def _dot_nt(a, b):
    return lax.dot_general(a, b, (((1,), (1,)), ((), ())), preferred_element_type=F32)


def _lane_iota(shape):
    return lax.broadcasted_iota(jnp.int32, shape, len(shape) - 1)


def _split_halves(q):
    qf = q.astype(F32)
    lane = _lane_iota(qf.shape)
    return jnp.concatenate([jnp.where(lane < 64, qf, 0.0), jnp.where(lane >= 64, qf, 0.0)], axis=0).astype(BF)


def _softmax_pv(s_list, v_list):
    m = functools.reduce(jnp.maximum, [jnp.max(s, axis=-1, keepdims=True) for s in s_list])
    acc, l = None, None
    for s, v in zip(s_list, v_list):
        p = jnp.exp2(s - m)
        ls = jnp.sum(p, axis=-1, keepdims=True)
        a = _dot(p.astype(BF), v)
        l = ls if l is None else l + ls
        acc = a if acc is None else acc + a
    return acc / l


def _diff_lambda(lamv, lam_init):
    a = jnp.exp(jnp.sum(lamv[0:1] * lamv[1:2], axis=-1, keepdims=True))
    b = jnp.exp(jnp.sum(lamv[2:3] * lamv[3:4], axis=-1, keepdims=True))
    return a - b + lam_init


def _diff_finish(o1, o2, lam, subln, lam_init):
    o = o1 - lam * o2
    return _rms(o, subln, SUBLN_EPS) * (1.0 - lam_init)


def _even_in_kernel(x_ref, g_ref, w0_ref, qn_ref, wq_ref, wqs_ref, kvn_ref, wk_ref, wv_ref, vone_ref,
                    cs_ref, sn_ref,
                    qa_ref, ka_ref, va_ref, ckv_ref, kr_ref, qd_ref, kdb_ref, vdb_ref, kd_ref, vd_ref,
                    *, a_scale, b_scale):
    h = _rms(x_ref[...], g_ref[...], NORM_EPS).astype(BF)
    y = _dot(h, w0_ref[...])
    cq, ckv = y[:, 0:384], y[:, 384:640]
    kr_m, kr_s = y[:, 640:768], y[:, 768:896]
    qd, kd, vd = y[:, 896:1408], y[:, 1408:1920], y[:, 1920:2432]
    cs, sn = cs_ref[...], sn_ref[...]
    cqn = _rms(cq, qn_ref[...], NORM_EPS).astype(BF)
    qm = _dot(cqn, wq_ref[...])
    qs = _dot(cqn, wqs_ref[...])
    ckvn = _rms(ckv, kvn_ref[...], NORM_EPS)
    ckv_ref[...] = ckvn
    krp = kr_m * cs + kr_s * sn
    kr_ref[...] = krp[:, NOPE:NOPE + ROPE]
    cb = ckvn.astype(BF)
    kn = _dot(cb, wk_ref[...])
    for hh in range(H_A):
        sl = slice(hh * HEAD_PAD, (hh + 1) * HEAD_PAD)
        qa_ref[:, sl] = ((qm[:, sl] * cs + qs[:, sl] * sn) * a_scale).astype(BF)
        ka_ref[:, sl] = (kn[:, sl] + krp).astype(BF)
    va_ref[...] = (_dot(cb, wv_ref[...]) + vone_ref[...]).astype(BF)
    qd_ref[...] = (qd * b_scale).astype(BF)
    for hh in range(H_B):
        kd_ref[:, hh, :] = kd[:, hh * V_B:(hh + 1) * V_B]
        vd_ref[:, hh, :] = vd[:, hh * V_B:(hh + 1) * V_B]
    kdb_ref[...] = kd.astype(BF)
    vdb_ref[...] = vd.astype(BF)


def _even_in(x, g, w0, qn, wq, wqs, kvn, wk, wv, vone, cs_tab, sn_tab):
    T, D = x.shape
    pos_blocks = cs_tab.shape[0] // ROW_TILE

    def full(a):
        return pl.BlockSpec(a.shape, lambda i: (0,) * a.ndim)

    def rows(*tail):
        return pl.BlockSpec((ROW_TILE,) + tail, lambda i: (i,) + (0,) * len(tail))

    pos_spec = pl.BlockSpec((ROW_TILE, LANE), lambda i: (i % pos_blocks, 0))
    outs = [((1024,), BF), ((1024,), BF), ((1024,), BF), ((KV_RANK,), F32), ((ROPE,), F32),
            ((512,), BF), ((512,), BF), ((512,), BF), ((H_B, V_B), F32), ((H_B, V_B), F32)]
    return pl.pallas_call(
        functools.partial(_even_in_kernel,
                          a_scale=(NOPE + ROPE) ** -0.5 * LOG2E, b_scale=DH_B ** -0.5 * LOG2E),
        grid=(T // ROW_TILE,),
        in_specs=[rows(D), full(g), full(w0), full(qn), full(wq), full(wqs), full(kvn), full(wk), full(wv),
                  full(vone), pos_spec, pos_spec],
        out_specs=[rows(*tail) for tail, _ in outs],
        out_shape=[jax.ShapeDtypeStruct((T,) + tail, dt) for tail, dt in outs],
        compiler_params=_cparams("parallel"),
        name="even_in",
    )(x, g, w0, qn, wq, wqs, kvn, wk, wv, vone, cs_tab, sn_tab)


def _chunk_causal_mask(tq, tk):
    row = lax.broadcasted_iota(jnp.int32, (tq, tk), 0)
    col = lax.broadcasted_iota(jnp.int32, (tq, tk), 1)
    return (col // CHUNK) <= (row // CHUNK)


def _softmax_tile(s_ref, p_ref, m_ref, l_ref, a_ref, *, add_ref=None, off=None):
    tile = s_ref.shape[0]
    reps = tile // LANE
    blocks = [slice(rb * SOFTMAX_ROWS, (rb + 1) * SOFTMAX_ROWS) for rb in range(tile // SOFTMAX_ROWS)]

    def biased(rs):
        s = s_ref[rs, :]
        if add_ref is not None:
            s = s + add_ref[rs, :]
        return s

    for rs in blocks:
        m_old = m_ref[rs, :]
        red = jnp.broadcast_to(jnp.max(biased(rs), axis=-1, keepdims=True), m_old.shape)
        if off is not None:
            red = red + off
        m_new = jnp.maximum(m_old, red)
        a_ref[rs, :] = jnp.exp2(m_old - m_new)
        m_ref[rs, :] = m_new
    for rs in blocks:
        m_new = m_ref[rs, :]
        shift = m_new if off is None else m_new - off
        p = jnp.exp2(biased(rs) - jnp.concatenate([shift] * reps, axis=1))
        if l_ref is not None:
            l_ref[rs, :] = a_ref[rs, :] * l_ref[rs, :] + jnp.broadcast_to(
                jnp.sum(p, axis=-1, keepdims=True), m_new.shape)
        p_ref[rs, :] = p.astype(BF)


def _flash_causal(qi, n_chains, bufs, scores, update):
    s0, s1, p0, p1, m_ref, l_ref, a_ref, acc_ref = bufs
    m_ref[...] = jnp.full(m_ref.shape, NEG_INF, F32)
    l_ref[...] = jnp.zeros(l_ref.shape, F32)
    acc_ref[...] = jnp.zeros(acc_ref.shape, F32)
    scores(0, s0)

    def pair(jj, carry):
        j = 2 * jj
        scores(j + 1, s1)
        update(j, s0, p0, False)
        scores(j + 2, s0)
        update(j + 1, s1, p1, False)
        return carry

    lax.fori_loop(0, qi // 2, pair, 0)

    @pl.when(qi % 2 == 1)
    def _():
        scores(qi, s1)
        update(qi - 1, s0, p0, False)
        s0[...] = s1[...]

    update(qi, s0, p0, True)


def _flash_scratch(n_chains, tile, acc_width=LANE):
    s = pltpu.VMEM((n_chains, tile, tile), F32)
    p = pltpu.VMEM((n_chains, tile, tile), BF)
    stat = pltpu.VMEM((n_chains, tile, LANE), F32)
    return [s, s, p, p, stat, stat, stat, pltpu.VMEM((n_chains, tile, acc_width), F32)]


def _mla_prompt_kernel(q_ref, k_ref, v_ref, o_ref, s0, s1, p0, p1, m_ref, l_ref, a_ref, acc_ref, dmask_ref,
                       *, tile):
    qi = pl.program_id(2)

    @pl.when(qi == 0)
    def _():
        dmask_ref[...] = jnp.where(_chunk_causal_mask(tile, tile), 0.0, NEG_INF)

    sls = [slice(hh * HEAD_PAD, (hh + 1) * HEAD_PAD) for hh in range(2)]

    def scores(j, s_buf):
        start = pl.multiple_of(j * tile, tile)
        for c, sl in enumerate(sls):
            s_buf[c] = _dot_nt(q_ref[:, sl], k_ref[pl.ds(start, tile), sl])

    def update(j, s_buf, p_buf, diag):
        start = pl.multiple_of(j * tile, tile)
        for c, sl in enumerate(sls):
            _softmax_tile(s_buf.at[c], p_buf.at[c], m_ref.at[c], None, a_ref.at[c],
                          add_ref=dmask_ref if diag else None)
            acc_ref[c] = a_ref[c] * acc_ref[c] + _dot(p_buf[c], v_ref[pl.ds(start, tile), sl])

    _flash_causal(qi, 2, (s0, s1, p0, p1, m_ref, l_ref, a_ref, acc_ref), scores, update)
    lane = _lane_iota((tile, LANE))
    a0, a1 = acc_ref[0], acc_ref[1]
    l0 = jnp.sum(jnp.where(lane == V_A, a0, 0.0), axis=-1, keepdims=True)
    l1 = jnp.sum(jnp.where(lane == 0, a1, 0.0), axis=-1, keepdims=True)
    o_ref[...] = jnp.where(lane < V_A, a0 / l0, a1 / l1).astype(BF)


def _mla_prompt(qa, ka, va, *, batch, seq):
    tile = min(ATT_TILE, seq)
    nq = seq // tile
    return pl.pallas_call(
        functools.partial(_mla_prompt_kernel, tile=tile),
        grid=(batch, H_A // 2, nq),
        in_specs=[pl.BlockSpec((tile, 2 * HEAD_PAD), lambda b, p, qi: (b * nq + qi, p)),
                  pl.BlockSpec((seq, 2 * HEAD_PAD), lambda b, p, qi: (b, p)),
                  pl.BlockSpec((seq, 2 * HEAD_PAD), lambda b, p, qi: (b, p))],
        out_specs=pl.BlockSpec((tile, LANE), lambda b, p, qi: (b * nq + qi, p)),
        out_shape=jax.ShapeDtypeStruct((batch * seq, H_A * V_A), BF),
        scratch_shapes=_flash_scratch(2, tile) + [pltpu.VMEM((tile, tile), F32)],
        compiler_params=_cparams("parallel", "parallel", "arbitrary"),
        name="mla_prompt",
    )(qa, ka, va)


def _diff_prompt_kernel(slopes_ref, lamv_ref, subln_ref, q_ref, k_ref, v_ref, o_ref,
                        s0, s1, p0, p1, m_ref, l_ref, a_ref, acc_ref, key_ref, dbias_ref, q2_ref,
                        *, tile, lam_init):
    h = pl.program_id(1)
    qi = pl.program_id(2)
    slope = slopes_ref[h]
    q2_ref[...] = _split_halves(q_ref[...])
    @pl.when(qi == 0)
    def _():
        key_ref[...] = slope * lax.broadcasted_iota(jnp.int32, (8, tile), 1).astype(F32)
        row = lax.broadcasted_iota(jnp.int32, (tile, tile), 0)
        col = lax.broadcasted_iota(jnp.int32, (tile, tile), 1)
        later = (2.0 * slope) * jnp.minimum(row - col, 0).astype(F32)
        dbias_ref[...] = jnp.where(_chunk_causal_mask(tile, tile), later, NEG_INF)

    def scores(j, s_buf):
        k = k_ref[pl.ds(pl.multiple_of(j * tile, tile), tile), :]
        key_term = jnp.concatenate([key_ref[...]] * (tile // 8), axis=0)
        for c in range(2):
            s_buf[c] = _dot_nt(q2_ref[c * tile:(c + 1) * tile, :], k) + key_term

    def update(j, s_buf, p_buf, diag):
        start = pl.multiple_of(j * tile, tile)
        off = slope * (j * tile).astype(F32)
        for c in range(2):
            _softmax_tile(s_buf.at[c], p_buf.at[c], m_ref.at[c], l_ref.at[c], a_ref.at[c],
                          add_ref=dbias_ref if diag else None, off=off)
            acc_ref[c] = a_ref[c] * acc_ref[c] + _dot(p_buf[c], v_ref[pl.ds(start, tile), :])

    _flash_causal(qi, 2, (s0, s1, p0, p1, m_ref, l_ref, a_ref, acc_ref), scores, update)
    lam = _diff_lambda(lamv_ref[...], lam_init)
    o_ref[...] = _diff_finish(acc_ref[0] / l_ref[0], acc_ref[1] / l_ref[1], lam, subln_ref[...],
                              lam_init).astype(BF)


def _diff_prompt(slopes, lamv, subln, qd, kdb, vdb, *, batch, seq, lam_init):
    tile = min(ATT_TILE, seq)
    nq = seq // tile
    return pl.pallas_call(
        functools.partial(_diff_prompt_kernel, tile=tile, lam_init=lam_init),
        grid=(batch, H_B, nq),
        in_specs=[pl.BlockSpec(memory_space=pltpu.SMEM),
                  pl.BlockSpec(lamv.shape, lambda b, h, qi: (0, 0)),
                  pl.BlockSpec(subln.shape, lambda b, h, qi: (0, 0)),
                  pl.BlockSpec((tile, LANE), lambda b, h, qi: (b * nq + qi, h)),
                  pl.BlockSpec((seq, LANE), lambda b, h, qi: (b, h)),
                  pl.BlockSpec((seq, LANE), lambda b, h, qi: (b, h))],
        out_specs=pl.BlockSpec((tile, LANE), lambda b, h, qi: (b * nq + qi, h)),
        out_shape=jax.ShapeDtypeStruct((batch * seq, H_B * V_B), BF),
        scratch_shapes=_flash_scratch(2, tile) + [pltpu.VMEM((8, tile), F32), pltpu.VMEM((tile, tile), F32),
                                                  pltpu.VMEM((2 * tile, LANE), BF)],
        compiler_params=_cparams("parallel", "parallel", "arbitrary"),
        name="diff_prompt",
    )(slopes, lamv, subln, qd, kdb, vdb)


def _even_sample_kernel(slopes_ref, lamv_ref, subln_ref, wk_ref, wv_ref, vone_ref, e_ref,
                        qa_ref, ka_ref, va_ref, ckv_ref, kr_ref,
                        qd_ref, kdb_ref, vdb_ref, ck_ref, cv_ref,
                        oa_ref, ob_ref, *, lam_init):
    n_new = qa_ref.shape[0]
    n_past = ckv_ref.shape[0]
    ckvp = ckv_ref[...].astype(BF)
    krp = kr_ref[...].astype(BF)
    lane = _lane_iota((n_new, LANE))
    for pr in range(H_A // 2):
        res = []
        for hh in range(2):
            sl = slice((2 * pr + hh) * HEAD_PAD, (2 * pr + hh + 1) * HEAD_PAD)
            q = qa_ref[:, sl]
            kp = (_dot(ckvp, wk_ref[:, sl]) + _dot(krp, e_ref[:, sl])).astype(BF)
            vp = (_dot(ckvp, wv_ref[:, sl]) + vone_ref[:, sl]).astype(BF)
            res.append(_softmax_pv([_dot_nt(q, kp), _dot_nt(q, ka_ref[:, sl])], [vp, va_ref[:, sl]]))
        oa_ref[:, pr * LANE:(pr + 1) * LANE] = jnp.where(lane < V_A, res[0], res[1]).astype(BF)
    rowp = lax.broadcasted_iota(jnp.int32, (n_new, n_past), 0)
    colp = lax.broadcasted_iota(jnp.int32, (n_new, n_past), 1)
    dist_p = (rowp - colp + n_past).astype(F32)
    dist_p = jnp.concatenate([dist_p, dist_p], axis=0)
    rown = lax.broadcasted_iota(jnp.int32, (n_new, n_new), 0)
    coln = lax.broadcasted_iota(jnp.int32, (n_new, n_new), 1)
    dist_n = jnp.abs(rown - coln).astype(F32)
    dist_n = jnp.concatenate([dist_n, dist_n], axis=0)
    lam = _diff_lambda(lamv_ref[...], lam_init)
    for h in range(H_B):
        sl = slice(h * LANE, (h + 1) * LANE)
        slope = slopes_ref[h]
        q2x = _split_halves(qd_ref[:, sl])
        kp = ck_ref[:, h, :].astype(BF)
        vp = cv_ref[:, h, :].astype(BF)
        s_p = _dot_nt(q2x, kp) - slope * dist_p
        s_n = _dot_nt(q2x, kdb_ref[:, sl]) - slope * dist_n
        o = _softmax_pv([s_p, s_n], [vp, vdb_ref[:, sl]])
        ob_ref[:, sl] = _diff_finish(o[:n_new], o[n_new:], lam, subln_ref[...], lam_init).astype(BF)


def _even_sample(slopes, lamv, subln, wk, wv, vone, e_mat, qa, ka, va, ckv_c, kr_c, qd, kdb, vdb, ck_c, cv_c,
                 *, n_prompt_rows, dec_seq, lam_init):
    dec_batch, n_past = ckv_c.shape[0], ckv_c.shape[1]
    base = n_prompt_rows // dec_seq

    def full(a):
        return pl.BlockSpec(a.shape, lambda s: (0,) * a.ndim)

    def new(width):
        return pl.BlockSpec((dec_seq, width), lambda s: (base + s, 0))

    def cache(*tail):
        return pl.BlockSpec((None, n_past) + tail, lambda s: (s, 0) + (0,) * len(tail))

    return pl.pallas_call(
        functools.partial(_even_sample_kernel, lam_init=lam_init),
        grid=(dec_batch,),
        in_specs=[pl.BlockSpec(memory_space=pltpu.SMEM), full(lamv), full(subln), full(wk), full(wv), full(vone),
                  full(e_mat), new(1024), new(1024), new(1024), cache(KV_RANK), cache(ROPE),
                  new(512), new(512), new(512), cache(H_B, V_B), cache(H_B, V_B)],
        out_specs=[pl.BlockSpec((dec_seq, 512), lambda s: (s, 0))] * 2,
        out_shape=[jax.ShapeDtypeStruct((dec_batch * dec_seq, 512), BF)] * 2,
        compiler_params=_cparams("parallel"),
        name="even_sample",
    )(slopes, lamv, subln, wk, wv, vone, e_mat, qa, ka, va, ckv_c, kr_c, qd, kdb, vdb, ck_c, cv_c)


def _route(logits, carry):
    tm = logits.shape[0]
    lane = _lane_iota(logits.shape).astype(F32)
    big = float(LANE)
    g_mask = lane < N_GROUPS
    gl = jnp.where(g_mask, logits, NEG_INF)
    gmax = jnp.max(gl, axis=-1, keepdims=True)
    g_sel = jnp.min(jnp.where(gl == gmax, lane, big), axis=-1, keepdims=True)
    p_grp = 1.0 / jnp.sum(jnp.exp(gl - gmax), axis=-1, keepdims=True)
    lo = ROUTE_OFF + EPG * g_sel
    el = jnp.where((lane >= lo) & (lane < lo + EPG), logits, NEG_INF)
    v1 = jnp.max(el, axis=-1, keepdims=True)
    i1 = jnp.min(jnp.where(el == v1, lane, big), axis=-1, keepdims=True)
    el2 = jnp.where(lane == i1, NEG_INF, el)
    v2 = jnp.max(el2, axis=-1, keepdims=True)
    i2 = jnp.min(jnp.where(el2 == v2, lane, big), axis=-1, keepdims=True)
    ex = jnp.exp(v2 - v1)
    den = 1.0 + ex
    gate1 = (1.0 / den) * p_grp
    gate2 = (ex / den) * p_grp
    onehot = jnp.where((lane == i1) | (lane == i2), 1.0, 0.0)
    row = lax.broadcasted_iota(jnp.int32, (tm, tm), 0)
    col = lax.broadcasted_iota(jnp.int32, (tm, tm), 1)
    tri = jnp.where(row > col, 1.0, 0.0).astype(BF)
    cum = _dot(tri, onehot.astype(BF)) + carry
    r1 = jnp.sum(jnp.where(lane == i1, cum, 0.0), axis=-1, keepdims=True)
    r2 = jnp.sum(jnp.where(lane == i2, cum, 0.0), axis=-1, keepdims=True)
    packed = jnp.zeros_like(logits)
    for pos, val in enumerate([i1 - ROUTE_OFF, i2 - ROUTE_OFF, gate1, gate2, r1, r2]):
        packed = jnp.where(lane == pos, val, packed)
    return packed, carry + jnp.sum(onehot, axis=0, keepdims=True)


def _out_router_kernel(*refs, splits, n_prompt_tiles):
    i = pl.program_id(0)
    offs = [sum(splits[:k]) for k in range(len(splits))]
    n_mix = len(splits) - 1

    def pick(k):
        parts = refs[offs[k]:offs[k] + splits[k]]
        if splits[k] == 1:
            return parts[0][...]
        return jnp.where(i < n_prompt_tiles, parts[0][...], parts[1][...])

    rest = refs[sum(splits):]
    w_refs = rest[:n_mix]
    g_ref, wr_ref, br_ref, x1_ref, xn_ref, route_ref, cnt_ref, carry_ref = rest[n_mix:]

    @pl.when(i == 0)
    def _():
        carry_ref[...] = jnp.zeros_like(carry_ref)

    x1 = pick(0)
    for k, w_ref in enumerate(w_refs):
        x1 = x1 + _dot(pick(1 + k), w_ref[...])
    x1_ref[...] = x1
    xb = _rms(x1, g_ref[...], NORM_EPS).astype(BF)
    xn_ref[...] = xb
    logits = _dot(xb, wr_ref[...]) + br_ref[...]
    packed, carry = _route(logits, carry_ref[...])
    route_ref[...] = packed
    carry_ref[...] = carry
    cnt_ref[...] = carry


def _out_router(row_inputs, ws, g, wr, br):
    splits = tuple(len(parts) for parts in row_inputs)
    T = sum(a.shape[0] for a in row_inputs[0])
    D = row_inputs[0][0].shape[1]
    tile = WIDE_TILE
    n_p = max([parts[0].shape[0] // tile for parts in row_inputs if len(parts) == 2], default=0)

    def row_specs(parts):
        if len(parts) == 1:
            return [pl.BlockSpec((tile, parts[0].shape[1]), lambda i: (i, 0))]
        return [pl.BlockSpec((tile, parts[0].shape[1]), lambda i: (jnp.minimum(i, n_p - 1), 0)),
                pl.BlockSpec((tile, parts[1].shape[1]), lambda i: (jnp.maximum(i - n_p, 0), 0))]

    def full(a):
        return pl.BlockSpec(a.shape, lambda i: (0,) * a.ndim)

    def rows(width):
        return pl.BlockSpec((tile, width), lambda i: (i, 0))

    flat_rows = [a for parts in row_inputs for a in parts]
    return pl.pallas_call(
        functools.partial(_out_router_kernel, splits=splits, n_prompt_tiles=n_p),
        grid=(T // tile,),
        in_specs=[s for parts in row_inputs for s in row_specs(parts)] + [full(w) for w in ws]
        + [full(g), full(wr), full(br)],
        out_specs=[rows(D), rows(D), rows(LANE), pl.BlockSpec((1, LANE), lambda i: (0, 0))],
        out_shape=[jax.ShapeDtypeStruct((T, D), F32), jax.ShapeDtypeStruct((T, D), BF),
                   jax.ShapeDtypeStruct((T, LANE), F32), jax.ShapeDtypeStruct((1, LANE), F32)],
        scratch_shapes=[pltpu.VMEM((1, LANE), F32)],
        compiler_params=_cparams("arbitrary"),
        name="out_router",
    )(*flat_rows, *ws, g, wr, br)


def _experts_kernel(be_ref, nu_ref, xb_ref, wg_ref, wu_ref, wd_ref, y_ref, wgb_ref, wub_ref, wdb_ref):
    i = pl.program_id(0)
    used = i < nu_ref[0]

    @pl.when(used & ((i == 0) | (be_ref[i] != be_ref[jnp.maximum(i - 1, 0)])))
    def _():
        wgb_ref[...] = wg_ref[...].astype(BF)
        wub_ref[...] = wu_ref[...].astype(BF)
        wdb_ref[...] = wd_ref[...].astype(BF)

    @pl.when(used)
    def _():
        xb = xb_ref[...]
        a = _dot(xb, wgb_ref[...])
        b = _dot(xb, wub_ref[...])
        hid = (a * jax.nn.sigmoid(a)) * b
        y_ref[...] = _dot(hid.astype(BF), wdb_ref[...]).astype(y_ref.dtype)

    @pl.when(jnp.logical_not(used))
    def _():
        y_ref[...] = jnp.zeros_like(y_ref)


def _experts(block_expert, n_used, xb, wg, wu, wd, *, layer):
    L, D = xb.shape
    n_blocks = L // MOE_BLOCK
    grid_spec = pltpu.PrefetchScalarGridSpec(
        num_scalar_prefetch=2,
        grid=(n_blocks,),
        in_specs=[pl.BlockSpec((MOE_BLOCK, D), lambda i, be, nu: (i, 0)),
                  pl.BlockSpec((None, None, D, D_EXPERT), lambda i, be, nu: (layer, be[i], 0, 0)),
                  pl.BlockSpec((None, None, D, D_EXPERT), lambda i, be, nu: (layer, be[i], 0, 0)),
                  pl.BlockSpec((None, None, D_EXPERT, D), lambda i, be, nu: (layer, be[i], 0, 0))],
        out_specs=pl.BlockSpec((MOE_BLOCK, D), lambda i, be, nu: (i, 0)),
        scratch_shapes=[pltpu.VMEM((D, D_EXPERT), BF), pltpu.VMEM((D, D_EXPERT), BF), pltpu.VMEM((D_EXPERT, D), BF)],
    )
    return pl.pallas_call(
        _experts_kernel,
        grid_spec=grid_spec,
        out_shape=jax.ShapeDtypeStruct((L, D), BF),
        compiler_params=_cparams("arbitrary"),
        name="experts",
    )(block_expert, n_used, xb, wg, wu, wd)


def _moe(xn, route, cnt, wg, wu, wd, *, layer):
    T = xn.shape[0]
    e = route[:, 0:2].astype(jnp.int32)
    rank = route[:, 4:6].astype(jnp.int32)
    counts = cnt[0, ROUTE_OFF:ROUTE_OFF + N_EXPERTS].astype(jnp.int32)
    padded = ((counts + MOE_BLOCK - 1) // MOE_BLOCK) * MOE_BLOCK
    pend = jnp.cumsum(padded)
    pstart = pend - padded
    dest = jnp.sum(jnp.where(e[:, :, None] == jnp.arange(N_EXPERTS)[None, None, :], pstart[None, None, :], 0),
                   axis=-1) + rank
    n_blocks = -(-(2 * T) // MOE_BLOCK) + N_EXPERTS
    L = n_blocks * MOE_BLOCK
    tok = jnp.repeat(jnp.arange(T, dtype=jnp.int32), 2)
    buf_tok = (jnp.arange(L, dtype=jnp.int32) % T).at[dest.reshape(-1)].set(
        tok, unique_indices=True, mode="promise_in_bounds")
    block_start = jnp.arange(n_blocks, dtype=jnp.int32) * MOE_BLOCK
    block_expert = jnp.minimum(jnp.sum(pend[None, :] <= block_start[:, None], axis=1), N_EXPERTS - 1).astype(jnp.int32)
    n_used = (pend[-1:] // MOE_BLOCK).astype(jnp.int32)
    xb = xn.at[buf_tok].get(mode="promise_in_bounds")
    yb = _experts(block_expert, n_used, xb, wg, wu, wd, layer=layer)
    return (yb.at[dest[:, 0]].get(mode="promise_in_bounds"), yb.at[dest[:, 1]].get(mode="promise_in_bounds"))


def _gates(route):
    lane = _lane_iota(route.shape)
    g0 = jnp.sum(jnp.where(lane == 2, route, 0.0), axis=-1, keepdims=True)
    g1 = jnp.sum(jnp.where(lane == 3, route, 0.0), axis=-1, keepdims=True)
    return g0, g1


def _odd_in_kernel(x_ref, y0_ref, y1_ref, route_ref, g_ref, w_ref, x2_ref, q_ref, k_ref, v_ref, *, scale):
    g0, g1 = _gates(route_ref[...])
    x2 = x_ref[...] + (y0_ref[...].astype(F32) * g0 + y1_ref[...].astype(F32) * g1)
    x2_ref[...] = x2
    h = _rms(x2, g_ref[...], NORM_EPS).astype(BF)
    y = _dot(h, w_ref[...])
    n = q_ref.shape[1]
    q_ref[...] = (y[:, :n] * scale).astype(BF)
    k_ref[...] = y[:, n:2 * n].astype(BF)
    v_ref[...] = y[:, 2 * n:].astype(BF)


def _odd_in(x1, y0, y1, route, g, w):
    T, D = x1.shape
    n = w.shape[1] // 3

    def rows(width):
        return pl.BlockSpec((WIDE_TILE, width), lambda i: (i, 0))

    def full(a):
        return pl.BlockSpec(a.shape, lambda i: (0,) * a.ndim)

    return pl.pallas_call(
        functools.partial(_odd_in_kernel, scale=DH_C ** -0.5 * LOG2E),
        grid=(T // WIDE_TILE,),
        in_specs=[rows(D), rows(D), rows(D), rows(LANE), full(g), full(w)],
        out_specs=[rows(D), rows(n), rows(n), rows(n)],
        out_shape=[jax.ShapeDtypeStruct((T, D), F32)] + [jax.ShapeDtypeStruct((T, n), BF)] * 3,
        compiler_params=_cparams("parallel"),
        name="odd_in",
    )(x1, y0, y1, route, g, w)


def _state_rows_kernel(ids_ref, x_ref, g_ref, w_ref, o_ref):
    del ids_ref
    h = _rms(x_ref[...], g_ref[...], NORM_EPS).astype(BF)
    o_ref[...] = _dot(h, w_ref[...])


def _state_rows(tile_ids, x, g, w):
    D = x.shape[1]
    n = tile_ids.shape[0]
    grid_spec = pltpu.PrefetchScalarGridSpec(
        num_scalar_prefetch=1,
        grid=(n,),
        in_specs=[pl.BlockSpec((ROW_TILE, D), lambda i, ids: (ids[i], 0)),
                  pl.BlockSpec(g.shape, lambda i, ids: (0, 0)),
                  pl.BlockSpec(w.shape, lambda i, ids: (0, 0))],
        out_specs=pl.BlockSpec((ROW_TILE, w.shape[1]), lambda i, ids: (i, 0)),
    )
    return pl.pallas_call(
        _state_rows_kernel,
        grid_spec=grid_spec,
        out_shape=jax.ShapeDtypeStruct((n * ROW_TILE, w.shape[1]), F32),
        compiler_params=_cparams("parallel"),
        name="state_rows",
    )(tile_ids, x, g, w)


def _band_prompt_kernel(q_ref, k_ref, v_ref, bias_ref, o_ref, q2_ref, s0, s1, p_ref, l_ref,
                        *, tile, tiles_per_step):
    lane = _lane_iota((tile, LANE))
    q2_ref[...] = _split_halves(q_ref[...])
    n_rows = tile * tiles_per_step

    def key_tiles(t):
        qi = pl.program_id(2) * tiles_per_step + t
        out = []
        for kt in range(3):
            start = (qi + kt - 2) * tile
            neg = jnp.where(start >= 0, 0.0, NEG_INF)
            out.append((pl.multiple_of(jnp.maximum(start, 0), tile), neg))
        return out

    def scores(t, s_buf):
        for hh in range(2):
            q = q2_ref[hh * n_rows + t * tile:hh * n_rows + (t + 1) * tile, :]
            for kt, (start, neg) in enumerate(key_tiles(t)):
                s = _dot_nt(q, k_ref[pl.ds(start, tile), :])
                s_buf[hh, :, kt * tile:(kt + 1) * tile] = s + neg if kt < 2 else s

    def softmax_pv(t, s_buf):
        outs = []
        slot = t % 2
        for hh in range(2):
            for rb in range(tile // SOFTMAX_ROWS):
                rs = slice(rb * SOFTMAX_ROWS, (rb + 1) * SOFTMAX_ROWS)
                s = s_buf[hh, rs, :] + bias_ref[hh * tile + rb * SOFTMAX_ROWS:hh * tile + (rb + 1) * SOFTMAX_ROWS, :]
                p = jnp.exp2(s - jnp.max(s, axis=-1, keepdims=True))
                l_ref[slot, hh, rs, :] = jnp.broadcast_to(jnp.sum(p, axis=-1, keepdims=True), (SOFTMAX_ROWS, LANE))
                p_ref[slot, hh, rs, :] = p.astype(BF)
            acc = None
            for kt, (start, _) in enumerate(key_tiles(t)):
                pv = _dot(p_ref[slot, hh, :, kt * tile:(kt + 1) * tile], v_ref[pl.ds(start, tile), :])
                acc = pv if acc is None else acc + pv
            outs.append(acc / l_ref[slot, hh])
        o_ref[t * tile:(t + 1) * tile, :] = jnp.where(lane < DH_C, outs[0], outs[1]).astype(BF)

    bufs = [s0, s1]
    scores(0, bufs[0])
    for t in range(tiles_per_step):
        if t + 1 < tiles_per_step:
            scores(t + 1, bufs[(t + 1) % 2])
        softmax_pv(t, bufs[t % 2])


def _band_prompt(q, k, v, bias, *, batch, seq):
    tile = BAND_TILE
    tps = min(BAND_TILES_PER_STEP, seq // tile)
    nq = seq // (tile * tps)
    return pl.pallas_call(
        functools.partial(_band_prompt_kernel, tile=tile, tiles_per_step=tps),
        grid=(H_C // 2, batch, nq),
        in_specs=[pl.BlockSpec((tile * tps, LANE), lambda p, b, qi: (b * nq + qi, p)),
                  pl.BlockSpec((seq, LANE), lambda p, b, qi: (b, p)),
                  pl.BlockSpec((seq, LANE), lambda p, b, qi: (b, p)),
                  pl.BlockSpec((None, 2 * tile, 3 * tile), lambda p, b, qi: (p, 0, 0))],
        out_specs=pl.BlockSpec((tile * tps, LANE), lambda p, b, qi: (b * nq + qi, p)),
        out_shape=jax.ShapeDtypeStruct((batch * seq, H_C * DH_C), BF),
        scratch_shapes=[pltpu.VMEM((2 * tile * tps, LANE), BF),
                        pltpu.VMEM((2, tile, 3 * tile), F32), pltpu.VMEM((2, tile, 3 * tile), F32),
                        pltpu.VMEM((2, 2, tile, 3 * tile), BF), pltpu.VMEM((2, 2, tile, LANE), F32)],
        compiler_params=_cparams("parallel", "parallel", "arbitrary"),
        name="band_prompt",
    )(q, k, v, bias)


def _band_sample_kernel(q_ref, k_ref, v_ref, ck_ref, cv_ref, bias_ref, o_ref):
    n_new = q_ref.shape[0]
    n_past = ck_ref.shape[0]
    lane = _lane_iota((n_new, LANE))
    for pr in range(H_C // 2):
        sl = slice(pr * LANE, (pr + 1) * LANE)
        q2x = _split_halves(q_ref[:, sl])
        s_p = _dot_nt(q2x, ck_ref[:, sl].astype(BF)) + bias_ref[pr, :, 0:n_past]
        s_n = _dot_nt(q2x, k_ref[:, sl]) + bias_ref[pr, :, n_past:n_past + n_new]
        o = _softmax_pv([s_p, s_n], [cv_ref[:, sl].astype(BF), v_ref[:, sl]])
        o_ref[:, sl] = jnp.where(lane < DH_C, o[:n_new], o[n_new:]).astype(BF)


def _band_sample(q, k, v, ck, cv, bias, *, n_prompt_rows, dec_seq):
    dec_batch, n_past, width = ck.shape
    base = n_prompt_rows // dec_seq
    new = pl.BlockSpec((dec_seq, width), lambda s: (base + s, 0))
    cache = pl.BlockSpec((None, n_past, width), lambda s: (s, 0, 0))
    return pl.pallas_call(
        _band_sample_kernel,
        grid=(dec_batch,),
        in_specs=[new, new, new, cache, cache, pl.BlockSpec(bias.shape, lambda s: (0, 0, 0))],
        out_specs=pl.BlockSpec((dec_seq, width), lambda s: (s, 0)),
        out_shape=jax.ShapeDtypeStruct((dec_batch * dec_seq, width), BF),
        compiler_params=_cparams("parallel"),
        name="band_sample",
    )(q, k, v, ck, cv, bias)


def _final_kernel(x_ref, y0_ref, y1_ref, route_ref, g_ref, o_ref):
    g0, g1 = _gates(route_ref[...])
    x = x_ref[...] + (y0_ref[...].astype(F32) * g0 + y1_ref[...].astype(F32) * g1)
    o_ref[...] = _rms(x, g_ref[...], NORM_EPS)


def _final(x, y0, y1, route, g, *, first_tile, n_tiles):
    D = x.shape[1]

    def rows(width):
        return pl.BlockSpec((WIDE_TILE, width), lambda i: (first_tile + i, 0))

    return pl.pallas_call(
        _final_kernel,
        grid=(n_tiles,),
        in_specs=[rows(D), rows(D), rows(D), rows(LANE), pl.BlockSpec(g.shape, lambda i: (0, 0))],
        out_specs=pl.BlockSpec((WIDE_TILE, D), lambda i: (i, 0)),
        out_shape=jax.ShapeDtypeStruct((n_tiles * WIDE_TILE, D), F32),
        compiler_params=_cparams("parallel"),
        name="final_norm",
    )(x, y0, y1, route, g)


def _prep_even_weights(w_in, w_qup, w_kvup):
    D = w_in.shape[0]
    a_in = Q_RANK + KV_RANK + ROPE
    bq = H_B * 2 * DH_B
    wcq, wckv, wkr = w_in[:, :Q_RANK], w_in[:, Q_RANK:Q_RANK + KV_RANK], w_in[:, Q_RANK + KV_RANK:a_in]
    wqd, wkd, wvd = w_in[:, a_in:a_in + bq], w_in[:, a_in + bq:a_in + 2 * bq], w_in[:, a_in + 2 * bq:]
    half = ROPE // 2

    def z(n):
        return jnp.zeros((D, n), w_in.dtype)

    kr_m = jnp.concatenate([z(NOPE), wkr, z(HEAD_PAD - NOPE - ROPE)], axis=1)
    kr_s = jnp.concatenate([z(NOPE), wkr[:, half:], wkr[:, :half], z(HEAD_PAD - NOPE - ROPE)], axis=1)
    w0 = jnp.concatenate([wcq, wckv, kr_m, kr_s, wqd, wkd, wvd], axis=1).astype(BF)
    wq3 = w_qup.reshape(Q_RANK, H_A, NOPE + ROPE)
    nope, r1, r2 = wq3[:, :, :NOPE], wq3[:, :, NOPE:NOPE + half], wq3[:, :, NOPE + half:]
    zq = jnp.zeros((Q_RANK, H_A, HEAD_PAD - NOPE - ROPE), w_qup.dtype)
    wq = jnp.concatenate([nope, r1, r2, zq], axis=-1).reshape(Q_RANK, H_A * HEAD_PAD).astype(BF)
    wqs = jnp.concatenate([jnp.zeros_like(nope), r2, r1, zq], axis=-1).reshape(Q_RANK, H_A * HEAD_PAD).astype(BF)
    wkv3 = w_kvup.reshape(KV_RANK, H_A, NOPE + V_A)
    wk = jnp.concatenate([wkv3[:, :, :NOPE], jnp.zeros((KV_RANK, H_A, HEAD_PAD - NOPE), w_kvup.dtype)],
                         axis=-1).reshape(KV_RANK, H_A * HEAD_PAD).astype(BF)
    wv4 = wkv3[:, :, NOPE:].reshape(KV_RANK, H_A // 2, 2, V_A)
    zv = jnp.zeros((KV_RANK, H_A // 2, HEAD_PAD - V_A), w_kvup.dtype)
    wv = jnp.concatenate([wv4[:, :, 0], zv, zv, wv4[:, :, 1]], axis=-1).reshape(KV_RANK, H_A * HEAD_PAD).astype(BF)
    vone = jnp.zeros((H_A // 2, 2 * HEAD_PAD), F32).at[:, V_A].set(1.0).at[:, HEAD_PAD].set(1.0)
    r = jnp.arange(ROPE)
    e_mat = jnp.zeros((ROPE, H_A, HEAD_PAD), F32).at[r[:, None], jnp.arange(H_A)[None, :], NOPE + r[:, None]].set(1.0)
    return w0, wq, wqs, wk, wv, vone.reshape(1, H_A * HEAD_PAD), e_mat.reshape(ROPE, H_A * HEAD_PAD).astype(BF)


def _rope_tables(pos):
    half = ROPE // 2
    inv = jnp.power(ROPE_BASE, -jnp.arange(half, dtype=F32) / half)
    ang = pos.astype(F32)[:, None] * inv[None, :]
    c, s = jnp.cos(ang), jnp.sin(ang)
    n = pos.shape[0]
    pad = jnp.zeros((n, HEAD_PAD - NOPE - ROPE), F32)
    cs = jnp.concatenate([jnp.ones((n, NOPE), F32), c, c, pad], axis=1)
    sn = jnp.concatenate([jnp.zeros((n, NOPE), F32), -s, s, pad], axis=1)
    return cs, sn


def _band_bias_tiles(table, n_rows, n_keys, key_offset, masked):
    i = jnp.arange(n_rows)[:, None]
    kpos = jnp.arange(n_keys)[None, :] - key_offset
    d_max = n_rows - 1 + key_offset
    w = n_rows + n_keys
    d = d_max - jnp.arange(w)
    rev = (table.astype(F32) * LOG2E)[:, jnp.clip(d, -REL_CLIP, REL_CLIP) + REL_CLIP]
    skew = jnp.tile(rev, (1, n_rows))[:, :n_rows * (w - 1)].reshape(-1, n_rows, w - 1)
    bias = skew[:, :, n_rows - 1:n_rows - 1 + n_keys]
    if masked:
        qc, kc = i // CHUNK, jnp.floor_divide(kpos, CHUNK)
        vis = (kc <= qc) & (kc >= qc - LEFT_CHUNKS)
        bias = jnp.where(vis[None], bias, NEG_INF)
    return bias.reshape(H_C // 2, 2 * n_rows, n_keys)


def kernel(x_prompt, x_sample, cache_mla_ckv, cache_mla_krope, cache_diff_k, cache_diff_v, cache_band_k, cache_band_v, ln_mix, w_in_even, mla_q_norm, mla_w_qup, mla_kv_norm, mla_w_kvup, diff_lam_q1, diff_lam_k1, diff_lam_q2, diff_lam_k2, diff_subln, w_out_even, w_in_odd, band_rel_bias, w_out_odd, ln_ffn, moe_w_group, moe_b_group, moe_w_router, moe_b_router, moe_w_gate, moe_w_up, moe_w_down, ln_final):
    B, S, D = x_prompt.shape
    DB, DS, _ = x_sample.shape
    n_past = cache_mla_ckv.shape[2]
    c_past = cache_band_k.shape[2]
    assert ln_mix.shape[0] == 2 and S % min(ATT_TILE, S) == 0 and S % BAND_TILE == 0
    assert (B * S) % WIDE_TILE == 0 and (DB * DS) % WIDE_TILE == 0 and ROW_TILE % DS == 0 and DS == CHUNK
    assert n_past % CHUNK == 0 and c_past == LEFT_CHUNKS * CHUNK and S >= c_past and c_past % ROW_TILE == 0
    BS, NS = B * S, DB * DS
    T = BS + NS
    n_p, n_s = BS // ROW_TILE, NS // ROW_TILE
    xp, xs = x_prompt.reshape(BS, D), x_sample.reshape(NS, D)
    row = lambda a: a.reshape(1, -1)

    w0, wq, wqs, wk, wv, vone, e_mat = _prep_even_weights(w_in_even[0], mla_w_qup[0], mla_w_kvup[0])
    cs_p, sn_p = _rope_tables(jnp.arange(S))
    cs_s, sn_s = _rope_tables(n_past + jnp.arange(DS))
    reps = ROW_TILE // DS
    even_w = (row(ln_mix[0]), w0, row(mla_q_norm[0]), wq, wqs, row(mla_kv_norm[0]), wk, wv, vone)
    qa, ka, va, ckv_p, kr_p, qd, kdb, vdb, kd_p, vd_p = _even_in(xp, *even_w, cs_p, sn_p)
    qa_s, ka_s, va_s, ckv_s, kr_s, qd_s, kdb_s, vdb_s, kd_s, vd_s = _even_in(
        xs, *even_w, jnp.tile(cs_s, (reps, 1)), jnp.tile(sn_s, (reps, 1)))
    slopes = jnp.exp2(-8.0 * jnp.arange(1, H_B + 1, dtype=F32) / H_B) * LOG2E
    lamv = jnp.stack([diff_lam_q1[0], diff_lam_k1[0], diff_lam_q2[0], diff_lam_k2[0]]).astype(F32)
    subln = row(diff_subln[0])
    lam_init = 0.8 - 0.6 * math.exp(-0.3 * 0)
    oa_p = _mla_prompt(qa, ka, va, batch=B, seq=S)
    ob_p = _diff_prompt(slopes, lamv, subln, qd, kdb, vdb, batch=B, seq=S, lam_init=lam_init)
    oa_s, ob_s = _even_sample(
        slopes, lamv, subln, wk, wv, vone, e_mat, qa_s, ka_s, va_s, cache_mla_ckv[0], cache_mla_krope[0],
        qd_s, kdb_s, vdb_s, cache_diff_k[0], cache_diff_v[0],
        n_prompt_rows=0, dec_seq=DS, lam_init=lam_init)

    def router_weights(l):
        wr = jnp.concatenate([moe_w_group[l], moe_w_router[l],
                              jnp.zeros((D, LANE - N_GROUPS - N_EXPERTS), F32)], axis=1).astype(BF)
        br = jnp.concatenate([moe_b_group[l], moe_b_router[l],
                              jnp.zeros((LANE - N_GROUPS - N_EXPERTS,), F32)]).astype(F32)
        return wr, row(br)

    n_a = H_A * V_A
    wo = w_out_even[0].astype(BF)
    wr, br = router_weights(0)
    x1, xn, route, cnt = _out_router([(xp, xs), (oa_p, oa_s), (ob_p, ob_s)], [wo[:n_a], wo[n_a:]],
                                     row(ln_ffn[0]), wr, br)
    y0, y1 = _moe(xn, route, cnt, moe_w_gate, moe_w_up, moe_w_down, layer=0)

    w_odd = w_in_odd[0].astype(BF)
    n_c = H_C * DH_C
    x2, qc, kc, vc = _odd_in(x1, y0, y1, route, row(ln_mix[1]), w_odd)
    tail = c_past // ROW_TILE
    tiles_per_seq = S // ROW_TILE
    tile_ids = jnp.concatenate(
        [jnp.arange(tiles_per_seq - tail, tiles_per_seq, dtype=jnp.int32) + b * tiles_per_seq for b in range(B)]
        + [jnp.arange(n_p, n_p + n_s, dtype=jnp.int32)])
    st = _state_rows(tile_ids, x2, row(ln_mix[1]), w_odd[:, n_c:])
    bias_p = _band_bias_tiles(band_rel_bias[0], BAND_TILE, 3 * BAND_TILE, 2 * BAND_TILE, True)
    bias_s = _band_bias_tiles(band_rel_bias[0], DS, c_past + DS, c_past, False)
    oc_p = _band_prompt(qc, kc, vc, bias_p, batch=B, seq=S)
    oc_s = _band_sample(qc, kc, vc, cache_band_k[0].reshape(DB, c_past, n_c), cache_band_v[0].reshape(DB, c_past, n_c),
                        bias_s, n_prompt_rows=BS, dec_seq=DS)
    wr, br = router_weights(1)
    x3, xn, route, cnt = _out_router([(x2,), (oc_p, oc_s)], [w_out_odd[0].astype(BF)], row(ln_ffn[1]), wr, br)
    y0, y1 = _moe(xn, route, cnt, moe_w_gate, moe_w_up, moe_w_down, layer=1)
    g_fin = row(ln_final)
    y_prompt = _final(x3, y0, y1, route, g_fin, first_tile=0, n_tiles=BS // WIDE_TILE).reshape(B, S, D)
    y_sample = _final(x3, y0, y1, route, g_fin, first_tile=BS // WIDE_TILE, n_tiles=NS // WIDE_TILE).reshape(DB, DS, D)

    def shaped(a_p, a_s, *tail_shape):
        return a_p.reshape(1, B, S, *tail_shape), a_s.reshape(1, DB, DS, *tail_shape)

    ckv_p, ckv_s = shaped(ckv_p, ckv_s, KV_RANK)
    kr_p, kr_s = shaped(kr_p, kr_s, ROPE)
    kd_p, kd_s = shaped(kd_p, kd_s, H_B, 2 * DH_B)
    vd_p, vd_s = shaped(vd_p, vd_s, H_B, V_B)
    n_tail = B * c_past
    bk_p = st[:n_tail, :n_c].reshape(1, B, c_past, H_C, DH_C)
    bv_p = st[:n_tail, n_c:].reshape(1, B, c_past, H_C, DH_C)
    k_new = st[n_tail:, :n_c].reshape(DB, DS, H_C, DH_C)
    v_new = st[n_tail:, n_c:].reshape(DB, DS, H_C, DH_C)
    bk_s = jnp.concatenate([cache_band_k[0][:, DS:], k_new], axis=1)[None]
    bv_s = jnp.concatenate([cache_band_v[0][:, DS:], v_new], axis=1)[None]
    return (y_prompt, y_sample, ckv_p, kr_p, kd_p, vd_p, bk_p, bv_p, ckv_s, kr_s, kd_s, vd_s, bk_s, bv_s)
```

```python
import functools
import math

import jax
import jax.numpy as jnp
from jax import lax
from jax.experimental import pallas as pl
from jax.experimental.pallas import tpu as pltpu

BF = jnp.bfloat16
F32 = jnp.float32
NEG_INF = float("-inf")
LOG2E = math.log2(math.e)

CHUNK = 64
NORM_EPS = 1e-6
SUBLN_EPS = 1e-5
H_A, NOPE, ROPE, V_A, Q_RANK, KV_RANK = 8, 64, 32, 64, 384, 256
ROPE_BASE = 10000.0
H_B, DH_B, V_B = 4, 64, 128
H_C, DH_C, LEFT_CHUNKS, REL_CLIP = 16, 64, 8, 128
N_GROUPS, EPG, N_EXPERTS, D_EXPERT = 4, 8, 32, 512
LANE = 128
HEAD_PAD = 128
ROUTE_OFF = N_GROUPS

ROW_TILE = 256
WIDE_TILE = 512
ATT_TILE = 512
SOFTMAX_ROWS = 32
BAND_TILE = 256
BAND_TILES_PER_STEP = 4
MOE_BLOCK = 512
VMEM_LIMIT = 56 * 1024 * 1024


def _cparams(*sem):
    return pltpu.CompilerParams(dimension_semantics=sem, vmem_limit_bytes=VMEM_LIMIT)


def _rms(x, g, eps):
    return x * lax.rsqrt(jnp.mean(x * x, axis=-1, keepdims=True) + eps) * g


def _dot(a, b):
    return jnp.dot(a, b, preferred_element_type=F32)


def _dot_nt(a, b):
    return lax.dot_general(a, b, (((1,), (1,)), ((), ())), preferred_element_type=F32)


def _lane_iota(shape):
    return lax.broadcasted_iota(jnp.int32, shape, len(shape) - 1)


def _split_halves(q):
    qf = q.astype(F32)
    lane = _lane_iota(qf.shape)
    return jnp.concatenate([jnp.where(lane < 64, qf, 0.0), jnp.where(lane >= 64, qf, 0.0)], axis=0).astype(BF)


def _softmax_pv(s_list, v_list):
    m = functools.reduce(jnp.maximum, [jnp.max(s, axis=-1, keepdims=True) for s in s_list])
    acc, l = None, None
    for s, v in zip(s_list, v_list):
        p = jnp.exp2(s - m)
        ls = jnp.sum(p, axis=-1, keepdims=True)
        a = _dot(p.astype(BF), v)
        l = ls if l is None else l + ls
        acc = a if acc is None else acc + a
    return acc / l


def _diff_lambda(lamv, lam_init):
    a = jnp.exp(jnp.sum(lamv[0:1] * lamv[1:2], axis=-1, keepdims=True))
    b = jnp.exp(jnp.sum(lamv[2:3] * lamv[3:4], axis=-1, keepdims=True))
    return a - b + lam_init


def _diff_finish(o1, o2, lam, subln, lam_init):
    o = o1 - lam * o2
    return _rms(o, subln, SUBLN_EPS) * (1.0 - lam_init)


def _even_in_kernel(x_ref, g_ref, w0_ref, qn_ref, wq_ref, wqs_ref, kvn_ref, wk_ref, wv_ref, vone_ref,
                    cs_ref, sn_ref,
                    qa_ref, ka_ref, va_ref, ckv_ref, kr_ref, qd_ref, kdb_ref, vdb_ref, kd_ref, vd_ref,
                    *, a_scale, b_scale):
    h = _rms(x_ref[...], g_ref[...], NORM_EPS).astype(BF)
    y = _dot(h, w0_ref[...])
    cq, ckv = y[:, 0:384], y[:, 384:640]
    kr_m, kr_s = y[:, 640:768], y[:, 768:896]
    qd, kd, vd = y[:, 896:1408], y[:, 1408:1920], y[:, 1920:2432]
    cs, sn = cs_ref[...], sn_ref[...]
    cqn = _rms(cq, qn_ref[...], NORM_EPS).astype(BF)
    qm = _dot(cqn, wq_ref[...])
    qs = _dot(cqn, wqs_ref[...])
    ckvn = _rms(ckv, kvn_ref[...], NORM_EPS)
    ckv_ref[...] = ckvn
    krp = kr_m * cs + kr_s * sn
    kr_ref[...] = krp[:, NOPE:NOPE + ROPE]
    cb = ckvn.astype(BF)
    kn = _dot(cb, wk_ref[...])
    for hh in range(H_A):
        sl = slice(hh * HEAD_PAD, (hh + 1) * HEAD_PAD)
        qa_ref[:, sl] = ((qm[:, sl] * cs + qs[:, sl] * sn) * a_scale).astype(BF)
        ka_ref[:, sl] = (kn[:, sl] + krp).astype(BF)
    va_ref[...] = (_dot(cb, wv_ref[...]) + vone_ref[...]).astype(BF)
    qd_ref[...] = (qd * b_scale).astype(BF)
    for hh in range(H_B):
        kd_ref[:, hh, :] = kd[:, hh * V_B:(hh + 1) * V_B]
        vd_ref[:, hh, :] = vd[:, hh * V_B:(hh + 1) * V_B]
    kdb_ref[...] = kd.astype(BF)
    vdb_ref[...] = vd.astype(BF)


def _even_in(x, g, w0, qn, wq, wqs, kvn, wk, wv, vone, cs_tab, sn_tab):
    T, D = x.shape
    pos_blocks = cs_tab.shape[0] // ROW_TILE

    def full(a):
        return pl.BlockSpec(a.shape, lambda i: (0,) * a.ndim)

    def rows(*tail):
        return pl.BlockSpec((ROW_TILE,) + tail, lambda i: (i,) + (0,) * len(tail))

    pos_spec = pl.BlockSpec((ROW_TILE, LANE), lambda i: (i % pos_blocks, 0))
    outs = [((1024,), BF), ((1024,), BF), ((1024,), BF), ((KV_RANK,), F32), ((ROPE,), F32),
            ((512,), BF), ((512,), BF), ((512,), BF), ((H_B, V_B), F32), ((H_B, V_B), F32)]
    return pl.pallas_call(
        functools.partial(_even_in_kernel,
                          a_scale=(NOPE + ROPE) ** -0.5 * LOG2E, b_scale=DH_B ** -0.5 * LOG2E),
        grid=(T // ROW_TILE,),
        in_specs=[rows(D), full(g), full(w0), full(qn), full(wq), full(wqs), full(kvn), full(wk), full(wv),
                  full(vone), pos_spec, pos_spec],
        out_specs=[rows(*tail) for tail, _ in outs],
        out_shape=[jax.ShapeDtypeStruct((T,) + tail, dt) for tail, dt in outs],
        compiler_params=_cparams("parallel"),
        name="even_in",
    )(x, g, w0, qn, wq, wqs, kvn, wk, wv, vone, cs_tab, sn_tab)


def _chunk_causal_mask(tq, tk):
    row = lax.broadcasted_iota(jnp.int32, (tq, tk), 0)
    col = lax.broadcasted_iota(jnp.int32, (tq, tk), 1)
    return (col // CHUNK) <= (row // CHUNK)


def _softmax_tile(s_ref, p_ref, m_ref, l_ref, a_ref, *, add_ref=None, off=None):
    tile = s_ref.shape[0]
    reps = tile // LANE
    blocks = [slice(rb * SOFTMAX_ROWS, (rb + 1) * SOFTMAX_ROWS) for rb in range(tile // SOFTMAX_ROWS)]

    def biased(rs):
        s = s_ref[rs, :]
        if add_ref is not None:
            s = s + add_ref[rs, :]
        return s

    for rs in blocks:
        m_old = m_ref[rs, :]
        red = jnp.broadcast_to(jnp.max(biased(rs), axis=-1, keepdims=True), m_old.shape)
        if off is not None:
            red = red + off
        m_new = jnp.maximum(m_old, red)
        a_ref[rs, :] = jnp.exp2(m_old - m_new)
        m_ref[rs, :] = m_new
    for rs in blocks:
        m_new = m_ref[rs, :]
        shift = m_new if off is None else m_new - off
        p = jnp.exp2(biased(rs) - jnp.concatenate([shift] * reps, axis=1))
        if l_ref is not None:
            l_ref[rs, :] = a_ref[rs, :] * l_ref[rs, :] + jnp.broadcast_to(
                jnp.sum(p, axis=-1, keepdims=True), m_new.shape)
        p_ref[rs, :] = p.astype(BF)


def _flash_causal(qi, bufs, scores, update):
    s0, s1, p0, p1, m_ref, l_ref, a_ref, acc_ref = bufs
    m_ref[...] = jnp.full(m_ref.shape, NEG_INF, F32)
    l_ref[...] = jnp.zeros(l_ref.shape, F32)
    acc_ref[...] = jnp.zeros(acc_ref.shape, F32)
    scores(0, s0)

    def pair(jj, carry):
        j = 2 * jj
        scores(j + 1, s1)
        update(j, s0, p0, False)
        scores(j + 2, s0)
        update(j + 1, s1, p1, False)
        return carry

    lax.fori_loop(0, qi // 2, pair, 0)

    @pl.when(qi % 2 == 1)
    def _():
        scores(qi, s1)
        update(qi - 1, s0, p0, False)
        s0[...] = s1[...]

    update(qi, s0, p0, True)


def _flash_causal_pair(t, bufs, scores, update):
    s0, s1, p0, p1, m_ref, l_ref, a_ref, acc_ref = bufs
    m_ref[...] = jnp.full(m_ref.shape, NEG_INF, F32)
    l_ref[...] = jnp.zeros(l_ref.shape, F32)
    acc_ref[...] = jnp.zeros(acc_ref.shape, F32)
    both = (0, 1)
    scores(0, s0, both)

    def pair(jj, carry):
        j = 2 * jj
        scores(j + 1, s1, both)
        update(j, s0, p0, both, ())
        scores(j + 2, s0, both)
        update(j + 1, s1, p1, both, ())
        return carry

    lax.fori_loop(0, t, pair, 0)
    scores(2 * t + 1, s1, (1,))
    update(2 * t, s0, p0, both, (0,))
    update(2 * t + 1, s1, p1, (1,), (1,))


def _flash_scratch(n_chains, rows, tile, acc_width=LANE):
    s = pltpu.VMEM((n_chains, rows, tile), F32)
    p = pltpu.VMEM((n_chains, rows, tile), BF)
    stat = pltpu.VMEM((n_chains, rows, LANE), F32)
    return [s, s, p, p, stat, stat, stat, pltpu.VMEM((n_chains, rows, acc_width), F32)]


def _rows_of(qts, tile):
    return slice(qts[0] * tile, (qts[-1] + 1) * tile)


def _mla_prompt_kernel(q_ref, k_ref, v_ref, o_ref, s0, s1, p0, p1, m_ref, l_ref, a_ref, acc_ref, dmask_ref,
                       *, tile):
    qi = pl.program_id(2)

    @pl.when(qi == 0)
    def _():
        dmask_ref[...] = jnp.where(_chunk_causal_mask(tile, tile), 0.0, NEG_INF)

    sls = [slice(hh * HEAD_PAD, (hh + 1) * HEAD_PAD) for hh in range(2)]

    def scores(j, s_buf):
        start = pl.multiple_of(j * tile, tile)
        for c, sl in enumerate(sls):
            s_buf[c] = _dot_nt(q_ref[:, sl], k_ref[pl.ds(start, tile), sl])

    def update(j, s_buf, p_buf, diag):
        start = pl.multiple_of(j * tile, tile)
        for c, sl in enumerate(sls):
            _softmax_tile(s_buf.at[c], p_buf.at[c], m_ref.at[c], None, a_ref.at[c],
                          add_ref=dmask_ref if diag else None)
            acc_ref[c] = a_ref[c] * acc_ref[c] + _dot(p_buf[c], v_ref[pl.ds(start, tile), sl])

    _flash_causal(qi, (s0, s1, p0, p1, m_ref, l_ref, a_ref, acc_ref), scores, update)
    lane = _lane_iota((tile, LANE))
    a0, a1 = acc_ref[0], acc_ref[1]
    l0 = jnp.sum(jnp.where(lane == V_A, a0, 0.0), axis=-1, keepdims=True)
    l1 = jnp.sum(jnp.where(lane == 0, a1, 0.0), axis=-1, keepdims=True)
    o_ref[...] = jnp.where(lane < V_A, a0 / l0, a1 / l1).astype(BF)


def _mla_prompt(qa, ka, va, *, batch, seq):
    tile = min(ATT_TILE, seq)
    nq = seq // tile
    return pl.pallas_call(
        functools.partial(_mla_prompt_kernel, tile=tile),
        grid=(batch, H_A // 2, nq),
        in_specs=[pl.BlockSpec((tile, 2 * HEAD_PAD), lambda b, p, qi: (b * nq + qi, p)),
                  pl.BlockSpec((seq, 2 * HEAD_PAD), lambda b, p, qi: (b, p)),
                  pl.BlockSpec((seq, 2 * HEAD_PAD), lambda b, p, qi: (b, p))],
        out_specs=pl.BlockSpec((tile, LANE), lambda b, p, qi: (b * nq + qi, p)),
        out_shape=jax.ShapeDtypeStruct((batch * seq, H_A * V_A), BF),
        scratch_shapes=_flash_scratch(2, tile, tile) + [pltpu.VMEM((tile, tile), F32)],
        compiler_params=_cparams("parallel", "parallel", "arbitrary"),
        name="mla_prompt",
    )(qa, ka, va)


def _diff_prompt_kernel(slopes_ref, lamv_ref, subln_ref, q_ref, k_ref, v_ref, o_ref,
                        s0, s1, p0, p1, m_ref, l_ref, a_ref, acc_ref, key_ref, dbias_ref, q2_ref,
                        *, tile, lam_init):
    h = pl.program_id(1)
    qi = pl.program_id(2)
    slope = slopes_ref[h]
    q2_ref[...] = _split_halves(q_ref[...])
    @pl.when(qi == 0)
    def _():
        key_ref[...] = slope * lax.broadcasted_iota(jnp.int32, (8, tile), 1).astype(F32)
        row = lax.broadcasted_iota(jnp.int32, (tile, tile), 0)
        col = lax.broadcasted_iota(jnp.int32, (tile, tile), 1)
        later = (2.0 * slope) * jnp.minimum(row - col, 0).astype(F32)
        dbias_ref[...] = jnp.where(_chunk_causal_mask(tile, tile), later, NEG_INF)

    n_rows = 2 * tile

    def scores(j, s_buf, qts):
        k = k_ref[pl.ds(pl.multiple_of(j * tile, tile), tile), :]
        rows = _rows_of(qts, tile)
        key_term = jnp.concatenate([key_ref[...]] * ((rows.stop - rows.start) // 8), axis=0)
        for c in range(2):
            q = q2_ref[c * n_rows + rows.start:c * n_rows + rows.stop, :]
            s_buf[c, rows, :] = _dot_nt(q, k) + key_term

    def update(j, s_buf, p_buf, qts, diag_qts):
        start = pl.multiple_of(j * tile, tile)
        off = slope * (j * tile).astype(F32)
        rows = _rows_of(qts, tile)
        for c in range(2):
            for qt in qts:
                r = _rows_of((qt,), tile)
                _softmax_tile(s_buf.at[c, r], p_buf.at[c, r], m_ref.at[c, r], l_ref.at[c, r], a_ref.at[c, r],
                              add_ref=dbias_ref if qt in diag_qts else None, off=off)
            acc_ref[c, rows, :] = (a_ref[c, rows, :] * acc_ref[c, rows, :]
                                   + _dot(p_buf[c, rows, :], v_ref[pl.ds(start, tile), :]))

    _flash_causal_pair(qi, (s0, s1, p0, p1, m_ref, l_ref, a_ref, acc_ref), scores, update)
    lam = _diff_lambda(lamv_ref[...], lam_init)
    o_ref[...] = _diff_finish(acc_ref[0] / l_ref[0], acc_ref[1] / l_ref[1], lam, subln_ref[...],
                              lam_init).astype(BF)


def _diff_prompt(slopes, lamv, subln, qd, kdb, vdb, *, batch, seq, lam_init):
    tile = min(ATT_TILE, seq // 2)
    nq = seq // (2 * tile)
    return pl.pallas_call(
        functools.partial(_diff_prompt_kernel, tile=tile, lam_init=lam_init),
        grid=(batch, H_B, nq),
        in_specs=[pl.BlockSpec(memory_space=pltpu.SMEM),
                  pl.BlockSpec(lamv.shape, lambda b, h, qi: (0, 0)),
                  pl.BlockSpec(subln.shape, lambda b, h, qi: (0, 0)),
                  pl.BlockSpec((2 * tile, LANE), lambda b, h, qi: (b * nq + qi, h)),
                  pl.BlockSpec((seq, LANE), lambda b, h, qi: (b, h)),
                  pl.BlockSpec((seq, LANE), lambda b, h, qi: (b, h))],
        out_specs=pl.BlockSpec((2 * tile, LANE), lambda b, h, qi: (b * nq + qi, h)),
        out_shape=jax.ShapeDtypeStruct((batch * seq, H_B * V_B), BF),
        scratch_shapes=_flash_scratch(2, 2 * tile, tile) + [pltpu.VMEM((8, tile), F32), pltpu.VMEM((tile, tile), F32),
                                                  pltpu.VMEM((4 * tile, LANE), BF)],
        compiler_params=_cparams("parallel", "parallel", "arbitrary"),
        name="diff_prompt",
    )(slopes, lamv, subln, qd, kdb, vdb)


def _even_sample_kernel(slopes_ref, lamv_ref, subln_ref, wk_ref, wv_ref, vone_ref, e_ref,
                        qa_ref, ka_ref, va_ref, ckv_ref, kr_ref,
                        qd_ref, kdb_ref, vdb_ref, ck_ref, cv_ref,
                        oa_ref, ob_ref, *, lam_init):
    n_new = qa_ref.shape[0]
    n_past = ckv_ref.shape[0]
    ckvp = ckv_ref[...].astype(BF)
    krp = kr_ref[...].astype(BF)
    lane = _lane_iota((n_new, LANE))
    for pr in range(H_A // 2):
        res = []
        for hh in range(2):
            sl = slice((2 * pr + hh) * HEAD_PAD, (2 * pr + hh + 1) * HEAD_PAD)
            q = qa_ref[:, sl]
            kp = (_dot(ckvp, wk_ref[:, sl]) + _dot(krp, e_ref[:, sl])).astype(BF)
            vp = (_dot(ckvp, wv_ref[:, sl]) + vone_ref[:, sl]).astype(BF)
            res.append(_softmax_pv([_dot_nt(q, kp), _dot_nt(q, ka_ref[:, sl])], [vp, va_ref[:, sl]]))
        oa_ref[:, pr * LANE:(pr + 1) * LANE] = jnp.where(lane < V_A, res[0], res[1]).astype(BF)
    rowp = lax.broadcasted_iota(jnp.int32, (n_new, n_past), 0)
    colp = lax.broadcasted_iota(jnp.int32, (n_new, n_past), 1)
    dist_p = (rowp - colp + n_past).astype(F32)
    dist_p = jnp.concatenate([dist_p, dist_p], axis=0)
    rown = lax.broadcasted_iota(jnp.int32, (n_new, n_new), 0)
    coln = lax.broadcasted_iota(jnp.int32, (n_new, n_new), 1)
    dist_n = jnp.abs(rown - coln).astype(F32)
    dist_n = jnp.concatenate([dist_n, dist_n], axis=0)
    lam = _diff_lambda(lamv_ref[...], lam_init)
    for h in range(H_B):
        sl = slice(h * LANE, (h + 1) * LANE)
        slope = slopes_ref[h]
        q2x = _split_halves(qd_ref[:, sl])
        kp = ck_ref[:, h, :].astype(BF)
        vp = cv_ref[:, h, :].astype(BF)
        s_p = _dot_nt(q2x, kp) - slope * dist_p
        s_n = _dot_nt(q2x, kdb_ref[:, sl]) - slope * dist_n
        o = _softmax_pv([s_p, s_n], [vp, vdb_ref[:, sl]])
        ob_ref[:, sl] = _diff_finish(o[:n_new], o[n_new:], lam, subln_ref[...], lam_init).astype(BF)


def _even_sample(slopes, lamv, subln, wk, wv, vone, e_mat, qa, ka, va, ckv_c, kr_c, qd, kdb, vdb, ck_c, cv_c,
                 *, n_prompt_rows, dec_seq, lam_init):
    dec_batch, n_past = ckv_c.shape[0], ckv_c.shape[1]
    base = n_prompt_rows // dec_seq

    def full(a):
        return pl.BlockSpec(a.shape, lambda s: (0,) * a.ndim)

    def new(width):
        return pl.BlockSpec((dec_seq, width), lambda s: (base + s, 0))

    def cache(*tail):
        return pl.BlockSpec((None, n_past) + tail, lambda s: (s, 0) + (0,) * len(tail))

    return pl.pallas_call(
        functools.partial(_even_sample_kernel, lam_init=lam_init),
        grid=(dec_batch,),
        in_specs=[pl.BlockSpec(memory_space=pltpu.SMEM), full(lamv), full(subln), full(wk), full(wv), full(vone),
                  full(e_mat), new(1024), new(1024), new(1024), cache(KV_RANK), cache(ROPE),
                  new(512), new(512), new(512), cache(H_B, V_B), cache(H_B, V_B)],
        out_specs=[pl.BlockSpec((dec_seq, 512), lambda s: (s, 0))] * 2,
        out_shape=[jax.ShapeDtypeStruct((dec_batch * dec_seq, 512), BF)] * 2,
        compiler_params=_cparams("parallel"),
        name="even_sample",
    )(slopes, lamv, subln, wk, wv, vone, e_mat, qa, ka, va, ckv_c, kr_c, qd, kdb, vdb, ck_c, cv_c)


def _route(logits, carry):
    tm = logits.shape[0]
    lane = _lane_iota(logits.shape).astype(F32)
    big = float(LANE)
    g_mask = lane < N_GROUPS
    gl = jnp.where(g_mask, logits, NEG_INF)
    gmax = jnp.max(gl, axis=-1, keepdims=True)
    g_sel = jnp.min(jnp.where(gl == gmax, lane, big), axis=-1, keepdims=True)
    p_grp = 1.0 / jnp.sum(jnp.exp(gl - gmax), axis=-1, keepdims=True)
    lo = ROUTE_OFF + EPG * g_sel
    el = jnp.where((lane >= lo) & (lane < lo + EPG), logits, NEG_INF)
    v1 = jnp.max(el, axis=-1, keepdims=True)
    i1 = jnp.min(jnp.where(el == v1, lane, big), axis=-1, keepdims=True)
    el2 = jnp.where(lane == i1, NEG_INF, el)
    v2 = jnp.max(el2, axis=-1, keepdims=True)
    i2 = jnp.min(jnp.where(el2 == v2, lane, big), axis=-1, keepdims=True)
    ex = jnp.exp(v2 - v1)
    den = 1.0 + ex
    gate1 = (1.0 / den) * p_grp
    gate2 = (ex / den) * p_grp
    onehot = jnp.where((lane == i1) | (lane == i2), 1.0, 0.0)
    row = lax.broadcasted_iota(jnp.int32, (tm, tm), 0)
    col = lax.broadcasted_iota(jnp.int32, (tm, tm), 1)
    tri = jnp.where(row > col, 1.0, 0.0).astype(BF)
    cum = _dot(tri, onehot.astype(BF)) + carry
    r1 = jnp.sum(jnp.where(lane == i1, cum, 0.0), axis=-1, keepdims=True)
    r2 = jnp.sum(jnp.where(lane == i2, cum, 0.0), axis=-1, keepdims=True)
    packed = jnp.zeros_like(logits)
    for pos, val in enumerate([i1 - ROUTE_OFF, i2 - ROUTE_OFF, gate1, gate2, r1, r2]):
        packed = jnp.where(lane == pos, val, packed)
    return packed, carry + jnp.sum(onehot, axis=0, keepdims=True)


def _out_router_kernel(*refs, splits, n_prompt_tiles):
    i = pl.program_id(0)
    offs = [sum(splits[:k]) for k in range(len(splits))]
    n_mix = len(splits) - 1

    def pick(k):
        parts = refs[offs[k]:offs[k] + splits[k]]
        if splits[k] == 1:
            return parts[0][...]
        return jnp.where(i < n_prompt_tiles, parts[0][...], parts[1][...])

    rest = refs[sum(splits):]
    w_refs = rest[:n_mix]
    g_ref, wr_ref, br_ref, x1_ref, xn_ref, route_ref, cnt_ref, carry_ref = rest[n_mix:]

    @pl.when(i == 0)
    def _():
        carry_ref[...] = jnp.zeros_like(carry_ref)

    x1 = pick(0)
    for k, w_ref in enumerate(w_refs):
        x1 = x1 + _dot(pick(1 + k), w_ref[...])
    x1_ref[...] = x1
    xb = _rms(x1, g_ref[...], NORM_EPS).astype(BF)
    xn_ref[...] = xb
    logits = _dot(xb, wr_ref[...]) + br_ref[...]
    packed, carry = _route(logits, carry_ref[...])
    route_ref[...] = packed
    carry_ref[...] = carry
    cnt_ref[...] = carry


def _out_router(row_inputs, ws, g, wr, br):
    splits = tuple(len(parts) for parts in row_inputs)
    T = sum(a.shape[0] for a in row_inputs[0])
    D = row_inputs[0][0].shape[1]
    tile = WIDE_TILE
    n_p = max([parts[0].shape[0] // tile for parts in row_inputs if len(parts) == 2], default=0)

    def row_specs(parts):
        if len(parts) == 1:
            return [pl.BlockSpec((tile, parts[0].shape[1]), lambda i: (i, 0))]
        return [pl.BlockSpec((tile, parts[0].shape[1]), lambda i: (jnp.minimum(i, n_p - 1), 0)),
                pl.BlockSpec((tile, parts[1].shape[1]), lambda i: (jnp.maximum(i - n_p, 0), 0))]

    def full(a):
        return pl.BlockSpec(a.shape, lambda i: (0,) * a.ndim)

    def rows(width):
        return pl.BlockSpec((tile, width), lambda i: (i, 0))

    flat_rows = [a for parts in row_inputs for a in parts]
    return pl.pallas_call(
        functools.partial(_out_router_kernel, splits=splits, n_prompt_tiles=n_p),
        grid=(T // tile,),
        in_specs=[s for parts in row_inputs for s in row_specs(parts)] + [full(w) for w in ws]
        + [full(g), full(wr), full(br)],
        out_specs=[rows(D), rows(D), rows(LANE), pl.BlockSpec((1, LANE), lambda i: (0, 0))],
        out_shape=[jax.ShapeDtypeStruct((T, D), F32), jax.ShapeDtypeStruct((T, D), BF),
                   jax.ShapeDtypeStruct((T, LANE), F32), jax.ShapeDtypeStruct((1, LANE), F32)],
        scratch_shapes=[pltpu.VMEM((1, LANE), F32)],
        compiler_params=_cparams("arbitrary"),
        name="out_router",
    )(*flat_rows, *ws, g, wr, br)


def _experts_kernel(be_ref, nu_ref, xb_ref, wg_ref, wu_ref, wd_ref, y_ref, wgb_ref, wub_ref, wdb_ref):
    i = pl.program_id(0)
    used = i < nu_ref[0]

    @pl.when(used & ((i == 0) | (be_ref[i] != be_ref[jnp.maximum(i - 1, 0)])))
    def _():
        wgb_ref[...] = wg_ref[...].astype(BF)
        wub_ref[...] = wu_ref[...].astype(BF)
        wdb_ref[...] = wd_ref[...].astype(BF)

    @pl.when(used)
    def _():
        xb = xb_ref[...]
        a = _dot(xb, wgb_ref[...])
        b = _dot(xb, wub_ref[...])
        hid = (a * jax.nn.sigmoid(a)) * b
        y_ref[...] = _dot(hid.astype(BF), wdb_ref[...]).astype(y_ref.dtype)

    @pl.when(jnp.logical_not(used))
    def _():
        y_ref[...] = jnp.zeros_like(y_ref)


def _experts(block_expert, n_used, xb, wg, wu, wd, *, layer):
    L, D = xb.shape
    n_blocks = L // MOE_BLOCK
    grid_spec = pltpu.PrefetchScalarGridSpec(
        num_scalar_prefetch=2,
        grid=(n_blocks,),
        in_specs=[pl.BlockSpec((MOE_BLOCK, D), lambda i, be, nu: (i, 0)),
                  pl.BlockSpec((None, None, D, D_EXPERT), lambda i, be, nu: (layer, be[i], 0, 0)),
                  pl.BlockSpec((None, None, D, D_EXPERT), lambda i, be, nu: (layer, be[i], 0, 0)),
                  pl.BlockSpec((None, None, D_EXPERT, D), lambda i, be, nu: (layer, be[i], 0, 0))],
        out_specs=pl.BlockSpec((MOE_BLOCK, D), lambda i, be, nu: (i, 0)),
        scratch_shapes=[pltpu.VMEM((D, D_EXPERT), BF), pltpu.VMEM((D, D_EXPERT), BF), pltpu.VMEM((D_EXPERT, D), BF)],
    )
    return pl.pallas_call(
        _experts_kernel,
        grid_spec=grid_spec,
        out_shape=jax.ShapeDtypeStruct((L, D), BF),
        compiler_params=_cparams("arbitrary"),
        name="experts",
    )(block_expert, n_used, xb, wg, wu, wd)


def _moe(xn, route, cnt, wg, wu, wd, *, layer):
    T = xn.shape[0]
    e = route[:, 0:2].astype(jnp.int32)
    rank = route[:, 4:6].astype(jnp.int32)
    counts = cnt[0, ROUTE_OFF:ROUTE_OFF + N_EXPERTS].astype(jnp.int32)
    padded = ((counts + MOE_BLOCK - 1) // MOE_BLOCK) * MOE_BLOCK
    pend = jnp.cumsum(padded)
    pstart = pend - padded
    dest = jnp.sum(jnp.where(e[:, :, None] == jnp.arange(N_EXPERTS)[None, None, :], pstart[None, None, :], 0),
                   axis=-1) + rank
    n_blocks = -(-(2 * T) // MOE_BLOCK) + N_EXPERTS
    L = n_blocks * MOE_BLOCK
    tok = jnp.repeat(jnp.arange(T, dtype=jnp.int32), 2)
    buf_tok = (jnp.arange(L, dtype=jnp.int32) % T).at[dest.reshape(-1)].set(
        tok, unique_indices=True, mode="promise_in_bounds")
    block_start = jnp.arange(n_blocks, dtype=jnp.int32) * MOE_BLOCK
    block_expert = jnp.minimum(jnp.sum(pend[None, :] <= block_start[:, None], axis=1), N_EXPERTS - 1).astype(jnp.int32)
    n_used = (pend[-1:] // MOE_BLOCK).astype(jnp.int32)
    xb = xn.at[buf_tok].get(mode="promise_in_bounds")
    yb = _experts(block_expert, n_used, xb, wg, wu, wd, layer=layer)
    return (yb.at[dest[:, 0]].get(mode="promise_in_bounds"), yb.at[dest[:, 1]].get(mode="promise_in_bounds"))


def _gates(route):
    lane = _lane_iota(route.shape)
    g0 = jnp.sum(jnp.where(lane == 2, route, 0.0), axis=-1, keepdims=True)
    g1 = jnp.sum(jnp.where(lane == 3, route, 0.0), axis=-1, keepdims=True)
    return g0, g1


def _odd_in_kernel(x_ref, y0_ref, y1_ref, route_ref, g_ref, w_ref, x2_ref, q_ref, k_ref, v_ref, *, scale):
    g0, g1 = _gates(route_ref[...])
    x2 = x_ref[...] + (y0_ref[...].astype(F32) * g0 + y1_ref[...].astype(F32) * g1)
    x2_ref[...] = x2
    h = _rms(x2, g_ref[...], NORM_EPS).astype(BF)
    y = _dot(h, w_ref[...])
    n = q_ref.shape[1]
    q_ref[...] = (y[:, :n] * scale).astype(BF)
    k_ref[...] = y[:, n:2 * n].astype(BF)
    v_ref[...] = y[:, 2 * n:].astype(BF)


def _odd_in(x1, y0, y1, route, g, w):
    T, D = x1.shape
    n = w.shape[1] // 3

    def rows(width):
        return pl.BlockSpec((WIDE_TILE, width), lambda i: (i, 0))

    def full(a):
        return pl.BlockSpec(a.shape, lambda i: (0,) * a.ndim)

    return pl.pallas_call(
        functools.partial(_odd_in_kernel, scale=DH_C ** -0.5 * LOG2E),
        grid=(T // WIDE_TILE,),
        in_specs=[rows(D), rows(D), rows(D), rows(LANE), full(g), full(w)],
        out_specs=[rows(D), rows(n), rows(n), rows(n)],
        out_shape=[jax.ShapeDtypeStruct((T, D), F32)] + [jax.ShapeDtypeStruct((T, n), BF)] * 3,
        compiler_params=_cparams("parallel"),
        name="odd_in",
    )(x1, y0, y1, route, g, w)


def _state_rows_kernel(ids_ref, x_ref, g_ref, w_ref, o_ref):
    del ids_ref
    h = _rms(x_ref[...], g_ref[...], NORM_EPS).astype(BF)
    o_ref[...] = _dot(h, w_ref[...])


def _state_rows(tile_ids, x, g, w):
    D = x.shape[1]
    n = tile_ids.shape[0]
    grid_spec = pltpu.PrefetchScalarGridSpec(
        num_scalar_prefetch=1,
        grid=(n,),
        in_specs=[pl.BlockSpec((ROW_TILE, D), lambda i, ids: (ids[i], 0)),
                  pl.BlockSpec(g.shape, lambda i, ids: (0, 0)),
                  pl.BlockSpec(w.shape, lambda i, ids: (0, 0))],
        out_specs=pl.BlockSpec((ROW_TILE, w.shape[1]), lambda i, ids: (i, 0)),
    )
    return pl.pallas_call(
        _state_rows_kernel,
        grid_spec=grid_spec,
        out_shape=jax.ShapeDtypeStruct((n * ROW_TILE, w.shape[1]), F32),
        compiler_params=_cparams("parallel"),
        name="state_rows",
    )(tile_ids, x, g, w)


def _band_prompt_kernel(q_ref, k_ref, v_ref, bias_ref, o_ref, q2_ref, s0, s1, p_ref, l_ref,
                        *, tile, tiles_per_step):
    lane = _lane_iota((tile, LANE))
    q2_ref[...] = _split_halves(q_ref[...])
    n_rows = tile * tiles_per_step

    def key_tiles(t):
        qi = pl.program_id(2) * tiles_per_step + t
        out = []
        for kt in range(3):
            start = (qi + kt - 2) * tile
            neg = jnp.where(start >= 0, 0.0, NEG_INF)
            out.append((pl.multiple_of(jnp.maximum(start, 0), tile), neg))
        return out

    def scores(t, s_buf):
        for hh in range(2):
            q = q2_ref[hh * n_rows + t * tile:hh * n_rows + (t + 1) * tile, :]
            for kt, (start, neg) in enumerate(key_tiles(t)):
                s = _dot_nt(q, k_ref[pl.ds(start, tile), :])
                s_buf[hh, :, kt * tile:(kt + 1) * tile] = s + neg if kt < 2 else s

    def softmax_pv(t, s_buf):
        outs = []
        slot = t % 2
        for hh in range(2):
            for rb in range(tile // SOFTMAX_ROWS):
                rs = slice(rb * SOFTMAX_ROWS, (rb + 1) * SOFTMAX_ROWS)
                s = s_buf[hh, rs, :] + bias_ref[hh * tile + rb * SOFTMAX_ROWS:hh * tile + (rb + 1) * SOFTMAX_ROWS, :]
                p = jnp.exp2(s - jnp.max(s, axis=-1, keepdims=True))
                l_ref[slot, hh, rs, :] = jnp.broadcast_to(jnp.sum(p, axis=-1, keepdims=True), (SOFTMAX_ROWS, LANE))
                p_ref[slot, hh, rs, :] = p.astype(BF)
            acc = None
            for kt, (start, _) in enumerate(key_tiles(t)):
                pv = _dot(p_ref[slot, hh, :, kt * tile:(kt + 1) * tile], v_ref[pl.ds(start, tile), :])
                acc = pv if acc is None else acc + pv
            outs.append(acc / l_ref[slot, hh])
        o_ref[t * tile:(t + 1) * tile, :] = jnp.where(lane < DH_C, outs[0], outs[1]).astype(BF)

    bufs = [s0, s1]
    scores(0, bufs[0])
    for t in range(tiles_per_step):
        if t + 1 < tiles_per_step:
            scores(t + 1, bufs[(t + 1) % 2])
        softmax_pv(t, bufs[t % 2])


def _band_prompt(q, k, v, bias, *, batch, seq):
    tile = BAND_TILE
    tps = min(BAND_TILES_PER_STEP, seq // tile)
    nq = seq // (tile * tps)
    return pl.pallas_call(
        functools.partial(_band_prompt_kernel, tile=tile, tiles_per_step=tps),
        grid=(H_C // 2, batch, nq),
        in_specs=[pl.BlockSpec((tile * tps, LANE), lambda p, b, qi: (b * nq + qi, p)),
                  pl.BlockSpec((seq, LANE), lambda p, b, qi: (b, p)),
                  pl.BlockSpec((seq, LANE), lambda p, b, qi: (b, p)),
                  pl.BlockSpec((None, 2 * tile, 3 * tile), lambda p, b, qi: (p, 0, 0))],
        out_specs=pl.BlockSpec((tile * tps, LANE), lambda p, b, qi: (b * nq + qi, p)),
        out_shape=jax.ShapeDtypeStruct((batch * seq, H_C * DH_C), BF),
        scratch_shapes=[pltpu.VMEM((2 * tile * tps, LANE), BF),
                        pltpu.VMEM((2, tile, 3 * tile), F32), pltpu.VMEM((2, tile, 3 * tile), F32),
                        pltpu.VMEM((2, 2, tile, 3 * tile), BF), pltpu.VMEM((2, 2, tile, LANE), F32)],
        compiler_params=_cparams("parallel", "parallel", "arbitrary"),
        name="band_prompt",
    )(q, k, v, bias)


def _band_sample_kernel(q_ref, k_ref, v_ref, ck_ref, cv_ref, bias_ref, o_ref):
    n_new = q_ref.shape[0]
    n_past = ck_ref.shape[0]
    lane = _lane_iota((n_new, LANE))
    for pr in range(H_C // 2):
        sl = slice(pr * LANE, (pr + 1) * LANE)
        q2x = _split_halves(q_ref[:, sl])
        s_p = _dot_nt(q2x, ck_ref[:, sl].astype(BF)) + bias_ref[pr, :, 0:n_past]
        s_n = _dot_nt(q2x, k_ref[:, sl]) + bias_ref[pr, :, n_past:n_past + n_new]
        o = _softmax_pv([s_p, s_n], [cv_ref[:, sl].astype(BF), v_ref[:, sl]])
        o_ref[:, sl] = jnp.where(lane < DH_C, o[:n_new], o[n_new:]).astype(BF)


def _band_sample(q, k, v, ck, cv, bias, *, n_prompt_rows, dec_seq):
    dec_batch, n_past, width = ck.shape
    base = n_prompt_rows // dec_seq
    new = pl.BlockSpec((dec_seq, width), lambda s: (base + s, 0))
    cache = pl.BlockSpec((None, n_past, width), lambda s: (s, 0, 0))
    return pl.pallas_call(
        _band_sample_kernel,
        grid=(dec_batch,),
        in_specs=[new, new, new, cache, cache, pl.BlockSpec(bias.shape, lambda s: (0, 0, 0))],
        out_specs=pl.BlockSpec((dec_seq, width), lambda s: (s, 0)),
        out_shape=jax.ShapeDtypeStruct((dec_batch * dec_seq, width), BF),
        compiler_params=_cparams("parallel"),
        name="band_sample",
    )(q, k, v, ck, cv, bias)


def _final_kernel(x_ref, y0_ref, y1_ref, route_ref, g_ref, o_ref):
    g0, g1 = _gates(route_ref[...])
    x = x_ref[...] + (y0_ref[...].astype(F32) * g0 + y1_ref[...].astype(F32) * g1)
    o_ref[...] = _rms(x, g_ref[...], NORM_EPS)


def _final(x, y0, y1, route, g, *, first_tile, n_tiles):
    D = x.shape[1]

    def rows(width):
        return pl.BlockSpec((WIDE_TILE, width), lambda i: (first_tile + i, 0))

    return pl.pallas_call(
        _final_kernel,
        grid=(n_tiles,),
        in_specs=[rows(D), rows(D), rows(D), rows(LANE), pl.BlockSpec(g.shape, lambda i: (0, 0))],
        out_specs=pl.BlockSpec((WIDE_TILE, D), lambda i: (i, 0)),
        out_shape=jax.ShapeDtypeStruct((n_tiles * WIDE_TILE, D), F32),
        compiler_params=_cparams("parallel"),
        name="final_norm",
    )(x, y0, y1, route, g)


def _prep_even_weights(w_in, w_qup, w_kvup):
    D = w_in.shape[0]
    a_in = Q_RANK + KV_RANK + ROPE
    bq = H_B * 2 * DH_B
    wcq, wckv, wkr = w_in[:, :Q_RANK], w_in[:, Q_RANK:Q_RANK + KV_RANK], w_in[:, Q_RANK + KV_RANK:a_in]
    wqd, wkd, wvd = w_in[:, a_in:a_in + bq], w_in[:, a_in + bq:a_in + 2 * bq], w_in[:, a_in + 2 * bq:]
    half = ROPE // 2

    def z(n):
        return jnp.zeros((D, n), w_in.dtype)

    kr_m = jnp.concatenate([z(NOPE), wkr, z(HEAD_PAD - NOPE - ROPE)], axis=1)
    kr_s = jnp.concatenate([z(NOPE), wkr[:, half:], wkr[:, :half], z(HEAD_PAD - NOPE - ROPE)], axis=1)
    w0 = jnp.concatenate([wcq, wckv, kr_m, kr_s, wqd, wkd, wvd], axis=1).astype(BF)
    wq3 = w_qup.reshape(Q_RANK, H_A, NOPE + ROPE)
    nope, r1, r2 = wq3[:, :, :NOPE], wq3[:, :, NOPE:NOPE + half], wq3[:, :, NOPE + half:]
    zq = jnp.zeros((Q_RANK, H_A, HEAD_PAD - NOPE - ROPE), w_qup.dtype)
    wq = jnp.concatenate([nope, r1, r2, zq], axis=-1).reshape(Q_RANK, H_A * HEAD_PAD).astype(BF)
    wqs = jnp.concatenate([jnp.zeros_like(nope), r2, r1, zq], axis=-1).reshape(Q_RANK, H_A * HEAD_PAD).astype(BF)
    wkv3 = w_kvup.reshape(KV_RANK, H_A, NOPE + V_A)
    wk = jnp.concatenate([wkv3[:, :, :NOPE], jnp.zeros((KV_RANK, H_A, HEAD_PAD - NOPE), w_kvup.dtype)],
                         axis=-1).reshape(KV_RANK, H_A * HEAD_PAD).astype(BF)
    wv4 = wkv3[:, :, NOPE:].reshape(KV_RANK, H_A // 2, 2, V_A)
    zv = jnp.zeros((KV_RANK, H_A // 2, HEAD_PAD - V_A), w_kvup.dtype)
    wv = jnp.concatenate([wv4[:, :, 0], zv, zv, wv4[:, :, 1]], axis=-1).reshape(KV_RANK, H_A * HEAD_PAD).astype(BF)
    vone = jnp.zeros((H_A // 2, 2 * HEAD_PAD), F32).at[:, V_A].set(1.0).at[:, HEAD_PAD].set(1.0)
    r = jnp.arange(ROPE)
    e_mat = jnp.zeros((ROPE, H_A, HEAD_PAD), F32).at[r[:, None], jnp.arange(H_A)[None, :], NOPE + r[:, None]].set(1.0)
    return w0, wq, wqs, wk, wv, vone.reshape(1, H_A * HEAD_PAD), e_mat.reshape(ROPE, H_A * HEAD_PAD).astype(BF)


def _rope_tables(pos):
    half = ROPE // 2
    inv = jnp.power(ROPE_BASE, -jnp.arange(half, dtype=F32) / half)
    ang = pos.astype(F32)[:, None] * inv[None, :]
    c, s = jnp.cos(ang), jnp.sin(ang)
    n = pos.shape[0]
    pad = jnp.zeros((n, HEAD_PAD - NOPE - ROPE), F32)
    cs = jnp.concatenate([jnp.ones((n, NOPE), F32), c, c, pad], axis=1)
    sn = jnp.concatenate([jnp.zeros((n, NOPE), F32), -s, s, pad], axis=1)
    return cs, sn


def _band_bias_tiles(table, n_rows, n_keys, key_offset, masked):
    i = jnp.arange(n_rows)[:, None]
    kpos = jnp.arange(n_keys)[None, :] - key_offset
    d_max = n_rows - 1 + key_offset
    w = n_rows + n_keys
    d = d_max - jnp.arange(w)
    rev = (table.astype(F32) * LOG2E)[:, jnp.clip(d, -REL_CLIP, REL_CLIP) + REL_CLIP]
    skew = jnp.tile(rev, (1, n_rows))[:, :n_rows * (w - 1)].reshape(-1, n_rows, w - 1)
    bias = skew[:, :, n_rows - 1:n_rows - 1 + n_keys]
    if masked:
        qc, kc = i // CHUNK, jnp.floor_divide(kpos, CHUNK)
        vis = (kc <= qc) & (kc >= qc - LEFT_CHUNKS)
        bias = jnp.where(vis[None], bias, NEG_INF)
    return bias.reshape(H_C // 2, 2 * n_rows, n_keys)


def kernel(x_prompt, x_sample, cache_mla_ckv, cache_mla_krope, cache_diff_k, cache_diff_v, cache_band_k, cache_band_v, ln_mix, w_in_even, mla_q_norm, mla_w_qup, mla_kv_norm, mla_w_kvup, diff_lam_q1, diff_lam_k1, diff_lam_q2, diff_lam_k2, diff_subln, w_out_even, w_in_odd, band_rel_bias, w_out_odd, ln_ffn, moe_w_group, moe_b_group, moe_w_router, moe_b_router, moe_w_gate, moe_w_up, moe_w_down, ln_final):
    B, S, D = x_prompt.shape
    DB, DS, _ = x_sample.shape
    n_past = cache_mla_ckv.shape[2]
    c_past = cache_band_k.shape[2]
    assert ln_mix.shape[0] == 2 and S % (2 * min(ATT_TILE, S // 2)) == 0 and S % BAND_TILE == 0
    assert (B * S) % WIDE_TILE == 0 and (DB * DS) % WIDE_TILE == 0 and ROW_TILE % DS == 0 and DS == CHUNK
    assert n_past % CHUNK == 0 and c_past == LEFT_CHUNKS * CHUNK and S >= c_past and c_past % ROW_TILE == 0
    BS, NS = B * S, DB * DS
    T = BS + NS
    n_p, n_s = BS // ROW_TILE, NS // ROW_TILE
    xp, xs = x_prompt.reshape(BS, D), x_sample.reshape(NS, D)
    row = lambda a: a.reshape(1, -1)

    w0, wq, wqs, wk, wv, vone, e_mat = _prep_even_weights(w_in_even[0], mla_w_qup[0], mla_w_kvup[0])
    cs_p, sn_p = _rope_tables(jnp.arange(S))
    cs_s, sn_s = _rope_tables(n_past + jnp.arange(DS))
    reps = ROW_TILE // DS
    even_w = (row(ln_mix[0]), w0, row(mla_q_norm[0]), wq, wqs, row(mla_kv_norm[0]), wk, wv, vone)
    qa, ka, va, ckv_p, kr_p, qd, kdb, vdb, kd_p, vd_p = _even_in(xp, *even_w, cs_p, sn_p)
    qa_s, ka_s, va_s, ckv_s, kr_s, qd_s, kdb_s, vdb_s, kd_s, vd_s = _even_in(
        xs, *even_w, jnp.tile(cs_s, (reps, 1)), jnp.tile(sn_s, (reps, 1)))
    slopes = jnp.exp2(-8.0 * jnp.arange(1, H_B + 1, dtype=F32) / H_B) * LOG2E
    lamv = jnp.stack([diff_lam_q1[0], diff_lam_k1[0], diff_lam_q2[0], diff_lam_k2[0]]).astype(F32)
    subln = row(diff_subln[0])
    lam_init = 0.8 - 0.6 * math.exp(-0.3 * 0)
    oa_p = _mla_prompt(qa, ka, va, batch=B, seq=S)
    ob_p = _diff_prompt(slopes, lamv, subln, qd, kdb, vdb, batch=B, seq=S, lam_init=lam_init)
    oa_s, ob_s = _even_sample(
        slopes, lamv, subln, wk, wv, vone, e_mat, qa_s, ka_s, va_s, cache_mla_ckv[0], cache_mla_krope[0],
        qd_s, kdb_s, vdb_s, cache_diff_k[0], cache_diff_v[0],
        n_prompt_rows=0, dec_seq=DS, lam_init=lam_init)

    def router_weights(l):
        wr = jnp.concatenate([moe_w_group[l], moe_w_router[l],
                              jnp.zeros((D, LANE - N_GROUPS - N_EXPERTS), F32)], axis=1).astype(BF)
        br = jnp.concatenate([moe_b_group[l], moe_b_router[l],
                              jnp.zeros((LANE - N_GROUPS - N_EXPERTS,), F32)]).astype(F32)
        return wr, row(br)

    n_a = H_A * V_A
    wo = w_out_even[0].astype(BF)
    wr, br = router_weights(0)
    x1, xn, route, cnt = _out_router([(xp, xs), (oa_p, oa_s), (ob_p, ob_s)], [wo[:n_a], wo[n_a:]],
                                     row(ln_ffn[0]), wr, br)
    y0, y1 = _moe(xn, route, cnt, moe_w_gate, moe_w_up, moe_w_down, layer=0)

    w_odd = w_in_odd[0].astype(BF)
    n_c = H_C * DH_C
    x2, qc, kc, vc = _odd_in(x1, y0, y1, route, row(ln_mix[1]), w_odd)
    tail = c_past // ROW_TILE
    tiles_per_seq = S // ROW_TILE
    tile_ids = jnp.concatenate(
        [jnp.arange(tiles_per_seq - tail, tiles_per_seq, dtype=jnp.int32) + b * tiles_per_seq for b in range(B)]
        + [jnp.arange(n_p, n_p + n_s, dtype=jnp.int32)])
    st = _state_rows(tile_ids, x2, row(ln_mix[1]), w_odd[:, n_c:])
    bias_p = _band_bias_tiles(band_rel_bias[0], BAND_TILE, 3 * BAND_TILE, 2 * BAND_TILE, True)
    bias_s = _band_bias_tiles(band_rel_bias[0], DS, c_past + DS, c_past, False)
    oc_p = _band_prompt(qc, kc, vc, bias_p, batch=B, seq=S)
    oc_s = _band_sample(qc, kc, vc, cache_band_k[0].reshape(DB, c_past, n_c), cache_band_v[0].reshape(DB, c_past, n_c),
                        bias_s, n_prompt_rows=BS, dec_seq=DS)
    wr, br = router_weights(1)
    x3, xn, route, cnt = _out_router([(x2,), (oc_p, oc_s)], [w_out_odd[0].astype(BF)], row(ln_ffn[1]), wr, br)
    y0, y1 = _moe(xn, route, cnt, moe_w_gate, moe_w_up, moe_w_down, layer=1)
    g_fin = row(ln_final)
    y_prompt = _final(x3, y0, y1, route, g_fin, first_tile=0, n_tiles=BS // WIDE_TILE).reshape(B, S, D)
    y_sample = _final(x3, y0, y1, route, g_fin, first_tile=BS // WIDE_TILE, n_tiles=NS // WIDE_TILE).reshape(DB, DS, D)

    def shaped(a_p, a_s, *tail_shape):
        return a_p.reshape(1, B, S, *tail_shape), a_s.reshape(1, DB, DS, *tail_shape)

    ckv_p, ckv_s = shaped(ckv_p, ckv_s, KV_RANK)
    kr_p, kr_s = shaped(kr_p, kr_s, ROPE)
    kd_p, kd_s = shaped(kd_p, kd_s, H_B, 2 * DH_B)
    vd_p, vd_s = shaped(vd_p, vd_s, H_B, V_B)
    n_tail = B * c_past
    bk_p = st[:n_tail, :n_c].reshape(1, B, c_past, H_C, DH_C)
    bv_p = st[:n_tail, n_c:].reshape(1, B, c_past, H_C, DH_C)
    k_new = st[n_tail:, :n_c].reshape(DB, DS, H_C, DH_C)
    v_new = st[n_tail:, n_c:].reshape(DB, DS, H_C, DH_C)
    bk_s = jnp.concatenate([cache_band_k[0][:, DS:], k_new], axis=1)[None]
    bv_s = jnp.concatenate([cache_band_v[0][:, DS:], v_new], axis=1)[None]
    return (y_prompt, y_sample, ckv_p, kr_p, kd_p, vd_p, bk_p, bv_p, ckv_s, kr_s, kd_s, vd_s, bk_s, bv_s)
```

```python
import functools
import math

import jax
import jax.numpy as jnp
from jax import lax
from jax.experimental import pallas as pl
from jax.experimental.pallas import tpu as pltpu

BF = jnp.bfloat16
F32 = jnp.float32
NEG_INF = float("-inf")
LOG2E = math.log2(math.e)

CHUNK = 64
NORM_EPS = 1e-6
SUBLN_EPS = 1e-5
H_A, NOPE, ROPE, V_A, Q_RANK, KV_RANK = 8, 64, 32, 64, 384, 256
ROPE_BASE = 10000.0
H_B, DH_B, V_B = 4, 64, 128
H_C, DH_C, LEFT_CHUNKS, REL_CLIP = 16, 64, 8, 128
N_GROUPS, EPG, N_EXPERTS, D_EXPERT = 4, 8, 32, 512
LANE = 128
HEAD_PAD = 128
ROUTE_OFF = N_GROUPS

ROW_TILE = 256
WIDE_TILE = 512
ROUTE_ROWS = 128
ATT_TILE = 512
SOFTMAX_ROWS = 32
BAND_TILE = 256
BAND_TILES_PER_STEP = 4
MOE_BLOCK = 512
VMEM_LIMIT = 56 * 1024 * 1024


def _cparams(*sem):
    return pltpu.CompilerParams(dimension_semantics=sem, vmem_limit_bytes=VMEM_LIMIT)


def _rms(x, g, eps):
    return x * lax.rsqrt(jnp.mean(x * x, axis=-1, keepdims=True) + eps) * g


def _dot(a, b):
    return jnp.dot(a, b, preferred_element_type=F32)


def _dot_nt(a, b):
    return lax.dot_general(a, b, (((1,), (1,)), ((), ())), preferred_element_type=F32)


def _lane_iota(shape):
    return lax.broadcasted_iota(jnp.int32, shape, len(shape) - 1)


def _split_halves(q):
    qf = q.astype(F32)
    lane = _lane_iota(qf.shape)
    return jnp.concatenate([jnp.where(lane < 64, qf, 0.0), jnp.where(lane >= 64, qf, 0.0)], axis=0).astype(BF)


def _softmax_pv(s_list, v_list):
    m = functools.reduce(jnp.maximum, [jnp.max(s, axis=-1, keepdims=True) for s in s_list])
    acc, l = None, None
    for s, v in zip(s_list, v_list):
        p = jnp.exp2(s - m)
        ls = jnp.sum(p, axis=-1, keepdims=True)
        a = _dot(p.astype(BF), v)
        l = ls if l is None else l + ls
        acc = a if acc is None else acc + a
    return acc / l


def _diff_lambda(lamv, lam_init):
    a = jnp.exp(jnp.sum(lamv[0:1] * lamv[1:2], axis=-1, keepdims=True))
    b = jnp.exp(jnp.sum(lamv[2:3] * lamv[3:4], axis=-1, keepdims=True))
    return a - b + lam_init


def _diff_finish(o1, o2, lam, subln, lam_init):
    o = o1 - lam * o2
    return _rms(o, subln, SUBLN_EPS) * (1.0 - lam_init)


def _even_in_kernel(x_ref, g_ref, w0_ref, qn_ref, wq_ref, wqs_ref, kvn_ref, wk_ref, wv_ref, vone_ref,
                    cs_ref, sn_ref,
                    qa_ref, ka_ref, va_ref, ckv_ref, kr_ref, qd_ref, kdb_ref, vdb_ref, kd_ref, vd_ref,
                    *, a_scale, b_scale):
    h = _rms(x_ref[...], g_ref[...], NORM_EPS).astype(BF)
    y = _dot(h, w0_ref[...])
    cq, ckv = y[:, 0:384], y[:, 384:640]
    kr_m, kr_s = y[:, 640:768], y[:, 768:896]
    qd, kd, vd = y[:, 896:1408], y[:, 1408:1920], y[:, 1920:2432]
    cs, sn = cs_ref[...], sn_ref[...]
    cqn = _rms(cq, qn_ref[...], NORM_EPS).astype(BF)
    qm = _dot(cqn, wq_ref[...])
    qs = _dot(cqn, wqs_ref[...])
    ckvn = _rms(ckv, kvn_ref[...], NORM_EPS)
    ckv_ref[...] = ckvn
    krp = kr_m * cs + kr_s * sn
    kr_ref[...] = krp[:, NOPE:NOPE + ROPE]
    cb = ckvn.astype(BF)
    kn = _dot(cb, wk_ref[...])
    for hh in range(H_A):
        sl = slice(hh * HEAD_PAD, (hh + 1) * HEAD_PAD)
        qa_ref[:, sl] = ((qm[:, sl] * cs + qs[:, sl] * sn) * a_scale).astype(BF)
        ka_ref[:, sl] = (kn[:, sl] + krp).astype(BF)
    va_ref[...] = (_dot(cb, wv_ref[...]) + vone_ref[...]).astype(BF)
    qd_ref[...] = (qd * b_scale).astype(BF)
    for hh in range(H_B):
        kd_ref[:, hh, :] = kd[:, hh * V_B:(hh + 1) * V_B]
        vd_ref[:, hh, :] = vd[:, hh * V_B:(hh + 1) * V_B]
    kdb_ref[...] = kd.astype(BF)
    vdb_ref[...] = vd.astype(BF)


def _even_in(x, g, w0, qn, wq, wqs, kvn, wk, wv, vone, cs_tab, sn_tab):
    T, D = x.shape
    pos_blocks = cs_tab.shape[0] // ROW_TILE

    def full(a):
        return pl.BlockSpec(a.shape, lambda i: (0,) * a.ndim)

    def rows(*tail):
        return pl.BlockSpec((ROW_TILE,) + tail, lambda i: (i,) + (0,) * len(tail))

    pos_spec = pl.BlockSpec((ROW_TILE, LANE), lambda i: (i % pos_blocks, 0))
    outs = [((1024,), BF), ((1024,), BF), ((1024,), BF), ((KV_RANK,), F32), ((ROPE,), F32),
            ((512,), BF), ((512,), BF), ((512,), BF), ((H_B, V_B), F32), ((H_B, V_B), F32)]
    return pl.pallas_call(
        functools.partial(_even_in_kernel,
                          a_scale=(NOPE + ROPE) ** -0.5 * LOG2E, b_scale=DH_B ** -0.5 * LOG2E),
        grid=(T // ROW_TILE,),
        in_specs=[rows(D), full(g), full(w0), full(qn), full(wq), full(wqs), full(kvn), full(wk), full(wv),
                  full(vone), pos_spec, pos_spec],
        out_specs=[rows(*tail) for tail, _ in outs],
        out_shape=[jax.ShapeDtypeStruct((T,) + tail, dt) for tail, dt in outs],
        compiler_params=_cparams("parallel"),
        name="even_in",
    )(x, g, w0, qn, wq, wqs, kvn, wk, wv, vone, cs_tab, sn_tab)


def _chunk_causal_mask(tq, tk):
    row = lax.broadcasted_iota(jnp.int32, (tq, tk), 0)
    col = lax.broadcasted_iota(jnp.int32, (tq, tk), 1)
    return (col // CHUNK) <= (row // CHUNK)


def _softmax_tile(s_ref, p_ref, m_ref, l_ref, a_ref, *, add_ref=None, off=None):
    tile = s_ref.shape[0]
    reps = tile // LANE
    blocks = [slice(rb * SOFTMAX_ROWS, (rb + 1) * SOFTMAX_ROWS) for rb in range(tile // SOFTMAX_ROWS)]

    def biased(rs):
        s = s_ref[rs, :]
        if add_ref is not None:
            s = s + add_ref[rs, :]
        return s

    for rs in blocks:
        m_old = m_ref[rs, :]
        red = jnp.broadcast_to(jnp.max(biased(rs), axis=-1, keepdims=True), m_old.shape)
        if off is not None:
            red = red + off
        m_new = jnp.maximum(m_old, red)
        a_ref[rs, :] = jnp.exp2(m_old - m_new)
        m_ref[rs, :] = m_new
    for rs in blocks:
        m_new = m_ref[rs, :]
        shift = m_new if off is None else m_new - off
        p = jnp.exp2(biased(rs) - jnp.concatenate([shift] * reps, axis=1))
        if l_ref is not None:
            l_ref[rs, :] = a_ref[rs, :] * l_ref[rs, :] + jnp.broadcast_to(
                jnp.sum(p, axis=-1, keepdims=True), m_new.shape)
        p_ref[rs, :] = p.astype(BF)


def _flash_causal(qi, bufs, scores, update):
    s0, s1, p0, p1, m_ref, l_ref, a_ref, acc_ref = bufs
    m_ref[...] = jnp.full(m_ref.shape, NEG_INF, F32)
    l_ref[...] = jnp.zeros(l_ref.shape, F32)
    acc_ref[...] = jnp.zeros(acc_ref.shape, F32)
    scores(0, s0)

    def pair(jj, carry):
        j = 2 * jj
        scores(j + 1, s1)
        update(j, s0, p0, False)
        scores(j + 2, s0)
        update(j + 1, s1, p1, False)
        return carry

    lax.fori_loop(0, qi // 2, pair, 0)

    @pl.when(qi % 2 == 1)
    def _():
        scores(qi, s1)
        update(qi - 1, s0, p0, False)
        s0[...] = s1[...]

    update(qi, s0, p0, True)


def _flash_causal_pair(t, bufs, scores, update):
    s0, s1, p0, p1, m_ref, l_ref, a_ref, acc_ref = bufs
    m_ref[...] = jnp.full(m_ref.shape, NEG_INF, F32)
    l_ref[...] = jnp.zeros(l_ref.shape, F32)
    acc_ref[...] = jnp.zeros(acc_ref.shape, F32)
    both = (0, 1)
    scores(0, s0, both)

    def pair(jj, carry):
        j = 2 * jj
        scores(j + 1, s1, both)
        update(j, s0, p0, both, ())
        scores(j + 2, s0, both)
        update(j + 1, s1, p1, both, ())
        return carry

    lax.fori_loop(0, t, pair, 0)
    scores(2 * t + 1, s1, (1,))
    update(2 * t, s0, p0, both, (0,))
    update(2 * t + 1, s1, p1, (1,), (1,))


def _flash_scratch(n_chains, rows, tile, acc_width=LANE):
    s = pltpu.VMEM((n_chains, rows, tile), F32)
    p = pltpu.VMEM((n_chains, rows, tile), BF)
    stat = pltpu.VMEM((n_chains, rows, LANE), F32)
    return [s, s, p, p, stat, stat, stat, pltpu.VMEM((n_chains, rows, acc_width), F32)]


def _rows_of(qts, tile):
    return slice(qts[0] * tile, (qts[-1] + 1) * tile)


def _mla_prompt_kernel(q_ref, k_ref, v_ref, o_ref, s0, s1, p0, p1, m_ref, l_ref, a_ref, acc_ref, dmask_ref,
                       *, tile):
    qi = pl.program_id(2)

    @pl.when(qi == 0)
    def _():
        dmask_ref[...] = jnp.where(_chunk_causal_mask(tile, tile), 0.0, NEG_INF)

    sls = [slice(hh * HEAD_PAD, (hh + 1) * HEAD_PAD) for hh in range(2)]

    def scores(j, s_buf):
        start = pl.multiple_of(j * tile, tile)
        for c, sl in enumerate(sls):
            s_buf[c] = _dot_nt(q_ref[:, sl], k_ref[pl.ds(start, tile), sl])

    def update(j, s_buf, p_buf, diag):
        start = pl.multiple_of(j * tile, tile)
        for c, sl in enumerate(sls):
            _softmax_tile(s_buf.at[c], p_buf.at[c], m_ref.at[c], None, a_ref.at[c],
                          add_ref=dmask_ref if diag else None)
            acc_ref[c] = a_ref[c] * acc_ref[c] + _dot(p_buf[c], v_ref[pl.ds(start, tile), sl])

    _flash_causal(qi, (s0, s1, p0, p1, m_ref, l_ref, a_ref, acc_ref), scores, update)
    lane = _lane_iota((tile, LANE))
    a0, a1 = acc_ref[0], acc_ref[1]
    l0 = jnp.sum(jnp.where(lane == V_A, a0, 0.0), axis=-1, keepdims=True)
    l1 = jnp.sum(jnp.where(lane == 0, a1, 0.0), axis=-1, keepdims=True)
    o_ref[...] = jnp.where(lane < V_A, a0 / l0, a1 / l1).astype(BF)


def _mla_prompt(qa, ka, va, *, batch, seq):
    tile = min(ATT_TILE, seq)
    nq = seq // tile
    return pl.pallas_call(
        functools.partial(_mla_prompt_kernel, tile=tile),
        grid=(batch, H_A // 2, nq),
        in_specs=[pl.BlockSpec((tile, 2 * HEAD_PAD), lambda b, p, qi: (b * nq + qi, p)),
                  pl.BlockSpec((seq, 2 * HEAD_PAD), lambda b, p, qi: (b, p)),
                  pl.BlockSpec((seq, 2 * HEAD_PAD), lambda b, p, qi: (b, p))],
        out_specs=pl.BlockSpec((tile, LANE), lambda b, p, qi: (b * nq + qi, p)),
        out_shape=jax.ShapeDtypeStruct((batch * seq, H_A * V_A), BF),
        scratch_shapes=_flash_scratch(2, tile, tile) + [pltpu.VMEM((tile, tile), F32)],
        compiler_params=_cparams("parallel", "parallel", "arbitrary"),
        name="mla_prompt",
    )(qa, ka, va)


def _diff_prompt_kernel(slopes_ref, lamv_ref, subln_ref, q_ref, k_ref, v_ref, o_ref,
                        s0, s1, p0, p1, m_ref, l_ref, a_ref, acc_ref, key_ref, dbias_ref, q2_ref,
                        *, tile, lam_init):
    h = pl.program_id(1)
    qi = pl.program_id(2)
    slope = slopes_ref[h]
    q2_ref[...] = _split_halves(q_ref[...])
    @pl.when(qi == 0)
    def _():
        key_ref[...] = slope * lax.broadcasted_iota(jnp.int32, (8, tile), 1).astype(F32)
        row = lax.broadcasted_iota(jnp.int32, (tile, tile), 0)
        col = lax.broadcasted_iota(jnp.int32, (tile, tile), 1)
        later = (2.0 * slope) * jnp.minimum(row - col, 0).astype(F32)
        dbias_ref[...] = jnp.where(_chunk_causal_mask(tile, tile), later, NEG_INF)

    n_rows = 2 * tile

    def scores(j, s_buf, qts):
        k = k_ref[pl.ds(pl.multiple_of(j * tile, tile), tile), :]
        rows = _rows_of(qts, tile)
        key_term = jnp.concatenate([key_ref[...]] * ((rows.stop - rows.start) // 8), axis=0)
        for c in range(2):
            q = q2_ref[c * n_rows + rows.start:c * n_rows + rows.stop, :]
            s_buf[c, rows, :] = _dot_nt(q, k) + key_term

    def update(j, s_buf, p_buf, qts, diag_qts):
        start = pl.multiple_of(j * tile, tile)
        off = slope * (j * tile).astype(F32)
        rows = _rows_of(qts, tile)
        for c in range(2):
            for qt in qts:
                r = _rows_of((qt,), tile)
                _softmax_tile(s_buf.at[c, r], p_buf.at[c, r], m_ref.at[c, r], l_ref.at[c, r], a_ref.at[c, r],
                              add_ref=dbias_ref if qt in diag_qts else None, off=off)
            acc_ref[c, rows, :] = (a_ref[c, rows, :] * acc_ref[c, rows, :]
                                   + _dot(p_buf[c, rows, :], v_ref[pl.ds(start, tile), :]))

    _flash_causal_pair(qi, (s0, s1, p0, p1, m_ref, l_ref, a_ref, acc_ref), scores, update)
    lam = _diff_lambda(lamv_ref[...], lam_init)
    o_ref[...] = _diff_finish(acc_ref[0] / l_ref[0], acc_ref[1] / l_ref[1], lam, subln_ref[...],
                              lam_init).astype(BF)


def _diff_prompt(slopes, lamv, subln, qd, kdb, vdb, *, batch, seq, lam_init):
    tile = min(ATT_TILE, seq // 2)
    nq = seq // (2 * tile)
    return pl.pallas_call(
        functools.partial(_diff_prompt_kernel, tile=tile, lam_init=lam_init),
        grid=(batch, H_B, nq),
        in_specs=[pl.BlockSpec(memory_space=pltpu.SMEM),
                  pl.BlockSpec(lamv.shape, lambda b, h, qi: (0, 0)),
                  pl.BlockSpec(subln.shape, lambda b, h, qi: (0, 0)),
                  pl.BlockSpec((2 * tile, LANE), lambda b, h, qi: (b * nq + qi, h)),
                  pl.BlockSpec((seq, LANE), lambda b, h, qi: (b, h)),
                  pl.BlockSpec((seq, LANE), lambda b, h, qi: (b, h))],
        out_specs=pl.BlockSpec((2 * tile, LANE), lambda b, h, qi: (b * nq + qi, h)),
        out_shape=jax.ShapeDtypeStruct((batch * seq, H_B * V_B), BF),
        scratch_shapes=_flash_scratch(2, 2 * tile, tile) + [pltpu.VMEM((8, tile), F32), pltpu.VMEM((tile, tile), F32),
                                                  pltpu.VMEM((4 * tile, LANE), BF)],
        compiler_params=_cparams("parallel", "parallel", "arbitrary"),
        name="diff_prompt",
    )(slopes, lamv, subln, qd, kdb, vdb)


def _even_sample_kernel(slopes_ref, lamv_ref, subln_ref, wk_ref, wv_ref, vone_ref, e_ref,
                        qa_ref, ka_ref, va_ref, ckv_ref, kr_ref,
                        qd_ref, kdb_ref, vdb_ref, ck_ref, cv_ref,
                        oa_ref, ob_ref, *, lam_init):
    n_new = qa_ref.shape[0]
    n_past = ckv_ref.shape[0]
    ckvp = ckv_ref[...].astype(BF)
    krp = kr_ref[...].astype(BF)
    lane = _lane_iota((n_new, LANE))
    for pr in range(H_A // 2):
        res = []
        for hh in range(2):
            sl = slice((2 * pr + hh) * HEAD_PAD, (2 * pr + hh + 1) * HEAD_PAD)
            q = qa_ref[:, sl]
            kp = (_dot(ckvp, wk_ref[:, sl]) + _dot(krp, e_ref[:, sl])).astype(BF)
            vp = (_dot(ckvp, wv_ref[:, sl]) + vone_ref[:, sl]).astype(BF)
            res.append(_softmax_pv([_dot_nt(q, kp), _dot_nt(q, ka_ref[:, sl])], [vp, va_ref[:, sl]]))
        oa_ref[:, pr * LANE:(pr + 1) * LANE] = jnp.where(lane < V_A, res[0], res[1]).astype(BF)
    rowp = lax.broadcasted_iota(jnp.int32, (n_new, n_past), 0)
    colp = lax.broadcasted_iota(jnp.int32, (n_new, n_past), 1)
    dist_p = (rowp - colp + n_past).astype(F32)
    dist_p = jnp.concatenate([dist_p, dist_p], axis=0)
    rown = lax.broadcasted_iota(jnp.int32, (n_new, n_new), 0)
    coln = lax.broadcasted_iota(jnp.int32, (n_new, n_new), 1)
    dist_n = jnp.abs(rown - coln).astype(F32)
    dist_n = jnp.concatenate([dist_n, dist_n], axis=0)
    lam = _diff_lambda(lamv_ref[...], lam_init)
    for h in range(H_B):
        sl = slice(h * LANE, (h + 1) * LANE)
        slope = slopes_ref[h]
        q2x = _split_halves(qd_ref[:, sl])
        kp = ck_ref[:, h, :].astype(BF)
        vp = cv_ref[:, h, :].astype(BF)
        s_p = _dot_nt(q2x, kp) - slope * dist_p
        s_n = _dot_nt(q2x, kdb_ref[:, sl]) - slope * dist_n
        o = _softmax_pv([s_p, s_n], [vp, vdb_ref[:, sl]])
        ob_ref[:, sl] = _diff_finish(o[:n_new], o[n_new:], lam, subln_ref[...], lam_init).astype(BF)


def _even_sample(slopes, lamv, subln, wk, wv, vone, e_mat, qa, ka, va, ckv_c, kr_c, qd, kdb, vdb, ck_c, cv_c,
                 *, n_prompt_rows, dec_seq, lam_init):
    dec_batch, n_past = ckv_c.shape[0], ckv_c.shape[1]
    base = n_prompt_rows // dec_seq

    def full(a):
        return pl.BlockSpec(a.shape, lambda s: (0,) * a.ndim)

    def new(width):
        return pl.BlockSpec((dec_seq, width), lambda s: (base + s, 0))

    def cache(*tail):
        return pl.BlockSpec((None, n_past) + tail, lambda s: (s, 0) + (0,) * len(tail))

    return pl.pallas_call(
        functools.partial(_even_sample_kernel, lam_init=lam_init),
        grid=(dec_batch,),
        in_specs=[pl.BlockSpec(memory_space=pltpu.SMEM), full(lamv), full(subln), full(wk), full(wv), full(vone),
                  full(e_mat), new(1024), new(1024), new(1024), cache(KV_RANK), cache(ROPE),
                  new(512), new(512), new(512), cache(H_B, V_B), cache(H_B, V_B)],
        out_specs=[pl.BlockSpec((dec_seq, 512), lambda s: (s, 0))] * 2,
        out_shape=[jax.ShapeDtypeStruct((dec_batch * dec_seq, 512), BF)] * 2,
        compiler_params=_cparams("parallel"),
        name="even_sample",
    )(slopes, lamv, subln, wk, wv, vone, e_mat, qa, ka, va, ckv_c, kr_c, qd, kdb, vdb, ck_c, cv_c)


def _route(logits, carry, live):
    tm = logits.shape[0]
    lane = _lane_iota(logits.shape).astype(F32)
    big = float(LANE)
    g_mask = lane < N_GROUPS
    gl = jnp.where(g_mask, logits, NEG_INF)
    gmax = jnp.max(gl, axis=-1, keepdims=True)
    g_sel = jnp.min(jnp.where(gl == gmax, lane, big), axis=-1, keepdims=True)
    p_grp = 1.0 / jnp.sum(jnp.exp(gl - gmax), axis=-1, keepdims=True)
    lo = ROUTE_OFF + EPG * g_sel
    el = jnp.where((lane >= lo) & (lane < lo + EPG), logits, NEG_INF)
    v1 = jnp.max(el, axis=-1, keepdims=True)
    i1 = jnp.min(jnp.where(el == v1, lane, big), axis=-1, keepdims=True)
    el2 = jnp.where(lane == i1, NEG_INF, el)
    v2 = jnp.max(el2, axis=-1, keepdims=True)
    i2 = jnp.min(jnp.where(el2 == v2, lane, big), axis=-1, keepdims=True)
    ex = jnp.exp(v2 - v1)
    den = 1.0 + ex
    gate1 = (1.0 / den) * p_grp
    gate2 = (ex / den) * p_grp
    onehot = jnp.where((lane == i1) | (lane == i2), 1.0, 0.0)
    row = lax.broadcasted_iota(jnp.int32, (tm, tm), 0)
    col = lax.broadcasted_iota(jnp.int32, (tm, tm), 1)
    tri = jnp.where(row > col, 1.0, 0.0).astype(BF)
    cum = _dot(tri, onehot.astype(BF)) + carry
    r1 = jnp.sum(jnp.where(lane == i1, cum, 0.0), axis=-1, keepdims=True)
    r2 = jnp.sum(jnp.where(lane == i2, cum, 0.0), axis=-1, keepdims=True)
    packed = jnp.zeros_like(logits)
    for pos, val in enumerate([i1 - ROUTE_OFF, i2 - ROUTE_OFF, gate1, gate2, r1, r2]):
        packed = jnp.where(lane == pos, val, packed)
    return packed, carry + live * jnp.sum(onehot, axis=0, keepdims=True)


def _out_router_kernel(*refs, splits, n_prompt_tiles):
    i = pl.program_id(0)
    offs = [sum(splits[:k]) for k in range(len(splits))]
    n_mix = len(splits) - 1

    def pick(k):
        parts = refs[offs[k]:offs[k] + splits[k]]
        if splits[k] == 1:
            return parts[0][...]
        return jnp.where(i < n_prompt_tiles, parts[0][...], parts[1][...])

    rest = refs[sum(splits):]
    w_refs = rest[:n_mix]
    g_ref, wr_ref, br_ref, x1_ref, xn_ref, route_ref, cnt_ref, carry_ref, logits_ref = rest[n_mix:]

    @pl.when(i == 0)
    def _():
        carry_ref[...] = jnp.zeros_like(carry_ref)
        logits_ref[...] = jnp.zeros_like(logits_ref)

    prev = logits_ref[...]
    x1 = pick(0)
    for k, w_ref in enumerate(w_refs):
        x1 = x1 + _dot(pick(1 + k), w_ref[...])
    x1_ref[...] = x1
    xb = _rms(x1, g_ref[...], NORM_EPS).astype(BF)
    xn_ref[...] = xb
    logits_ref[...] = _dot(xb, wr_ref[...]) + br_ref[...]
    live = jnp.where(i > 0, 1.0, 0.0)
    carry = carry_ref[...]
    for r0 in range(0, prev.shape[0], ROUTE_ROWS):
        packed, carry = _route(prev[r0:r0 + ROUTE_ROWS], carry, live)
        route_ref[r0:r0 + ROUTE_ROWS, :] = packed
    carry_ref[...] = carry
    cnt_ref[...] = carry


def _out_router(row_inputs, ws, g, wr, br):
    splits = tuple(len(parts) for parts in row_inputs)
    T = sum(a.shape[0] for a in row_inputs[0])
    D = row_inputs[0][0].shape[1]
    tile = WIDE_TILE
    n = T // tile
    n_p = max([parts[0].shape[0] // tile for parts in row_inputs if len(parts) == 2], default=0)

    def row_specs(parts):
        if len(parts) == 1:
            return [pl.BlockSpec((tile, parts[0].shape[1]), lambda i: (jnp.minimum(i, n - 1), 0))]
        n_s = parts[1].shape[0] // tile
        return [pl.BlockSpec((tile, parts[0].shape[1]), lambda i: (jnp.minimum(i, n_p - 1), 0)),
                pl.BlockSpec((tile, parts[1].shape[1]), lambda i: (jnp.clip(i - n_p, 0, n_s - 1), 0))]

    def full(a):
        return pl.BlockSpec(a.shape, lambda i: (0,) * a.ndim)

    def rows(width, lag=0):
        return pl.BlockSpec((tile, width), lambda i: (jnp.clip(i - lag, 0, n - 1), 0))

    flat_rows = [a for parts in row_inputs for a in parts]
    return pl.pallas_call(
        functools.partial(_out_router_kernel, splits=splits, n_prompt_tiles=n_p),
        grid=(n + 1,),
        in_specs=[s for parts in row_inputs for s in row_specs(parts)] + [full(w) for w in ws]
        + [full(g), full(wr), full(br)],
        out_specs=[rows(D), rows(D), rows(LANE, lag=1), pl.BlockSpec((1, LANE), lambda i: (0, 0))],
        out_shape=[jax.ShapeDtypeStruct((T, D), F32), jax.ShapeDtypeStruct((T, D), BF),
                   jax.ShapeDtypeStruct((T, LANE), F32), jax.ShapeDtypeStruct((1, LANE), F32)],
        scratch_shapes=[pltpu.VMEM((1, LANE), F32), pltpu.VMEM((tile, LANE), F32)],
        compiler_params=_cparams("arbitrary"),
        name="out_router",
    )(*flat_rows, *ws, g, wr, br)


def _experts_kernel(be_ref, nu_ref, xb_ref, wg_ref, wu_ref, wd_ref, y_ref, wgb_ref, wub_ref, wdb_ref):
    i = pl.program_id(0)
    used = i < nu_ref[0]

    @pl.when(used & ((i == 0) | (be_ref[i] != be_ref[jnp.maximum(i - 1, 0)])))
    def _():
        wgb_ref[...] = wg_ref[...].astype(BF)
        wub_ref[...] = wu_ref[...].astype(BF)
        wdb_ref[...] = wd_ref[...].astype(BF)

    @pl.when(used)
    def _():
        xb = xb_ref[...]
        a = _dot(xb, wgb_ref[...])
        b = _dot(xb, wub_ref[...])
        hid = (a * jax.nn.sigmoid(a)) * b
        y_ref[...] = _dot(hid.astype(BF), wdb_ref[...]).astype(y_ref.dtype)

    @pl.when(jnp.logical_not(used))
    def _():
        y_ref[...] = jnp.zeros_like(y_ref)


def _experts(block_expert, n_used, xb, wg, wu, wd, *, layer):
    L, D = xb.shape
    n_blocks = L // MOE_BLOCK
    grid_spec = pltpu.PrefetchScalarGridSpec(
        num_scalar_prefetch=2,
        grid=(n_blocks,),
        in_specs=[pl.BlockSpec((MOE_BLOCK, D), lambda i, be, nu: (i, 0)),
                  pl.BlockSpec((None, None, D, D_EXPERT), lambda i, be, nu: (layer, be[i], 0, 0)),
                  pl.BlockSpec((None, None, D, D_EXPERT), lambda i, be, nu: (layer, be[i], 0, 0)),
                  pl.BlockSpec((None, None, D_EXPERT, D), lambda i, be, nu: (layer, be[i], 0, 0))],
        out_specs=pl.BlockSpec((MOE_BLOCK, D), lambda i, be, nu: (i, 0)),
        scratch_shapes=[pltpu.VMEM((D, D_EXPERT), BF), pltpu.VMEM((D, D_EXPERT), BF), pltpu.VMEM((D_EXPERT, D), BF)],
    )
    return pl.pallas_call(
        _experts_kernel,
        grid_spec=grid_spec,
        out_shape=jax.ShapeDtypeStruct((L, D), BF),
        compiler_params=_cparams("arbitrary"),
        name="experts",
    )(block_expert, n_used, xb, wg, wu, wd)


def _moe(xn, route, cnt, wg, wu, wd, *, layer):
    T = xn.shape[0]
    counts = cnt[0, ROUTE_OFF:ROUTE_OFF + N_EXPERTS].astype(jnp.int32)
    padded = ((counts + MOE_BLOCK - 1) // MOE_BLOCK) * MOE_BLOCK
    pend = jnp.cumsum(padded)
    pstart = pend - padded

    def slot_rows(k):
        e, rank = route[:, k].astype(jnp.int32), route[:, 4 + k].astype(jnp.int32)
        return jnp.sum(jnp.where(e[:, None] == jnp.arange(N_EXPERTS)[None, :], pstart[None, :], 0), axis=-1) + rank

    dest = [slot_rows(0), slot_rows(1)]
    n_blocks = -(-(2 * T) // MOE_BLOCK) + N_EXPERTS
    L = n_blocks * MOE_BLOCK
    tok = jnp.arange(T, dtype=jnp.int32)
    buf_tok = (jnp.arange(L, dtype=jnp.int32) % T).at[jnp.concatenate(dest)].set(
        jnp.concatenate([tok, tok]), unique_indices=True, mode="promise_in_bounds")
    block_start = jnp.arange(n_blocks, dtype=jnp.int32) * MOE_BLOCK
    block_expert = jnp.minimum(jnp.sum(pend[None, :] <= block_start[:, None], axis=1), N_EXPERTS - 1).astype(jnp.int32)
    n_used = (pend[-1:] // MOE_BLOCK).astype(jnp.int32)
    xb = xn.at[buf_tok].get(mode="promise_in_bounds")
    yb = _experts(block_expert, n_used, xb, wg, wu, wd, layer=layer)
    return (yb.at[dest[0]].get(mode="promise_in_bounds"), yb.at[dest[1]].get(mode="promise_in_bounds"))


def _gates(route):
    lane = _lane_iota(route.shape)
    g0 = jnp.sum(jnp.where(lane == 2, route, 0.0), axis=-1, keepdims=True)
    g1 = jnp.sum(jnp.where(lane == 3, route, 0.0), axis=-1, keepdims=True)
    return g0, g1


def _odd_in_kernel(x_ref, y0_ref, y1_ref, route_ref, g_ref, w_ref, x2_ref, q_ref, k_ref, v_ref, *, scale):
    g0, g1 = _gates(route_ref[...])
    x2 = x_ref[...] + (y0_ref[...].astype(F32) * g0 + y1_ref[...].astype(F32) * g1)
    x2_ref[...] = x2
    h = _rms(x2, g_ref[...], NORM_EPS).astype(BF)
    y = _dot(h, w_ref[...])
    n = q_ref.shape[1]
    q_ref[...] = (y[:, :n] * scale).astype(BF)
    k_ref[...] = y[:, n:2 * n].astype(BF)
    v_ref[...] = y[:, 2 * n:].astype(BF)


def _odd_in(x1, y0, y1, route, g, w):
    T, D = x1.shape
    n = w.shape[1] // 3

    def rows(width):
        return pl.BlockSpec((WIDE_TILE, width), lambda i: (i, 0))

    def full(a):
        return pl.BlockSpec(a.shape, lambda i: (0,) * a.ndim)

    return pl.pallas_call(
        functools.partial(_odd_in_kernel, scale=DH_C ** -0.5 * LOG2E),
        grid=(T // WIDE_TILE,),
        in_specs=[rows(D), rows(D), rows(D), rows(LANE), full(g), full(w)],
        out_specs=[rows(D), rows(n), rows(n), rows(n)],
        out_shape=[jax.ShapeDtypeStruct((T, D), F32)] + [jax.ShapeDtypeStruct((T, n), BF)] * 3,
        compiler_params=_cparams("parallel"),
        name="odd_in",
    )(x1, y0, y1, route, g, w)


def _state_rows_kernel(ids_ref, x_ref, g_ref, w_ref, o_ref):
    del ids_ref
    h = _rms(x_ref[...], g_ref[...], NORM_EPS).astype(BF)
    o_ref[...] = _dot(h, w_ref[...])


def _state_rows(tile_ids, x, g, w):
    D = x.shape[1]
    n = tile_ids.shape[0]
    grid_spec = pltpu.PrefetchScalarGridSpec(
        num_scalar_prefetch=1,
        grid=(n,),
        in_specs=[pl.BlockSpec((ROW_TILE, D), lambda i, ids: (ids[i], 0)),
                  pl.BlockSpec(g.shape, lambda i, ids: (0, 0)),
                  pl.BlockSpec(w.shape, lambda i, ids: (0, 0))],
        out_specs=pl.BlockSpec((ROW_TILE, w.shape[1]), lambda i, ids: (i, 0)),
    )
    return pl.pallas_call(
        _state_rows_kernel,
        grid_spec=grid_spec,
        out_shape=jax.ShapeDtypeStruct((n * ROW_TILE, w.shape[1]), F32),
        compiler_params=_cparams("parallel"),
        name="state_rows",
    )(tile_ids, x, g, w)


def _band_prompt_kernel(q_ref, k_ref, v_ref, bias_ref, o_ref, q2_ref, s0, s1, p_ref, l_ref,
                        *, tile, tiles_per_step):
    lane = _lane_iota((tile, LANE))
    q2_ref[...] = _split_halves(q_ref[...])
    n_rows = tile * tiles_per_step

    def key_tiles(t):
        qi = pl.program_id(2) * tiles_per_step + t
        out = []
        for kt in range(3):
            start = (qi + kt - 2) * tile
            neg = jnp.where(start >= 0, 0.0, NEG_INF)
            out.append((pl.multiple_of(jnp.maximum(start, 0), tile), neg))
        return out

    def scores(t, s_buf):
        for hh in range(2):
            q = q2_ref[hh * n_rows + t * tile:hh * n_rows + (t + 1) * tile, :]
            for kt, (start, neg) in enumerate(key_tiles(t)):
                s = _dot_nt(q, k_ref[pl.ds(start, tile), :])
                s_buf[hh, :, kt * tile:(kt + 1) * tile] = s + neg if kt < 2 else s

    def softmax_pv(t, s_buf):
        outs = []
        slot = t % 2
        for hh in range(2):
            for rb in range(tile // SOFTMAX_ROWS):
                rs = slice(rb * SOFTMAX_ROWS, (rb + 1) * SOFTMAX_ROWS)
                s = s_buf[hh, rs, :] + bias_ref[hh * tile + rb * SOFTMAX_ROWS:hh * tile + (rb + 1) * SOFTMAX_ROWS, :]
                p = jnp.exp2(s - jnp.max(s, axis=-1, keepdims=True))
                l_ref[slot, hh, rs, :] = jnp.broadcast_to(jnp.sum(p, axis=-1, keepdims=True), (SOFTMAX_ROWS, LANE))
                p_ref[slot, hh, rs, :] = p.astype(BF)
            acc = None
            for kt, (start, _) in enumerate(key_tiles(t)):
                pv = _dot(p_ref[slot, hh, :, kt * tile:(kt + 1) * tile], v_ref[pl.ds(start, tile), :])
                acc = pv if acc is None else acc + pv
            outs.append(acc / l_ref[slot, hh])
        o_ref[t * tile:(t + 1) * tile, :] = jnp.where(lane < DH_C, outs[0], outs[1]).astype(BF)

    bufs = [s0, s1]
    scores(0, bufs[0])
    for t in range(tiles_per_step):
        if t + 1 < tiles_per_step:
            scores(t + 1, bufs[(t + 1) % 2])
        softmax_pv(t, bufs[t % 2])


def _band_prompt(q, k, v, bias, *, batch, seq):
    tile = BAND_TILE
    tps = min(BAND_TILES_PER_STEP, seq // tile)
    nq = seq // (tile * tps)
    return pl.pallas_call(
        functools.partial(_band_prompt_kernel, tile=tile, tiles_per_step=tps),
        grid=(H_C // 2, batch, nq),
        in_specs=[pl.BlockSpec((tile * tps, LANE), lambda p, b, qi: (b * nq + qi, p)),
                  pl.BlockSpec((seq, LANE), lambda p, b, qi: (b, p)),
                  pl.BlockSpec((seq, LANE), lambda p, b, qi: (b, p)),
                  pl.BlockSpec((None, 2 * tile, 3 * tile), lambda p, b, qi: (p, 0, 0))],
        out_specs=pl.BlockSpec((tile * tps, LANE), lambda p, b, qi: (b * nq + qi, p)),
        out_shape=jax.ShapeDtypeStruct((batch * seq, H_C * DH_C), BF),
        scratch_shapes=[pltpu.VMEM((2 * tile * tps, LANE), BF),
                        pltpu.VMEM((2, tile, 3 * tile), F32), pltpu.VMEM((2, tile, 3 * tile), F32),
                        pltpu.VMEM((2, 2, tile, 3 * tile), BF), pltpu.VMEM((2, 2, tile, LANE), F32)],
        compiler_params=_cparams("parallel", "parallel", "arbitrary"),
        name="band_prompt",
    )(q, k, v, bias)


def _band_sample_kernel(q_ref, k_ref, v_ref, ck_ref, cv_ref, bias_ref, o_ref):
    n_new = q_ref.shape[0]
    n_past = ck_ref.shape[0]
    lane = _lane_iota((n_new, LANE))
    for pr in range(H_C // 2):
        sl = slice(pr * LANE, (pr + 1) * LANE)
        q2x = _split_halves(q_ref[:, sl])
        s_p = _dot_nt(q2x, ck_ref[:, sl].astype(BF)) + bias_ref[pr, :, 0:n_past]
        s_n = _dot_nt(q2x, k_ref[:, sl]) + bias_ref[pr, :, n_past:n_past + n_new]
        o = _softmax_pv([s_p, s_n], [cv_ref[:, sl].astype(BF), v_ref[:, sl]])
        o_ref[:, sl] = jnp.where(lane < DH_C, o[:n_new], o[n_new:]).astype(BF)


def _band_sample(q, k, v, ck, cv, bias, *, n_prompt_rows, dec_seq):
    dec_batch, n_past, width = ck.shape
    base = n_prompt_rows // dec_seq
    new = pl.BlockSpec((dec_seq, width), lambda s: (base + s, 0))
    cache = pl.BlockSpec((None, n_past, width), lambda s: (s, 0, 0))
    return pl.pallas_call(
        _band_sample_kernel,
        grid=(dec_batch,),
        in_specs=[new, new, new, cache, cache, pl.BlockSpec(bias.shape, lambda s: (0, 0, 0))],
        out_specs=pl.BlockSpec((dec_seq, width), lambda s: (s, 0)),
        out_shape=jax.ShapeDtypeStruct((dec_batch * dec_seq, width), BF),
        compiler_params=_cparams("parallel"),
        name="band_sample",
    )(q, k, v, ck, cv, bias)


def _final_kernel(x_ref, y0_ref, y1_ref, route_ref, g_ref, o_ref):
    g0, g1 = _gates(route_ref[...])
    x = x_ref[...] + (y0_ref[...].astype(F32) * g0 + y1_ref[...].astype(F32) * g1)
    o_ref[...] = _rms(x, g_ref[...], NORM_EPS)


def _final(x, y0, y1, route, g, *, first_tile, n_tiles):
    D = x.shape[1]

    def rows(width):
        return pl.BlockSpec((WIDE_TILE, width), lambda i: (first_tile + i, 0))

    return pl.pallas_call(
        _final_kernel,
        grid=(n_tiles,),
        in_specs=[rows(D), rows(D), rows(D), rows(LANE), pl.BlockSpec(g.shape, lambda i: (0, 0))],
        out_specs=pl.BlockSpec((WIDE_TILE, D), lambda i: (i, 0)),
        out_shape=jax.ShapeDtypeStruct((n_tiles * WIDE_TILE, D), F32),
        compiler_params=_cparams("parallel"),
        name="final_norm",
    )(x, y0, y1, route, g)


def _prep_even_weights(w_in, w_qup, w_kvup):
    D = w_in.shape[0]
    a_in = Q_RANK + KV_RANK + ROPE
    bq = H_B * 2 * DH_B
    wcq, wckv, wkr = w_in[:, :Q_RANK], w_in[:, Q_RANK:Q_RANK + KV_RANK], w_in[:, Q_RANK + KV_RANK:a_in]
    wqd, wkd, wvd = w_in[:, a_in:a_in + bq], w_in[:, a_in + bq:a_in + 2 * bq], w_in[:, a_in + 2 * bq:]
    half = ROPE // 2

    def z(n):
        return jnp.zeros((D, n), w_in.dtype)

    kr_m = jnp.concatenate([z(NOPE), wkr, z(HEAD_PAD - NOPE - ROPE)], axis=1)
    kr_s = jnp.concatenate([z(NOPE), wkr[:, half:], wkr[:, :half], z(HEAD_PAD - NOPE - ROPE)], axis=1)
    w0 = jnp.concatenate([wcq, wckv, kr_m, kr_s, wqd, wkd, wvd], axis=1).astype(BF)
    wq3 = w_qup.reshape(Q_RANK, H_A, NOPE + ROPE)
    nope, r1, r2 = wq3[:, :, :NOPE], wq3[:, :, NOPE:NOPE + half], wq3[:, :, NOPE + half:]
    zq = jnp.zeros((Q_RANK, H_A, HEAD_PAD - NOPE - ROPE), w_qup.dtype)
    wq = jnp.concatenate([nope, r1, r2, zq], axis=-1).reshape(Q_RANK, H_A * HEAD_PAD).astype(BF)
    wqs = jnp.concatenate([jnp.zeros_like(nope), r2, r1, zq], axis=-1).reshape(Q_RANK, H_A * HEAD_PAD).astype(BF)
    wkv3 = w_kvup.reshape(KV_RANK, H_A, NOPE + V_A)
    wk = jnp.concatenate([wkv3[:, :, :NOPE], jnp.zeros((KV_RANK, H_A, HEAD_PAD - NOPE), w_kvup.dtype)],
                         axis=-1).reshape(KV_RANK, H_A * HEAD_PAD).astype(BF)
    wv4 = wkv3[:, :, NOPE:].reshape(KV_RANK, H_A // 2, 2, V_A)
    zv = jnp.zeros((KV_RANK, H_A // 2, HEAD_PAD - V_A), w_kvup.dtype)
    wv = jnp.concatenate([wv4[:, :, 0], zv, zv, wv4[:, :, 1]], axis=-1).reshape(KV_RANK, H_A * HEAD_PAD).astype(BF)
    vone = jnp.zeros((H_A // 2, 2 * HEAD_PAD), F32).at[:, V_A].set(1.0).at[:, HEAD_PAD].set(1.0)
    r = jnp.arange(ROPE)
    e_mat = jnp.zeros((ROPE, H_A, HEAD_PAD), F32).at[r[:, None], jnp.arange(H_A)[None, :], NOPE + r[:, None]].set(1.0)
    return w0, wq, wqs, wk, wv, vone.reshape(1, H_A * HEAD_PAD), e_mat.reshape(ROPE, H_A * HEAD_PAD).astype(BF)


def _rope_tables(pos):
    half = ROPE // 2
    inv = jnp.power(ROPE_BASE, -jnp.arange(half, dtype=F32) / half)
    ang = pos.astype(F32)[:, None] * inv[None, :]
    c, s = jnp.cos(ang), jnp.sin(ang)
    n = pos.shape[0]
    pad = jnp.zeros((n, HEAD_PAD - NOPE - ROPE), F32)
    cs = jnp.concatenate([jnp.ones((n, NOPE), F32), c, c, pad], axis=1)
    sn = jnp.concatenate([jnp.zeros((n, NOPE), F32), -s, s, pad], axis=1)
    return cs, sn


def _band_bias_tiles(table, n_rows, n_keys, key_offset, masked):
    i = jnp.arange(n_rows)[:, None]
    kpos = jnp.arange(n_keys)[None, :] - key_offset
    d_max = n_rows - 1 + key_offset
    w = n_rows + n_keys
    d = d_max - jnp.arange(w)
    rev = (table.astype(F32) * LOG2E)[:, jnp.clip(d, -REL_CLIP, REL_CLIP) + REL_CLIP]
    skew = jnp.tile(rev, (1, n_rows))[:, :n_rows * (w - 1)].reshape(-1, n_rows, w - 1)
    bias = skew[:, :, n_rows - 1:n_rows - 1 + n_keys]
    if masked:
        qc, kc = i // CHUNK, jnp.floor_divide(kpos, CHUNK)
        vis = (kc <= qc) & (kc >= qc - LEFT_CHUNKS)
        bias = jnp.where(vis[None], bias, NEG_INF)
    return bias.reshape(H_C // 2, 2 * n_rows, n_keys)


def kernel(x_prompt, x_sample, cache_mla_ckv, cache_mla_krope, cache_diff_k, cache_diff_v, cache_band_k, cache_band_v, ln_mix, w_in_even, mla_q_norm, mla_w_qup, mla_kv_norm, mla_w_kvup, diff_lam_q1, diff_lam_k1, diff_lam_q2, diff_lam_k2, diff_subln, w_out_even, w_in_odd, band_rel_bias, w_out_odd, ln_ffn, moe_w_group, moe_b_group, moe_w_router, moe_b_router, moe_w_gate, moe_w_up, moe_w_down, ln_final):
    B, S, D = x_prompt.shape
    DB, DS, _ = x_sample.shape
    n_past = cache_mla_ckv.shape[2]
    c_past = cache_band_k.shape[2]
    assert ln_mix.shape[0] == 2 and S % (2 * min(ATT_TILE, S // 2)) == 0 and S % BAND_TILE == 0
    assert (B * S) % WIDE_TILE == 0 and (DB * DS) % WIDE_TILE == 0 and ROW_TILE % DS == 0 and DS == CHUNK
    assert n_past % CHUNK == 0 and c_past == LEFT_CHUNKS * CHUNK and S >= c_past and c_past % ROW_TILE == 0
    BS, NS = B * S, DB * DS
    T = BS + NS
    n_p, n_s = BS // ROW_TILE, NS // ROW_TILE
    xp, xs = x_prompt.reshape(BS, D), x_sample.reshape(NS, D)
    row = lambda a: a.reshape(1, -1)

    w0, wq, wqs, wk, wv, vone, e_mat = _prep_even_weights(w_in_even[0], mla_w_qup[0], mla_w_kvup[0])
    cs_p, sn_p = _rope_tables(jnp.arange(S))
    cs_s, sn_s = _rope_tables(n_past + jnp.arange(DS))
    reps = ROW_TILE // DS
    even_w = (row(ln_mix[0]), w0, row(mla_q_norm[0]), wq, wqs, row(mla_kv_norm[0]), wk, wv, vone)
    qa, ka, va, ckv_p, kr_p, qd, kdb, vdb, kd_p, vd_p = _even_in(xp, *even_w, cs_p, sn_p)
    qa_s, ka_s, va_s, ckv_s, kr_s, qd_s, kdb_s, vdb_s, kd_s, vd_s = _even_in(
        xs, *even_w, jnp.tile(cs_s, (reps, 1)), jnp.tile(sn_s, (reps, 1)))
    slopes = jnp.exp2(-8.0 * jnp.arange(1, H_B + 1, dtype=F32) / H_B) * LOG2E
    lamv = jnp.stack([diff_lam_q1[0], diff_lam_k1[0], diff_lam_q2[0], diff_lam_k2[0]]).astype(F32)
    subln = row(diff_subln[0])
    lam_init = 0.8 - 0.6 * math.exp(-0.3 * 0)
    oa_p = _mla_prompt(qa, ka, va, batch=B, seq=S)
    ob_p = _diff_prompt(slopes, lamv, subln, qd, kdb, vdb, batch=B, seq=S, lam_init=lam_init)
    oa_s, ob_s = _even_sample(
        slopes, lamv, subln, wk, wv, vone, e_mat, qa_s, ka_s, va_s, cache_mla_ckv[0], cache_mla_krope[0],
        qd_s, kdb_s, vdb_s, cache_diff_k[0], cache_diff_v[0],
        n_prompt_rows=0, dec_seq=DS, lam_init=lam_init)

    def router_weights(l):
        wr = jnp.concatenate([moe_w_group[l], moe_w_router[l],
                              jnp.zeros((D, LANE - N_GROUPS - N_EXPERTS), F32)], axis=1).astype(BF)
        br = jnp.concatenate([moe_b_group[l], moe_b_router[l],
                              jnp.zeros((LANE - N_GROUPS - N_EXPERTS,), F32)]).astype(F32)
        return wr, row(br)

    n_a = H_A * V_A
    wo = w_out_even[0].astype(BF)
    wr, br = router_weights(0)
    x1, xn, route, cnt = _out_router([(xp, xs), (oa_p, oa_s), (ob_p, ob_s)], [wo[:n_a], wo[n_a:]],
                                     row(ln_ffn[0]), wr, br)
    y0, y1 = _moe(xn, route, cnt, moe_w_gate, moe_w_up, moe_w_down, layer=0)

    w_odd = w_in_odd[0].astype(BF)
    n_c = H_C * DH_C
    x2, qc, kc, vc = _odd_in(x1, y0, y1, route, row(ln_mix[1]), w_odd)
    tail = c_past // ROW_TILE
    tiles_per_seq = S // ROW_TILE
    tile_ids = jnp.concatenate(
        [jnp.arange(tiles_per_seq - tail, tiles_per_seq, dtype=jnp.int32) + b * tiles_per_seq for b in range(B)]
        + [jnp.arange(n_p, n_p + n_s, dtype=jnp.int32)])
    st = _state_rows(tile_ids, x2, row(ln_mix[1]), w_odd[:, n_c:])
    bias_p = _band_bias_tiles(band_rel_bias[0], BAND_TILE, 3 * BAND_TILE, 2 * BAND_TILE, True)
    bias_s = _band_bias_tiles(band_rel_bias[0], DS, c_past + DS, c_past, False)
    oc_p = _band_prompt(qc, kc, vc, bias_p, batch=B, seq=S)
    oc_s = _band_sample(qc, kc, vc, cache_band_k[0].reshape(DB, c_past, n_c), cache_band_v[0].reshape(DB, c_past, n_c),
                        bias_s, n_prompt_rows=BS, dec_seq=DS)
    wr, br = router_weights(1)
    x3, xn, route, cnt = _out_router([(x2,), (oc_p, oc_s)], [w_out_odd[0].astype(BF)], row(ln_ffn[1]), wr, br)
    y0, y1 = _moe(xn, route, cnt, moe_w_gate, moe_w_up, moe_w_down, layer=1)
    g_fin = row(ln_final)
    y_prompt = _final(x3, y0, y1, route, g_fin, first_tile=0, n_tiles=BS // WIDE_TILE).reshape(B, S, D)
    y_sample = _final(x3, y0, y1, route, g_fin, first_tile=BS // WIDE_TILE, n_tiles=NS // WIDE_TILE).reshape(DB, DS, D)

    def shaped(a_p, a_s, *tail_shape):
        return a_p.reshape(1, B, S, *tail_shape), a_s.reshape(1, DB, DS, *tail_shape)

    ckv_p, ckv_s = shaped(ckv_p, ckv_s, KV_RANK)
    kr_p, kr_s = shaped(kr_p, kr_s, ROPE)
    kd_p, kd_s = shaped(kd_p, kd_s, H_B, 2 * DH_B)
    vd_p, vd_s = shaped(vd_p, vd_s, H_B, V_B)
    n_tail = B * c_past
    bk_p = st[:n_tail, :n_c].reshape(1, B, c_past, H_C, DH_C)
    bv_p = st[:n_tail, n_c:].reshape(1, B, c_past, H_C, DH_C)
    k_new = st[n_tail:, :n_c].reshape(DB, DS, H_C, DH_C)
    v_new = st[n_tail:, n_c:].reshape(DB, DS, H_C, DH_C)
    bk_s = jnp.concatenate([cache_band_k[0][:, DS:], k_new], axis=1)[None]
    bv_s = jnp.concatenate([cache_band_v[0][:, DS:], v_new], axis=1)[None]
    return (y_prompt, y_sample, ckv_p, kr_p, kd_p, vd_p, bk_p, bv_p, ckv_s, kr_s, kd_s, vd_s, bk_s, bv_s)
```

```python
import functools
import math

import jax
import jax.numpy as jnp
from jax import lax
from jax.experimental import pallas as pl
from jax.experimental.pallas import tpu as pltpu

BF = jnp.bfloat16
F32 = jnp.float32
NEG_INF = float("-inf")
LOG2E = math.log2(math.e)

CHUNK = 64
NORM_EPS = 1e-6
SUBLN_EPS = 1e-5
H_A, NOPE, ROPE, V_A, Q_RANK, KV_RANK = 8, 64, 32, 64, 384, 256
ROPE_BASE = 10000.0
H_B, DH_B, V_B = 4, 64, 128
H_C, DH_C, LEFT_CHUNKS, REL_CLIP = 16, 64, 8, 128
N_GROUPS, EPG, N_EXPERTS, D_EXPERT = 4, 8, 32, 512
LANE = 128
HEAD_PAD = 128
ROUTE_OFF = N_GROUPS

ROW_TILE = 256
WIDE_TILE = 512
ROUTE_ROWS = 128
ATT_TILE = 512
SOFTMAX_ROWS = 32
BAND_TILE = 256
BAND_TILES_PER_STEP = 4
MOE_BLOCK = 512
VMEM_LIMIT = 56 * 1024 * 1024


def _cparams(*sem):
    return pltpu.CompilerParams(dimension_semantics=sem, vmem_limit_bytes=VMEM_LIMIT)


def _rms(x, g, eps):
    return x * lax.rsqrt(jnp.mean(x * x, axis=-1, keepdims=True) + eps) * g


def _dot(a, b):
    return jnp.dot(a, b, preferred_element_type=F32)


def _dot_nt(a, b):
    return lax.dot_general(a, b, (((1,), (1,)), ((), ())), preferred_element_type=F32)


def _lane_iota(shape):
    return lax.broadcasted_iota(jnp.int32, shape, len(shape) - 1)


def _split_halves(q):
    qf = q.astype(F32)
    lane = _lane_iota(qf.shape)
    return jnp.concatenate([jnp.where(lane < 64, qf, 0.0), jnp.where(lane >= 64, qf, 0.0)], axis=0).astype(BF)


def _softmax_pv(s_list, v_list):
    m = functools.reduce(jnp.maximum, [jnp.max(s, axis=-1, keepdims=True) for s in s_list])
    acc, l = None, None
    for s, v in zip(s_list, v_list):
        p = jnp.exp2(s - m)
        ls = jnp.sum(p, axis=-1, keepdims=True)
        a = _dot(p.astype(BF), v)
        l = ls if l is None else l + ls
        acc = a if acc is None else acc + a
    return acc / l


def _diff_lambda(lamv, lam_init):
    a = jnp.exp(jnp.sum(lamv[0:1] * lamv[1:2], axis=-1, keepdims=True))
    b = jnp.exp(jnp.sum(lamv[2:3] * lamv[3:4], axis=-1, keepdims=True))
    return a - b + lam_init


def _diff_finish(o1, o2, lam, subln, lam_init):
    o = o1 - lam * o2
    return _rms(o, subln, SUBLN_EPS) * (1.0 - lam_init)


def _even_in_kernel(x_ref, g_ref, w0_ref, qn_ref, wq_ref, wqs_ref, kvn_ref, wk_ref, wv_ref, vone_ref,
                    cs_ref, sn_ref, kpos_ref,
                    qa_ref, ka_ref, va_ref, ckv_ref, kr_ref, qd_ref, kdb_ref, vdb_ref, kd_ref, vd_ref, kdm_ref,
                    *, a_scale, b_scale):
    h = _rms(x_ref[...], g_ref[...], NORM_EPS).astype(BF)
    y = _dot(h, w0_ref[...])
    cq, ckv = y[:, 0:384], y[:, 384:640]
    kr_m, kr_s = y[:, 640:768], y[:, 768:896]
    qd, kd, vd = y[:, 896:1408], y[:, 1408:1920], y[:, 1920:2432]
    cs, sn = cs_ref[...], sn_ref[...]
    cqn = _rms(cq, qn_ref[...], NORM_EPS).astype(BF)
    qm = _dot(cqn, wq_ref[...])
    qs = _dot(cqn, wqs_ref[...])
    ckvn = _rms(ckv, kvn_ref[...], NORM_EPS)
    ckv_ref[...] = ckvn
    krp = kr_m * cs + kr_s * sn
    kr_ref[...] = krp[:, NOPE:NOPE + ROPE]
    cb = ckvn.astype(BF)
    kn = _dot(cb, wk_ref[...])
    for hh in range(H_A):
        sl = slice(hh * HEAD_PAD, (hh + 1) * HEAD_PAD)
        qa_ref[:, sl] = ((qm[:, sl] * cs + qs[:, sl] * sn) * a_scale).astype(BF)
        ka_ref[:, sl] = (kn[:, sl] + krp).astype(BF)
    va_ref[...] = (_dot(cb, wv_ref[...]) + vone_ref[...]).astype(BF)
    qd_ref[...] = (qd * b_scale).astype(BF)
    for hh in range(H_B):
        kd_ref[:, hh, :] = kd[:, hh * V_B:(hh + 1) * V_B]
        vd_ref[:, hh, :] = vd[:, hh * V_B:(hh + 1) * V_B]
    kdb_ref[...] = kd.astype(BF)
    vdb_ref[...] = vd.astype(BF)
    kpos = kpos_ref[...]
    lane = _lane_iota(kpos.shape)
    for hh in range(H_B):
        kh = kd[:, hh * V_B:(hh + 1) * V_B]
        for c, km in enumerate([kh, pltpu.roll(kh, DH_B, 1)]):
            kdm_ref[:, (2 * hh + c) * LANE:(2 * hh + c + 1) * LANE] = (jnp.where(lane < DH_B, km, 0.0) + kpos).astype(BF)


def _even_in(x, g, w0, qn, wq, wqs, kvn, wk, wv, vone, cs_tab, sn_tab, kpos_tab):
    T, D = x.shape
    pos_blocks = cs_tab.shape[0] // ROW_TILE

    def full(a):
        return pl.BlockSpec(a.shape, lambda i: (0,) * a.ndim)

    def rows(*tail):
        return pl.BlockSpec((ROW_TILE,) + tail, lambda i: (i,) + (0,) * len(tail))

    pos_spec = pl.BlockSpec((ROW_TILE, LANE), lambda i: (i % pos_blocks, 0))
    outs = [((1024,), BF), ((1024,), BF), ((1024,), BF), ((KV_RANK,), F32), ((ROPE,), F32),
            ((512,), BF), ((512,), BF), ((512,), BF), ((H_B, V_B), F32), ((H_B, V_B), F32),
            ((2 * H_B * LANE,), BF)]
    return pl.pallas_call(
        functools.partial(_even_in_kernel,
                          a_scale=(NOPE + ROPE) ** -0.5 * LOG2E, b_scale=DH_B ** -0.5 * LOG2E),
        grid=(T // ROW_TILE,),
        in_specs=[rows(D), full(g), full(w0), full(qn), full(wq), full(wqs), full(kvn), full(wk), full(wv),
                  full(vone), pos_spec, pos_spec, pos_spec],
        out_specs=[rows(*tail) for tail, _ in outs],
        out_shape=[jax.ShapeDtypeStruct((T,) + tail, dt) for tail, dt in outs],
        compiler_params=_cparams("parallel"),
        name="even_in",
    )(x, g, w0, qn, wq, wqs, kvn, wk, wv, vone, cs_tab, sn_tab, kpos_tab)


def _chunk_causal_mask(tq, tk):
    row = lax.broadcasted_iota(jnp.int32, (tq, tk), 0)
    col = lax.broadcasted_iota(jnp.int32, (tq, tk), 1)
    return (col // CHUNK) <= (row // CHUNK)


def _softmax_tile(s_ref, p_ref, m_ref, l_ref, a_ref, *, add_ref=None, off=None):
    tile = s_ref.shape[0]
    reps = tile // LANE
    blocks = [slice(rb * SOFTMAX_ROWS, (rb + 1) * SOFTMAX_ROWS) for rb in range(tile // SOFTMAX_ROWS)]

    def biased(rs):
        s = s_ref[rs, :]
        if add_ref is not None:
            s = s + add_ref[rs, :]
        return s

    for rs in blocks:
        m_old = m_ref[rs, :]
        red = jnp.broadcast_to(jnp.max(biased(rs), axis=-1, keepdims=True), m_old.shape)
        if off is not None:
            red = red + off
        m_new = jnp.maximum(m_old, red)
        a_ref[rs, :] = jnp.exp2(m_old - m_new)
        m_ref[rs, :] = m_new
    for rs in blocks:
        m_new = m_ref[rs, :]
        shift = m_new if off is None else m_new - off
        p = jnp.exp2(biased(rs) - jnp.concatenate([shift] * reps, axis=1))
        if l_ref is not None:
            l_ref[rs, :] = a_ref[rs, :] * l_ref[rs, :] + jnp.broadcast_to(
                jnp.sum(p, axis=-1, keepdims=True), m_new.shape)
        p_ref[rs, :] = p.astype(BF)


def _flash_causal(qi, bufs, scores, update):
    s0, s1, p0, p1, m_ref, l_ref, a_ref, acc_ref = bufs
    m_ref[...] = jnp.full(m_ref.shape, NEG_INF, F32)
    l_ref[...] = jnp.zeros(l_ref.shape, F32)
    acc_ref[...] = jnp.zeros(acc_ref.shape, F32)
    scores(0, s0)

    def pair(jj, carry):
        j = 2 * jj
        scores(j + 1, s1)
        update(j, s0, p0, False)
        scores(j + 2, s0)
        update(j + 1, s1, p1, False)
        return carry

    lax.fori_loop(0, qi // 2, pair, 0)

    @pl.when(qi % 2 == 1)
    def _():
        scores(qi, s1)
        update(qi - 1, s0, p0, False)
        s0[...] = s1[...]

    update(qi, s0, p0, True)


def _flash_causal_pair(t, bufs, scores, update):
    s0, s1, p0, p1, m_ref, l_ref, a_ref, acc_ref = bufs
    m_ref[...] = jnp.full(m_ref.shape, NEG_INF, F32)
    l_ref[...] = jnp.zeros(l_ref.shape, F32)
    acc_ref[...] = jnp.zeros(acc_ref.shape, F32)
    both = (0, 1)
    scores(0, s0, both)

    def pair(jj, carry):
        j = 2 * jj
        scores(j + 1, s1, both)
        update(j, s0, p0, both, ())
        scores(j + 2, s0, both)
        update(j + 1, s1, p1, both, ())
        return carry

    lax.fori_loop(0, t, pair, 0)
    scores(2 * t + 1, s1, (1,))
    update(2 * t, s0, p0, both, (0,))
    update(2 * t + 1, s1, p1, (1,), (1,))


def _flash_scratch(n_chains, rows, tile, acc_width=LANE):
    s = pltpu.VMEM((n_chains, rows, tile), F32)
    p = pltpu.VMEM((n_chains, rows, tile), BF)
    stat = pltpu.VMEM((n_chains, rows, LANE), F32)
    return [s, s, p, p, stat, stat, stat, pltpu.VMEM((n_chains, rows, acc_width), F32)]


def _rows_of(qts, tile):
    return slice(qts[0] * tile, (qts[-1] + 1) * tile)


def _mla_prompt_kernel(q_ref, k_ref, v_ref, o_ref, s0, s1, p0, p1, m_ref, l_ref, a_ref, acc_ref, dmask_ref,
                       *, tile):
    qi = pl.program_id(2)

    @pl.when(qi == 0)
    def _():
        dmask_ref[...] = jnp.where(_chunk_causal_mask(tile, tile), 0.0, NEG_INF)

    sls = [slice(hh * HEAD_PAD, (hh + 1) * HEAD_PAD) for hh in range(2)]

    def scores(j, s_buf):
        start = pl.multiple_of(j * tile, tile)
        for c, sl in enumerate(sls):
            s_buf[c] = _dot_nt(q_ref[:, sl], k_ref[pl.ds(start, tile), sl])

    def update(j, s_buf, p_buf, diag):
        start = pl.multiple_of(j * tile, tile)
        for c, sl in enumerate(sls):
            _softmax_tile(s_buf.at[c], p_buf.at[c], m_ref.at[c], None, a_ref.at[c],
                          add_ref=dmask_ref if diag else None)
            acc_ref[c] = a_ref[c] * acc_ref[c] + _dot(p_buf[c], v_ref[pl.ds(start, tile), sl])

    _flash_causal(qi, (s0, s1, p0, p1, m_ref, l_ref, a_ref, acc_ref), scores, update)
    lane = _lane_iota((tile, LANE))
    a0, a1 = acc_ref[0], acc_ref[1]
    l0 = jnp.sum(jnp.where(lane == V_A, a0, 0.0), axis=-1, keepdims=True)
    l1 = jnp.sum(jnp.where(lane == 0, a1, 0.0), axis=-1, keepdims=True)
    o_ref[...] = jnp.where(lane < V_A, a0 / l0, a1 / l1).astype(BF)


def _mla_prompt(qa, ka, va, *, batch, seq):
    tile = min(ATT_TILE, seq)
    nq = seq // tile
    return pl.pallas_call(
        functools.partial(_mla_prompt_kernel, tile=tile),
        grid=(batch, H_A // 2, nq),
        in_specs=[pl.BlockSpec((tile, 2 * HEAD_PAD), lambda b, p, qi: (b * nq + qi, p)),
                  pl.BlockSpec((seq, 2 * HEAD_PAD), lambda b, p, qi: (b, p)),
                  pl.BlockSpec((seq, 2 * HEAD_PAD), lambda b, p, qi: (b, p))],
        out_specs=pl.BlockSpec((tile, LANE), lambda b, p, qi: (b * nq + qi, p)),
        out_shape=jax.ShapeDtypeStruct((batch * seq, H_A * V_A), BF),
        scratch_shapes=_flash_scratch(2, tile, tile) + [pltpu.VMEM((tile, tile), F32)],
        compiler_params=_cparams("parallel", "parallel", "arbitrary"),
        name="mla_prompt",
    )(qa, ka, va)


def _diff_prompt_kernel(slopes_ref, lamv_ref, subln_ref, q_ref, k_ref, v_ref, o_ref,
                        s0, s1, p0, p1, m_ref, l_ref, a_ref, acc_ref, dbias_ref, q2_ref,
                        *, tile, lam_init):
    h = pl.program_id(1)
    qi = pl.program_id(2)
    slope = slopes_ref[h]
    n_rows = 2 * tile
    qf = q_ref[...].astype(F32)
    lane = _lane_iota(qf.shape)
    sv = jnp.full((16, LANE), slope, F32)
    hi = sv.astype(BF).astype(F32)
    mid = (sv - hi).astype(BF).astype(F32)
    lo = (sv - hi - mid).astype(BF).astype(F32)
    l16 = _lane_iota((16, LANE))
    pieces = jnp.where(l16 < DH_B + 2, hi, jnp.where(l16 < DH_B + 4, mid, lo))
    q_digits = jnp.where((l16 >= DH_B) & (l16 < DH_B + 6), pieces, 0.0)[0:1]
    for c, qm in enumerate([qf, pltpu.roll(qf, DH_B, 1)]):
        q2_ref[c * n_rows:(c + 1) * n_rows, :] = (jnp.where(lane < DH_B, qm, 0.0) + q_digits).astype(BF)

    @pl.when(qi == 0)
    def _():
        row = lax.broadcasted_iota(jnp.int32, (tile, tile), 0)
        col = lax.broadcasted_iota(jnp.int32, (tile, tile), 1)
        later = (2.0 * slope) * jnp.minimum(row - col, 0).astype(F32)
        dbias_ref[...] = jnp.where(_chunk_causal_mask(tile, tile), later, NEG_INF)

    def scores(j, s_buf, qts):
        start = pl.multiple_of(j * tile, tile)
        rows = _rows_of(qts, tile)
        for c in range(2):
            q = q2_ref[c * n_rows + rows.start:c * n_rows + rows.stop, :]
            s_buf[c, rows, :] = _dot_nt(q, k_ref[pl.ds(start, tile), c * LANE:(c + 1) * LANE])

    def update(j, s_buf, p_buf, qts, diag_qts):
        start = pl.multiple_of(j * tile, tile)
        off = slope * (j * tile).astype(F32)
        rows = _rows_of(qts, tile)
        for c in range(2):
            for qt in qts:
                r = _rows_of((qt,), tile)
                _softmax_tile(s_buf.at[c, r], p_buf.at[c, r], m_ref.at[c, r], l_ref.at[c, r], a_ref.at[c, r],
                              add_ref=dbias_ref if qt in diag_qts else None, off=off)
            acc_ref[c, rows, :] = (a_ref[c, rows, :] * acc_ref[c, rows, :]
                                   + _dot(p_buf[c, rows, :], v_ref[pl.ds(start, tile), :]))

    _flash_causal_pair(qi, (s0, s1, p0, p1, m_ref, l_ref, a_ref, acc_ref), scores, update)
    lam = _diff_lambda(lamv_ref[...], lam_init)
    o_ref[...] = _diff_finish(acc_ref[0] / l_ref[0], acc_ref[1] / l_ref[1], lam, subln_ref[...],
                              lam_init).astype(BF)


def _diff_prompt(slopes, lamv, subln, qd, kdb, vdb, *, batch, seq, lam_init):
    tile = min(ATT_TILE, seq // 2)
    nq = seq // (2 * tile)
    return pl.pallas_call(
        functools.partial(_diff_prompt_kernel, tile=tile, lam_init=lam_init),
        grid=(batch, H_B, nq),
        in_specs=[pl.BlockSpec(memory_space=pltpu.SMEM),
                  pl.BlockSpec(lamv.shape, lambda b, h, qi: (0, 0)),
                  pl.BlockSpec(subln.shape, lambda b, h, qi: (0, 0)),
                  pl.BlockSpec((2 * tile, LANE), lambda b, h, qi: (b * nq + qi, h)),
                  pl.BlockSpec((seq, 2 * LANE), lambda b, h, qi: (b, h)),
                  pl.BlockSpec((seq, LANE), lambda b, h, qi: (b, h))],
        out_specs=pl.BlockSpec((2 * tile, LANE), lambda b, h, qi: (b * nq + qi, h)),
        out_shape=jax.ShapeDtypeStruct((batch * seq, H_B * V_B), BF),
        scratch_shapes=_flash_scratch(2, 2 * tile, tile) + [pltpu.VMEM((tile, tile), F32),
                                                            pltpu.VMEM((4 * tile, LANE), BF)],
        compiler_params=_cparams("parallel", "parallel", "arbitrary"),
        name="diff_prompt",
    )(slopes, lamv, subln, qd, kdb, vdb)


def _even_sample_kernel(slopes_ref, lamv_ref, subln_ref, wk_ref, wv_ref, vone_ref, e_ref,
                        qa_ref, ka_ref, va_ref, ckv_ref, kr_ref,
                        qd_ref, kdb_ref, vdb_ref, ck_ref, cv_ref,
                        oa_ref, ob_ref, *, lam_init):
    n_new = qa_ref.shape[0]
    n_past = ckv_ref.shape[0]
    ckvp = ckv_ref[...].astype(BF)
    krp = kr_ref[...].astype(BF)
    lane = _lane_iota((n_new, LANE))
    for pr in range(H_A // 2):
        res = []
        for hh in range(2):
            sl = slice((2 * pr + hh) * HEAD_PAD, (2 * pr + hh + 1) * HEAD_PAD)
            q = qa_ref[:, sl]
            kp = (_dot(ckvp, wk_ref[:, sl]) + _dot(krp, e_ref[:, sl])).astype(BF)
            vp = (_dot(ckvp, wv_ref[:, sl]) + vone_ref[:, sl]).astype(BF)
            res.append(_softmax_pv([_dot_nt(q, kp), _dot_nt(q, ka_ref[:, sl])], [vp, va_ref[:, sl]]))
        oa_ref[:, pr * LANE:(pr + 1) * LANE] = jnp.where(lane < V_A, res[0], res[1]).astype(BF)
    rowp = lax.broadcasted_iota(jnp.int32, (n_new, n_past), 0)
    colp = lax.broadcasted_iota(jnp.int32, (n_new, n_past), 1)
    dist_p = (rowp - colp + n_past).astype(F32)
    dist_p = jnp.concatenate([dist_p, dist_p], axis=0)
    rown = lax.broadcasted_iota(jnp.int32, (n_new, n_new), 0)
    coln = lax.broadcasted_iota(jnp.int32, (n_new, n_new), 1)
    dist_n = jnp.abs(rown - coln).astype(F32)
    dist_n = jnp.concatenate([dist_n, dist_n], axis=0)
    lam = _diff_lambda(lamv_ref[...], lam_init)
    for h in range(H_B):
        sl = slice(h * LANE, (h + 1) * LANE)
        slope = slopes_ref[h]
        q2x = _split_halves(qd_ref[:, sl])
        kp = ck_ref[:, h, :].astype(BF)
        vp = cv_ref[:, h, :].astype(BF)
        s_p = _dot_nt(q2x, kp) - slope * dist_p
        s_n = _dot_nt(q2x, kdb_ref[:, sl]) - slope * dist_n
        o = _softmax_pv([s_p, s_n], [vp, vdb_ref[:, sl]])
        ob_ref[:, sl] = _diff_finish(o[:n_new], o[n_new:], lam, subln_ref[...], lam_init).astype(BF)


def _even_sample(slopes, lamv, subln, wk, wv, vone, e_mat, qa, ka, va, ckv_c, kr_c, qd, kdb, vdb, ck_c, cv_c,
                 *, n_prompt_rows, dec_seq, lam_init):
    dec_batch, n_past = ckv_c.shape[0], ckv_c.shape[1]
    base = n_prompt_rows // dec_seq

    def full(a):
        return pl.BlockSpec(a.shape, lambda s: (0,) * a.ndim)

    def new(width):
        return pl.BlockSpec((dec_seq, width), lambda s: (base + s, 0))

    def cache(*tail):
        return pl.BlockSpec((None, n_past) + tail, lambda s: (s, 0) + (0,) * len(tail))

    return pl.pallas_call(
        functools.partial(_even_sample_kernel, lam_init=lam_init),
        grid=(dec_batch,),
        in_specs=[pl.BlockSpec(memory_space=pltpu.SMEM), full(lamv), full(subln), full(wk), full(wv), full(vone),
                  full(e_mat), new(1024), new(1024), new(1024), cache(KV_RANK), cache(ROPE),
                  new(512), new(512), new(512), cache(H_B, V_B), cache(H_B, V_B)],
        out_specs=[pl.BlockSpec((dec_seq, 512), lambda s: (s, 0))] * 2,
        out_shape=[jax.ShapeDtypeStruct((dec_batch * dec_seq, 512), BF)] * 2,
        compiler_params=_cparams("parallel"),
        name="even_sample",
    )(slopes, lamv, subln, wk, wv, vone, e_mat, qa, ka, va, ckv_c, kr_c, qd, kdb, vdb, ck_c, cv_c)


def _route(logits, carry, live):
    tm = logits.shape[0]
    lane = _lane_iota(logits.shape).astype(F32)
    big = float(LANE)
    g_mask = lane < N_GROUPS
    gl = jnp.where(g_mask, logits, NEG_INF)
    gmax = jnp.max(gl, axis=-1, keepdims=True)
    g_sel = jnp.min(jnp.where(gl == gmax, lane, big), axis=-1, keepdims=True)
    p_grp = 1.0 / jnp.sum(jnp.exp(gl - gmax), axis=-1, keepdims=True)
    lo = ROUTE_OFF + EPG * g_sel
    el = jnp.where((lane >= lo) & (lane < lo + EPG), logits, NEG_INF)
    v1 = jnp.max(el, axis=-1, keepdims=True)
    i1 = jnp.min(jnp.where(el == v1, lane, big), axis=-1, keepdims=True)
    el2 = jnp.where(lane == i1, NEG_INF, el)
    v2 = jnp.max(el2, axis=-1, keepdims=True)
    i2 = jnp.min(jnp.where(el2 == v2, lane, big), axis=-1, keepdims=True)
    ex = jnp.exp(v2 - v1)
    den = 1.0 + ex
    gate1 = (1.0 / den) * p_grp
    gate2 = (ex / den) * p_grp
    onehot = jnp.where((lane == i1) | (lane == i2), 1.0, 0.0)
    row = lax.broadcasted_iota(jnp.int32, (tm, tm), 0)
    col = lax.broadcasted_iota(jnp.int32, (tm, tm), 1)
    tri = jnp.where(row > col, 1.0, 0.0).astype(BF)
    cum = _dot(tri, onehot.astype(BF)) + carry
    r1 = jnp.sum(jnp.where(lane == i1, cum, 0.0), axis=-1, keepdims=True)
    r2 = jnp.sum(jnp.where(lane == i2, cum, 0.0), axis=-1, keepdims=True)
    packed = jnp.zeros_like(logits)
    for pos, val in enumerate([i1 - ROUTE_OFF, i2 - ROUTE_OFF, gate1, gate2, r1, r2]):
        packed = jnp.where(lane == pos, val, packed)
    return packed, carry + live * jnp.sum(onehot, axis=0, keepdims=True)


def _out_router_kernel(*refs, splits, n_prompt_tiles):
    i = pl.program_id(0)
    offs = [sum(splits[:k]) for k in range(len(splits))]
    n_mix = len(splits) - 1

    def pick(k):
        parts = refs[offs[k]:offs[k] + splits[k]]
        if splits[k] == 1:
            return parts[0][...]
        return jnp.where(i < n_prompt_tiles, parts[0][...], parts[1][...])

    rest = refs[sum(splits):]
    w_refs = rest[:n_mix]
    g_ref, wr_ref, br_ref, x1_ref, xn_ref, route_ref, cnt_ref, carry_ref, logits_ref = rest[n_mix:]

    @pl.when(i == 0)
    def _():
        carry_ref[...] = jnp.zeros_like(carry_ref)
        logits_ref[...] = jnp.zeros_like(logits_ref)

    prev = logits_ref[...]
    x1 = pick(0)
    for k, w_ref in enumerate(w_refs):
        x1 = x1 + _dot(pick(1 + k), w_ref[...])
    x1_ref[...] = x1
    xb = _rms(x1, g_ref[...], NORM_EPS).astype(BF)
    xn_ref[...] = xb
    logits_ref[...] = _dot(xb, wr_ref[...]) + br_ref[...]
    live = jnp.where(i > 0, 1.0, 0.0)
    carry = carry_ref[...]
    for r0 in range(0, prev.shape[0], ROUTE_ROWS):
        packed, carry = _route(prev[r0:r0 + ROUTE_ROWS], carry, live)
        route_ref[r0:r0 + ROUTE_ROWS, :] = packed
    carry_ref[...] = carry
    cnt_ref[...] = carry


def _out_router(row_inputs, ws, g, wr, br):
    splits = tuple(len(parts) for parts in row_inputs)
    T = sum(a.shape[0] for a in row_inputs[0])
    D = row_inputs[0][0].shape[1]
    tile = WIDE_TILE
    n = T // tile
    n_p = max([parts[0].shape[0] // tile for parts in row_inputs if len(parts) == 2], default=0)

    def row_specs(parts):
        if len(parts) == 1:
            return [pl.BlockSpec((tile, parts[0].shape[1]), lambda i: (jnp.minimum(i, n - 1), 0))]
        n_s = parts[1].shape[0] // tile
        return [pl.BlockSpec((tile, parts[0].shape[1]), lambda i: (jnp.minimum(i, n_p - 1), 0)),
                pl.BlockSpec((tile, parts[1].shape[1]), lambda i: (jnp.clip(i - n_p, 0, n_s - 1), 0))]

    def full(a):
        return pl.BlockSpec(a.shape, lambda i: (0,) * a.ndim)

    def rows(width, lag=0):
        return pl.BlockSpec((tile, width), lambda i: (jnp.clip(i - lag, 0, n - 1), 0))

    flat_rows = [a for parts in row_inputs for a in parts]
    return pl.pallas_call(
        functools.partial(_out_router_kernel, splits=splits, n_prompt_tiles=n_p),
        grid=(n + 1,),
        in_specs=[s for parts in row_inputs for s in row_specs(parts)] + [full(w) for w in ws]
        + [full(g), full(wr), full(br)],
        out_specs=[rows(D), rows(D), rows(LANE, lag=1), pl.BlockSpec((1, LANE), lambda i: (0, 0))],
        out_shape=[jax.ShapeDtypeStruct((T, D), F32), jax.ShapeDtypeStruct((T, D), BF),
                   jax.ShapeDtypeStruct((T, LANE), F32), jax.ShapeDtypeStruct((1, LANE), F32)],
        scratch_shapes=[pltpu.VMEM((1, LANE), F32), pltpu.VMEM((tile, LANE), F32)],
        compiler_params=_cparams("arbitrary"),
        name="out_router",
    )(*flat_rows, *ws, g, wr, br)


def _experts_kernel(be_ref, nu_ref, xb_ref, wg_ref, wu_ref, wd_ref, y_ref, wgb_ref, wub_ref, wdb_ref):
    i = pl.program_id(0)
    used = i < nu_ref[0]

    @pl.when(used & ((i == 0) | (be_ref[i] != be_ref[jnp.maximum(i - 1, 0)])))
    def _():
        wgb_ref[...] = wg_ref[...].astype(BF)
        wub_ref[...] = wu_ref[...].astype(BF)
        wdb_ref[...] = wd_ref[...].astype(BF)

    @pl.when(used)
    def _():
        xb = xb_ref[...]
        a = _dot(xb, wgb_ref[...])
        b = _dot(xb, wub_ref[...])
        hid = (a * jax.nn.sigmoid(a)) * b
        y_ref[...] = _dot(hid.astype(BF), wdb_ref[...]).astype(y_ref.dtype)

    @pl.when(jnp.logical_not(used))
    def _():
        y_ref[...] = jnp.zeros_like(y_ref)


def _experts(block_expert, n_used, xb, wg, wu, wd, *, layer):
    L, D = xb.shape
    n_blocks = L // MOE_BLOCK
    grid_spec = pltpu.PrefetchScalarGridSpec(
        num_scalar_prefetch=2,
        grid=(n_blocks,),
        in_specs=[pl.BlockSpec((MOE_BLOCK, D), lambda i, be, nu: (i, 0)),
                  pl.BlockSpec((None, None, D, D_EXPERT), lambda i, be, nu: (layer, be[i], 0, 0)),
                  pl.BlockSpec((None, None, D, D_EXPERT), lambda i, be, nu: (layer, be[i], 0, 0)),
                  pl.BlockSpec((None, None, D_EXPERT, D), lambda i, be, nu: (layer, be[i], 0, 0))],
        out_specs=pl.BlockSpec((MOE_BLOCK, D), lambda i, be, nu: (i, 0)),
        scratch_shapes=[pltpu.VMEM((D, D_EXPERT), BF), pltpu.VMEM((D, D_EXPERT), BF), pltpu.VMEM((D_EXPERT, D), BF)],
    )
    return pl.pallas_call(
        _experts_kernel,
        grid_spec=grid_spec,
        out_shape=jax.ShapeDtypeStruct((L, D), BF),
        compiler_params=_cparams("arbitrary"),
        name="experts",
    )(block_expert, n_used, xb, wg, wu, wd)


def _moe(xn, route, cnt, wg, wu, wd, *, layer):
    T = xn.shape[0]
    counts = cnt[0, ROUTE_OFF:ROUTE_OFF + N_EXPERTS].astype(jnp.int32)
    padded = ((counts + MOE_BLOCK - 1) // MOE_BLOCK) * MOE_BLOCK
    pend = jnp.cumsum(padded)
    pstart = pend - padded

    def slot_rows(k):
        e, rank = route[:, k].astype(jnp.int32), route[:, 4 + k].astype(jnp.int32)
        return jnp.sum(jnp.where(e[:, None] == jnp.arange(N_EXPERTS)[None, :], pstart[None, :], 0), axis=-1) + rank

    dest = [slot_rows(0), slot_rows(1)]
    n_blocks = -(-(2 * T) // MOE_BLOCK) + N_EXPERTS
    L = n_blocks * MOE_BLOCK
    tok = jnp.arange(T, dtype=jnp.int32)
    buf_tok = (jnp.arange(L, dtype=jnp.int32) % T).at[jnp.concatenate(dest)].set(
        jnp.concatenate([tok, tok]), unique_indices=True, mode="promise_in_bounds")
    block_start = jnp.arange(n_blocks, dtype=jnp.int32) * MOE_BLOCK
    block_expert = jnp.minimum(jnp.sum(pend[None, :] <= block_start[:, None], axis=1), N_EXPERTS - 1).astype(jnp.int32)
    n_used = (pend[-1:] // MOE_BLOCK).astype(jnp.int32)
    xb = xn.at[buf_tok].get(mode="promise_in_bounds")
    yb = _experts(block_expert, n_used, xb, wg, wu, wd, layer=layer)
    return (yb.at[dest[0]].get(mode="promise_in_bounds"), yb.at[dest[1]].get(mode="promise_in_bounds"))


def _gates(route):
    lane = _lane_iota(route.shape)
    g0 = jnp.sum(jnp.where(lane == 2, route, 0.0), axis=-1, keepdims=True)
    g1 = jnp.sum(jnp.where(lane == 3, route, 0.0), axis=-1, keepdims=True)
    return g0, g1


def _odd_in_kernel(x_ref, y0_ref, y1_ref, route_ref, g_ref, w_ref, x2_ref, q_ref, k_ref, v_ref, *, scale):
    g0, g1 = _gates(route_ref[...])
    x2 = x_ref[...] + (y0_ref[...].astype(F32) * g0 + y1_ref[...].astype(F32) * g1)
    x2_ref[...] = x2
    h = _rms(x2, g_ref[...], NORM_EPS).astype(BF)
    y = _dot(h, w_ref[...])
    n = q_ref.shape[1]
    q_ref[...] = (y[:, :n] * scale).astype(BF)
    k_ref[...] = y[:, n:2 * n].astype(BF)
    v_ref[...] = y[:, 2 * n:].astype(BF)


def _odd_in(x1, y0, y1, route, g, w):
    T, D = x1.shape
    n = w.shape[1] // 3

    def rows(width):
        return pl.BlockSpec((WIDE_TILE, width), lambda i: (i, 0))

    def full(a):
        return pl.BlockSpec(a.shape, lambda i: (0,) * a.ndim)

    return pl.pallas_call(
        functools.partial(_odd_in_kernel, scale=DH_C ** -0.5 * LOG2E),
        grid=(T // WIDE_TILE,),
        in_specs=[rows(D), rows(D), rows(D), rows(LANE), full(g), full(w)],
        out_specs=[rows(D), rows(n), rows(n), rows(n)],
        out_shape=[jax.ShapeDtypeStruct((T, D), F32)] + [jax.ShapeDtypeStruct((T, n), BF)] * 3,
        compiler_params=_cparams("parallel"),
        name="odd_in",
    )(x1, y0, y1, route, g, w)


def _state_rows_kernel(ids_ref, x_ref, g_ref, w_ref, o_ref):
    del ids_ref
    h = _rms(x_ref[...], g_ref[...], NORM_EPS).astype(BF)
    o_ref[...] = _dot(h, w_ref[...])


def _state_rows(tile_ids, x, g, w):
    D = x.shape[1]
    n = tile_ids.shape[0]
    grid_spec = pltpu.PrefetchScalarGridSpec(
        num_scalar_prefetch=1,
        grid=(n,),
        in_specs=[pl.BlockSpec((ROW_TILE, D), lambda i, ids: (ids[i], 0)),
                  pl.BlockSpec(g.shape, lambda i, ids: (0, 0)),
                  pl.BlockSpec(w.shape, lambda i, ids: (0, 0))],
        out_specs=pl.BlockSpec((ROW_TILE, w.shape[1]), lambda i, ids: (i, 0)),
    )
    return pl.pallas_call(
        _state_rows_kernel,
        grid_spec=grid_spec,
        out_shape=jax.ShapeDtypeStruct((n * ROW_TILE, w.shape[1]), F32),
        compiler_params=_cparams("parallel"),
        name="state_rows",
    )(tile_ids, x, g, w)


def _band_prompt_kernel(q_ref, k_ref, v_ref, bias_ref, o_ref, q2_ref, s0, s1, p_ref, l_ref,
                        *, tile, tiles_per_step):
    lane = _lane_iota((tile, LANE))
    q2_ref[...] = _split_halves(q_ref[...])
    n_rows = tile * tiles_per_step

    def key_tiles(t):
        qi = pl.program_id(2) * tiles_per_step + t
        out = []
        for kt in range(3):
            start = (qi + kt - 2) * tile
            neg = jnp.where(start >= 0, 0.0, NEG_INF)
            out.append((pl.multiple_of(jnp.maximum(start, 0), tile), neg))
        return out

    def scores(t, s_buf):
        for hh in range(2):
            q = q2_ref[hh * n_rows + t * tile:hh * n_rows + (t + 1) * tile, :]
            for kt, (start, neg) in enumerate(key_tiles(t)):
                s = _dot_nt(q, k_ref[pl.ds(start, tile), :])
                s_buf[hh, :, kt * tile:(kt + 1) * tile] = s + neg if kt < 2 else s

    def softmax_pv(t, s_buf):
        outs = []
        slot = t % 2
        for hh in range(2):
            for rb in range(tile // SOFTMAX_ROWS):
                rs = slice(rb * SOFTMAX_ROWS, (rb + 1) * SOFTMAX_ROWS)
                s = s_buf[hh, rs, :] + bias_ref[hh * tile + rb * SOFTMAX_ROWS:hh * tile + (rb + 1) * SOFTMAX_ROWS, :]
                p = jnp.exp2(s - jnp.max(s, axis=-1, keepdims=True))
                l_ref[slot, hh, rs, :] = jnp.broadcast_to(jnp.sum(p, axis=-1, keepdims=True), (SOFTMAX_ROWS, LANE))
                p_ref[slot, hh, rs, :] = p.astype(BF)
            acc = None
            for kt, (start, _) in enumerate(key_tiles(t)):
                pv = _dot(p_ref[slot, hh, :, kt * tile:(kt + 1) * tile], v_ref[pl.ds(start, tile), :])
                acc = pv if acc is None else acc + pv
            outs.append(acc / l_ref[slot, hh])
        o_ref[t * tile:(t + 1) * tile, :] = jnp.where(lane < DH_C, outs[0], outs[1]).astype(BF)

    bufs = [s0, s1]
    scores(0, bufs[0])
    for t in range(tiles_per_step):
        if t + 1 < tiles_per_step:
            scores(t + 1, bufs[(t + 1) % 2])
        softmax_pv(t, bufs[t % 2])


def _band_prompt(q, k, v, bias, *, batch, seq):
    tile = BAND_TILE
    tps = min(BAND_TILES_PER_STEP, seq // tile)
    nq = seq // (tile * tps)
    return pl.pallas_call(
        functools.partial(_band_prompt_kernel, tile=tile, tiles_per_step=tps),
        grid=(H_C // 2, batch, nq),
        in_specs=[pl.BlockSpec((tile * tps, LANE), lambda p, b, qi: (b * nq + qi, p)),
                  pl.BlockSpec((seq, LANE), lambda p, b, qi: (b, p)),
                  pl.BlockSpec((seq, LANE), lambda p, b, qi: (b, p)),
                  pl.BlockSpec((None, 2 * tile, 3 * tile), lambda p, b, qi: (p, 0, 0))],
        out_specs=pl.BlockSpec((tile * tps, LANE), lambda p, b, qi: (b * nq + qi, p)),
        out_shape=jax.ShapeDtypeStruct((batch * seq, H_C * DH_C), BF),
        scratch_shapes=[pltpu.VMEM((2 * tile * tps, LANE), BF),
                        pltpu.VMEM((2, tile, 3 * tile), F32), pltpu.VMEM((2, tile, 3 * tile), F32),
                        pltpu.VMEM((2, 2, tile, 3 * tile), BF), pltpu.VMEM((2, 2, tile, LANE), F32)],
        compiler_params=_cparams("parallel", "parallel", "arbitrary"),
        name="band_prompt",
    )(q, k, v, bias)


def _band_sample_kernel(q_ref, k_ref, v_ref, ck_ref, cv_ref, bias_ref, o_ref):
    n_new = q_ref.shape[0]
    n_past = ck_ref.shape[0]
    lane = _lane_iota((n_new, LANE))
    for pr in range(H_C // 2):
        sl = slice(pr * LANE, (pr + 1) * LANE)
        q2x = _split_halves(q_ref[:, sl])
        s_p = _dot_nt(q2x, ck_ref[:, sl].astype(BF)) + bias_ref[pr, :, 0:n_past]
        s_n = _dot_nt(q2x, k_ref[:, sl]) + bias_ref[pr, :, n_past:n_past + n_new]
        o = _softmax_pv([s_p, s_n], [cv_ref[:, sl].astype(BF), v_ref[:, sl]])
        o_ref[:, sl] = jnp.where(lane < DH_C, o[:n_new], o[n_new:]).astype(BF)


def _band_sample(q, k, v, ck, cv, bias, *, n_prompt_rows, dec_seq):
    dec_batch, n_past, width = ck.shape
    base = n_prompt_rows // dec_seq
    new = pl.BlockSpec((dec_seq, width), lambda s: (base + s, 0))
    cache = pl.BlockSpec((None, n_past, width), lambda s: (s, 0, 0))
    return pl.pallas_call(
        _band_sample_kernel,
        grid=(dec_batch,),
        in_specs=[new, new, new, cache, cache, pl.BlockSpec(bias.shape, lambda s: (0, 0, 0))],
        out_specs=pl.BlockSpec((dec_seq, width), lambda s: (s, 0)),
        out_shape=jax.ShapeDtypeStruct((dec_batch * dec_seq, width), BF),
        compiler_params=_cparams("parallel"),
        name="band_sample",
    )(q, k, v, ck, cv, bias)


def _final_kernel(x_ref, y0_ref, y1_ref, route_ref, g_ref, o_ref):
    g0, g1 = _gates(route_ref[...])
    x = x_ref[...] + (y0_ref[...].astype(F32) * g0 + y1_ref[...].astype(F32) * g1)
    o_ref[...] = _rms(x, g_ref[...], NORM_EPS)


def _final(x, y0, y1, route, g, *, first_tile, n_tiles):
    D = x.shape[1]

    def rows(width):
        return pl.BlockSpec((WIDE_TILE, width), lambda i: (first_tile + i, 0))

    return pl.pallas_call(
        _final_kernel,
        grid=(n_tiles,),
        in_specs=[rows(D), rows(D), rows(D), rows(LANE), pl.BlockSpec(g.shape, lambda i: (0, 0))],
        out_specs=pl.BlockSpec((WIDE_TILE, D), lambda i: (i, 0)),
        out_shape=jax.ShapeDtypeStruct((n_tiles * WIDE_TILE, D), F32),
        compiler_params=_cparams("parallel"),
        name="final_norm",
    )(x, y0, y1, route, g)


def _prep_even_weights(w_in, w_qup, w_kvup):
    D = w_in.shape[0]
    a_in = Q_RANK + KV_RANK + ROPE
    bq = H_B * 2 * DH_B
    wcq, wckv, wkr = w_in[:, :Q_RANK], w_in[:, Q_RANK:Q_RANK + KV_RANK], w_in[:, Q_RANK + KV_RANK:a_in]
    wqd, wkd, wvd = w_in[:, a_in:a_in + bq], w_in[:, a_in + bq:a_in + 2 * bq], w_in[:, a_in + 2 * bq:]
    half = ROPE // 2

    def z(n):
        return jnp.zeros((D, n), w_in.dtype)

    kr_m = jnp.concatenate([z(NOPE), wkr, z(HEAD_PAD - NOPE - ROPE)], axis=1)
    kr_s = jnp.concatenate([z(NOPE), wkr[:, half:], wkr[:, :half], z(HEAD_PAD - NOPE - ROPE)], axis=1)
    w0 = jnp.concatenate([wcq, wckv, kr_m, kr_s, wqd, wkd, wvd], axis=1).astype(BF)
    wq3 = w_qup.reshape(Q_RANK, H_A, NOPE + ROPE)
    nope, r1, r2 = wq3[:, :, :NOPE], wq3[:, :, NOPE:NOPE + half], wq3[:, :, NOPE + half:]
    zq = jnp.zeros((Q_RANK, H_A, HEAD_PAD - NOPE - ROPE), w_qup.dtype)
    wq = jnp.concatenate([nope, r1, r2, zq], axis=-1).reshape(Q_RANK, H_A * HEAD_PAD).astype(BF)
    wqs = jnp.concatenate([jnp.zeros_like(nope), r2, r1, zq], axis=-1).reshape(Q_RANK, H_A * HEAD_PAD).astype(BF)
    wkv3 = w_kvup.reshape(KV_RANK, H_A, NOPE + V_A)
    wk = jnp.concatenate([wkv3[:, :, :NOPE], jnp.zeros((KV_RANK, H_A, HEAD_PAD - NOPE), w_kvup.dtype)],
                         axis=-1).reshape(KV_RANK, H_A * HEAD_PAD).astype(BF)
    wv4 = wkv3[:, :, NOPE:].reshape(KV_RANK, H_A // 2, 2, V_A)
    zv = jnp.zeros((KV_RANK, H_A // 2, HEAD_PAD - V_A), w_kvup.dtype)
    wv = jnp.concatenate([wv4[:, :, 0], zv, zv, wv4[:, :, 1]], axis=-1).reshape(KV_RANK, H_A * HEAD_PAD).astype(BF)
    vone = jnp.zeros((H_A // 2, 2 * HEAD_PAD), F32).at[:, V_A].set(1.0).at[:, HEAD_PAD].set(1.0)
    r = jnp.arange(ROPE)
    e_mat = jnp.zeros((ROPE, H_A, HEAD_PAD), F32).at[r[:, None], jnp.arange(H_A)[None, :], NOPE + r[:, None]].set(1.0)
    return w0, wq, wqs, wk, wv, vone.reshape(1, H_A * HEAD_PAD), e_mat.reshape(ROPE, H_A * HEAD_PAD).astype(BF)


def _rope_tables(pos):
    half = ROPE // 2
    inv = jnp.power(ROPE_BASE, -jnp.arange(half, dtype=F32) / half)
    ang = pos.astype(F32)[:, None] * inv[None, :]
    c, s = jnp.cos(ang), jnp.sin(ang)
    n = pos.shape[0]
    pad = jnp.zeros((n, HEAD_PAD - NOPE - ROPE), F32)
    cs = jnp.concatenate([jnp.ones((n, NOPE), F32), c, c, pad], axis=1)
    sn = jnp.concatenate([jnp.zeros((n, NOPE), F32), -s, s, pad], axis=1)
    return cs, sn


def _band_bias_tiles(table, n_rows, n_keys, key_offset, masked):
    i = jnp.arange(n_rows)[:, None]
    kpos = jnp.arange(n_keys)[None, :] - key_offset
    d_max = n_rows - 1 + key_offset
    w = n_rows + n_keys
    d = d_max - jnp.arange(w)
    rev = (table.astype(F32) * LOG2E)[:, jnp.clip(d, -REL_CLIP, REL_CLIP) + REL_CLIP]
    skew = jnp.tile(rev, (1, n_rows))[:, :n_rows * (w - 1)].reshape(-1, n_rows, w - 1)
    bias = skew[:, :, n_rows - 1:n_rows - 1 + n_keys]
    if masked:
        qc, kc = i // CHUNK, jnp.floor_divide(kpos, CHUNK)
        vis = (kc <= qc) & (kc >= qc - LEFT_CHUNKS)
        bias = jnp.where(vis[None], bias, NEG_INF)
    return bias.reshape(H_C // 2, 2 * n_rows, n_keys)


def kernel(x_prompt, x_sample, cache_mla_ckv, cache_mla_krope, cache_diff_k, cache_diff_v, cache_band_k, cache_band_v, ln_mix, w_in_even, mla_q_norm, mla_w_qup, mla_kv_norm, mla_w_kvup, diff_lam_q1, diff_lam_k1, diff_lam_q2, diff_lam_k2, diff_subln, w_out_even, w_in_odd, band_rel_bias, w_out_odd, ln_ffn, moe_w_group, moe_b_group, moe_w_router, moe_b_router, moe_w_gate, moe_w_up, moe_w_down, ln_final):
    B, S, D = x_prompt.shape
    DB, DS, _ = x_sample.shape
    n_past = cache_mla_ckv.shape[2]
    c_past = cache_band_k.shape[2]
    assert ln_mix.shape[0] == 2 and S % (2 * min(ATT_TILE, S // 2)) == 0 and S % BAND_TILE == 0
    assert (B * S) % WIDE_TILE == 0 and (DB * DS) % WIDE_TILE == 0 and ROW_TILE % DS == 0 and DS == CHUNK
    assert n_past % CHUNK == 0 and c_past == LEFT_CHUNKS * CHUNK and S >= c_past and c_past % ROW_TILE == 0
    BS, NS = B * S, DB * DS
    T = BS + NS
    n_p, n_s = BS // ROW_TILE, NS // ROW_TILE
    xp, xs = x_prompt.reshape(BS, D), x_sample.reshape(NS, D)
    row = lambda a: a.reshape(1, -1)

    w0, wq, wqs, wk, wv, vone, e_mat = _prep_even_weights(w_in_even[0], mla_w_qup[0], mla_w_kvup[0])
    cs_p, sn_p = _rope_tables(jnp.arange(S))
    cs_s, sn_s = _rope_tables(n_past + jnp.arange(DS))
    reps = ROW_TILE // DS
    even_w = (row(ln_mix[0]), w0, row(mla_q_norm[0]), wq, wqs, row(mla_kv_norm[0]), wk, wv, vone)
    jj = jnp.arange(S) % min(ATT_TILE, S // 2)
    digits = jnp.stack([16 * (jj // 16), jj % 16] * 3, axis=1).astype(F32)
    kpos_p = jnp.zeros((S, LANE), F32).at[:, DH_B:DH_B + 6].set(digits)
    qa, ka, va, ckv_p, kr_p, qd, _, vdb, kd_p, vd_p, kdm = _even_in(xp, *even_w, cs_p, sn_p, kpos_p)
    qa_s, ka_s, va_s, ckv_s, kr_s, qd_s, kdb_s, vdb_s, kd_s, vd_s, _ = _even_in(
        xs, *even_w, jnp.tile(cs_s, (reps, 1)), jnp.tile(sn_s, (reps, 1)), jnp.zeros((ROW_TILE, LANE), F32))
    slopes = jnp.exp2(-8.0 * jnp.arange(1, H_B + 1, dtype=F32) / H_B) * LOG2E
    lamv = jnp.stack([diff_lam_q1[0], diff_lam_k1[0], diff_lam_q2[0], diff_lam_k2[0]]).astype(F32)
    subln = row(diff_subln[0])
    lam_init = 0.8 - 0.6 * math.exp(-0.3 * 0)
    oa_p = _mla_prompt(qa, ka, va, batch=B, seq=S)
    ob_p = _diff_prompt(slopes, lamv, subln, qd, kdm, vdb, batch=B, seq=S, lam_init=lam_init)
    oa_s, ob_s = _even_sample(
        slopes, lamv, subln, wk, wv, vone, e_mat, qa_s, ka_s, va_s, cache_mla_ckv[0], cache_mla_krope[0],
        qd_s, kdb_s, vdb_s, cache_diff_k[0], cache_diff_v[0],
        n_prompt_rows=0, dec_seq=DS, lam_init=lam_init)

    def router_weights(l):
        wr = jnp.concatenate([moe_w_group[l], moe_w_router[l],
                              jnp.zeros((D, LANE - N_GROUPS - N_EXPERTS), F32)], axis=1).astype(BF)
        br = jnp.concatenate([moe_b_group[l], moe_b_router[l],
                              jnp.zeros((LANE - N_GROUPS - N_EXPERTS,), F32)]).astype(F32)
        return wr, row(br)

    n_a = H_A * V_A
    wo = w_out_even[0].astype(BF)
    wr, br = router_weights(0)
    x1, xn, route, cnt = _out_router([(xp, xs), (oa_p, oa_s), (ob_p, ob_s)], [wo[:n_a], wo[n_a:]],
                                     row(ln_ffn[0]), wr, br)
    y0, y1 = _moe(xn, route, cnt, moe_w_gate, moe_w_up, moe_w_down, layer=0)

    w_odd = w_in_odd[0].astype(BF)
    n_c = H_C * DH_C
    x2, qc, kc, vc = _odd_in(x1, y0, y1, route, row(ln_mix[1]), w_odd)
    tail = c_past // ROW_TILE
    tiles_per_seq = S // ROW_TILE
    tile_ids = jnp.concatenate(
        [jnp.arange(tiles_per_seq - tail, tiles_per_seq, dtype=jnp.int32) + b * tiles_per_seq for b in range(B)]
        + [jnp.arange(n_p, n_p + n_s, dtype=jnp.int32)])
    st = _state_rows(tile_ids, x2, row(ln_mix[1]), w_odd[:, n_c:])
    bias_p = _band_bias_tiles(band_rel_bias[0], BAND_TILE, 3 * BAND_TILE, 2 * BAND_TILE, True)
    bias_s = _band_bias_tiles(band_rel_bias[0], DS, c_past + DS, c_past, False)
    oc_p = _band_prompt(qc, kc, vc, bias_p, batch=B, seq=S)
    oc_s = _band_sample(qc, kc, vc, cache_band_k[0].reshape(DB, c_past, n_c), cache_band_v[0].reshape(DB, c_past, n_c),
                        bias_s, n_prompt_rows=BS, dec_seq=DS)
    wr, br = router_weights(1)
    x3, xn, route, cnt = _out_router([(x2,), (oc_p, oc_s)], [w_out_odd[0].astype(BF)], row(ln_ffn[1]), wr, br)
    y0, y1 = _moe(xn, route, cnt, moe_w_gate, moe_w_up, moe_w_down, layer=1)
    g_fin = row(ln_final)
    y_prompt = _final(x3, y0, y1, route, g_fin, first_tile=0, n_tiles=BS // WIDE_TILE).reshape(B, S, D)
    y_sample = _final(x3, y0, y1, route, g_fin, first_tile=BS // WIDE_TILE, n_tiles=NS // WIDE_TILE).reshape(DB, DS, D)

    def shaped(a_p, a_s, *tail_shape):
        return a_p.reshape(1, B, S, *tail_shape), a_s.reshape(1, DB, DS, *tail_shape)

    ckv_p, ckv_s = shaped(ckv_p, ckv_s, KV_RANK)
    kr_p, kr_s = shaped(kr_p, kr_s, ROPE)
    kd_p, kd_s = shaped(kd_p, kd_s, H_B, 2 * DH_B)
    vd_p, vd_s = shaped(vd_p, vd_s, H_B, V_B)
    n_tail = B * c_past
    bk_p = st[:n_tail, :n_c].reshape(1, B, c_past, H_C, DH_C)
    bv_p = st[:n_tail, n_c:].reshape(1, B, c_past, H_C, DH_C)
    k_new = st[n_tail:, :n_c].reshape(DB, DS, H_C, DH_C)
    v_new = st[n_tail:, n_c:].reshape(DB, DS, H_C, DH_C)
    bk_s = jnp.concatenate([cache_band_k[0][:, DS:], k_new], axis=1)[None]
    bv_s = jnp.concatenate([cache_band_v[0][:, DS:], v_new], axis=1)[None]
    return (y_prompt, y_sample, ckv_p, kr_p, kd_p, vd_p, bk_p, bv_p, ckv_s, kr_s, kd_s, vd_s, bk_s, bv_s)
```

```python
import functools
import math

import jax
import jax.numpy as jnp
from jax import lax
from jax.experimental import pallas as pl
from jax.experimental.pallas import tpu as pltpu

BF = jnp.bfloat16
F32 = jnp.float32
NEG_INF = float("-inf")
LOG2E = math.log2(math.e)

CHUNK = 64
NORM_EPS = 1e-6
SUBLN_EPS = 1e-5
H_A, NOPE, ROPE, V_A, Q_RANK, KV_RANK = 8, 64, 32, 64, 384, 256
ROPE_BASE = 10000.0
H_B, DH_B, V_B = 4, 64, 128
H_C, DH_C, LEFT_CHUNKS, REL_CLIP = 16, 64, 8, 128
N_GROUPS, EPG, N_EXPERTS, D_EXPERT = 4, 8, 32, 512
LANE = 128
HEAD_PAD = 128
ROUTE_OFF = N_GROUPS

ROW_TILE = 256
WIDE_TILE = 512
ROUTE_ROWS = 128
ATT_TILE = 512
SOFTMAX_ROWS = 32
BAND_TILE = 256
BAND_TILES_PER_STEP = 8
MOE_BLOCK = 512
VMEM_LIMIT = 56 * 1024 * 1024


def _cparams(*sem):
    return pltpu.CompilerParams(dimension_semantics=sem, vmem_limit_bytes=VMEM_LIMIT)


def _rms(x, g, eps):
    return x * lax.rsqrt(jnp.mean(x * x, axis=-1, keepdims=True) + eps) * g


def _dot(a, b):
    return jnp.dot(a, b, preferred_element_type=F32)


def _dot_nt(a, b):
    return lax.dot_general(a, b, (((1,), (1,)), ((), ())), preferred_element_type=F32)


def _lane_iota(shape):
    return lax.broadcasted_iota(jnp.int32, shape, len(shape) - 1)


def _split_halves(q):
    qf = q.astype(F32)
    lane = _lane_iota(qf.shape)
    return jnp.concatenate([jnp.where(lane < 64, qf, 0.0), jnp.where(lane >= 64, qf, 0.0)], axis=0).astype(BF)


def _softmax_pv(s_list, v_list):
    m = functools.reduce(jnp.maximum, [jnp.max(s, axis=-1, keepdims=True) for s in s_list])
    acc, l = None, None
    for s, v in zip(s_list, v_list):
        p = jnp.exp2(s - m)
        ls = jnp.sum(p, axis=-1, keepdims=True)
        a = _dot(p.astype(BF), v)
        l = ls if l is None else l + ls
        acc = a if acc is None else acc + a
    return acc / l


def _diff_lambda(lamv, lam_init):
    a = jnp.exp(jnp.sum(lamv[0:1] * lamv[1:2], axis=-1, keepdims=True))
    b = jnp.exp(jnp.sum(lamv[2:3] * lamv[3:4], axis=-1, keepdims=True))
    return a - b + lam_init


def _diff_finish(o1, o2, lam, subln, lam_init):
    o = o1 - lam * o2
    return _rms(o, subln, SUBLN_EPS) * (1.0 - lam_init)


def _even_in_kernel(x_ref, g_ref, w0_ref, qn_ref, wq_ref, wqs_ref, kvn_ref, wk_ref, wv_ref, vone_ref,
                    cs_ref, sn_ref,
                    qa_ref, ka_ref, va_ref, ckv_ref, kr_ref, qd_ref, kdb_ref, vdb_ref, kd_ref, vd_ref,
                    *, a_scale, b_scale):
    h = _rms(x_ref[...], g_ref[...], NORM_EPS).astype(BF)
    y = _dot(h, w0_ref[...])
    cq, ckv = y[:, 0:384], y[:, 384:640]
    kr_m, kr_s = y[:, 640:768], y[:, 768:896]
    qd, kd, vd = y[:, 896:1408], y[:, 1408:1920], y[:, 1920:2432]
    cs, sn = cs_ref[...], sn_ref[...]
    cqn = _rms(cq, qn_ref[...], NORM_EPS).astype(BF)
    qm = _dot(cqn, wq_ref[...])
    qs = _dot(cqn, wqs_ref[...])
    ckvn = _rms(ckv, kvn_ref[...], NORM_EPS)
    ckv_ref[...] = ckvn
    krp = kr_m * cs + kr_s * sn
    kr_ref[...] = krp[:, NOPE:NOPE + ROPE]
    cb = ckvn.astype(BF)
    kn = _dot(cb, wk_ref[...])
    for hh in range(H_A):
        sl = slice(hh * HEAD_PAD, (hh + 1) * HEAD_PAD)
        qa_ref[:, sl] = ((qm[:, sl] * cs + qs[:, sl] * sn) * a_scale).astype(BF)
        ka_ref[:, sl] = (kn[:, sl] + krp).astype(BF)
    va_ref[...] = (_dot(cb, wv_ref[...]) + vone_ref[...]).astype(BF)
    qd_ref[...] = (qd * b_scale).astype(BF)
    for hh in range(H_B):
        kd_ref[:, hh, :] = kd[:, hh * V_B:(hh + 1) * V_B]
        vd_ref[:, hh, :] = vd[:, hh * V_B:(hh + 1) * V_B]
    kdb_ref[...] = kd.astype(BF)
    vdb_ref[...] = vd.astype(BF)


def _even_in(x, g, w0, qn, wq, wqs, kvn, wk, wv, vone, cs_tab, sn_tab):
    T, D = x.shape
    pos_blocks = cs_tab.shape[0] // ROW_TILE

    def full(a):
        return pl.BlockSpec(a.shape, lambda i: (0,) * a.ndim)

    def rows(*tail):
        return pl.BlockSpec((ROW_TILE,) + tail, lambda i: (i,) + (0,) * len(tail))

    pos_spec = pl.BlockSpec((ROW_TILE, LANE), lambda i: (i % pos_blocks, 0))
    outs = [((1024,), BF), ((1024,), BF), ((1024,), BF), ((KV_RANK,), F32), ((ROPE,), F32),
            ((512,), BF), ((512,), BF), ((512,), BF), ((H_B, V_B), F32), ((H_B, V_B), F32)]
    return pl.pallas_call(
        functools.partial(_even_in_kernel,
                          a_scale=(NOPE + ROPE) ** -0.5 * LOG2E, b_scale=DH_B ** -0.5 * LOG2E),
        grid=(T // ROW_TILE,),
        in_specs=[rows(D), full(g), full(w0), full(qn), full(wq), full(wqs), full(kvn), full(wk), full(wv),
                  full(vone), pos_spec, pos_spec],
        out_specs=[rows(*tail) for tail, _ in outs],
        out_shape=[jax.ShapeDtypeStruct((T,) + tail, dt) for tail, dt in outs],
        compiler_params=_cparams("parallel"),
        name="even_in",
    )(x, g, w0, qn, wq, wqs, kvn, wk, wv, vone, cs_tab, sn_tab)


def _chunk_causal_mask(tq, tk):
    row = lax.broadcasted_iota(jnp.int32, (tq, tk), 0)
    col = lax.broadcasted_iota(jnp.int32, (tq, tk), 1)
    return (col // CHUNK) <= (row // CHUNK)


def _softmax_tile(s_ref, p_ref, m_ref, l_ref, a_ref, *, add_ref=None, off=None):
    tile = s_ref.shape[0]
    reps = tile // LANE
    blocks = [slice(rb * SOFTMAX_ROWS, (rb + 1) * SOFTMAX_ROWS) for rb in range(tile // SOFTMAX_ROWS)]

    def biased(rs):
        s = s_ref[rs, :]
        if add_ref is not None:
            s = s + add_ref[rs, :]
        return s

    for rs in blocks:
        m_old = m_ref[rs, :]
        red = jnp.broadcast_to(jnp.max(biased(rs), axis=-1, keepdims=True), m_old.shape)
        if off is not None:
            red = red + off
        m_new = jnp.maximum(m_old, red)
        a_ref[rs, :] = jnp.exp2(m_old - m_new)
        m_ref[rs, :] = m_new
    for rs in blocks:
        m_new = m_ref[rs, :]
        shift = m_new if off is None else m_new - off
        p = jnp.exp2(biased(rs) - jnp.concatenate([shift] * reps, axis=1))
        if l_ref is not None:
            l_ref[rs, :] = a_ref[rs, :] * l_ref[rs, :] + jnp.broadcast_to(
                jnp.sum(p, axis=-1, keepdims=True), m_new.shape)
        p_ref[rs, :] = p.astype(BF)


def _flash_causal(qi, bufs, scores, update):
    s0, s1, p0, p1, m_ref, l_ref, a_ref, acc_ref = bufs
    m_ref[...] = jnp.full(m_ref.shape, NEG_INF, F32)
    l_ref[...] = jnp.zeros(l_ref.shape, F32)
    acc_ref[...] = jnp.zeros(acc_ref.shape, F32)
    scores(0, s0)

    def pair(jj, carry):
        j = 2 * jj
        scores(j + 1, s1)
        update(j, s0, p0, False)
        scores(j + 2, s0)
        update(j + 1, s1, p1, False)
        return carry

    lax.fori_loop(0, qi // 2, pair, 0)

    @pl.when(qi % 2 == 1)
    def _():
        scores(qi, s1)
        update(qi - 1, s0, p0, False)
        s0[...] = s1[...]

    update(qi, s0, p0, True)


def _flash_causal_pair(t, bufs, scores, update):
    s0, s1, p0, p1, m_ref, l_ref, a_ref, acc_ref = bufs
    m_ref[...] = jnp.full(m_ref.shape, NEG_INF, F32)
    l_ref[...] = jnp.zeros(l_ref.shape, F32)
    acc_ref[...] = jnp.zeros(acc_ref.shape, F32)
    both = (0, 1)
    scores(0, s0, both)

    def pair(jj, carry):
        j = 2 * jj
        scores(j + 1, s1, both)
        update(j, s0, p0, both, ())
        scores(j + 2, s0, both)
        update(j + 1, s1, p1, both, ())
        return carry

    lax.fori_loop(0, t, pair, 0)
    scores(2 * t + 1, s1, (1,))
    update(2 * t, s0, p0, both, (0,))
    update(2 * t + 1, s1, p1, (1,), (1,))


def _flash_scratch(n_chains, rows, tile, acc_width=LANE):
    s = pltpu.VMEM((n_chains, rows, tile), F32)
    p = pltpu.VMEM((n_chains, rows, tile), BF)
    stat = pltpu.VMEM((n_chains, rows, LANE), F32)
    return [s, s, p, p, stat, stat, stat, pltpu.VMEM((n_chains, rows, acc_width), F32)]


def _rows_of(qts, tile):
    return slice(qts[0] * tile, (qts[-1] + 1) * tile)


def _mla_prompt_kernel(q_ref, k_ref, v_ref, o_ref, s0, s1, p0, p1, m_ref, l_ref, a_ref, acc_ref, dmask_ref,
                       *, tile):
    qi = pl.program_id(2)

    @pl.when(qi == 0)
    def _():
        dmask_ref[...] = jnp.where(_chunk_causal_mask(tile, tile), 0.0, NEG_INF)

    sls = [slice(hh * HEAD_PAD, (hh + 1) * HEAD_PAD) for hh in range(2)]

    def scores(j, s_buf):
        start = pl.multiple_of(j * tile, tile)
        for c, sl in enumerate(sls):
            s_buf[c] = _dot_nt(q_ref[:, sl], k_ref[pl.ds(start, tile), sl])

    def update(j, s_buf, p_buf, diag):
        start = pl.multiple_of(j * tile, tile)
        for c, sl in enumerate(sls):
            _softmax_tile(s_buf.at[c], p_buf.at[c], m_ref.at[c], None, a_ref.at[c],
                          add_ref=dmask_ref if diag else None)
            acc_ref[c] = a_ref[c] * acc_ref[c] + _dot(p_buf[c], v_ref[pl.ds(start, tile), sl])

    _flash_causal(qi, (s0, s1, p0, p1, m_ref, l_ref, a_ref, acc_ref), scores, update)
    lane = _lane_iota((tile, LANE))
    a0, a1 = acc_ref[0], acc_ref[1]
    l0 = jnp.sum(jnp.where(lane == V_A, a0, 0.0), axis=-1, keepdims=True)
    l1 = jnp.sum(jnp.where(lane == 0, a1, 0.0), axis=-1, keepdims=True)
    o_ref[...] = jnp.where(lane < V_A, a0 / l0, a1 / l1).astype(BF)


def _mla_prompt(qa, ka, va, *, batch, seq):
    tile = min(ATT_TILE, seq)
    nq = seq // tile
    return pl.pallas_call(
        functools.partial(_mla_prompt_kernel, tile=tile),
        grid=(batch, H_A // 2, nq),
        in_specs=[pl.BlockSpec((tile, 2 * HEAD_PAD), lambda b, p, qi: (b * nq + qi, p)),
                  pl.BlockSpec((seq, 2 * HEAD_PAD), lambda b, p, qi: (b, p)),
                  pl.BlockSpec((seq, 2 * HEAD_PAD), lambda b, p, qi: (b, p))],
        out_specs=pl.BlockSpec((tile, LANE), lambda b, p, qi: (b * nq + qi, p)),
        out_shape=jax.ShapeDtypeStruct((batch * seq, H_A * V_A), BF),
        scratch_shapes=_flash_scratch(2, tile, tile) + [pltpu.VMEM((tile, tile), F32)],
        compiler_params=_cparams("parallel", "parallel", "arbitrary"),
        name="mla_prompt",
    )(qa, ka, va)


def _diff_prompt_kernel(slopes_ref, lamv_ref, subln_ref, q_ref, k_ref, v_ref, o_ref,
                        s0, s1, p0, p1, m_ref, l_ref, a_ref, acc_ref, key_ref, dbias_ref, q2_ref,
                        *, tile, lam_init):
    h = pl.program_id(1)
    qi = pl.program_id(2)
    slope = slopes_ref[h]
    q2_ref[...] = _split_halves(q_ref[...])
    @pl.when(qi == 0)
    def _():
        key_ref[...] = slope * lax.broadcasted_iota(jnp.int32, (8, tile), 1).astype(F32)
        row = lax.broadcasted_iota(jnp.int32, (tile, tile), 0)
        col = lax.broadcasted_iota(jnp.int32, (tile, tile), 1)
        later = (2.0 * slope) * jnp.minimum(row - col, 0).astype(F32)
        dbias_ref[...] = jnp.where(_chunk_causal_mask(tile, tile), later, NEG_INF)

    n_rows = 2 * tile

    def scores(j, s_buf, qts):
        k = k_ref[pl.ds(pl.multiple_of(j * tile, tile), tile), :]
        rows = _rows_of(qts, tile)
        key_term = jnp.concatenate([key_ref[...]] * ((rows.stop - rows.start) // 8), axis=0)
        for c in range(2):
            q = q2_ref[c * n_rows + rows.start:c * n_rows + rows.stop, :]
            s_buf[c, rows, :] = _dot_nt(q, k) + key_term

    def update(j, s_buf, p_buf, qts, diag_qts):
        start = pl.multiple_of(j * tile, tile)
        off = slope * (j * tile).astype(F32)
        rows = _rows_of(qts, tile)
        for c in range(2):
            for qt in qts:
                r = _rows_of((qt,), tile)
                _softmax_tile(s_buf.at[c, r], p_buf.at[c, r], m_ref.at[c, r], l_ref.at[c, r], a_ref.at[c, r],
                              add_ref=dbias_ref if qt in diag_qts else None, off=off)
            acc_ref[c, rows, :] = (a_ref[c, rows, :] * acc_ref[c, rows, :]
                                   + _dot(p_buf[c, rows, :], v_ref[pl.ds(start, tile), :]))

    _flash_causal_pair(qi, (s0, s1, p0, p1, m_ref, l_ref, a_ref, acc_ref), scores, update)
    lam = _diff_lambda(lamv_ref[...], lam_init)
    o_ref[...] = _diff_finish(acc_ref[0] / l_ref[0], acc_ref[1] / l_ref[1], lam, subln_ref[...],
                              lam_init).astype(BF)


def _diff_prompt(slopes, lamv, subln, qd, kdb, vdb, *, batch, seq, lam_init):
    tile = min(ATT_TILE, seq // 2)
    nq = seq // (2 * tile)
    return pl.pallas_call(
        functools.partial(_diff_prompt_kernel, tile=tile, lam_init=lam_init),
        grid=(batch, H_B, nq),
        in_specs=[pl.BlockSpec(memory_space=pltpu.SMEM),
                  pl.BlockSpec(lamv.shape, lambda b, h, qi: (0, 0)),
                  pl.BlockSpec(subln.shape, lambda b, h, qi: (0, 0)),
                  pl.BlockSpec((2 * tile, LANE), lambda b, h, qi: (b * nq + qi, h)),
                  pl.BlockSpec((seq, LANE), lambda b, h, qi: (b, h)),
                  pl.BlockSpec((seq, LANE), lambda b, h, qi: (b, h))],
        out_specs=pl.BlockSpec((2 * tile, LANE), lambda b, h, qi: (b * nq + qi, h)),
        out_shape=jax.ShapeDtypeStruct((batch * seq, H_B * V_B), BF),
        scratch_shapes=_flash_scratch(2, 2 * tile, tile) + [pltpu.VMEM((8, tile), F32), pltpu.VMEM((tile, tile), F32),
                                                  pltpu.VMEM((4 * tile, LANE), BF)],
        compiler_params=_cparams("parallel", "parallel", "arbitrary"),
        name="diff_prompt",
    )(slopes, lamv, subln, qd, kdb, vdb)


def _even_sample_kernel(slopes_ref, lamv_ref, subln_ref, wk_ref, wv_ref, vone_ref, e_ref,
                        qa_ref, ka_ref, va_ref, ckv_ref, kr_ref,
                        qd_ref, kdb_ref, vdb_ref, ck_ref, cv_ref,
                        oa_ref, ob_ref, *, lam_init):
    n_new = qa_ref.shape[0]
    n_past = ckv_ref.shape[0]
    ckvp = ckv_ref[...].astype(BF)
    krp = kr_ref[...].astype(BF)
    lane = _lane_iota((n_new, LANE))
    for pr in range(H_A // 2):
        res = []
        for hh in range(2):
            sl = slice((2 * pr + hh) * HEAD_PAD, (2 * pr + hh + 1) * HEAD_PAD)
            q = qa_ref[:, sl]
            kp = (_dot(ckvp, wk_ref[:, sl]) + _dot(krp, e_ref[:, sl])).astype(BF)
            vp = (_dot(ckvp, wv_ref[:, sl]) + vone_ref[:, sl]).astype(BF)
            res.append(_softmax_pv([_dot_nt(q, kp), _dot_nt(q, ka_ref[:, sl])], [vp, va_ref[:, sl]]))
        oa_ref[:, pr * LANE:(pr + 1) * LANE] = jnp.where(lane < V_A, res[0], res[1]).astype(BF)
    rowp = lax.broadcasted_iota(jnp.int32, (n_new, n_past), 0)
    colp = lax.broadcasted_iota(jnp.int32, (n_new, n_past), 1)
    dist_p = (rowp - colp + n_past).astype(F32)
    dist_p = jnp.concatenate([dist_p, dist_p], axis=0)
    rown = lax.broadcasted_iota(jnp.int32, (n_new, n_new), 0)
    coln = lax.broadcasted_iota(jnp.int32, (n_new, n_new), 1)
    dist_n = jnp.abs(rown - coln).astype(F32)
    dist_n = jnp.concatenate([dist_n, dist_n], axis=0)
    lam = _diff_lambda(lamv_ref[...], lam_init)
    for h in range(H_B):
        sl = slice(h * LANE, (h + 1) * LANE)
        slope = slopes_ref[h]
        q2x = _split_halves(qd_ref[:, sl])
        kp = ck_ref[:, h, :].astype(BF)
        vp = cv_ref[:, h, :].astype(BF)
        s_p = _dot_nt(q2x, kp) - slope * dist_p
        s_n = _dot_nt(q2x, kdb_ref[:, sl]) - slope * dist_n
        o = _softmax_pv([s_p, s_n], [vp, vdb_ref[:, sl]])
        ob_ref[:, sl] = _diff_finish(o[:n_new], o[n_new:], lam, subln_ref[...], lam_init).astype(BF)


def _even_sample(slopes, lamv, subln, wk, wv, vone, e_mat, qa, ka, va, ckv_c, kr_c, qd, kdb, vdb, ck_c, cv_c,
                 *, n_prompt_rows, dec_seq, lam_init):
    dec_batch, n_past = ckv_c.shape[0], ckv_c.shape[1]
    base = n_prompt_rows // dec_seq

    def full(a):
        return pl.BlockSpec(a.shape, lambda s: (0,) * a.ndim)

    def new(width):
        return pl.BlockSpec((dec_seq, width), lambda s: (base + s, 0))

    def cache(*tail):
        return pl.BlockSpec((None, n_past) + tail, lambda s: (s, 0) + (0,) * len(tail))

    return pl.pallas_call(
        functools.partial(_even_sample_kernel, lam_init=lam_init),
        grid=(dec_batch,),
        in_specs=[pl.BlockSpec(memory_space=pltpu.SMEM), full(lamv), full(subln), full(wk), full(wv), full(vone),
                  full(e_mat), new(1024), new(1024), new(1024), cache(KV_RANK), cache(ROPE),
                  new(512), new(512), new(512), cache(H_B, V_B), cache(H_B, V_B)],
        out_specs=[pl.BlockSpec((dec_seq, 512), lambda s: (s, 0))] * 2,
        out_shape=[jax.ShapeDtypeStruct((dec_batch * dec_seq, 512), BF)] * 2,
        compiler_params=_cparams("parallel"),
        name="even_sample",
    )(slopes, lamv, subln, wk, wv, vone, e_mat, qa, ka, va, ckv_c, kr_c, qd, kdb, vdb, ck_c, cv_c)


def _route(logits, carry, live):
    tm = logits.shape[0]
    lane = _lane_iota(logits.shape).astype(F32)
    big = float(LANE)
    g_mask = lane < N_GROUPS
    gl = jnp.where(g_mask, logits, NEG_INF)
    gmax = jnp.max(gl, axis=-1, keepdims=True)
    g_sel = jnp.min(jnp.where(gl == gmax, lane, big), axis=-1, keepdims=True)
    p_grp = 1.0 / jnp.sum(jnp.exp(gl - gmax), axis=-1, keepdims=True)
    lo = ROUTE_OFF + EPG * g_sel
    el = jnp.where((lane >= lo) & (lane < lo + EPG), logits, NEG_INF)
    v1 = jnp.max(el, axis=-1, keepdims=True)
    i1 = jnp.min(jnp.where(el == v1, lane, big), axis=-1, keepdims=True)
    el2 = jnp.where(lane == i1, NEG_INF, el)
    v2 = jnp.max(el2, axis=-1, keepdims=True)
    i2 = jnp.min(jnp.where(el2 == v2, lane, big), axis=-1, keepdims=True)
    ex = jnp.exp(v2 - v1)
    den = 1.0 + ex
    gate1 = (1.0 / den) * p_grp
    gate2 = (ex / den) * p_grp
    onehot = jnp.where((lane == i1) | (lane == i2), 1.0, 0.0)
    row = lax.broadcasted_iota(jnp.int32, (tm, tm), 0)
    col = lax.broadcasted_iota(jnp.int32, (tm, tm), 1)
    tri = jnp.where(row > col, 1.0, 0.0).astype(BF)
    cum = _dot(tri, onehot.astype(BF)) + carry
    r1 = jnp.sum(jnp.where(lane == i1, cum, 0.0), axis=-1, keepdims=True)
    r2 = jnp.sum(jnp.where(lane == i2, cum, 0.0), axis=-1, keepdims=True)
    packed = jnp.zeros_like(logits)
    for pos, val in enumerate([i1 - ROUTE_OFF, i2 - ROUTE_OFF, gate1, gate2, r1, r2]):
        packed = jnp.where(lane == pos, val, packed)
    return packed, carry + live * jnp.sum(onehot, axis=0, keepdims=True)


def _out_router_kernel(*refs, splits, n_prompt_tiles):
    i = pl.program_id(0)
    offs = [sum(splits[:k]) for k in range(len(splits))]
    n_mix = len(splits) - 1

    def pick(k):
        parts = refs[offs[k]:offs[k] + splits[k]]
        if splits[k] == 1:
            return parts[0][...]
        return jnp.where(i < n_prompt_tiles, parts[0][...], parts[1][...])

    rest = refs[sum(splits):]
    w_refs = rest[:n_mix]
    g_ref, wr_ref, br_ref, x1_ref, xn_ref, route_ref, cnt_ref, carry_ref, logits_ref = rest[n_mix:]

    @pl.when(i == 0)
    def _():
        carry_ref[...] = jnp.zeros_like(carry_ref)
        logits_ref[...] = jnp.zeros_like(logits_ref)

    prev = logits_ref[...]
    x1 = pick(0)
    for k, w_ref in enumerate(w_refs):
        x1 = x1 + _dot(pick(1 + k), w_ref[...])
    x1_ref[...] = x1
    xb = _rms(x1, g_ref[...], NORM_EPS).astype(BF)
    xn_ref[...] = xb
    logits_ref[...] = _dot(xb, wr_ref[...]) + br_ref[...]
    live = jnp.where(i > 0, 1.0, 0.0)
    carry = carry_ref[...]
    for r0 in range(0, prev.shape[0], ROUTE_ROWS):
        packed, carry = _route(prev[r0:r0 + ROUTE_ROWS], carry, live)
        route_ref[r0:r0 + ROUTE_ROWS, :] = packed
    carry_ref[...] = carry
    cnt_ref[...] = carry


def _out_router(row_inputs, ws, g, wr, br):
    splits = tuple(len(parts) for parts in row_inputs)
    T = sum(a.shape[0] for a in row_inputs[0])
    D = row_inputs[0][0].shape[1]
    tile = WIDE_TILE
    n = T // tile
    n_p = max([parts[0].shape[0] // tile for parts in row_inputs if len(parts) == 2], default=0)

    def row_specs(parts):
        if len(parts) == 1:
            return [pl.BlockSpec((tile, parts[0].shape[1]), lambda i: (jnp.minimum(i, n - 1), 0))]
        n_s = parts[1].shape[0] // tile
        return [pl.BlockSpec((tile, parts[0].shape[1]), lambda i: (jnp.minimum(i, n_p - 1), 0)),
                pl.BlockSpec((tile, parts[1].shape[1]), lambda i: (jnp.clip(i - n_p, 0, n_s - 1), 0))]

    def full(a):
        return pl.BlockSpec(a.shape, lambda i: (0,) * a.ndim)

    def rows(width, lag=0):
        return pl.BlockSpec((tile, width), lambda i: (jnp.clip(i - lag, 0, n - 1), 0))

    flat_rows = [a for parts in row_inputs for a in parts]
    return pl.pallas_call(
        functools.partial(_out_router_kernel, splits=splits, n_prompt_tiles=n_p),
        grid=(n + 1,),
        in_specs=[s for parts in row_inputs for s in row_specs(parts)] + [full(w) for w in ws]
        + [full(g), full(wr), full(br)],
        out_specs=[rows(D), rows(D), rows(LANE, lag=1), pl.BlockSpec((1, LANE), lambda i: (0, 0))],
        out_shape=[jax.ShapeDtypeStruct((T, D), F32), jax.ShapeDtypeStruct((T, D), BF),
                   jax.ShapeDtypeStruct((T, LANE), F32), jax.ShapeDtypeStruct((1, LANE), F32)],
        scratch_shapes=[pltpu.VMEM((1, LANE), F32), pltpu.VMEM((tile, LANE), F32)],
        compiler_params=_cparams("arbitrary"),
        name="out_router",
    )(*flat_rows, *ws, g, wr, br)


def _experts_kernel(be_ref, nu_ref, xb_ref, wg_ref, wu_ref, wd_ref, y_ref, wgb_ref, wub_ref, wdb_ref):
    i = pl.program_id(0)
    used = i < nu_ref[0]

    @pl.when(used & ((i == 0) | (be_ref[i] != be_ref[jnp.maximum(i - 1, 0)])))
    def _():
        wgb_ref[...] = wg_ref[...].astype(BF)
        wub_ref[...] = wu_ref[...].astype(BF)
        wdb_ref[...] = wd_ref[...].astype(BF)

    @pl.when(used)
    def _():
        xb = xb_ref[...]
        a = _dot(xb, wgb_ref[...])
        b = _dot(xb, wub_ref[...])
        hid = (a * jax.nn.sigmoid(a)) * b
        y_ref[...] = _dot(hid.astype(BF), wdb_ref[...]).astype(y_ref.dtype)

    @pl.when(jnp.logical_not(used))
    def _():
        y_ref[...] = jnp.zeros_like(y_ref)


def _experts(block_expert, n_used, xb, wg, wu, wd, *, layer):
    L, D = xb.shape
    n_blocks = L // MOE_BLOCK
    grid_spec = pltpu.PrefetchScalarGridSpec(
        num_scalar_prefetch=2,
        grid=(n_blocks,),
        in_specs=[pl.BlockSpec((MOE_BLOCK, D), lambda i, be, nu: (i, 0)),
                  pl.BlockSpec((None, None, D, D_EXPERT), lambda i, be, nu: (layer, be[i], 0, 0)),
                  pl.BlockSpec((None, None, D, D_EXPERT), lambda i, be, nu: (layer, be[i], 0, 0)),
                  pl.BlockSpec((None, None, D_EXPERT, D), lambda i, be, nu: (layer, be[i], 0, 0))],
        out_specs=pl.BlockSpec((MOE_BLOCK, D), lambda i, be, nu: (i, 0)),
        scratch_shapes=[pltpu.VMEM((D, D_EXPERT), BF), pltpu.VMEM((D, D_EXPERT), BF), pltpu.VMEM((D_EXPERT, D), BF)],
    )
    return pl.pallas_call(
        _experts_kernel,
        grid_spec=grid_spec,
        out_shape=jax.ShapeDtypeStruct((L, D), BF),
        compiler_params=_cparams("arbitrary"),
        name="experts",
    )(block_expert, n_used, xb, wg, wu, wd)


def _moe(xn, route, cnt, wg, wu, wd, *, layer):
    T = xn.shape[0]
    counts = cnt[0, ROUTE_OFF:ROUTE_OFF + N_EXPERTS].astype(jnp.int32)
    padded = ((counts + MOE_BLOCK - 1) // MOE_BLOCK) * MOE_BLOCK
    pend = jnp.cumsum(padded)
    pstart = pend - padded

    def slot_rows(k):
        e, rank = route[:, k].astype(jnp.int32), route[:, 4 + k].astype(jnp.int32)
        return jnp.sum(jnp.where(e[:, None] == jnp.arange(N_EXPERTS)[None, :], pstart[None, :], 0), axis=-1) + rank

    dest = [slot_rows(0), slot_rows(1)]
    n_blocks = -(-(2 * T) // MOE_BLOCK) + N_EXPERTS
    L = n_blocks * MOE_BLOCK
    tok = jnp.arange(T, dtype=jnp.int32)
    buf_tok = (jnp.arange(L, dtype=jnp.int32) % T).at[jnp.concatenate(dest)].set(
        jnp.concatenate([tok, tok]), unique_indices=True, mode="promise_in_bounds")
    block_start = jnp.arange(n_blocks, dtype=jnp.int32) * MOE_BLOCK
    block_expert = jnp.minimum(jnp.sum(pend[None, :] <= block_start[:, None], axis=1), N_EXPERTS - 1).astype(jnp.int32)
    n_used = (pend[-1:] // MOE_BLOCK).astype(jnp.int32)
    xb = xn.at[buf_tok].get(mode="promise_in_bounds")
    yb = _experts(block_expert, n_used, xb, wg, wu, wd, layer=layer)
    return (yb.at[dest[0]].get(mode="promise_in_bounds"), yb.at[dest[1]].get(mode="promise_in_bounds"))


def _gates(route):
    lane = _lane_iota(route.shape)
    g0 = jnp.sum(jnp.where(lane == 2, route, 0.0), axis=-1, keepdims=True)
    g1 = jnp.sum(jnp.where(lane == 3, route, 0.0), axis=-1, keepdims=True)
    return g0, g1


def _odd_in_kernel(x_ref, y0_ref, y1_ref, route_ref, g_ref, w_ref, x2_ref, q_ref, k_ref, v_ref, *, scale):
    g0, g1 = _gates(route_ref[...])
    x2 = x_ref[...] + (y0_ref[...].astype(F32) * g0 + y1_ref[...].astype(F32) * g1)
    x2_ref[...] = x2
    h = _rms(x2, g_ref[...], NORM_EPS).astype(BF)
    y = _dot(h, w_ref[...])
    n = q_ref.shape[1]
    q_ref[...] = (y[:, :n] * scale).astype(BF)
    k_ref[...] = y[:, n:2 * n].astype(BF)
    v_ref[...] = y[:, 2 * n:].astype(BF)


def _odd_in(x1, y0, y1, route, g, w):
    T, D = x1.shape
    n = w.shape[1] // 3

    def rows(width):
        return pl.BlockSpec((WIDE_TILE, width), lambda i: (i, 0))

    def full(a):
        return pl.BlockSpec(a.shape, lambda i: (0,) * a.ndim)

    return pl.pallas_call(
        functools.partial(_odd_in_kernel, scale=DH_C ** -0.5 * LOG2E),
        grid=(T // WIDE_TILE,),
        in_specs=[rows(D), rows(D), rows(D), rows(LANE), full(g), full(w)],
        out_specs=[rows(D), rows(n), rows(n), rows(n)],
        out_shape=[jax.ShapeDtypeStruct((T, D), F32)] + [jax.ShapeDtypeStruct((T, n), BF)] * 3,
        compiler_params=_cparams("parallel"),
        name="odd_in",
    )(x1, y0, y1, route, g, w)


def _state_rows_kernel(ids_ref, x_ref, g_ref, w_ref, o_ref):
    del ids_ref
    h = _rms(x_ref[...], g_ref[...], NORM_EPS).astype(BF)
    o_ref[...] = _dot(h, w_ref[...])


def _state_rows(tile_ids, x, g, w):
    D = x.shape[1]
    n = tile_ids.shape[0]
    grid_spec = pltpu.PrefetchScalarGridSpec(
        num_scalar_prefetch=1,
        grid=(n,),
        in_specs=[pl.BlockSpec((ROW_TILE, D), lambda i, ids: (ids[i], 0)),
                  pl.BlockSpec(g.shape, lambda i, ids: (0, 0)),
                  pl.BlockSpec(w.shape, lambda i, ids: (0, 0))],
        out_specs=pl.BlockSpec((ROW_TILE, w.shape[1]), lambda i, ids: (i, 0)),
    )
    return pl.pallas_call(
        _state_rows_kernel,
        grid_spec=grid_spec,
        out_shape=jax.ShapeDtypeStruct((n * ROW_TILE, w.shape[1]), F32),
        compiler_params=_cparams("parallel"),
        name="state_rows",
    )(tile_ids, x, g, w)


def _band_prompt_kernel(q_ref, k_ref, v_ref, bias_ref, o_ref, q2_ref, s0, s1, p_ref, l_ref,
                        *, tile, tiles_per_step):
    lane = _lane_iota((tile, LANE))
    q2_ref[...] = _split_halves(q_ref[...])
    n_rows = tile * tiles_per_step

    def key_tiles(t):
        qi = pl.program_id(2) * tiles_per_step + t
        out = []
        for kt in range(3):
            start = (qi + kt - 2) * tile
            neg = jnp.where(start >= 0, 0.0, NEG_INF)
            out.append((pl.multiple_of(jnp.maximum(start, 0), tile), neg))
        return out

    def scores(t, s_buf):
        for hh in range(2):
            q = q2_ref[hh * n_rows + t * tile:hh * n_rows + (t + 1) * tile, :]
            for kt, (start, neg) in enumerate(key_tiles(t)):
                s = _dot_nt(q, k_ref[pl.ds(start, tile), :])
                s_buf[hh, :, kt * tile:(kt + 1) * tile] = s + neg if kt < 2 else s

    def softmax_pv(t, s_buf):
        outs = []
        slot = t % 2
        for hh in range(2):
            for rb in range(tile // SOFTMAX_ROWS):
                rs = slice(rb * SOFTMAX_ROWS, (rb + 1) * SOFTMAX_ROWS)
                s = s_buf[hh, rs, :] + bias_ref[hh * tile + rb * SOFTMAX_ROWS:hh * tile + (rb + 1) * SOFTMAX_ROWS, :]
                p = jnp.exp2(s - jnp.max(s, axis=-1, keepdims=True))
                l_ref[slot, hh, rs, :] = jnp.broadcast_to(jnp.sum(p, axis=-1, keepdims=True), (SOFTMAX_ROWS, LANE))
                p_ref[slot, hh, rs, :] = p.astype(BF)
            acc = None
            for kt, (start, _) in enumerate(key_tiles(t)):
                pv = _dot(p_ref[slot, hh, :, kt * tile:(kt + 1) * tile], v_ref[pl.ds(start, tile), :])
                acc = pv if acc is None else acc + pv
            outs.append(acc / l_ref[slot, hh])
        o_ref[t * tile:(t + 1) * tile, :] = jnp.where(lane < DH_C, outs[0], outs[1]).astype(BF)

    bufs = [s0, s1]
    scores(0, bufs[0])
    for t in range(tiles_per_step):
        if t + 1 < tiles_per_step:
            scores(t + 1, bufs[(t + 1) % 2])
        softmax_pv(t, bufs[t % 2])


def _band_prompt(q, k, v, bias, *, batch, seq):
    tile = BAND_TILE
    tps = min(BAND_TILES_PER_STEP, seq // tile)
    nq = seq // (tile * tps)
    return pl.pallas_call(
        functools.partial(_band_prompt_kernel, tile=tile, tiles_per_step=tps),
        grid=(H_C // 2, batch, nq),
        in_specs=[pl.BlockSpec((tile * tps, LANE), lambda p, b, qi: (b * nq + qi, p)),
                  pl.BlockSpec((seq, LANE), lambda p, b, qi: (b, p)),
                  pl.BlockSpec((seq, LANE), lambda p, b, qi: (b, p)),
                  pl.BlockSpec((None, 2 * tile, 3 * tile), lambda p, b, qi: (p, 0, 0))],
        out_specs=pl.BlockSpec((tile * tps, LANE), lambda p, b, qi: (b * nq + qi, p)),
        out_shape=jax.ShapeDtypeStruct((batch * seq, H_C * DH_C), BF),
        scratch_shapes=[pltpu.VMEM((2 * tile * tps, LANE), BF),
                        pltpu.VMEM((2, tile, 3 * tile), F32), pltpu.VMEM((2, tile, 3 * tile), F32),
                        pltpu.VMEM((2, 2, tile, 3 * tile), BF), pltpu.VMEM((2, 2, tile, LANE), F32)],
        compiler_params=_cparams("parallel", "parallel", "arbitrary"),
        name="band_prompt",
    )(q, k, v, bias)


def _band_sample_kernel(q_ref, k_ref, v_ref, ck_ref, cv_ref, bias_ref, o_ref):
    n_new = q_ref.shape[0]
    n_past = ck_ref.shape[0]
    lane = _lane_iota((n_new, LANE))
    for pr in range(H_C // 2):
        sl = slice(pr * LANE, (pr + 1) * LANE)
        q2x = _split_halves(q_ref[:, sl])
        s_p = _dot_nt(q2x, ck_ref[:, sl].astype(BF)) + bias_ref[pr, :, 0:n_past]
        s_n = _dot_nt(q2x, k_ref[:, sl]) + bias_ref[pr, :, n_past:n_past + n_new]
        o = _softmax_pv([s_p, s_n], [cv_ref[:, sl].astype(BF), v_ref[:, sl]])
        o_ref[:, sl] = jnp.where(lane < DH_C, o[:n_new], o[n_new:]).astype(BF)


def _band_sample(q, k, v, ck, cv, bias, *, n_prompt_rows, dec_seq):
    dec_batch, n_past, width = ck.shape
    base = n_prompt_rows // dec_seq
    new = pl.BlockSpec((dec_seq, width), lambda s: (base + s, 0))
    cache = pl.BlockSpec((None, n_past, width), lambda s: (s, 0, 0))
    return pl.pallas_call(
        _band_sample_kernel,
        grid=(dec_batch,),
        in_specs=[new, new, new, cache, cache, pl.BlockSpec(bias.shape, lambda s: (0, 0, 0))],
        out_specs=pl.BlockSpec((dec_seq, width), lambda s: (s, 0)),
        out_shape=jax.ShapeDtypeStruct((dec_batch * dec_seq, width), BF),
        compiler_params=_cparams("parallel"),
        name="band_sample",
    )(q, k, v, ck, cv, bias)


def _final_kernel(x_ref, y0_ref, y1_ref, route_ref, g_ref, o_ref):
    g0, g1 = _gates(route_ref[...])
    x = x_ref[...] + (y0_ref[...].astype(F32) * g0 + y1_ref[...].astype(F32) * g1)
    o_ref[...] = _rms(x, g_ref[...], NORM_EPS)


def _final(x, y0, y1, route, g, *, first_tile, n_tiles):
    D = x.shape[1]

    def rows(width):
        return pl.BlockSpec((WIDE_TILE, width), lambda i: (first_tile + i, 0))

    return pl.pallas_call(
        _final_kernel,
        grid=(n_tiles,),
        in_specs=[rows(D), rows(D), rows(D), rows(LANE), pl.BlockSpec(g.shape, lambda i: (0, 0))],
        out_specs=pl.BlockSpec((WIDE_TILE, D), lambda i: (i, 0)),
        out_shape=jax.ShapeDtypeStruct((n_tiles * WIDE_TILE, D), F32),
        compiler_params=_cparams("parallel"),
        name="final_norm",
    )(x, y0, y1, route, g)


def _prep_even_weights(w_in, w_qup, w_kvup):
    D = w_in.shape[0]
    a_in = Q_RANK + KV_RANK + ROPE
    bq = H_B * 2 * DH_B
    wcq, wckv, wkr = w_in[:, :Q_RANK], w_in[:, Q_RANK:Q_RANK + KV_RANK], w_in[:, Q_RANK + KV_RANK:a_in]
    wqd, wkd, wvd = w_in[:, a_in:a_in + bq], w_in[:, a_in + bq:a_in + 2 * bq], w_in[:, a_in + 2 * bq:]
    half = ROPE // 2

    def z(n):
        return jnp.zeros((D, n), w_in.dtype)

    kr_m = jnp.concatenate([z(NOPE), wkr, z(HEAD_PAD - NOPE - ROPE)], axis=1)
    kr_s = jnp.concatenate([z(NOPE), wkr[:, half:], wkr[:, :half], z(HEAD_PAD - NOPE - ROPE)], axis=1)
    w0 = jnp.concatenate([wcq, wckv, kr_m, kr_s, wqd, wkd, wvd], axis=1).astype(BF)
    wq3 = w_qup.reshape(Q_RANK, H_A, NOPE + ROPE)
    nope, r1, r2 = wq3[:, :, :NOPE], wq3[:, :, NOPE:NOPE + half], wq3[:, :, NOPE + half:]
    zq = jnp.zeros((Q_RANK, H_A, HEAD_PAD - NOPE - ROPE), w_qup.dtype)
    wq = jnp.concatenate([nope, r1, r2, zq], axis=-1).reshape(Q_RANK, H_A * HEAD_PAD).astype(BF)
    wqs = jnp.concatenate([jnp.zeros_like(nope), r2, r1, zq], axis=-1).reshape(Q_RANK, H_A * HEAD_PAD).astype(BF)
    wkv3 = w_kvup.reshape(KV_RANK, H_A, NOPE + V_A)
    wk = jnp.concatenate([wkv3[:, :, :NOPE], jnp.zeros((KV_RANK, H_A, HEAD_PAD - NOPE), w_kvup.dtype)],
                         axis=-1).reshape(KV_RANK, H_A * HEAD_PAD).astype(BF)
    wv4 = wkv3[:, :, NOPE:].reshape(KV_RANK, H_A // 2, 2, V_A)
    zv = jnp.zeros((KV_RANK, H_A // 2, HEAD_PAD - V_A), w_kvup.dtype)
    wv = jnp.concatenate([wv4[:, :, 0], zv, zv, wv4[:, :, 1]], axis=-1).reshape(KV_RANK, H_A * HEAD_PAD).astype(BF)
    vone = jnp.zeros((H_A // 2, 2 * HEAD_PAD), F32).at[:, V_A].set(1.0).at[:, HEAD_PAD].set(1.0)
    r = jnp.arange(ROPE)
    e_mat = jnp.zeros((ROPE, H_A, HEAD_PAD), F32).at[r[:, None], jnp.arange(H_A)[None, :], NOPE + r[:, None]].set(1.0)
    return w0, wq, wqs, wk, wv, vone.reshape(1, H_A * HEAD_PAD), e_mat.reshape(ROPE, H_A * HEAD_PAD).astype(BF)


def _rope_tables(pos):
    half = ROPE // 2
    inv = jnp.power(ROPE_BASE, -jnp.arange(half, dtype=F32) / half)
    ang = pos.astype(F32)[:, None] * inv[None, :]
    c, s = jnp.cos(ang), jnp.sin(ang)
    n = pos.shape[0]
    pad = jnp.zeros((n, HEAD_PAD - NOPE - ROPE), F32)
    cs = jnp.concatenate([jnp.ones((n, NOPE), F32), c, c, pad], axis=1)
    sn = jnp.concatenate([jnp.zeros((n, NOPE), F32), -s, s, pad], axis=1)
    return cs, sn


def _band_bias_tiles(table, n_rows, n_keys, key_offset, masked):
    i = jnp.arange(n_rows)[:, None]
    kpos = jnp.arange(n_keys)[None, :] - key_offset
    d_max = n_rows - 1 + key_offset
    w = n_rows + n_keys
    d = d_max - jnp.arange(w)
    rev = (table.astype(F32) * LOG2E)[:, jnp.clip(d, -REL_CLIP, REL_CLIP) + REL_CLIP]
    skew = jnp.tile(rev, (1, n_rows))[:, :n_rows * (w - 1)].reshape(-1, n_rows, w - 1)
    bias = skew[:, :, n_rows - 1:n_rows - 1 + n_keys]
    if masked:
        qc, kc = i // CHUNK, jnp.floor_divide(kpos, CHUNK)
        vis = (kc <= qc) & (kc >= qc - LEFT_CHUNKS)
        bias = jnp.where(vis[None], bias, NEG_INF)
    return bias.reshape(H_C // 2, 2 * n_rows, n_keys)


def kernel(x_prompt, x_sample, cache_mla_ckv, cache_mla_krope, cache_diff_k, cache_diff_v, cache_band_k, cache_band_v, ln_mix, w_in_even, mla_q_norm, mla_w_qup, mla_kv_norm, mla_w_kvup, diff_lam_q1, diff_lam_k1, diff_lam_q2, diff_lam_k2, diff_subln, w_out_even, w_in_odd, band_rel_bias, w_out_odd, ln_ffn, moe_w_group, moe_b_group, moe_w_router, moe_b_router, moe_w_gate, moe_w_up, moe_w_down, ln_final):
    B, S, D = x_prompt.shape
    DB, DS, _ = x_sample.shape
    n_past = cache_mla_ckv.shape[2]
    c_past = cache_band_k.shape[2]
    assert ln_mix.shape[0] == 2 and S % (2 * min(ATT_TILE, S // 2)) == 0 and S % BAND_TILE == 0
    assert (B * S) % WIDE_TILE == 0 and (DB * DS) % WIDE_TILE == 0 and ROW_TILE % DS == 0 and DS == CHUNK
    assert n_past % CHUNK == 0 and c_past == LEFT_CHUNKS * CHUNK and S >= c_past and c_past % ROW_TILE == 0
    BS, NS = B * S, DB * DS
    T = BS + NS
    n_p, n_s = BS // ROW_TILE, NS // ROW_TILE
    xp, xs = x_prompt.reshape(BS, D), x_sample.reshape(NS, D)
    row = lambda a: a.reshape(1, -1)

    w0, wq, wqs, wk, wv, vone, e_mat = _prep_even_weights(w_in_even[0], mla_w_qup[0], mla_w_kvup[0])
    cs_p, sn_p = _rope_tables(jnp.arange(S))
    cs_s, sn_s = _rope_tables(n_past + jnp.arange(DS))
    reps = ROW_TILE // DS
    even_w = (row(ln_mix[0]), w0, row(mla_q_norm[0]), wq, wqs, row(mla_kv_norm[0]), wk, wv, vone)
    qa, ka, va, ckv_p, kr_p, qd, kdb, vdb, kd_p, vd_p = _even_in(xp, *even_w, cs_p, sn_p)
    qa_s, ka_s, va_s, ckv_s, kr_s, qd_s, kdb_s, vdb_s, kd_s, vd_s = _even_in(
        xs, *even_w, jnp.tile(cs_s, (reps, 1)), jnp.tile(sn_s, (reps, 1)))
    slopes = jnp.exp2(-8.0 * jnp.arange(1, H_B + 1, dtype=F32) / H_B) * LOG2E
    lamv = jnp.stack([diff_lam_q1[0], diff_lam_k1[0], diff_lam_q2[0], diff_lam_k2[0]]).astype(F32)
    subln = row(diff_subln[0])
    lam_init = 0.8 - 0.6 * math.exp(-0.3 * 0)
    oa_p = _mla_prompt(qa, ka, va, batch=B, seq=S)
    ob_p = _diff_prompt(slopes, lamv, subln, qd, kdb, vdb, batch=B, seq=S, lam_init=lam_init)
    oa_s, ob_s = _even_sample(
        slopes, lamv, subln, wk, wv, vone, e_mat, qa_s, ka_s, va_s, cache_mla_ckv[0], cache_mla_krope[0],
        qd_s, kdb_s, vdb_s, cache_diff_k[0], cache_diff_v[0],
        n_prompt_rows=0, dec_seq=DS, lam_init=lam_init)

    def router_weights(l):
        wr = jnp.concatenate([moe_w_group[l], moe_w_router[l],
                              jnp.zeros((D, LANE - N_GROUPS - N_EXPERTS), F32)], axis=1).astype(BF)
        br = jnp.concatenate([moe_b_group[l], moe_b_router[l],
                              jnp.zeros((LANE - N_GROUPS - N_EXPERTS,), F32)]).astype(F32)
        return wr, row(br)

    n_a = H_A * V_A
    wo = w_out_even[0].astype(BF)
    wr, br = router_weights(0)
    x1, xn, route, cnt = _out_router([(xp, xs), (oa_p, oa_s), (ob_p, ob_s)], [wo[:n_a], wo[n_a:]],
                                     row(ln_ffn[0]), wr, br)
    y0, y1 = _moe(xn, route, cnt, moe_w_gate, moe_w_up, moe_w_down, layer=0)

    w_odd = w_in_odd[0].astype(BF)
    n_c = H_C * DH_C
    x2, qc, kc, vc = _odd_in(x1, y0, y1, route, row(ln_mix[1]), w_odd)
    tail = c_past // ROW_TILE
    tiles_per_seq = S // ROW_TILE
    tile_ids = jnp.concatenate(
        [jnp.arange(tiles_per_seq - tail, tiles_per_seq, dtype=jnp.int32) + b * tiles_per_seq for b in range(B)]
        + [jnp.arange(n_p, n_p + n_s, dtype=jnp.int32)])
    st = _state_rows(tile_ids, x2, row(ln_mix[1]), w_odd[:, n_c:])
    bias_p = _band_bias_tiles(band_rel_bias[0], BAND_TILE, 3 * BAND_TILE, 2 * BAND_TILE, True)
    bias_s = _band_bias_tiles(band_rel_bias[0], DS, c_past + DS, c_past, False)
    oc_p = _band_prompt(qc, kc, vc, bias_p, batch=B, seq=S)
    oc_s = _band_sample(qc, kc, vc, cache_band_k[0].reshape(DB, c_past, n_c), cache_band_v[0].reshape(DB, c_past, n_c),
                        bias_s, n_prompt_rows=BS, dec_seq=DS)
    wr, br = router_weights(1)
    x3, xn, route, cnt = _out_router([(x2,), (oc_p, oc_s)], [w_out_odd[0].astype(BF)], row(ln_ffn[1]), wr, br)
    y0, y1 = _moe(xn, route, cnt, moe_w_gate, moe_w_up, moe_w_down, layer=1)
    g_fin = row(ln_final)
    y_prompt = _final(x3, y0, y1, route, g_fin, first_tile=0, n_tiles=BS // WIDE_TILE).reshape(B, S, D)
    y_sample = _final(x3, y0, y1, route, g_fin, first_tile=BS // WIDE_TILE, n_tiles=NS // WIDE_TILE).reshape(DB, DS, D)

    def shaped(a_p, a_s, *tail_shape):
        return a_p.reshape(1, B, S, *tail_shape), a_s.reshape(1, DB, DS, *tail_shape)

    ckv_p, ckv_s = shaped(ckv_p, ckv_s, KV_RANK)
    kr_p, kr_s = shaped(kr_p, kr_s, ROPE)
    kd_p, kd_s = shaped(kd_p, kd_s, H_B, 2 * DH_B)
    vd_p, vd_s = shaped(vd_p, vd_s, H_B, V_B)
    n_tail = B * c_past
    bk_p = st[:n_tail, :n_c].reshape(1, B, c_past, H_C, DH_C)
    bv_p = st[:n_tail, n_c:].reshape(1, B, c_past, H_C, DH_C)
    k_new = st[n_tail:, :n_c].reshape(DB, DS, H_C, DH_C)
    v_new = st[n_tail:, n_c:].reshape(DB, DS, H_C, DH_C)
    bk_s = jnp.concatenate([cache_band_k[0][:, DS:], k_new], axis=1)[None]
    bv_s = jnp.concatenate([cache_band_v[0][:, DS:], v_new], axis=1)[None]
    return (y_prompt, y_sample, ckv_p, kr_p, kd_p, vd_p, bk_p, bv_p, ckv_s, kr_s, kd_s, vd_s, bk_s, bv_s)
```

```python
import functools
import math

import jax
import jax.numpy as jnp
from jax import lax
from jax.experimental import pallas as pl
from jax.experimental.pallas import tpu as pltpu

BF = jnp.bfloat16
F32 = jnp.float32
NEG_INF = float("-inf")
LOG2E = math.log2(math.e)

CHUNK = 64
NORM_EPS = 1e-6
SUBLN_EPS = 1e-5
H_A, NOPE, ROPE, V_A, Q_RANK, KV_RANK = 8, 64, 32, 64, 384, 256
ROPE_BASE = 10000.0
H_B, DH_B, V_B = 4, 64, 128
H_C, DH_C, LEFT_CHUNKS, REL_CLIP = 16, 64, 8, 128
N_GROUPS, EPG, N_EXPERTS, D_EXPERT = 4, 8, 32, 512
LANE = 128
HEAD_PAD = 128
ROUTE_OFF = N_GROUPS

ROW_TILE = 256
WIDE_TILE = 512
ROUTE_ROWS = 512
ATT_TILE = 512
SOFTMAX_ROWS = 32
BAND_TILE = 256
BAND_TILES_PER_STEP = 8
MOE_BLOCK = 512
VMEM_LIMIT = 56 * 1024 * 1024


def _cparams(*sem):
    return pltpu.CompilerParams(dimension_semantics=sem, vmem_limit_bytes=VMEM_LIMIT)


def _rms(x, g, eps):
    return x * lax.rsqrt(jnp.mean(x * x, axis=-1, keepdims=True) + eps) * g


def _dot(a, b):
    return jnp.dot(a, b, preferred_element_type=F32)


def _dot_nt(a, b):
    return lax.dot_general(a, b, (((1,), (1,)), ((), ())), preferred_element_type=F32)


def _lane_iota(shape):
    return lax.broadcasted_iota(jnp.int32, shape, len(shape) - 1)


def _split_halves(q):
    qf = q.astype(F32)
    lane = _lane_iota(qf.shape)
    return jnp.concatenate([jnp.where(lane < 64, qf, 0.0), jnp.where(lane >= 64, qf, 0.0)], axis=0).astype(BF)


def _softmax_pv(s_list, v_list):
    m = functools.reduce(jnp.maximum, [jnp.max(s, axis=-1, keepdims=True) for s in s_list])
    acc, l = None, None
    for s, v in zip(s_list, v_list):
        p = jnp.exp2(s - m)
        ls = jnp.sum(p, axis=-1, keepdims=True)
        a = _dot(p.astype(BF), v)
        l = ls if l is None else l + ls
        acc = a if acc is None else acc + a
    return acc / l


def _diff_lambda(lamv, lam_init):
    a = jnp.exp(jnp.sum(lamv[0:1] * lamv[1:2], axis=-1, keepdims=True))
    b = jnp.exp(jnp.sum(lamv[2:3] * lamv[3:4], axis=-1, keepdims=True))
    return a - b + lam_init


def _diff_finish(o1, o2, lam, subln, lam_init):
    o = o1 - lam * o2
    return _rms(o, subln, SUBLN_EPS) * (1.0 - lam_init)


def _even_in_kernel(x_ref, g_ref, w0_ref, qn_ref, wq_ref, wqs_ref, kvn_ref, wk_ref, wv_ref, vone_ref,
                    cs_ref, sn_ref,
                    qa_ref, ka_ref, va_ref, ckv_ref, kr_ref, qd_ref, kdb_ref, vdb_ref, kd_ref, vd_ref,
                    *, a_scale, b_scale):
    h = _rms(x_ref[...], g_ref[...], NORM_EPS).astype(BF)
    y = _dot(h, w0_ref[...])
    cq, ckv = y[:, 0:384], y[:, 384:640]
    kr_m, kr_s = y[:, 640:768], y[:, 768:896]
    qd, kd, vd = y[:, 896:1408], y[:, 1408:1920], y[:, 1920:2432]
    cs, sn = cs_ref[...], sn_ref[...]
    cqn = _rms(cq, qn_ref[...], NORM_EPS).astype(BF)
    qm = _dot(cqn, wq_ref[...])
    qs = _dot(cqn, wqs_ref[...])
    ckvn = _rms(ckv, kvn_ref[...], NORM_EPS)
    ckv_ref[...] = ckvn
    krp = kr_m * cs + kr_s * sn
    kr_ref[...] = krp[:, NOPE:NOPE + ROPE]
    cb = ckvn.astype(BF)
    kn = _dot(cb, wk_ref[...])
    for hh in range(H_A):
        sl = slice(hh * HEAD_PAD, (hh + 1) * HEAD_PAD)
        qa_ref[:, sl] = ((qm[:, sl] * cs + qs[:, sl] * sn) * a_scale).astype(BF)
        ka_ref[:, sl] = (kn[:, sl] + krp).astype(BF)
    va_ref[...] = (_dot(cb, wv_ref[...]) + vone_ref[...]).astype(BF)
    qd_ref[...] = (qd * b_scale).astype(BF)
    for hh in range(H_B):
        kd_ref[:, hh, :] = kd[:, hh * V_B:(hh + 1) * V_B]
        vd_ref[:, hh, :] = vd[:, hh * V_B:(hh + 1) * V_B]
    kdb_ref[...] = kd.astype(BF)
    vdb_ref[...] = vd.astype(BF)


def _even_in(x, g, w0, qn, wq, wqs, kvn, wk, wv, vone, cs_tab, sn_tab):
    T, D = x.shape
    pos_blocks = cs_tab.shape[0] // ROW_TILE

    def full(a):
        return pl.BlockSpec(a.shape, lambda i: (0,) * a.ndim)

    def rows(*tail):
        return pl.BlockSpec((ROW_TILE,) + tail, lambda i: (i,) + (0,) * len(tail))

    pos_spec = pl.BlockSpec((ROW_TILE, LANE), lambda i: (i % pos_blocks, 0))
    outs = [((1024,), BF), ((1024,), BF), ((1024,), BF), ((KV_RANK,), F32), ((ROPE,), F32),
            ((512,), BF), ((512,), BF), ((512,), BF), ((H_B, V_B), F32), ((H_B, V_B), F32)]
    return pl.pallas_call(
        functools.partial(_even_in_kernel,
                          a_scale=(NOPE + ROPE) ** -0.5 * LOG2E, b_scale=DH_B ** -0.5 * LOG2E),
        grid=(T // ROW_TILE,),
        in_specs=[rows(D), full(g), full(w0), full(qn), full(wq), full(wqs), full(kvn), full(wk), full(wv),
                  full(vone), pos_spec, pos_spec],
        out_specs=[rows(*tail) for tail, _ in outs],
        out_shape=[jax.ShapeDtypeStruct((T,) + tail, dt) for tail, dt in outs],
        compiler_params=_cparams("parallel"),
        name="even_in",
    )(x, g, w0, qn, wq, wqs, kvn, wk, wv, vone, cs_tab, sn_tab)


def _chunk_causal_mask(tq, tk):
    row = lax.broadcasted_iota(jnp.int32, (tq, tk), 0)
    col = lax.broadcasted_iota(jnp.int32, (tq, tk), 1)
    return (col // CHUNK) <= (row // CHUNK)


def _softmax_tile(s_ref, p_ref, m_ref, l_ref, a_ref, *, add_ref=None, off=None):
    tile = s_ref.shape[0]
    reps = tile // LANE
    blocks = [slice(rb * SOFTMAX_ROWS, (rb + 1) * SOFTMAX_ROWS) for rb in range(tile // SOFTMAX_ROWS)]

    def biased(rs):
        s = s_ref[rs, :]
        if add_ref is not None:
            s = s + add_ref[rs, :]
        return s

    for rs in blocks:
        m_old = m_ref[rs, :]
        red = jnp.broadcast_to(jnp.max(biased(rs), axis=-1, keepdims=True), m_old.shape)
        if off is not None:
            red = red + off
        m_new = jnp.maximum(m_old, red)
        a_ref[rs, :] = jnp.exp2(m_old - m_new)
        m_ref[rs, :] = m_new
    for rs in blocks:
        m_new = m_ref[rs, :]
        shift = m_new if off is None else m_new - off
        p = jnp.exp2(biased(rs) - jnp.concatenate([shift] * reps, axis=1))
        if l_ref is not None:
            l_ref[rs, :] = a_ref[rs, :] * l_ref[rs, :] + jnp.broadcast_to(
                jnp.sum(p, axis=-1, keepdims=True), m_new.shape)
        p_ref[rs, :] = p.astype(BF)


def _flash_causal(qi, bufs, scores, update):
    s0, s1, p0, p1, m_ref, l_ref, a_ref, acc_ref = bufs
    m_ref[...] = jnp.full(m_ref.shape, NEG_INF, F32)
    l_ref[...] = jnp.zeros(l_ref.shape, F32)
    acc_ref[...] = jnp.zeros(acc_ref.shape, F32)
    scores(0, s0)

    def pair(jj, carry):
        j = 2 * jj
        scores(j + 1, s1)
        update(j, s0, p0, False)
        scores(j + 2, s0)
        update(j + 1, s1, p1, False)
        return carry

    lax.fori_loop(0, qi // 2, pair, 0)

    @pl.when(qi % 2 == 1)
    def _():
        scores(qi, s1)
        update(qi - 1, s0, p0, False)
        s0[...] = s1[...]

    update(qi, s0, p0, True)


def _flash_causal_pair(t, bufs, scores, update):
    s0, s1, p0, p1, m_ref, l_ref, a_ref, acc_ref = bufs
    m_ref[...] = jnp.full(m_ref.shape, NEG_INF, F32)
    l_ref[...] = jnp.zeros(l_ref.shape, F32)
    acc_ref[...] = jnp.zeros(acc_ref.shape, F32)
    both = (0, 1)
    scores(0, s0, both)

    def pair(jj, carry):
        j = 2 * jj
        scores(j + 1, s1, both)
        update(j, s0, p0, both, ())
        scores(j + 2, s0, both)
        update(j + 1, s1, p1, both, ())
        return carry

    lax.fori_loop(0, t, pair, 0)
    scores(2 * t + 1, s1, (1,))
    update(2 * t, s0, p0, both, (0,))
    update(2 * t + 1, s1, p1, (1,), (1,))


def _flash_scratch(n_chains, rows, tile, acc_width=LANE):
    s = pltpu.VMEM((n_chains, rows, tile), F32)
    p = pltpu.VMEM((n_chains, rows, tile), BF)
    stat = pltpu.VMEM((n_chains, rows, LANE), F32)
    return [s, s, p, p, stat, stat, stat, pltpu.VMEM((n_chains, rows, acc_width), F32)]


def _rows_of(qts, tile):
    return slice(qts[0] * tile, (qts[-1] + 1) * tile)


def _mla_prompt_kernel(q_ref, k_ref, v_ref, o_ref, s0, s1, p0, p1, m_ref, l_ref, a_ref, acc_ref, dmask_ref,
                       *, tile):
    qi = pl.program_id(2)

    @pl.when(qi == 0)
    def _():
        dmask_ref[...] = jnp.where(_chunk_causal_mask(tile, tile), 0.0, NEG_INF)

    sls = [slice(hh * HEAD_PAD, (hh + 1) * HEAD_PAD) for hh in range(2)]

    def scores(j, s_buf):
        start = pl.multiple_of(j * tile, tile)
        for c, sl in enumerate(sls):
            s_buf[c] = _dot_nt(q_ref[:, sl], k_ref[pl.ds(start, tile), sl])

    def update(j, s_buf, p_buf, diag):
        start = pl.multiple_of(j * tile, tile)
        for c, sl in enumerate(sls):
            _softmax_tile(s_buf.at[c], p_buf.at[c], m_ref.at[c], None, a_ref.at[c],
                          add_ref=dmask_ref if diag else None)
            acc_ref[c] = a_ref[c] * acc_ref[c] + _dot(p_buf[c], v_ref[pl.ds(start, tile), sl])

    _flash_causal(qi, (s0, s1, p0, p1, m_ref, l_ref, a_ref, acc_ref), scores, update)
    lane = _lane_iota((tile, LANE))
    a0, a1 = acc_ref[0], acc_ref[1]
    l0 = jnp.sum(jnp.where(lane == V_A, a0, 0.0), axis=-1, keepdims=True)
    l1 = jnp.sum(jnp.where(lane == 0, a1, 0.0), axis=-1, keepdims=True)
    o_ref[...] = jnp.where(lane < V_A, a0 / l0, a1 / l1).astype(BF)


def _mla_prompt(qa, ka, va, *, batch, seq):
    tile = min(ATT_TILE, seq)
    nq = seq // tile
    return pl.pallas_call(
        functools.partial(_mla_prompt_kernel, tile=tile),
        grid=(batch, H_A // 2, nq),
        in_specs=[pl.BlockSpec((tile, 2 * HEAD_PAD), lambda b, p, qi: (b * nq + qi, p)),
                  pl.BlockSpec((seq, 2 * HEAD_PAD), lambda b, p, qi: (b, p)),
                  pl.BlockSpec((seq, 2 * HEAD_PAD), lambda b, p, qi: (b, p))],
        out_specs=pl.BlockSpec((tile, LANE), lambda b, p, qi: (b * nq + qi, p)),
        out_shape=jax.ShapeDtypeStruct((batch * seq, H_A * V_A), BF),
        scratch_shapes=_flash_scratch(2, tile, tile) + [pltpu.VMEM((tile, tile), F32)],
        compiler_params=_cparams("parallel", "parallel", "arbitrary"),
        name="mla_prompt",
    )(qa, ka, va)


def _diff_prompt_kernel(slopes_ref, lamv_ref, subln_ref, q_ref, k_ref, v_ref, o_ref,
                        s0, s1, p0, p1, m_ref, l_ref, a_ref, acc_ref, key_ref, dbias_ref, q2_ref,
                        *, tile, lam_init):
    h = pl.program_id(1)
    qi = pl.program_id(2)
    slope = slopes_ref[h]
    q2_ref[...] = _split_halves(q_ref[...])
    @pl.when(qi == 0)
    def _():
        key_ref[...] = slope * lax.broadcasted_iota(jnp.int32, (8, tile), 1).astype(F32)
        row = lax.broadcasted_iota(jnp.int32, (tile, tile), 0)
        col = lax.broadcasted_iota(jnp.int32, (tile, tile), 1)
        later = (2.0 * slope) * jnp.minimum(row - col, 0).astype(F32)
        dbias_ref[...] = jnp.where(_chunk_causal_mask(tile, tile), later, NEG_INF)

    n_rows = 2 * tile

    def scores(j, s_buf, qts):
        k = k_ref[pl.ds(pl.multiple_of(j * tile, tile), tile), :]
        rows = _rows_of(qts, tile)
        key_term = jnp.concatenate([key_ref[...]] * ((rows.stop - rows.start) // 8), axis=0)
        for c in range(2):
            q = q2_ref[c * n_rows + rows.start:c * n_rows + rows.stop, :]
            s_buf[c, rows, :] = _dot_nt(q, k) + key_term

    def update(j, s_buf, p_buf, qts, diag_qts):
        start = pl.multiple_of(j * tile, tile)
        off = slope * (j * tile).astype(F32)
        rows = _rows_of(qts, tile)
        for c in range(2):
            for qt in qts:
                r = _rows_of((qt,), tile)
                _softmax_tile(s_buf.at[c, r], p_buf.at[c, r], m_ref.at[c, r], l_ref.at[c, r], a_ref.at[c, r],
                              add_ref=dbias_ref if qt in diag_qts else None, off=off)
            acc_ref[c, rows, :] = (a_ref[c, rows, :] * acc_ref[c, rows, :]
                                   + _dot(p_buf[c, rows, :], v_ref[pl.ds(start, tile), :]))

    _flash_causal_pair(qi, (s0, s1, p0, p1, m_ref, l_ref, a_ref, acc_ref), scores, update)
    lam = _diff_lambda(lamv_ref[...], lam_init)
    o_ref[...] = _diff_finish(acc_ref[0] / l_ref[0], acc_ref[1] / l_ref[1], lam, subln_ref[...],
                              lam_init).astype(BF)


def _diff_prompt(slopes, lamv, subln, qd, kdb, vdb, *, batch, seq, lam_init):
    tile = min(ATT_TILE, seq // 2)
    nq = seq // (2 * tile)
    return pl.pallas_call(
        functools.partial(_diff_prompt_kernel, tile=tile, lam_init=lam_init),
        grid=(batch, H_B, nq),
        in_specs=[pl.BlockSpec(memory_space=pltpu.SMEM),
                  pl.BlockSpec(lamv.shape, lambda b, h, qi: (0, 0)),
                  pl.BlockSpec(subln.shape, lambda b, h, qi: (0, 0)),
                  pl.BlockSpec((2 * tile, LANE), lambda b, h, qi: (b * nq + qi, h)),
                  pl.BlockSpec((seq, LANE), lambda b, h, qi: (b, h)),
                  pl.BlockSpec((seq, LANE), lambda b, h, qi: (b, h))],
        out_specs=pl.BlockSpec((2 * tile, LANE), lambda b, h, qi: (b * nq + qi, h)),
        out_shape=jax.ShapeDtypeStruct((batch * seq, H_B * V_B), BF),
        scratch_shapes=_flash_scratch(2, 2 * tile, tile) + [pltpu.VMEM((8, tile), F32), pltpu.VMEM((tile, tile), F32),
                                                  pltpu.VMEM((4 * tile, LANE), BF)],
        compiler_params=_cparams("parallel", "parallel", "arbitrary"),
        name="diff_prompt",
    )(slopes, lamv, subln, qd, kdb, vdb)


def _even_sample_kernel(slopes_ref, lamv_ref, subln_ref, wk_ref, wv_ref, vone_ref, e_ref,
                        qa_ref, ka_ref, va_ref, ckv_ref, kr_ref,
                        qd_ref, kdb_ref, vdb_ref, ck_ref, cv_ref,
                        oa_ref, ob_ref, *, lam_init):
    n_new = qa_ref.shape[0]
    n_past = ckv_ref.shape[0]
    ckvp = ckv_ref[...].astype(BF)
    krp = kr_ref[...].astype(BF)
    lane = _lane_iota((n_new, LANE))
    for pr in range(H_A // 2):
        res = []
        for hh in range(2):
            sl = slice((2 * pr + hh) * HEAD_PAD, (2 * pr + hh + 1) * HEAD_PAD)
            q = qa_ref[:, sl]
            kp = (_dot(ckvp, wk_ref[:, sl]) + _dot(krp, e_ref[:, sl])).astype(BF)
            vp = (_dot(ckvp, wv_ref[:, sl]) + vone_ref[:, sl]).astype(BF)
            res.append(_softmax_pv([_dot_nt(q, kp), _dot_nt(q, ka_ref[:, sl])], [vp, va_ref[:, sl]]))
        oa_ref[:, pr * LANE:(pr + 1) * LANE] = jnp.where(lane < V_A, res[0], res[1]).astype(BF)
    rowp = lax.broadcasted_iota(jnp.int32, (n_new, n_past), 0)
    colp = lax.broadcasted_iota(jnp.int32, (n_new, n_past), 1)
    dist_p = (rowp - colp + n_past).astype(F32)
    dist_p = jnp.concatenate([dist_p, dist_p], axis=0)
    rown = lax.broadcasted_iota(jnp.int32, (n_new, n_new), 0)
    coln = lax.broadcasted_iota(jnp.int32, (n_new, n_new), 1)
    dist_n = jnp.abs(rown - coln).astype(F32)
    dist_n = jnp.concatenate([dist_n, dist_n], axis=0)
    lam = _diff_lambda(lamv_ref[...], lam_init)
    for h in range(H_B):
        sl = slice(h * LANE, (h + 1) * LANE)
        slope = slopes_ref[h]
        q2x = _split_halves(qd_ref[:, sl])
        kp = ck_ref[:, h, :].astype(BF)
        vp = cv_ref[:, h, :].astype(BF)
        s_p = _dot_nt(q2x, kp) - slope * dist_p
        s_n = _dot_nt(q2x, kdb_ref[:, sl]) - slope * dist_n
        o = _softmax_pv([s_p, s_n], [vp, vdb_ref[:, sl]])
        ob_ref[:, sl] = _diff_finish(o[:n_new], o[n_new:], lam, subln_ref[...], lam_init).astype(BF)


def _even_sample(slopes, lamv, subln, wk, wv, vone, e_mat, qa, ka, va, ckv_c, kr_c, qd, kdb, vdb, ck_c, cv_c,
                 *, n_prompt_rows, dec_seq, lam_init):
    dec_batch, n_past = ckv_c.shape[0], ckv_c.shape[1]
    base = n_prompt_rows // dec_seq

    def full(a):
        return pl.BlockSpec(a.shape, lambda s: (0,) * a.ndim)

    def new(width):
        return pl.BlockSpec((dec_seq, width), lambda s: (base + s, 0))

    def cache(*tail):
        return pl.BlockSpec((None, n_past) + tail, lambda s: (s, 0) + (0,) * len(tail))

    return pl.pallas_call(
        functools.partial(_even_sample_kernel, lam_init=lam_init),
        grid=(dec_batch,),
        in_specs=[pl.BlockSpec(memory_space=pltpu.SMEM), full(lamv), full(subln), full(wk), full(wv), full(vone),
                  full(e_mat), new(1024), new(1024), new(1024), cache(KV_RANK), cache(ROPE),
                  new(512), new(512), new(512), cache(H_B, V_B), cache(H_B, V_B)],
        out_specs=[pl.BlockSpec((dec_seq, 512), lambda s: (s, 0))] * 2,
        out_shape=[jax.ShapeDtypeStruct((dec_batch * dec_seq, 512), BF)] * 2,
        compiler_params=_cparams("parallel"),
        name="even_sample",
    )(slopes, lamv, subln, wk, wv, vone, e_mat, qa, ka, va, ckv_c, kr_c, qd, kdb, vdb, ck_c, cv_c)


def _route(logits, carry, live):
    tm = logits.shape[0]
    lane = _lane_iota(logits.shape).astype(F32)
    big = float(LANE)
    g_mask = lane < N_GROUPS
    gl = jnp.where(g_mask, logits, NEG_INF)
    gmax = jnp.max(gl, axis=-1, keepdims=True)
    g_sel = jnp.min(jnp.where(gl == gmax, lane, big), axis=-1, keepdims=True)
    p_grp = 1.0 / jnp.sum(jnp.exp(gl - gmax), axis=-1, keepdims=True)
    lo = ROUTE_OFF + EPG * g_sel
    el = jnp.where((lane >= lo) & (lane < lo + EPG), logits, NEG_INF)
    v1 = jnp.max(el, axis=-1, keepdims=True)
    i1 = jnp.min(jnp.where(el == v1, lane, big), axis=-1, keepdims=True)
    el2 = jnp.where(lane == i1, NEG_INF, el)
    v2 = jnp.max(el2, axis=-1, keepdims=True)
    i2 = jnp.min(jnp.where(el2 == v2, lane, big), axis=-1, keepdims=True)
    ex = jnp.exp(v2 - v1)
    den = 1.0 + ex
    gate1 = (1.0 / den) * p_grp
    gate2 = (ex / den) * p_grp
    onehot = jnp.where((lane == i1) | (lane == i2), 1.0, 0.0)
    row = lax.broadcasted_iota(jnp.int32, (tm, tm), 0)
    col = lax.broadcasted_iota(jnp.int32, (tm, tm), 1)
    tri = jnp.where(row > col, 1.0, 0.0).astype(BF)
    cum = _dot(tri, onehot.astype(BF)) + carry
    r1 = jnp.sum(jnp.where(lane == i1, cum, 0.0), axis=-1, keepdims=True)
    r2 = jnp.sum(jnp.where(lane == i2, cum, 0.0), axis=-1, keepdims=True)
    packed = jnp.zeros_like(logits)
    for pos, val in enumerate([i1 - ROUTE_OFF, i2 - ROUTE_OFF, gate1, gate2, r1, r2]):
        packed = jnp.where(lane == pos, val, packed)
    return packed, carry + live * jnp.sum(onehot, axis=0, keepdims=True)


def _out_router_kernel(*refs, splits, n_prompt_tiles):
    i = pl.program_id(0)
    offs = [sum(splits[:k]) for k in range(len(splits))]
    n_mix = len(splits) - 1

    def pick(k):
        parts = refs[offs[k]:offs[k] + splits[k]]
        if splits[k] == 1:
            return parts[0][...]
        return jnp.where(i < n_prompt_tiles, parts[0][...], parts[1][...])

    rest = refs[sum(splits):]
    w_refs = rest[:n_mix]
    g_ref, wr_ref, br_ref, x1_ref, xn_ref, route_ref, cnt_ref, carry_ref, logits_ref = rest[n_mix:]

    @pl.when(i == 0)
    def _():
        carry_ref[...] = jnp.zeros_like(carry_ref)
        logits_ref[...] = jnp.zeros_like(logits_ref)

    prev = logits_ref[...]
    x1 = pick(0)
    for k, w_ref in enumerate(w_refs):
        x1 = x1 + _dot(pick(1 + k), w_ref[...])
    x1_ref[...] = x1
    xb = _rms(x1, g_ref[...], NORM_EPS).astype(BF)
    xn_ref[...] = xb
    logits_ref[...] = _dot(xb, wr_ref[...]) + br_ref[...]
    live = jnp.where(i > 0, 1.0, 0.0)
    carry = carry_ref[...]
    for r0 in range(0, prev.shape[0], ROUTE_ROWS):
        packed, carry = _route(prev[r0:r0 + ROUTE_ROWS], carry, live)
        route_ref[r0:r0 + ROUTE_ROWS, :] = packed
    carry_ref[...] = carry
    cnt_ref[...] = carry


def _out_router(row_inputs, ws, g, wr, br):
    splits = tuple(len(parts) for parts in row_inputs)
    T = sum(a.shape[0] for a in row_inputs[0])
    D = row_inputs[0][0].shape[1]
    tile = WIDE_TILE
    n = T // tile
    n_p = max([parts[0].shape[0] // tile for parts in row_inputs if len(parts) == 2], default=0)

    def row_specs(parts):
        if len(parts) == 1:
            return [pl.BlockSpec((tile, parts[0].shape[1]), lambda i: (jnp.minimum(i, n - 1), 0))]
        n_s = parts[1].shape[0] // tile
        return [pl.BlockSpec((tile, parts[0].shape[1]), lambda i: (jnp.minimum(i, n_p - 1), 0)),
                pl.BlockSpec((tile, parts[1].shape[1]), lambda i: (jnp.clip(i - n_p, 0, n_s - 1), 0))]

    def full(a):
        return pl.BlockSpec(a.shape, lambda i: (0,) * a.ndim)

    def rows(width, lag=0):
        return pl.BlockSpec((tile, width), lambda i: (jnp.clip(i - lag, 0, n - 1), 0))

    flat_rows = [a for parts in row_inputs for a in parts]
    return pl.pallas_call(
        functools.partial(_out_router_kernel, splits=splits, n_prompt_tiles=n_p),
        grid=(n + 1,),
        in_specs=[s for parts in row_inputs for s in row_specs(parts)] + [full(w) for w in ws]
        + [full(g), full(wr), full(br)],
        out_specs=[rows(D), rows(D), rows(LANE, lag=1), pl.BlockSpec((1, LANE), lambda i: (0, 0))],
        out_shape=[jax.ShapeDtypeStruct((T, D), F32), jax.ShapeDtypeStruct((T, D), BF),
                   jax.ShapeDtypeStruct((T, LANE), F32), jax.ShapeDtypeStruct((1, LANE), F32)],
        scratch_shapes=[pltpu.VMEM((1, LANE), F32), pltpu.VMEM((tile, LANE), F32)],
        compiler_params=_cparams("arbitrary"),
        name="out_router",
    )(*flat_rows, *ws, g, wr, br)


def _experts_kernel(be_ref, nu_ref, xb_ref, wg_ref, wu_ref, wd_ref, y_ref, wgb_ref, wub_ref, wdb_ref):
    i = pl.program_id(0)
    used = i < nu_ref[0]

    @pl.when(used & ((i == 0) | (be_ref[i] != be_ref[jnp.maximum(i - 1, 0)])))
    def _():
        wgb_ref[...] = wg_ref[...].astype(BF)
        wub_ref[...] = wu_ref[...].astype(BF)
        wdb_ref[...] = wd_ref[...].astype(BF)

    @pl.when(used)
    def _():
        xb = xb_ref[...]
        a = _dot(xb, wgb_ref[...])
        b = _dot(xb, wub_ref[...])
        hid = (a * jax.nn.sigmoid(a)) * b
        y_ref[...] = _dot(hid.astype(BF), wdb_ref[...]).astype(y_ref.dtype)

    @pl.when(jnp.logical_not(used))
    def _():
        y_ref[...] = jnp.zeros_like(y_ref)


def _experts(block_expert, n_used, xb, wg, wu, wd, *, layer):
    L, D = xb.shape
    n_blocks = L // MOE_BLOCK
    grid_spec = pltpu.PrefetchScalarGridSpec(
        num_scalar_prefetch=2,
        grid=(n_blocks,),
        in_specs=[pl.BlockSpec((MOE_BLOCK, D), lambda i, be, nu: (i, 0)),
                  pl.BlockSpec((None, None, D, D_EXPERT), lambda i, be, nu: (layer, be[i], 0, 0)),
                  pl.BlockSpec((None, None, D, D_EXPERT), lambda i, be, nu: (layer, be[i], 0, 0)),
                  pl.BlockSpec((None, None, D_EXPERT, D), lambda i, be, nu: (layer, be[i], 0, 0))],
        out_specs=pl.BlockSpec((MOE_BLOCK, D), lambda i, be, nu: (i, 0)),
        scratch_shapes=[pltpu.VMEM((D, D_EXPERT), BF), pltpu.VMEM((D, D_EXPERT), BF), pltpu.VMEM((D_EXPERT, D), BF)],
    )
    return pl.pallas_call(
        _experts_kernel,
        grid_spec=grid_spec,
        out_shape=jax.ShapeDtypeStruct((L, D), BF),
        compiler_params=_cparams("arbitrary"),
        name="experts",
    )(block_expert, n_used, xb, wg, wu, wd)


def _moe(xn, route, cnt, wg, wu, wd, *, layer):
    T = xn.shape[0]
    counts = cnt[0, ROUTE_OFF:ROUTE_OFF + N_EXPERTS].astype(jnp.int32)
    padded = ((counts + MOE_BLOCK - 1) // MOE_BLOCK) * MOE_BLOCK
    pend = jnp.cumsum(padded)
    pstart = pend - padded

    def slot_rows(k):
        e, rank = route[:, k].astype(jnp.int32), route[:, 4 + k].astype(jnp.int32)
        return jnp.sum(jnp.where(e[:, None] == jnp.arange(N_EXPERTS)[None, :], pstart[None, :], 0), axis=-1) + rank

    dest = [slot_rows(0), slot_rows(1)]
    n_blocks = -(-(2 * T) // MOE_BLOCK) + N_EXPERTS
    L = n_blocks * MOE_BLOCK
    tok = jnp.arange(T, dtype=jnp.int32)
    buf_tok = (jnp.arange(L, dtype=jnp.int32) % T).at[jnp.concatenate(dest)].set(
        jnp.concatenate([tok, tok]), unique_indices=True, mode="promise_in_bounds")
    block_start = jnp.arange(n_blocks, dtype=jnp.int32) * MOE_BLOCK
    block_expert = jnp.minimum(jnp.sum(pend[None, :] <= block_start[:, None], axis=1), N_EXPERTS - 1).astype(jnp.int32)
    n_used = (pend[-1:] // MOE_BLOCK).astype(jnp.int32)
    xb = xn.at[buf_tok].get(mode="promise_in_bounds")
    yb = _experts(block_expert, n_used, xb, wg, wu, wd, layer=layer)
    return (yb.at[dest[0]].get(mode="promise_in_bounds"), yb.at[dest[1]].get(mode="promise_in_bounds"))


def _gates(route):
    lane = _lane_iota(route.shape)
    g0 = jnp.sum(jnp.where(lane == 2, route, 0.0), axis=-1, keepdims=True)
    g1 = jnp.sum(jnp.where(lane == 3, route, 0.0), axis=-1, keepdims=True)
    return g0, g1


def _odd_in_kernel(x_ref, y0_ref, y1_ref, route_ref, g_ref, w_ref, x2_ref, q_ref, k_ref, v_ref, *, scale):
    g0, g1 = _gates(route_ref[...])
    x2 = x_ref[...] + (y0_ref[...].astype(F32) * g0 + y1_ref[...].astype(F32) * g1)
    x2_ref[...] = x2
    h = _rms(x2, g_ref[...], NORM_EPS).astype(BF)
    y = _dot(h, w_ref[...])
    n = q_ref.shape[1]
    q_ref[...] = (y[:, :n] * scale).astype(BF)
    k_ref[...] = y[:, n:2 * n].astype(BF)
    v_ref[...] = y[:, 2 * n:].astype(BF)


def _odd_in(x1, y0, y1, route, g, w):
    T, D = x1.shape
    n = w.shape[1] // 3

    def rows(width):
        return pl.BlockSpec((WIDE_TILE, width), lambda i: (i, 0))

    def full(a):
        return pl.BlockSpec(a.shape, lambda i: (0,) * a.ndim)

    return pl.pallas_call(
        functools.partial(_odd_in_kernel, scale=DH_C ** -0.5 * LOG2E),
        grid=(T // WIDE_TILE,),
        in_specs=[rows(D), rows(D), rows(D), rows(LANE), full(g), full(w)],
        out_specs=[rows(D), rows(n), rows(n), rows(n)],
        out_shape=[jax.ShapeDtypeStruct((T, D), F32)] + [jax.ShapeDtypeStruct((T, n), BF)] * 3,
        compiler_params=_cparams("parallel"),
        name="odd_in",
    )(x1, y0, y1, route, g, w)


def _state_rows_kernel(ids_ref, x_ref, g_ref, w_ref, o_ref):
    del ids_ref
    h = _rms(x_ref[...], g_ref[...], NORM_EPS).astype(BF)
    o_ref[...] = _dot(h, w_ref[...])


def _state_rows(tile_ids, x, g, w):
    D = x.shape[1]
    n = tile_ids.shape[0]
    grid_spec = pltpu.PrefetchScalarGridSpec(
        num_scalar_prefetch=1,
        grid=(n,),
        in_specs=[pl.BlockSpec((ROW_TILE, D), lambda i, ids: (ids[i], 0)),
                  pl.BlockSpec(g.shape, lambda i, ids: (0, 0)),
                  pl.BlockSpec(w.shape, lambda i, ids: (0, 0))],
        out_specs=pl.BlockSpec((ROW_TILE, w.shape[1]), lambda i, ids: (i, 0)),
    )
    return pl.pallas_call(
        _state_rows_kernel,
        grid_spec=grid_spec,
        out_shape=jax.ShapeDtypeStruct((n * ROW_TILE, w.shape[1]), F32),
        compiler_params=_cparams("parallel"),
        name="state_rows",
    )(tile_ids, x, g, w)


def _band_prompt_kernel(q_ref, k_ref, v_ref, bias_ref, o_ref, q2_ref, s0, s1, p_ref, l_ref,
                        *, tile, tiles_per_step):
    lane = _lane_iota((tile, LANE))
    q2_ref[...] = _split_halves(q_ref[...])
    n_rows = tile * tiles_per_step

    def key_tiles(t):
        qi = pl.program_id(2) * tiles_per_step + t
        out = []
        for kt in range(3):
            start = (qi + kt - 2) * tile
            neg = jnp.where(start >= 0, 0.0, NEG_INF)
            out.append((pl.multiple_of(jnp.maximum(start, 0), tile), neg))
        return out

    def scores(t, s_buf):
        for hh in range(2):
            q = q2_ref[hh * n_rows + t * tile:hh * n_rows + (t + 1) * tile, :]
            for kt, (start, neg) in enumerate(key_tiles(t)):
                s = _dot_nt(q, k_ref[pl.ds(start, tile), :])
                s_buf[hh, :, kt * tile:(kt + 1) * tile] = s + neg if kt < 2 else s

    def softmax_pv(t, s_buf):
        outs = []
        slot = t % 2
        for hh in range(2):
            for rb in range(tile // SOFTMAX_ROWS):
                rs = slice(rb * SOFTMAX_ROWS, (rb + 1) * SOFTMAX_ROWS)
                s = s_buf[hh, rs, :] + bias_ref[hh * tile + rb * SOFTMAX_ROWS:hh * tile + (rb + 1) * SOFTMAX_ROWS, :]
                p = jnp.exp2(s - jnp.max(s, axis=-1, keepdims=True))
                l_ref[slot, hh, rs, :] = jnp.broadcast_to(jnp.sum(p, axis=-1, keepdims=True), (SOFTMAX_ROWS, LANE))
                p_ref[slot, hh, rs, :] = p.astype(BF)
            acc = None
            for kt, (start, _) in enumerate(key_tiles(t)):
                pv = _dot(p_ref[slot, hh, :, kt * tile:(kt + 1) * tile], v_ref[pl.ds(start, tile), :])
                acc = pv if acc is None else acc + pv
            outs.append(acc / l_ref[slot, hh])
        o_ref[t * tile:(t + 1) * tile, :] = jnp.where(lane < DH_C, outs[0], outs[1]).astype(BF)

    bufs = [s0, s1]
    scores(0, bufs[0])
    for t in range(tiles_per_step):
        if t + 1 < tiles_per_step:
            scores(t + 1, bufs[(t + 1) % 2])
        softmax_pv(t, bufs[t % 2])


def _band_prompt(q, k, v, bias, *, batch, seq):
    tile = BAND_TILE
    tps = min(BAND_TILES_PER_STEP, seq // tile)
    nq = seq // (tile * tps)
    return pl.pallas_call(
        functools.partial(_band_prompt_kernel, tile=tile, tiles_per_step=tps),
        grid=(H_C // 2, batch, nq),
        in_specs=[pl.BlockSpec((tile * tps, LANE), lambda p, b, qi: (b * nq + qi, p)),
                  pl.BlockSpec((seq, LANE), lambda p, b, qi: (b, p)),
                  pl.BlockSpec((seq, LANE), lambda p, b, qi: (b, p)),
                  pl.BlockSpec((None, 2 * tile, 3 * tile), lambda p, b, qi: (p, 0, 0))],
        out_specs=pl.BlockSpec((tile * tps, LANE), lambda p, b, qi: (b * nq + qi, p)),
        out_shape=jax.ShapeDtypeStruct((batch * seq, H_C * DH_C), BF),
        scratch_shapes=[pltpu.VMEM((2 * tile * tps, LANE), BF),
                        pltpu.VMEM((2, tile, 3 * tile), F32), pltpu.VMEM((2, tile, 3 * tile), F32),
                        pltpu.VMEM((2, 2, tile, 3 * tile), BF), pltpu.VMEM((2, 2, tile, LANE), F32)],
        compiler_params=_cparams("parallel", "parallel", "arbitrary"),
        name="band_prompt",
    )(q, k, v, bias)


def _band_sample_kernel(q_ref, k_ref, v_ref, ck_ref, cv_ref, bias_ref, o_ref):
    n_new = q_ref.shape[0]
    n_past = ck_ref.shape[0]
    lane = _lane_iota((n_new, LANE))
    for pr in range(H_C // 2):
        sl = slice(pr * LANE, (pr + 1) * LANE)
        q2x = _split_halves(q_ref[:, sl])
        s_p = _dot_nt(q2x, ck_ref[:, sl].astype(BF)) + bias_ref[pr, :, 0:n_past]
        s_n = _dot_nt(q2x, k_ref[:, sl]) + bias_ref[pr, :, n_past:n_past + n_new]
        o = _softmax_pv([s_p, s_n], [cv_ref[:, sl].astype(BF), v_ref[:, sl]])
        o_ref[:, sl] = jnp.where(lane < DH_C, o[:n_new], o[n_new:]).astype(BF)


def _band_sample(q, k, v, ck, cv, bias, *, n_prompt_rows, dec_seq):
    dec_batch, n_past, width = ck.shape
    base = n_prompt_rows // dec_seq
    new = pl.BlockSpec((dec_seq, width), lambda s: (base + s, 0))
    cache = pl.BlockSpec((None, n_past, width), lambda s: (s, 0, 0))
    return pl.pallas_call(
        _band_sample_kernel,
        grid=(dec_batch,),
        in_specs=[new, new, new, cache, cache, pl.BlockSpec(bias.shape, lambda s: (0, 0, 0))],
        out_specs=pl.BlockSpec((dec_seq, width), lambda s: (s, 0)),
        out_shape=jax.ShapeDtypeStruct((dec_batch * dec_seq, width), BF),
        compiler_params=_cparams("parallel"),
        name="band_sample",
    )(q, k, v, ck, cv, bias)


def _final_kernel(x_ref, y0_ref, y1_ref, route_ref, g_ref, o_ref):
    g0, g1 = _gates(route_ref[...])
    x = x_ref[...] + (y0_ref[...].astype(F32) * g0 + y1_ref[...].astype(F32) * g1)
    o_ref[...] = _rms(x, g_ref[...], NORM_EPS)


def _final(x, y0, y1, route, g, *, first_tile, n_tiles):
    D = x.shape[1]

    def rows(width):
        return pl.BlockSpec((WIDE_TILE, width), lambda i: (first_tile + i, 0))

    return pl.pallas_call(
        _final_kernel,
        grid=(n_tiles,),
        in_specs=[rows(D), rows(D), rows(D), rows(LANE), pl.BlockSpec(g.shape, lambda i: (0, 0))],
        out_specs=pl.BlockSpec((WIDE_TILE, D), lambda i: (i, 0)),
        out_shape=jax.ShapeDtypeStruct((n_tiles * WIDE_TILE, D), F32),
        compiler_params=_cparams("parallel"),
        name="final_norm",
    )(x, y0, y1, route, g)


def _prep_even_weights(w_in, w_qup, w_kvup):
    D = w_in.shape[0]
    a_in = Q_RANK + KV_RANK + ROPE
    bq = H_B * 2 * DH_B
    wcq, wckv, wkr = w_in[:, :Q_RANK], w_in[:, Q_RANK:Q_RANK + KV_RANK], w_in[:, Q_RANK + KV_RANK:a_in]
    wqd, wkd, wvd = w_in[:, a_in:a_in + bq], w_in[:, a_in + bq:a_in + 2 * bq], w_in[:, a_in + 2 * bq:]
    half = ROPE // 2

    def z(n):
        return jnp.zeros((D, n), w_in.dtype)

    kr_m = jnp.concatenate([z(NOPE), wkr, z(HEAD_PAD - NOPE - ROPE)], axis=1)
    kr_s = jnp.concatenate([z(NOPE), wkr[:, half:], wkr[:, :half], z(HEAD_PAD - NOPE - ROPE)], axis=1)
    w0 = jnp.concatenate([wcq, wckv, kr_m, kr_s, wqd, wkd, wvd], axis=1).astype(BF)
    wq3 = w_qup.reshape(Q_RANK, H_A, NOPE + ROPE)
    nope, r1, r2 = wq3[:, :, :NOPE], wq3[:, :, NOPE:NOPE + half], wq3[:, :, NOPE + half:]
    zq = jnp.zeros((Q_RANK, H_A, HEAD_PAD - NOPE - ROPE), w_qup.dtype)
    wq = jnp.concatenate([nope, r1, r2, zq], axis=-1).reshape(Q_RANK, H_A * HEAD_PAD).astype(BF)
    wqs = jnp.concatenate([jnp.zeros_like(nope), r2, r1, zq], axis=-1).reshape(Q_RANK, H_A * HEAD_PAD).astype(BF)
    wkv3 = w_kvup.reshape(KV_RANK, H_A, NOPE + V_A)
    wk = jnp.concatenate([wkv3[:, :, :NOPE], jnp.zeros((KV_RANK, H_A, HEAD_PAD - NOPE), w_kvup.dtype)],
                         axis=-1).reshape(KV_RANK, H_A * HEAD_PAD).astype(BF)
    wv4 = wkv3[:, :, NOPE:].reshape(KV_RANK, H_A // 2, 2, V_A)
    zv = jnp.zeros((KV_RANK, H_A // 2, HEAD_PAD - V_A), w_kvup.dtype)
    wv = jnp.concatenate([wv4[:, :, 0], zv, zv, wv4[:, :, 1]], axis=-1).reshape(KV_RANK, H_A * HEAD_PAD).astype(BF)
    vone = jnp.zeros((H_A // 2, 2 * HEAD_PAD), F32).at[:, V_A].set(1.0).at[:, HEAD_PAD].set(1.0)
    r = jnp.arange(ROPE)
    e_mat = jnp.zeros((ROPE, H_A, HEAD_PAD), F32).at[r[:, None], jnp.arange(H_A)[None, :], NOPE + r[:, None]].set(1.0)
    return w0, wq, wqs, wk, wv, vone.reshape(1, H_A * HEAD_PAD), e_mat.reshape(ROPE, H_A * HEAD_PAD).astype(BF)


def _rope_tables(pos):
    half = ROPE // 2
    inv = jnp.power(ROPE_BASE, -jnp.arange(half, dtype=F32) / half)
    ang = pos.astype(F32)[:, None] * inv[None, :]
    c, s = jnp.cos(ang), jnp.sin(ang)
    n = pos.shape[0]
    pad = jnp.zeros((n, HEAD_PAD - NOPE - ROPE), F32)
    cs = jnp.concatenate([jnp.ones((n, NOPE), F32), c, c, pad], axis=1)
    sn = jnp.concatenate([jnp.zeros((n, NOPE), F32), -s, s, pad], axis=1)
    return cs, sn


def _band_bias_tiles(table, n_rows, n_keys, key_offset, masked):
    i = jnp.arange(n_rows)[:, None]
    kpos = jnp.arange(n_keys)[None, :] - key_offset
    d_max = n_rows - 1 + key_offset
    w = n_rows + n_keys
    d = d_max - jnp.arange(w)
    rev = (table.astype(F32) * LOG2E)[:, jnp.clip(d, -REL_CLIP, REL_CLIP) + REL_CLIP]
    skew = jnp.tile(rev, (1, n_rows))[:, :n_rows * (w - 1)].reshape(-1, n_rows, w - 1)
    bias = skew[:, :, n_rows - 1:n_rows - 1 + n_keys]
    if masked:
        qc, kc = i // CHUNK, jnp.floor_divide(kpos, CHUNK)
        vis = (kc <= qc) & (kc >= qc - LEFT_CHUNKS)
        bias = jnp.where(vis[None], bias, NEG_INF)
    return bias.reshape(H_C // 2, 2 * n_rows, n_keys)


def kernel(x_prompt, x_sample, cache_mla_ckv, cache_mla_krope, cache_diff_k, cache_diff_v, cache_band_k, cache_band_v, ln_mix, w_in_even, mla_q_norm, mla_w_qup, mla_kv_norm, mla_w_kvup, diff_lam_q1, diff_lam_k1, diff_lam_q2, diff_lam_k2, diff_subln, w_out_even, w_in_odd, band_rel_bias, w_out_odd, ln_ffn, moe_w_group, moe_b_group, moe_w_router, moe_b_router, moe_w_gate, moe_w_up, moe_w_down, ln_final):
    B, S, D = x_prompt.shape
    DB, DS, _ = x_sample.shape
    n_past = cache_mla_ckv.shape[2]
    c_past = cache_band_k.shape[2]
    assert ln_mix.shape[0] == 2 and S % (2 * min(ATT_TILE, S // 2)) == 0 and S % BAND_TILE == 0
    assert (B * S) % WIDE_TILE == 0 and (DB * DS) % WIDE_TILE == 0 and ROW_TILE % DS == 0 and DS == CHUNK
    assert n_past % CHUNK == 0 and c_past == LEFT_CHUNKS * CHUNK and S >= c_past and c_past % ROW_TILE == 0
    BS, NS = B * S, DB * DS
    T = BS + NS
    n_p, n_s = BS // ROW_TILE, NS // ROW_TILE
    xp, xs = x_prompt.reshape(BS, D), x_sample.reshape(NS, D)
    row = lambda a: a.reshape(1, -1)

    w0, wq, wqs, wk, wv, vone, e_mat = _prep_even_weights(w_in_even[0], mla_w_qup[0], mla_w_kvup[0])
    cs_p, sn_p = _rope_tables(jnp.arange(S))
    cs_s, sn_s = _rope_tables(n_past + jnp.arange(DS))
    reps = ROW_TILE // DS
    even_w = (row(ln_mix[0]), w0, row(mla_q_norm[0]), wq, wqs, row(mla_kv_norm[0]), wk, wv, vone)
    qa, ka, va, ckv_p, kr_p, qd, kdb, vdb, kd_p, vd_p = _even_in(xp, *even_w, cs_p, sn_p)
    qa_s, ka_s, va_s, ckv_s, kr_s, qd_s, kdb_s, vdb_s, kd_s, vd_s = _even_in(
        xs, *even_w, jnp.tile(cs_s, (reps, 1)), jnp.tile(sn_s, (reps, 1)))
    slopes = jnp.exp2(-8.0 * jnp.arange(1, H_B + 1, dtype=F32) / H_B) * LOG2E
    lamv = jnp.stack([diff_lam_q1[0], diff_lam_k1[0], diff_lam_q2[0], diff_lam_k2[0]]).astype(F32)
    subln = row(diff_subln[0])
    lam_init = 0.8 - 0.6 * math.exp(-0.3 * 0)
    oa_p = _mla_prompt(qa, ka, va, batch=B, seq=S)
    ob_p = _diff_prompt(slopes, lamv, subln, qd, kdb, vdb, batch=B, seq=S, lam_init=lam_init)
    oa_s, ob_s = _even_sample(
        slopes, lamv, subln, wk, wv, vone, e_mat, qa_s, ka_s, va_s, cache_mla_ckv[0], cache_mla_krope[0],
        qd_s, kdb_s, vdb_s, cache_diff_k[0], cache_diff_v[0],
        n_prompt_rows=0, dec_seq=DS, lam_init=lam_init)

    def router_weights(l):
        wr = jnp.concatenate([moe_w_group[l], moe_w_router[l],
                              jnp.zeros((D, LANE - N_GROUPS - N_EXPERTS), F32)], axis=1).astype(BF)
        br = jnp.concatenate([moe_b_group[l], moe_b_router[l],
                              jnp.zeros((LANE - N_GROUPS - N_EXPERTS,), F32)]).astype(F32)
        return wr, row(br)

    n_a = H_A * V_A
    wo = w_out_even[0].astype(BF)
    wr, br = router_weights(0)
    x1, xn, route, cnt = _out_router([(xp, xs), (oa_p, oa_s), (ob_p, ob_s)], [wo[:n_a], wo[n_a:]],
                                     row(ln_ffn[0]), wr, br)
    y0, y1 = _moe(xn, route, cnt, moe_w_gate, moe_w_up, moe_w_down, layer=0)

    w_odd = w_in_odd[0].astype(BF)
    n_c = H_C * DH_C
    x2, qc, kc, vc = _odd_in(x1, y0, y1, route, row(ln_mix[1]), w_odd)
    tail = c_past // ROW_TILE
    tiles_per_seq = S // ROW_TILE
    tile_ids = jnp.concatenate(
        [jnp.arange(tiles_per_seq - tail, tiles_per_seq, dtype=jnp.int32) + b * tiles_per_seq for b in range(B)]
        + [jnp.arange(n_p, n_p + n_s, dtype=jnp.int32)])
    st = _state_rows(tile_ids, x2, row(ln_mix[1]), w_odd[:, n_c:])
    bias_p = _band_bias_tiles(band_rel_bias[0], BAND_TILE, 3 * BAND_TILE, 2 * BAND_TILE, True)
    bias_s = _band_bias_tiles(band_rel_bias[0], DS, c_past + DS, c_past, False)
    oc_p = _band_prompt(qc, kc, vc, bias_p, batch=B, seq=S)
    oc_s = _band_sample(qc, kc, vc, cache_band_k[0].reshape(DB, c_past, n_c), cache_band_v[0].reshape(DB, c_past, n_c),
                        bias_s, n_prompt_rows=BS, dec_seq=DS)
    wr, br = router_weights(1)
    x3, xn, route, cnt = _out_router([(x2,), (oc_p, oc_s)], [w_out_odd[0].astype(BF)], row(ln_ffn[1]), wr, br)
    y0, y1 = _moe(xn, route, cnt, moe_w_gate, moe_w_up, moe_w_down, layer=1)
    g_fin = row(ln_final)
    y_prompt = _final(x3, y0, y1, route, g_fin, first_tile=0, n_tiles=BS // WIDE_TILE).reshape(B, S, D)
    y_sample = _final(x3, y0, y1, route, g_fin, first_tile=BS // WIDE_TILE, n_tiles=NS // WIDE_TILE).reshape(DB, DS, D)

    def shaped(a_p, a_s, *tail_shape):
        return a_p.reshape(1, B, S, *tail_shape), a_s.reshape(1, DB, DS, *tail_shape)

    ckv_p, ckv_s = shaped(ckv_p, ckv_s, KV_RANK)
    kr_p, kr_s = shaped(kr_p, kr_s, ROPE)
    kd_p, kd_s = shaped(kd_p, kd_s, H_B, 2 * DH_B)
    vd_p, vd_s = shaped(vd_p, vd_s, H_B, V_B)
    n_tail = B * c_past
    bk_p = st[:n_tail, :n_c].reshape(1, B, c_past, H_C, DH_C)
    bv_p = st[:n_tail, n_c:].reshape(1, B, c_past, H_C, DH_C)
    k_new = st[n_tail:, :n_c].reshape(DB, DS, H_C, DH_C)
    v_new = st[n_tail:, n_c:].reshape(DB, DS, H_C, DH_C)
    bk_s = jnp.concatenate([cache_band_k[0][:, DS:], k_new], axis=1)[None]
    bv_s = jnp.concatenate([cache_band_v[0][:, DS:], v_new], axis=1)[None]
    return (y_prompt, y_sample, ckv_p, kr_p, kd_p, vd_p, bk_p, bv_p, ckv_s, kr_s, kd_s, vd_s, bk_s, bv_s)
```

```python
import functools
import math

import jax
import jax.numpy as jnp
from jax import lax
from jax.experimental import pallas as pl
from jax.experimental.pallas import tpu as pltpu

BF = jnp.bfloat16
F32 = jnp.float32
NEG_INF = float("-inf")
LOG2E = math.log2(math.e)

CHUNK = 64
NORM_EPS = 1e-6
SUBLN_EPS = 1e-5
H_A, NOPE, ROPE, V_A, Q_RANK, KV_RANK = 8, 64, 32, 64, 384, 256
ROPE_BASE = 10000.0
H_B, DH_B, V_B = 4, 64, 128
H_C, DH_C, LEFT_CHUNKS, REL_CLIP = 16, 64, 8, 128
N_GROUPS, EPG, N_EXPERTS, D_EXPERT = 4, 8, 32, 512
LANE = 128
HEAD_PAD = 128
ROUTE_OFF = N_GROUPS

ROW_TILE = 256
WIDE_TILE = 512
ROUTE_ROWS = 512
ATT_TILE = 512
SOFTMAX_ROWS = 32
BAND_TILE = 256
BAND_TILES_PER_STEP = 8
MOE_BLOCK = 512
VMEM_LIMIT = 56 * 1024 * 1024


def _cparams(*sem):
    return pltpu.CompilerParams(dimension_semantics=sem, vmem_limit_bytes=VMEM_LIMIT)


def _rms(x, g, eps):
    return x * lax.rsqrt(jnp.mean(x * x, axis=-1, keepdims=True) + eps) * g


def _dot(a, b):
    return jnp.dot(a, b, preferred_element_type=F32)


def _dot_nt(a, b):
    return lax.dot_general(a, b, (((1,), (1,)), ((), ())), preferred_element_type=F32)


def _lane_iota(shape):
    return lax.broadcasted_iota(jnp.int32, shape, len(shape) - 1)


def _split_halves(q):
    qf = q.astype(F32)
    lane = _lane_iota(qf.shape)
    return jnp.concatenate([jnp.where(lane < 64, qf, 0.0), jnp.where(lane >= 64, qf, 0.0)], axis=0).astype(BF)


def _softmax_pv(s_list, v_list):
    m = functools.reduce(jnp.maximum, [jnp.max(s, axis=-1, keepdims=True) for s in s_list])
    acc, l = None, None
    for s, v in zip(s_list, v_list):
        p = jnp.exp2(s - m)
        ls = jnp.sum(p, axis=-1, keepdims=True)
        a = _dot(p.astype(BF), v)
        l = ls if l is None else l + ls
        acc = a if acc is None else acc + a
    return acc / l


def _diff_lambda(lamv, lam_init):
    a = jnp.exp(jnp.sum(lamv[0:1] * lamv[1:2], axis=-1, keepdims=True))
    b = jnp.exp(jnp.sum(lamv[2:3] * lamv[3:4], axis=-1, keepdims=True))
    return a - b + lam_init


def _diff_finish(o1, o2, lam, subln, lam_init):
    o = o1 - lam * o2
    return _rms(o, subln, SUBLN_EPS) * (1.0 - lam_init)


def _even_in_kernel(x_ref, g_ref, w0_ref, qn_ref, wq_ref, wqs_ref, kvn_ref, wk_ref, wv_ref, vone_ref,
                    cs_ref, sn_ref,
                    qa_ref, ka_ref, va_ref, ckv_ref, kr_ref, qd_ref, kdb_ref, vdb_ref, kd_ref, vd_ref,
                    *, a_scale, b_scale):
    h = _rms(x_ref[...], g_ref[...], NORM_EPS).astype(BF)
    y = _dot(h, w0_ref[...])
    cq, ckv = y[:, 0:384], y[:, 384:640]
    kr_m, kr_s = y[:, 640:768], y[:, 768:896]
    qd, kd, vd = y[:, 896:1408], y[:, 1408:1920], y[:, 1920:2432]
    cs, sn = cs_ref[...], sn_ref[...]
    cqn = _rms(cq, qn_ref[...], NORM_EPS).astype(BF)
    qm = _dot(cqn, wq_ref[...])
    qs = _dot(cqn, wqs_ref[...])
    ckvn = _rms(ckv, kvn_ref[...], NORM_EPS)
    ckv_ref[...] = ckvn
    krp = kr_m * cs + kr_s * sn
    kr_ref[...] = krp[:, NOPE:NOPE + ROPE]
    cb = ckvn.astype(BF)
    kn = _dot(cb, wk_ref[...])
    for hh in range(H_A):
        sl = slice(hh * HEAD_PAD, (hh + 1) * HEAD_PAD)
        qa_ref[:, sl] = ((qm[:, sl] * cs + qs[:, sl] * sn) * a_scale).astype(BF)
        ka_ref[:, sl] = (kn[:, sl] + krp).astype(BF)
    va_ref[...] = (_dot(cb, wv_ref[...]) + vone_ref[...]).astype(BF)
    qd_ref[...] = (qd * b_scale).astype(BF)
    for hh in range(H_B):
        kd_ref[:, hh, :] = kd[:, hh * V_B:(hh + 1) * V_B]
        vd_ref[:, hh, :] = vd[:, hh * V_B:(hh + 1) * V_B]
    kdb_ref[...] = kd.astype(BF)
    vdb_ref[...] = vd.astype(BF)


def _even_in(x, g, w0, qn, wq, wqs, kvn, wk, wv, vone, cs_tab, sn_tab):
    T, D = x.shape
    pos_blocks = cs_tab.shape[0] // ROW_TILE

    def full(a):
        return pl.BlockSpec(a.shape, lambda i: (0,) * a.ndim)

    def rows(*tail):
        return pl.BlockSpec((ROW_TILE,) + tail, lambda i: (i,) + (0,) * len(tail))

    pos_spec = pl.BlockSpec((ROW_TILE, LANE), lambda i: (i % pos_blocks, 0))
    outs = [((1024,), BF), ((1024,), BF), ((1024,), BF), ((KV_RANK,), F32), ((ROPE,), F32),
            ((512,), BF), ((512,), BF), ((512,), BF), ((H_B, V_B), F32), ((H_B, V_B), F32)]
    return pl.pallas_call(
        functools.partial(_even_in_kernel,
                          a_scale=(NOPE + ROPE) ** -0.5 * LOG2E, b_scale=DH_B ** -0.5 * LOG2E),
        grid=(T // ROW_TILE,),
        in_specs=[rows(D), full(g), full(w0), full(qn), full(wq), full(wqs), full(kvn), full(wk), full(wv),
                  full(vone), pos_spec, pos_spec],
        out_specs=[rows(*tail) for tail, _ in outs],
        out_shape=[jax.ShapeDtypeStruct((T,) + tail, dt) for tail, dt in outs],
        compiler_params=_cparams("parallel"),
        name="even_in",
    )(x, g, w0, qn, wq, wqs, kvn, wk, wv, vone, cs_tab, sn_tab)


def _chunk_causal_mask(tq, tk):
    row = lax.broadcasted_iota(jnp.int32, (tq, tk), 0)
    col = lax.broadcasted_iota(jnp.int32, (tq, tk), 1)
    return (col // CHUNK) <= (row // CHUNK)


def _softmax_tile(s_ref, p_ref, m_ref, l_ref, a_ref, *, add_ref=None, off=None):
    tile = s_ref.shape[0]
    reps = tile // LANE
    blocks = [slice(rb * SOFTMAX_ROWS, (rb + 1) * SOFTMAX_ROWS) for rb in range(tile // SOFTMAX_ROWS)]

    def biased(rs):
        s = s_ref[rs, :]
        if add_ref is not None:
            s = s + add_ref[rs, :]
        return s

    for rs in blocks:
        m_old = m_ref[rs, :]
        red = jnp.broadcast_to(jnp.max(biased(rs), axis=-1, keepdims=True), m_old.shape)
        if off is not None:
            red = red + off
        m_new = jnp.maximum(m_old, red)
        a_ref[rs, :] = jnp.exp2(m_old - m_new)
        m_ref[rs, :] = m_new
    for rs in blocks:
        m_new = m_ref[rs, :]
        shift = m_new if off is None else m_new - off
        p = jnp.exp2(biased(rs) - jnp.concatenate([shift] * reps, axis=1))
        if l_ref is not None:
            l_ref[rs, :] = a_ref[rs, :] * l_ref[rs, :] + jnp.broadcast_to(
                jnp.sum(p, axis=-1, keepdims=True), m_new.shape)
        p_ref[rs, :] = p.astype(BF)


def _flash_causal(qi, bufs, scores, update):
    s0, s1, p0, p1, m_ref, l_ref, a_ref, acc_ref = bufs
    m_ref[...] = jnp.full(m_ref.shape, NEG_INF, F32)
    l_ref[...] = jnp.zeros(l_ref.shape, F32)
    acc_ref[...] = jnp.zeros(acc_ref.shape, F32)
    scores(0, s0)

    def pair(jj, carry):
        j = 2 * jj
        scores(j + 1, s1)
        update(j, s0, p0, False)
        scores(j + 2, s0)
        update(j + 1, s1, p1, False)
        return carry

    lax.fori_loop(0, qi // 2, pair, 0)

    @pl.when(qi % 2 == 1)
    def _():
        scores(qi, s1)
        update(qi - 1, s0, p0, False)
        s0[...] = s1[...]

    update(qi, s0, p0, True)


def _flash_causal_pair(t, bufs, scores, update):
    s0, s1, p0, p1, m_ref, l_ref, a_ref, acc_ref = bufs
    m_ref[...] = jnp.full(m_ref.shape, NEG_INF, F32)
    l_ref[...] = jnp.zeros(l_ref.shape, F32)
    acc_ref[...] = jnp.zeros(acc_ref.shape, F32)
    both = (0, 1)
    scores(0, s0, both)

    def pair(jj, carry):
        j = 2 * jj
        scores(j + 1, s1, both)
        update(j, s0, p0, both, ())
        scores(j + 2, s0, both)
        update(j + 1, s1, p1, both, ())
        return carry

    lax.fori_loop(0, t, pair, 0)
    scores(2 * t + 1, s1, (1,))
    update(2 * t, s0, p0, both, (0,))
    update(2 * t + 1, s1, p1, (1,), (1,))


def _flash_scratch(n_chains, rows, tile, acc_width=LANE):
    s = pltpu.VMEM((n_chains, rows, tile), F32)
    p = pltpu.VMEM((n_chains, rows, tile), BF)
    stat = pltpu.VMEM((n_chains, rows, LANE), F32)
    return [s, s, p, p, stat, stat, stat, pltpu.VMEM((n_chains, rows, acc_width), F32)]


def _rows_of(qts, tile):
    return slice(qts[0] * tile, (qts[-1] + 1) * tile)


def _mla_prompt_kernel(q_ref, k_ref, v_ref, o_ref, s0, s1, p0, p1, m_ref, l_ref, a_ref, acc_ref, dmask_ref,
                       *, tile):
    qi = pl.program_id(2)

    @pl.when(qi == 0)
    def _():
        dmask_ref[...] = jnp.where(_chunk_causal_mask(tile, tile), 0.0, NEG_INF)

    sls = [slice(hh * HEAD_PAD, (hh + 1) * HEAD_PAD) for hh in range(2)]

    def scores(j, s_buf):
        start = pl.multiple_of(j * tile, tile)
        for c, sl in enumerate(sls):
            s_buf[c] = _dot_nt(q_ref[:, sl], k_ref[pl.ds(start, tile), sl])

    def update(j, s_buf, p_buf, diag):
        start = pl.multiple_of(j * tile, tile)
        for c, sl in enumerate(sls):
            _softmax_tile(s_buf.at[c], p_buf.at[c], m_ref.at[c], None, a_ref.at[c],
                          add_ref=dmask_ref if diag else None)
            acc_ref[c] = a_ref[c] * acc_ref[c] + _dot(p_buf[c], v_ref[pl.ds(start, tile), sl])

    _flash_causal(qi, (s0, s1, p0, p1, m_ref, l_ref, a_ref, acc_ref), scores, update)
    lane = _lane_iota((tile, LANE))
    a0, a1 = acc_ref[0], acc_ref[1]
    l0 = jnp.sum(jnp.where(lane == V_A, a0, 0.0), axis=-1, keepdims=True)
    l1 = jnp.sum(jnp.where(lane == 0, a1, 0.0), axis=-1, keepdims=True)
    o_ref[...] = jnp.where(lane < V_A, a0 / l0, a1 / l1).astype(BF)


def _mla_prompt(qa, ka, va, *, batch, seq):
    tile = min(ATT_TILE, seq)
    nq = seq // tile
    return pl.pallas_call(
        functools.partial(_mla_prompt_kernel, tile=tile),
        grid=(batch, H_A // 2, nq),
        in_specs=[pl.BlockSpec((tile, 2 * HEAD_PAD), lambda b, p, qi: (b * nq + qi, p)),
                  pl.BlockSpec((seq, 2 * HEAD_PAD), lambda b, p, qi: (b, p)),
                  pl.BlockSpec((seq, 2 * HEAD_PAD), lambda b, p, qi: (b, p))],
        out_specs=pl.BlockSpec((tile, LANE), lambda b, p, qi: (b * nq + qi, p)),
        out_shape=jax.ShapeDtypeStruct((batch * seq, H_A * V_A), BF),
        scratch_shapes=_flash_scratch(2, tile, tile) + [pltpu.VMEM((tile, tile), F32)],
        compiler_params=_cparams("parallel", "parallel", "arbitrary"),
        name="mla_prompt",
    )(qa, ka, va)


def _diff_prompt_kernel(slopes_ref, lamv_ref, subln_ref, q_ref, k_ref, v_ref, o_ref,
                        s0, s1, p0, p1, m_ref, l_ref, a_ref, acc_ref, key_ref, dbias_ref, q2_ref,
                        *, tile, lam_init):
    h = pl.program_id(1)
    qi = pl.program_id(2)
    slope = slopes_ref[h]
    q2_ref[...] = _split_halves(q_ref[...])
    @pl.when(qi == 0)
    def _():
        key_ref[...] = slope * lax.broadcasted_iota(jnp.int32, (8, tile), 1).astype(F32)
        row = lax.broadcasted_iota(jnp.int32, (tile, tile), 0)
        col = lax.broadcasted_iota(jnp.int32, (tile, tile), 1)
        later = (2.0 * slope) * jnp.minimum(row - col, 0).astype(F32)
        dbias_ref[...] = jnp.where(_chunk_causal_mask(tile, tile), later, NEG_INF)

    n_rows = 2 * tile

    def scores(j, s_buf, qts):
        k = k_ref[pl.ds(pl.multiple_of(j * tile, tile), tile), :]
        rows = _rows_of(qts, tile)
        key_term = jnp.concatenate([key_ref[...]] * ((rows.stop - rows.start) // 8), axis=0)
        for c in range(2):
            q = q2_ref[c * n_rows + rows.start:c * n_rows + rows.stop, :]
            s_buf[c, rows, :] = _dot_nt(q, k) + key_term

    def update(j, s_buf, p_buf, qts, diag_qts):
        start = pl.multiple_of(j * tile, tile)
        off = slope * (j * tile).astype(F32)
        rows = _rows_of(qts, tile)
        for c in range(2):
            for qt in qts:
                r = _rows_of((qt,), tile)
                _softmax_tile(s_buf.at[c, r], p_buf.at[c, r], m_ref.at[c, r], l_ref.at[c, r], a_ref.at[c, r],
                              add_ref=dbias_ref if qt in diag_qts else None, off=off)
            acc_ref[c, rows, :] = (a_ref[c, rows, :] * acc_ref[c, rows, :]
                                   + _dot(p_buf[c, rows, :], v_ref[pl.ds(start, tile), :]))

    _flash_causal_pair(qi, (s0, s1, p0, p1, m_ref, l_ref, a_ref, acc_ref), scores, update)
    lam = _diff_lambda(lamv_ref[...], lam_init)
    o_ref[...] = _diff_finish(acc_ref[0] / l_ref[0], acc_ref[1] / l_ref[1], lam, subln_ref[...],
                              lam_init).astype(BF)


def _diff_prompt(slopes, lamv, subln, qd, kdb, vdb, *, batch, seq, lam_init):
    tile = min(ATT_TILE, seq // 2)
    nq = seq // (2 * tile)
    return pl.pallas_call(
        functools.partial(_diff_prompt_kernel, tile=tile, lam_init=lam_init),
        grid=(batch, H_B, nq),
        in_specs=[pl.BlockSpec(memory_space=pltpu.SMEM),
                  pl.BlockSpec(lamv.shape, lambda b, h, qi: (0, 0)),
                  pl.BlockSpec(subln.shape, lambda b, h, qi: (0, 0)),
                  pl.BlockSpec((2 * tile, LANE), lambda b, h, qi: (b * nq + qi, h)),
                  pl.BlockSpec((seq, LANE), lambda b, h, qi: (b, h)),
                  pl.BlockSpec((seq, LANE), lambda b, h, qi: (b, h))],
        out_specs=pl.BlockSpec((2 * tile, LANE), lambda b, h, qi: (b * nq + qi, h)),
        out_shape=jax.ShapeDtypeStruct((batch * seq, H_B * V_B), BF),
        scratch_shapes=_flash_scratch(2, 2 * tile, tile) + [pltpu.VMEM((8, tile), F32), pltpu.VMEM((tile, tile), F32),
                                                  pltpu.VMEM((4 * tile, LANE), BF)],
        compiler_params=_cparams("parallel", "parallel", "arbitrary"),
        name="diff_prompt",
    )(slopes, lamv, subln, qd, kdb, vdb)


def _even_sample_kernel(slopes_ref, lamv_ref, subln_ref, wk_ref, wv_ref, vone_ref, e_ref,
                        qa_ref, ka_ref, va_ref, ckv_ref, kr_ref,
                        qd_ref, kdb_ref, vdb_ref, ck_ref, cv_ref,
                        oa_ref, ob_ref, *, lam_init):
    n_new = qa_ref.shape[0]
    n_past = ckv_ref.shape[0]
    ckvp = ckv_ref[...].astype(BF)
    krp = kr_ref[...].astype(BF)
    lane = _lane_iota((n_new, LANE))
    for pr in range(H_A // 2):
        res = []
        for hh in range(2):
            sl = slice((2 * pr + hh) * HEAD_PAD, (2 * pr + hh + 1) * HEAD_PAD)
            q = qa_ref[:, sl]
            kp = (_dot(ckvp, wk_ref[:, sl]) + _dot(krp, e_ref[:, sl])).astype(BF)
            vp = (_dot(ckvp, wv_ref[:, sl]) + vone_ref[:, sl]).astype(BF)
            res.append(_softmax_pv([_dot_nt(q, kp), _dot_nt(q, ka_ref[:, sl])], [vp, va_ref[:, sl]]))
        oa_ref[:, pr * LANE:(pr + 1) * LANE] = jnp.where(lane < V_A, res[0], res[1]).astype(BF)
    rowp = lax.broadcasted_iota(jnp.int32, (n_new, n_past), 0)
    colp = lax.broadcasted_iota(jnp.int32, (n_new, n_past), 1)
    dist_p = (rowp - colp + n_past).astype(F32)
    dist_p = jnp.concatenate([dist_p, dist_p], axis=0)
    rown = lax.broadcasted_iota(jnp.int32, (n_new, n_new), 0)
    coln = lax.broadcasted_iota(jnp.int32, (n_new, n_new), 1)
    dist_n = jnp.abs(rown - coln).astype(F32)
    dist_n = jnp.concatenate([dist_n, dist_n], axis=0)
    lam = _diff_lambda(lamv_ref[...], lam_init)
    for h in range(H_B):
        sl = slice(h * LANE, (h + 1) * LANE)
        slope = slopes_ref[h]
        q2x = _split_halves(qd_ref[:, sl])
        kp = ck_ref[:, h, :].astype(BF)
        vp = cv_ref[:, h, :].astype(BF)
        s_p = _dot_nt(q2x, kp) - slope * dist_p
        s_n = _dot_nt(q2x, kdb_ref[:, sl]) - slope * dist_n
        o = _softmax_pv([s_p, s_n], [vp, vdb_ref[:, sl]])
        ob_ref[:, sl] = _diff_finish(o[:n_new], o[n_new:], lam, subln_ref[...], lam_init).astype(BF)


def _even_sample(slopes, lamv, subln, wk, wv, vone, e_mat, qa, ka, va, ckv_c, kr_c, qd, kdb, vdb, ck_c, cv_c,
                 *, n_prompt_rows, dec_seq, lam_init):
    dec_batch, n_past = ckv_c.shape[0], ckv_c.shape[1]
    base = n_prompt_rows // dec_seq

    def full(a):
        return pl.BlockSpec(a.shape, lambda s: (0,) * a.ndim)

    def new(width):
        return pl.BlockSpec((dec_seq, width), lambda s: (base + s, 0))

    def cache(*tail):
        return pl.BlockSpec((None, n_past) + tail, lambda s: (s, 0) + (0,) * len(tail))

    return pl.pallas_call(
        functools.partial(_even_sample_kernel, lam_init=lam_init),
        grid=(dec_batch,),
        in_specs=[pl.BlockSpec(memory_space=pltpu.SMEM), full(lamv), full(subln), full(wk), full(wv), full(vone),
                  full(e_mat), new(1024), new(1024), new(1024), cache(KV_RANK), cache(ROPE),
                  new(512), new(512), new(512), cache(H_B, V_B), cache(H_B, V_B)],
        out_specs=[pl.BlockSpec((dec_seq, 512), lambda s: (s, 0))] * 2,
        out_shape=[jax.ShapeDtypeStruct((dec_batch * dec_seq, 512), BF)] * 2,
        compiler_params=_cparams("parallel"),
        name="even_sample",
    )(slopes, lamv, subln, wk, wv, vone, e_mat, qa, ka, va, ckv_c, kr_c, qd, kdb, vdb, ck_c, cv_c)


def _route(logits, carry, live):
    tm = logits.shape[0]
    lane = _lane_iota(logits.shape).astype(F32)
    big = float(LANE)
    g_mask = lane < N_GROUPS
    gl = jnp.where(g_mask, logits, NEG_INF)
    gmax = jnp.max(gl, axis=-1, keepdims=True)
    g_sel = jnp.min(jnp.where(gl == gmax, lane, big), axis=-1, keepdims=True)
    p_grp = 1.0 / jnp.sum(jnp.exp(gl - gmax), axis=-1, keepdims=True)
    lo = ROUTE_OFF + EPG * g_sel
    el = jnp.where((lane >= lo) & (lane < lo + EPG), logits, NEG_INF)
    v1 = jnp.max(el, axis=-1, keepdims=True)
    i1 = jnp.min(jnp.where(el == v1, lane, big), axis=-1, keepdims=True)
    el2 = jnp.where(lane == i1, NEG_INF, el)
    v2 = jnp.max(el2, axis=-1, keepdims=True)
    i2 = jnp.min(jnp.where(el2 == v2, lane, big), axis=-1, keepdims=True)
    ex = jnp.exp(v2 - v1)
    den = 1.0 + ex
    gate1 = (1.0 / den) * p_grp
    gate2 = (ex / den) * p_grp
    onehot = jnp.where((lane == i1) | (lane == i2), 1.0, 0.0)
    row = lax.broadcasted_iota(jnp.int32, (tm, tm), 0)
    col = lax.broadcasted_iota(jnp.int32, (tm, tm), 1)
    tri = jnp.where(row > col, 1.0, 0.0).astype(BF)
    cum = _dot(tri, onehot.astype(BF)) + carry
    r1 = jnp.sum(jnp.where(lane == i1, cum, 0.0), axis=-1, keepdims=True)
    r2 = jnp.sum(jnp.where(lane == i2, cum, 0.0), axis=-1, keepdims=True)
    packed = jnp.zeros_like(logits)
    for pos, val in enumerate([i1 - ROUTE_OFF, i2 - ROUTE_OFF, gate1, gate2, r1, r2]):
        packed = jnp.where(lane == pos, val, packed)
    return packed, carry + live * jnp.sum(onehot, axis=0, keepdims=True)


def _out_router_kernel(*refs, splits, n_prompt_tiles):
    i = pl.program_id(0)
    offs = [sum(splits[:k]) for k in range(len(splits))]
    n_mix = len(splits) - 1

    def pick(k):
        parts = refs[offs[k]:offs[k] + splits[k]]
        if splits[k] == 1:
            return parts[0][...]
        return jnp.where(i < n_prompt_tiles, parts[0][...], parts[1][...])

    rest = refs[sum(splits):]
    w_refs = rest[:n_mix]
    g_ref, wr_ref, br_ref, x1_ref, xn_ref, route_ref, cnt_ref, carry_ref, logits_ref = rest[n_mix:]

    @pl.when(i == 0)
    def _():
        carry_ref[...] = jnp.zeros_like(carry_ref)
        logits_ref[...] = jnp.zeros_like(logits_ref)

    prev = logits_ref[...]
    x1 = pick(0)
    for k, w_ref in enumerate(w_refs):
        x1 = x1 + _dot(pick(1 + k), w_ref[...])
    x1_ref[...] = x1
    xb = _rms(x1, g_ref[...], NORM_EPS).astype(BF)
    xn_ref[...] = xb
    logits_ref[...] = _dot(xb, wr_ref[...]) + br_ref[...]
    live = jnp.where(i > 0, 1.0, 0.0)
    carry = carry_ref[...]
    for r0 in range(0, prev.shape[0], ROUTE_ROWS):
        packed, carry = _route(prev[r0:r0 + ROUTE_ROWS], carry, live)
        route_ref[r0:r0 + ROUTE_ROWS, :] = packed
    carry_ref[...] = carry
    cnt_ref[...] = carry


def _out_router(row_inputs, ws, g, wr, br):
    splits = tuple(len(parts) for parts in row_inputs)
    T = sum(a.shape[0] for a in row_inputs[0])
    D = row_inputs[0][0].shape[1]
    tile = WIDE_TILE
    n = T // tile
    n_p = max([parts[0].shape[0] // tile for parts in row_inputs if len(parts) == 2], default=0)

    def row_specs(parts):
        if len(parts) == 1:
            return [pl.BlockSpec((tile, parts[0].shape[1]), lambda i: (jnp.minimum(i, n - 1), 0))]
        n_s = parts[1].shape[0] // tile
        return [pl.BlockSpec((tile, parts[0].shape[1]), lambda i: (jnp.minimum(i, n_p - 1), 0)),
                pl.BlockSpec((tile, parts[1].shape[1]), lambda i: (jnp.clip(i - n_p, 0, n_s - 1), 0))]

    def full(a):
        return pl.BlockSpec(a.shape, lambda i: (0,) * a.ndim)

    def rows(width, lag=0):
        return pl.BlockSpec((tile, width), lambda i: (jnp.clip(i - lag, 0, n - 1), 0))

    flat_rows = [a for parts in row_inputs for a in parts]
    return pl.pallas_call(
        functools.partial(_out_router_kernel, splits=splits, n_prompt_tiles=n_p),
        grid=(n + 1,),
        in_specs=[s for parts in row_inputs for s in row_specs(parts)] + [full(w) for w in ws]
        + [full(g), full(wr), full(br)],
        out_specs=[rows(D), rows(D), rows(LANE, lag=1), pl.BlockSpec((1, LANE), lambda i: (0, 0))],
        out_shape=[jax.ShapeDtypeStruct((T, D), F32), jax.ShapeDtypeStruct((T, D), BF),
                   jax.ShapeDtypeStruct((T, LANE), F32), jax.ShapeDtypeStruct((1, LANE), F32)],
        scratch_shapes=[pltpu.VMEM((1, LANE), F32), pltpu.VMEM((tile, LANE), F32)],
        compiler_params=_cparams("arbitrary"),
        name="out_router",
    )(*flat_rows, *ws, g, wr, br)


def _experts_kernel(be_ref, nu_ref, xb_ref, wg_ref, wu_ref, wd_ref, y_ref, wgu_ref, wdb_ref):
    i = pl.program_id(0)
    used = i < nu_ref[0]
    n_hid = wg_ref.shape[1]

    @pl.when(used & ((i == 0) | (be_ref[i] != be_ref[jnp.maximum(i - 1, 0)])))
    def _():
        wgu_ref[:, :n_hid] = wg_ref[...].astype(BF)
        wgu_ref[:, n_hid:] = wu_ref[...].astype(BF)
        wdb_ref[...] = wd_ref[...].astype(BF)

    @pl.when(used)
    def _():
        ab = _dot(xb_ref[...], wgu_ref[...])
        a, b = ab[:, :n_hid], ab[:, n_hid:]
        hid = (a * jax.nn.sigmoid(a)) * b
        y_ref[...] = _dot(hid.astype(BF), wdb_ref[...]).astype(y_ref.dtype)

    @pl.when(jnp.logical_not(used))
    def _():
        y_ref[...] = jnp.zeros_like(y_ref)


def _experts(block_expert, n_used, xb, wg, wu, wd, *, layer):
    L, D = xb.shape
    n_blocks = L // MOE_BLOCK
    grid_spec = pltpu.PrefetchScalarGridSpec(
        num_scalar_prefetch=2,
        grid=(n_blocks,),
        in_specs=[pl.BlockSpec((MOE_BLOCK, D), lambda i, be, nu: (i, 0)),
                  pl.BlockSpec((None, None, D, D_EXPERT), lambda i, be, nu: (layer, be[i], 0, 0)),
                  pl.BlockSpec((None, None, D, D_EXPERT), lambda i, be, nu: (layer, be[i], 0, 0)),
                  pl.BlockSpec((None, None, D_EXPERT, D), lambda i, be, nu: (layer, be[i], 0, 0))],
        out_specs=pl.BlockSpec((MOE_BLOCK, D), lambda i, be, nu: (i, 0)),
        scratch_shapes=[pltpu.VMEM((D, 2 * D_EXPERT), BF), pltpu.VMEM((D_EXPERT, D), BF)],
    )
    return pl.pallas_call(
        _experts_kernel,
        grid_spec=grid_spec,
        out_shape=jax.ShapeDtypeStruct((L, D), BF),
        compiler_params=_cparams("arbitrary"),
        name="experts",
    )(block_expert, n_used, xb, wg, wu, wd)


def _moe(xn, route, cnt, wg, wu, wd, *, layer):
    T = xn.shape[0]
    counts = cnt[0, ROUTE_OFF:ROUTE_OFF + N_EXPERTS].astype(jnp.int32)
    padded = ((counts + MOE_BLOCK - 1) // MOE_BLOCK) * MOE_BLOCK
    pend = jnp.cumsum(padded)
    pstart = pend - padded

    def slot_rows(k):
        e, rank = route[:, k].astype(jnp.int32), route[:, 4 + k].astype(jnp.int32)
        return jnp.sum(jnp.where(e[:, None] == jnp.arange(N_EXPERTS)[None, :], pstart[None, :], 0), axis=-1) + rank

    dest = [slot_rows(0), slot_rows(1)]
    n_blocks = -(-(2 * T) // MOE_BLOCK) + N_EXPERTS
    L = n_blocks * MOE_BLOCK
    tok = jnp.arange(T, dtype=jnp.int32)
    buf_tok = (jnp.arange(L, dtype=jnp.int32) % T).at[jnp.concatenate(dest)].set(
        jnp.concatenate([tok, tok]), unique_indices=True, mode="promise_in_bounds")
    block_start = jnp.arange(n_blocks, dtype=jnp.int32) * MOE_BLOCK
    block_expert = jnp.minimum(jnp.sum(pend[None, :] <= block_start[:, None], axis=1), N_EXPERTS - 1).astype(jnp.int32)
    n_used = (pend[-1:] // MOE_BLOCK).astype(jnp.int32)
    xb = xn.at[buf_tok].get(mode="promise_in_bounds")
    yb = _experts(block_expert, n_used, xb, wg, wu, wd, layer=layer)
    return (yb.at[dest[0]].get(mode="promise_in_bounds"), yb.at[dest[1]].get(mode="promise_in_bounds"))


def _gates(route):
    lane = _lane_iota(route.shape)
    g0 = jnp.sum(jnp.where(lane == 2, route, 0.0), axis=-1, keepdims=True)
    g1 = jnp.sum(jnp.where(lane == 3, route, 0.0), axis=-1, keepdims=True)
    return g0, g1


def _odd_in_kernel(x_ref, y0_ref, y1_ref, route_ref, g_ref, w_ref, x2_ref, q_ref, k_ref, v_ref, *, scale):
    g0, g1 = _gates(route_ref[...])
    x2 = x_ref[...] + (y0_ref[...].astype(F32) * g0 + y1_ref[...].astype(F32) * g1)
    x2_ref[...] = x2
    h = _rms(x2, g_ref[...], NORM_EPS).astype(BF)
    y = _dot(h, w_ref[...])
    n = q_ref.shape[1]
    q_ref[...] = (y[:, :n] * scale).astype(BF)
    k_ref[...] = y[:, n:2 * n].astype(BF)
    v_ref[...] = y[:, 2 * n:].astype(BF)


def _odd_in(x1, y0, y1, route, g, w):
    T, D = x1.shape
    n = w.shape[1] // 3

    def rows(width):
        return pl.BlockSpec((WIDE_TILE, width), lambda i: (i, 0))

    def full(a):
        return pl.BlockSpec(a.shape, lambda i: (0,) * a.ndim)

    return pl.pallas_call(
        functools.partial(_odd_in_kernel, scale=DH_C ** -0.5 * LOG2E),
        grid=(T // WIDE_TILE,),
        in_specs=[rows(D), rows(D), rows(D), rows(LANE), full(g), full(w)],
        out_specs=[rows(D), rows(n), rows(n), rows(n)],
        out_shape=[jax.ShapeDtypeStruct((T, D), F32)] + [jax.ShapeDtypeStruct((T, n), BF)] * 3,
        compiler_params=_cparams("parallel"),
        name="odd_in",
    )(x1, y0, y1, route, g, w)


def _state_rows_kernel(ids_ref, x_ref, g_ref, w_ref, o_ref):
    del ids_ref
    h = _rms(x_ref[...], g_ref[...], NORM_EPS).astype(BF)
    o_ref[...] = _dot(h, w_ref[...])


def _state_rows(tile_ids, x, g, w):
    D = x.shape[1]
    n = tile_ids.shape[0]
    grid_spec = pltpu.PrefetchScalarGridSpec(
        num_scalar_prefetch=1,
        grid=(n,),
        in_specs=[pl.BlockSpec((ROW_TILE, D), lambda i, ids: (ids[i], 0)),
                  pl.BlockSpec(g.shape, lambda i, ids: (0, 0)),
                  pl.BlockSpec(w.shape, lambda i, ids: (0, 0))],
        out_specs=pl.BlockSpec((ROW_TILE, w.shape[1]), lambda i, ids: (i, 0)),
    )
    return pl.pallas_call(
        _state_rows_kernel,
        grid_spec=grid_spec,
        out_shape=jax.ShapeDtypeStruct((n * ROW_TILE, w.shape[1]), F32),
        compiler_params=_cparams("parallel"),
        name="state_rows",
    )(tile_ids, x, g, w)


def _band_prompt_kernel(q_ref, k_ref, v_ref, bias_ref, o_ref, q2_ref, s0, s1, p_ref, l_ref,
                        *, tile, tiles_per_step):
    lane = _lane_iota((tile, LANE))
    q2_ref[...] = _split_halves(q_ref[...])
    n_rows = tile * tiles_per_step

    def key_tiles(t):
        qi = pl.program_id(2) * tiles_per_step + t
        out = []
        for kt in range(3):
            start = (qi + kt - 2) * tile
            neg = jnp.where(start >= 0, 0.0, NEG_INF)
            out.append((pl.multiple_of(jnp.maximum(start, 0), tile), neg))
        return out

    def scores(t, s_buf):
        for hh in range(2):
            q = q2_ref[hh * n_rows + t * tile:hh * n_rows + (t + 1) * tile, :]
            for kt, (start, neg) in enumerate(key_tiles(t)):
                s = _dot_nt(q, k_ref[pl.ds(start, tile), :])
                s_buf[hh, :, kt * tile:(kt + 1) * tile] = s + neg if kt < 2 else s

    def softmax_pv(t, s_buf):
        outs = []
        slot = t % 2
        for hh in range(2):
            for rb in range(tile // SOFTMAX_ROWS):
                rs = slice(rb * SOFTMAX_ROWS, (rb + 1) * SOFTMAX_ROWS)
                s = s_buf[hh, rs, :] + bias_ref[hh * tile + rb * SOFTMAX_ROWS:hh * tile + (rb + 1) * SOFTMAX_ROWS, :]
                p = jnp.exp2(s - jnp.max(s, axis=-1, keepdims=True))
                l_ref[slot, hh, rs, :] = jnp.broadcast_to(jnp.sum(p, axis=-1, keepdims=True), (SOFTMAX_ROWS, LANE))
                p_ref[slot, hh, rs, :] = p.astype(BF)
            acc = None
            for kt, (start, _) in enumerate(key_tiles(t)):
                pv = _dot(p_ref[slot, hh, :, kt * tile:(kt + 1) * tile], v_ref[pl.ds(start, tile), :])
                acc = pv if acc is None else acc + pv
            outs.append(acc / l_ref[slot, hh])
        o_ref[t * tile:(t + 1) * tile, :] = jnp.where(lane < DH_C, outs[0], outs[1]).astype(BF)

    bufs = [s0, s1]
    scores(0, bufs[0])
    for t in range(tiles_per_step):
        if t + 1 < tiles_per_step:
            scores(t + 1, bufs[(t + 1) % 2])
        softmax_pv(t, bufs[t % 2])


def _band_prompt(q, k, v, bias, *, batch, seq):
    tile = BAND_TILE
    tps = min(BAND_TILES_PER_STEP, seq // tile)
    nq = seq // (tile * tps)
    return pl.pallas_call(
        functools.partial(_band_prompt_kernel, tile=tile, tiles_per_step=tps),
        grid=(H_C // 2, batch, nq),
        in_specs=[pl.BlockSpec((tile * tps, LANE), lambda p, b, qi: (b * nq + qi, p)),
                  pl.BlockSpec((seq, LANE), lambda p, b, qi: (b, p)),
                  pl.BlockSpec((seq, LANE), lambda p, b, qi: (b, p)),
                  pl.BlockSpec((None, 2 * tile, 3 * tile), lambda p, b, qi: (p, 0, 0))],
        out_specs=pl.BlockSpec((tile * tps, LANE), lambda p, b, qi: (b * nq + qi, p)),
        out_shape=jax.ShapeDtypeStruct((batch * seq, H_C * DH_C), BF),
        scratch_shapes=[pltpu.VMEM((2 * tile * tps, LANE), BF),
                        pltpu.VMEM((2, tile, 3 * tile), F32), pltpu.VMEM((2, tile, 3 * tile), F32),
                        pltpu.VMEM((2, 2, tile, 3 * tile), BF), pltpu.VMEM((2, 2, tile, LANE), F32)],
        compiler_params=_cparams("parallel", "parallel", "arbitrary"),
        name="band_prompt",
    )(q, k, v, bias)


def _band_sample_kernel(q_ref, k_ref, v_ref, ck_ref, cv_ref, bias_ref, o_ref):
    n_new = q_ref.shape[0]
    n_past = ck_ref.shape[0]
    lane = _lane_iota((n_new, LANE))
    for pr in range(H_C // 2):
        sl = slice(pr * LANE, (pr + 1) * LANE)
        q2x = _split_halves(q_ref[:, sl])
        s_p = _dot_nt(q2x, ck_ref[:, sl].astype(BF)) + bias_ref[pr, :, 0:n_past]
        s_n = _dot_nt(q2x, k_ref[:, sl]) + bias_ref[pr, :, n_past:n_past + n_new]
        o = _softmax_pv([s_p, s_n], [cv_ref[:, sl].astype(BF), v_ref[:, sl]])
        o_ref[:, sl] = jnp.where(lane < DH_C, o[:n_new], o[n_new:]).astype(BF)


def _band_sample(q, k, v, ck, cv, bias, *, n_prompt_rows, dec_seq):
    dec_batch, n_past, width = ck.shape
    base = n_prompt_rows // dec_seq
    new = pl.BlockSpec((dec_seq, width), lambda s: (base + s, 0))
    cache = pl.BlockSpec((None, n_past, width), lambda s: (s, 0, 0))
    return pl.pallas_call(
        _band_sample_kernel,
        grid=(dec_batch,),
        in_specs=[new, new, new, cache, cache, pl.BlockSpec(bias.shape, lambda s: (0, 0, 0))],
        out_specs=pl.BlockSpec((dec_seq, width), lambda s: (s, 0)),
        out_shape=jax.ShapeDtypeStruct((dec_batch * dec_seq, width), BF),
        compiler_params=_cparams("parallel"),
        name="band_sample",
    )(q, k, v, ck, cv, bias)


def _final_kernel(x_ref, y0_ref, y1_ref, route_ref, g_ref, o_ref):
    g0, g1 = _gates(route_ref[...])
    x = x_ref[...] + (y0_ref[...].astype(F32) * g0 + y1_ref[...].astype(F32) * g1)
    o_ref[...] = _rms(x, g_ref[...], NORM_EPS)


def _final(x, y0, y1, route, g, *, first_tile, n_tiles):
    D = x.shape[1]

    def rows(width):
        return pl.BlockSpec((WIDE_TILE, width), lambda i: (first_tile + i, 0))

    return pl.pallas_call(
        _final_kernel,
        grid=(n_tiles,),
        in_specs=[rows(D), rows(D), rows(D), rows(LANE), pl.BlockSpec(g.shape, lambda i: (0, 0))],
        out_specs=pl.BlockSpec((WIDE_TILE, D), lambda i: (i, 0)),
        out_shape=jax.ShapeDtypeStruct((n_tiles * WIDE_TILE, D), F32),
        compiler_params=_cparams("parallel"),
        name="final_norm",
    )(x, y0, y1, route, g)


def _prep_even_weights(w_in, w_qup, w_kvup):
    D = w_in.shape[0]
    a_in = Q_RANK + KV_RANK + ROPE
    bq = H_B * 2 * DH_B
    wcq, wckv, wkr = w_in[:, :Q_RANK], w_in[:, Q_RANK:Q_RANK + KV_RANK], w_in[:, Q_RANK + KV_RANK:a_in]
    wqd, wkd, wvd = w_in[:, a_in:a_in + bq], w_in[:, a_in + bq:a_in + 2 * bq], w_in[:, a_in + 2 * bq:]
    half = ROPE // 2

    def z(n):
        return jnp.zeros((D, n), w_in.dtype)

    kr_m = jnp.concatenate([z(NOPE), wkr, z(HEAD_PAD - NOPE - ROPE)], axis=1)
    kr_s = jnp.concatenate([z(NOPE), wkr[:, half:], wkr[:, :half], z(HEAD_PAD - NOPE - ROPE)], axis=1)
    w0 = jnp.concatenate([wcq, wckv, kr_m, kr_s, wqd, wkd, wvd], axis=1).astype(BF)
    wq3 = w_qup.reshape(Q_RANK, H_A, NOPE + ROPE)
    nope, r1, r2 = wq3[:, :, :NOPE], wq3[:, :, NOPE:NOPE + half], wq3[:, :, NOPE + half:]
    zq = jnp.zeros((Q_RANK, H_A, HEAD_PAD - NOPE - ROPE), w_qup.dtype)
    wq = jnp.concatenate([nope, r1, r2, zq], axis=-1).reshape(Q_RANK, H_A * HEAD_PAD).astype(BF)
    wqs = jnp.concatenate([jnp.zeros_like(nope), r2, r1, zq], axis=-1).reshape(Q_RANK, H_A * HEAD_PAD).astype(BF)
    wkv3 = w_kvup.reshape(KV_RANK, H_A, NOPE + V_A)
    wk = jnp.concatenate([wkv3[:, :, :NOPE], jnp.zeros((KV_RANK, H_A, HEAD_PAD - NOPE), w_kvup.dtype)],
                         axis=-1).reshape(KV_RANK, H_A * HEAD_PAD).astype(BF)
    wv4 = wkv3[:, :, NOPE:].reshape(KV_RANK, H_A // 2, 2, V_A)
    zv = jnp.zeros((KV_RANK, H_A // 2, HEAD_PAD - V_A), w_kvup.dtype)
    wv = jnp.concatenate([wv4[:, :, 0], zv, zv, wv4[:, :, 1]], axis=-1).reshape(KV_RANK, H_A * HEAD_PAD).astype(BF)
    vone = jnp.zeros((H_A // 2, 2 * HEAD_PAD), F32).at[:, V_A].set(1.0).at[:, HEAD_PAD].set(1.0)
    r = jnp.arange(ROPE)
    e_mat = jnp.zeros((ROPE, H_A, HEAD_PAD), F32).at[r[:, None], jnp.arange(H_A)[None, :], NOPE + r[:, None]].set(1.0)
    return w0, wq, wqs, wk, wv, vone.reshape(1, H_A * HEAD_PAD), e_mat.reshape(ROPE, H_A * HEAD_PAD).astype(BF)


def _rope_tables(pos):
    half = ROPE // 2
    inv = jnp.power(ROPE_BASE, -jnp.arange(half, dtype=F32) / half)
    ang = pos.astype(F32)[:, None] * inv[None, :]
    c, s = jnp.cos(ang), jnp.sin(ang)
    n = pos.shape[0]
    pad = jnp.zeros((n, HEAD_PAD - NOPE - ROPE), F32)
    cs = jnp.concatenate([jnp.ones((n, NOPE), F32), c, c, pad], axis=1)
    sn = jnp.concatenate([jnp.zeros((n, NOPE), F32), -s, s, pad], axis=1)
    return cs, sn


def _band_bias_tiles(table, n_rows, n_keys, key_offset, masked):
    i = jnp.arange(n_rows)[:, None]
    kpos = jnp.arange(n_keys)[None, :] - key_offset
    d_max = n_rows - 1 + key_offset
    w = n_rows + n_keys
    d = d_max - jnp.arange(w)
    rev = (table.astype(F32) * LOG2E)[:, jnp.clip(d, -REL_CLIP, REL_CLIP) + REL_CLIP]
    skew = jnp.tile(rev, (1, n_rows))[:, :n_rows * (w - 1)].reshape(-1, n_rows, w - 1)
    bias = skew[:, :, n_rows - 1:n_rows - 1 + n_keys]
    if masked:
        qc, kc = i // CHUNK, jnp.floor_divide(kpos, CHUNK)
        vis = (kc <= qc) & (kc >= qc - LEFT_CHUNKS)
        bias = jnp.where(vis[None], bias, NEG_INF)
    return bias.reshape(H_C // 2, 2 * n_rows, n_keys)


def kernel(x_prompt, x_sample, cache_mla_ckv, cache_mla_krope, cache_diff_k, cache_diff_v, cache_band_k, cache_band_v, ln_mix, w_in_even, mla_q_norm, mla_w_qup, mla_kv_norm, mla_w_kvup, diff_lam_q1, diff_lam_k1, diff_lam_q2, diff_lam_k2, diff_subln, w_out_even, w_in_odd, band_rel_bias, w_out_odd, ln_ffn, moe_w_group, moe_b_group, moe_w_router, moe_b_router, moe_w_gate, moe_w_up, moe_w_down, ln_final):
    B, S, D = x_prompt.shape
    DB, DS, _ = x_sample.shape
    n_past = cache_mla_ckv.shape[2]
    c_past = cache_band_k.shape[2]
    assert ln_mix.shape[0] == 2 and S % (2 * min(ATT_TILE, S // 2)) == 0 and S % BAND_TILE == 0
    assert (B * S) % WIDE_TILE == 0 and (DB * DS) % WIDE_TILE == 0 and ROW_TILE % DS == 0 and DS == CHUNK
    assert n_past % CHUNK == 0 and c_past == LEFT_CHUNKS * CHUNK and S >= c_past and c_past % ROW_TILE == 0
    BS, NS = B * S, DB * DS
    T = BS + NS
    n_p, n_s = BS // ROW_TILE, NS // ROW_TILE
    xp, xs = x_prompt.reshape(BS, D), x_sample.reshape(NS, D)
    row = lambda a: a.reshape(1, -1)

    w0, wq, wqs, wk, wv, vone, e_mat = _prep_even_weights(w_in_even[0], mla_w_qup[0], mla_w_kvup[0])
    cs_p, sn_p = _rope_tables(jnp.arange(S))
    cs_s, sn_s = _rope_tables(n_past + jnp.arange(DS))
    reps = ROW_TILE // DS
    even_w = (row(ln_mix[0]), w0, row(mla_q_norm[0]), wq, wqs, row(mla_kv_norm[0]), wk, wv, vone)
    qa, ka, va, ckv_p, kr_p, qd, kdb, vdb, kd_p, vd_p = _even_in(xp, *even_w, cs_p, sn_p)
    qa_s, ka_s, va_s, ckv_s, kr_s, qd_s, kdb_s, vdb_s, kd_s, vd_s = _even_in(
        xs, *even_w, jnp.tile(cs_s, (reps, 1)), jnp.tile(sn_s, (reps, 1)))
    slopes = jnp.exp2(-8.0 * jnp.arange(1, H_B + 1, dtype=F32) / H_B) * LOG2E
    lamv = jnp.stack([diff_lam_q1[0], diff_lam_k1[0], diff_lam_q2[0], diff_lam_k2[0]]).astype(F32)
    subln = row(diff_subln[0])
    lam_init = 0.8 - 0.6 * math.exp(-0.3 * 0)
    oa_p = _mla_prompt(qa, ka, va, batch=B, seq=S)
    ob_p = _diff_prompt(slopes, lamv, subln, qd, kdb, vdb, batch=B, seq=S, lam_init=lam_init)
    oa_s, ob_s = _even_sample(
        slopes, lamv, subln, wk, wv, vone, e_mat, qa_s, ka_s, va_s, cache_mla_ckv[0], cache_mla_krope[0],
        qd_s, kdb_s, vdb_s, cache_diff_k[0], cache_diff_v[0],
        n_prompt_rows=0, dec_seq=DS, lam_init=lam_init)

    def router_weights(l):
        wr = jnp.concatenate([moe_w_group[l], moe_w_router[l],
                              jnp.zeros((D, LANE - N_GROUPS - N_EXPERTS), F32)], axis=1).astype(BF)
        br = jnp.concatenate([moe_b_group[l], moe_b_router[l],
                              jnp.zeros((LANE - N_GROUPS - N_EXPERTS,), F32)]).astype(F32)
        return wr, row(br)

    n_a = H_A * V_A
    wo = w_out_even[0].astype(BF)
    wr, br = router_weights(0)
    x1, xn, route, cnt = _out_router([(xp, xs), (oa_p, oa_s), (ob_p, ob_s)], [wo[:n_a], wo[n_a:]],
                                     row(ln_ffn[0]), wr, br)
    y0, y1 = _moe(xn, route, cnt, moe_w_gate, moe_w_up, moe_w_down, layer=0)

    w_odd = w_in_odd[0].astype(BF)
    n_c = H_C * DH_C
    x2, qc, kc, vc = _odd_in(x1, y0, y1, route, row(ln_mix[1]), w_odd)
    tail = c_past // ROW_TILE
    tiles_per_seq = S // ROW_TILE
    tile_ids = jnp.concatenate(
        [jnp.arange(tiles_per_seq - tail, tiles_per_seq, dtype=jnp.int32) + b * tiles_per_seq for b in range(B)]
        + [jnp.arange(n_p, n_p + n_s, dtype=jnp.int32)])
    st = _state_rows(tile_ids, x2, row(ln_mix[1]), w_odd[:, n_c:])
    bias_p = _band_bias_tiles(band_rel_bias[0], BAND_TILE, 3 * BAND_TILE, 2 * BAND_TILE, True)
    bias_s = _band_bias_tiles(band_rel_bias[0], DS, c_past + DS, c_past, False)
    oc_p = _band_prompt(qc, kc, vc, bias_p, batch=B, seq=S)
    oc_s = _band_sample(qc, kc, vc, cache_band_k[0].reshape(DB, c_past, n_c), cache_band_v[0].reshape(DB, c_past, n_c),
                        bias_s, n_prompt_rows=BS, dec_seq=DS)
    wr, br = router_weights(1)
    x3, xn, route, cnt = _out_router([(x2,), (oc_p, oc_s)], [w_out_odd[0].astype(BF)], row(ln_ffn[1]), wr, br)
    y0, y1 = _moe(xn, route, cnt, moe_w_gate, moe_w_up, moe_w_down, layer=1)
    g_fin = row(ln_final)
    y_prompt = _final(x3, y0, y1, route, g_fin, first_tile=0, n_tiles=BS // WIDE_TILE).reshape(B, S, D)
    y_sample = _final(x3, y0, y1, route, g_fin, first_tile=BS // WIDE_TILE, n_tiles=NS // WIDE_TILE).reshape(DB, DS, D)

    def shaped(a_p, a_s, *tail_shape):
        return a_p.reshape(1, B, S, *tail_shape), a_s.reshape(1, DB, DS, *tail_shape)

    ckv_p, ckv_s = shaped(ckv_p, ckv_s, KV_RANK)
    kr_p, kr_s = shaped(kr_p, kr_s, ROPE)
    kd_p, kd_s = shaped(kd_p, kd_s, H_B, 2 * DH_B)
    vd_p, vd_s = shaped(vd_p, vd_s, H_B, V_B)
    n_tail = B * c_past
    bk_p = st[:n_tail, :n_c].reshape(1, B, c_past, H_C, DH_C)
    bv_p = st[:n_tail, n_c:].reshape(1, B, c_past, H_C, DH_C)
    k_new = st[n_tail:, :n_c].reshape(DB, DS, H_C, DH_C)
    v_new = st[n_tail:, n_c:].reshape(DB, DS, H_C, DH_C)
    bk_s = jnp.concatenate([cache_band_k[0][:, DS:], k_new], axis=1)[None]
    bv_s = jnp.concatenate([cache_band_v[0][:, DS:], v_new], axis=1)[None]
    return (y_prompt, y_sample, ckv_p, kr_p, kd_p, vd_p, bk_p, bv_p, ckv_s, kr_s, kd_s, vd_s, bk_s, bv_s)
```
